```python
import jax, jax.numpy as jnp
from jax import lax
import numpy as np

D_MODEL = 1024
BATCH = 8
SEQ = 2048
DEPTH = 1
DEC_BATCH = 32
DEC_SEQ = 64
PAST_LEN = 4096

CHUNK = 64
Q_BLOCK = 128
FOX_HEADS = 8
FOX_DIM = 64
MLA_HEADS = 8
MLA_NOPE = 64
MLA_ROPE = 32
MLA_VDIM = 64
Q_LORA = 384
KV_LORA = 256
D_FF = 2816
CONV_W = 3
ROPE_THETA = 10000.0
EPS = 1e-6
NEG = -1e30

FOX_W = FOX_HEADS * FOX_DIM
MLA_W = MLA_HEADS * MLA_VDIM
D_MIX = FOX_W + MLA_W
IN_SIZES = [FOX_W, FOX_W, FOX_W, FOX_HEADS, Q_LORA, KV_LORA, MLA_ROPE]
D_IN = sum(IN_SIZES)
IN_SPLIT_POINTS = [int(s) for s in np.cumsum(IN_SIZES)[:-1]]

kernel_name = "fox_mla_hybrid_convffn_stream_step"


def rmsnorm(x, g):
    xf = x.astype(jnp.float32)
    y = xf * lax.rsqrt(jnp.mean(xf * xf, axis=-1, keepdims=True) + EPS)
    return (y * g.astype(jnp.float32)).astype(x.dtype)


def rope(x, pos):
    half = MLA_ROPE // 2
    inv = ROPE_THETA ** (-jnp.arange(half, dtype=jnp.float32) / half)
    ang = pos.astype(jnp.float32)[:, None] * inv[None, :]
    cos = jnp.cos(ang)[:, None, :]
    sin = jnp.sin(ang)[:, None, :]
    xf = x.astype(jnp.float32)
    x1, x2 = xf[..., :half], xf[..., half:]
    return jnp.concatenate([x1 * cos - x2 * sin, x1 * sin + x2 * cos], axis=-1).astype(x.dtype)


def over_query_blocks(attend, q_side, q_pos):
    T = q_pos.shape[0]
    blk = min(Q_BLOCK, T)
    nb = T // blk

    def split(a):
        return jnp.moveaxis(a.reshape(a.shape[0], nb, blk, *a.shape[2:]), 1, 0)

    xs = (tuple(split(a) for a in q_side), q_pos.reshape(nb, blk))
    out = lax.map(lambda b: attend(*b[0], b[1]), xs)
    out = jnp.moveaxis(out, 0, 1)
    return out.reshape(out.shape[0], T, *out.shape[3:])


def fox_attend(q, k, v, fq, fk, q_pos, k_pos):
    s = jnp.einsum('bqhd,bkhd->bhqk', q, k, preferred_element_type=jnp.float32) * (FOX_DIM ** -0.5)
    s = s + jnp.transpose(fq, (0, 2, 1))[:, :, :, None] - jnp.transpose(fk, (0, 2, 1))[:, :, None, :]
    mask = k_pos[None, :] <= q_pos[:, None]
    p = jax.nn.softmax(jnp.where(mask, s, NEG), axis=-1)
    return jnp.einsum('bhqk,bkhd->bqhd', p.astype(v.dtype), v)


def mla_attend(q_lat, q_rope, q_pos, c_kv, k_rope, k_pos, w_uv):
    s = (jnp.einsum('bqhc,bkc->bhqk', q_lat, c_kv, preferred_element_type=jnp.float32)
         + jnp.einsum('bqhr,bkr->bhqk', q_rope, k_rope, preferred_element_type=jnp.float32))
    s = s * ((MLA_NOPE + MLA_ROPE) ** -0.5)
    mask = (k_pos // CHUNK)[None, :] <= (q_pos // CHUNK)[:, None]
    p = jax.nn.softmax(jnp.where(mask, s, NEG), axis=-1)
    o_lat = jnp.einsum('bhqk,bkc->bqhc', p.astype(c_kv.dtype), c_kv)
    return jnp.einsum('bqhc,chd->bqhd', o_lat, w_uv)


def layer(x, past, lp):
    (g_attn, w_in, b_f, g_q, w_q_up, g_kv, w_uk, w_uv, w_out,
     g_ffn, w_up, conv_w, conv_b, w_down) = lp
    pk, pv, plogf, pc, pkr, pconv = past
    B, T, _ = x.shape
    P = pk.shape[1]
    q_pos = P + jnp.arange(T, dtype=jnp.int32)
    k_pos = jnp.arange(P + T, dtype=jnp.int32)

    h = rmsnorm(x, g_attn)
    z = h @ w_in
    q, k, v, f_lin, q_c, c_kv, k_r = jnp.split(z, IN_SPLIT_POINTS, axis=-1)

    q = q.reshape(B, T, FOX_HEADS, FOX_DIM)
    k = k.reshape(B, T, FOX_HEADS, FOX_DIM)
    v = v.reshape(B, T, FOX_HEADS, FOX_DIM)
    logf = jax.nn.log_sigmoid(f_lin.astype(jnp.float32) + b_f.astype(jnp.float32))
    plogf32 = plogf.astype(jnp.float32)
    f_past = jnp.cumsum(plogf32, axis=1)
    f_new = jnp.sum(plogf32, axis=1, keepdims=True) + jnp.cumsum(logf, axis=1)
    f_keys = jnp.concatenate([f_past, f_new], axis=1)
    k_all = jnp.concatenate([pk.astype(k.dtype), k], axis=1)
    v_all = jnp.concatenate([pv.astype(v.dtype), v], axis=1)
    fox = over_query_blocks(
        lambda qb, fb, pb: fox_attend(qb, k_all, v_all, fb, f_keys, pb, k_pos),
        (q, f_new), q_pos)

    q_full = (rmsnorm(q_c, g_q) @ w_q_up).reshape(B, T, MLA_HEADS, MLA_NOPE + MLA_ROPE)
    q_nope = q_full[..., :MLA_NOPE]
    q_rope = rope(q_full[..., MLA_NOPE:], q_pos)
    q_lat = jnp.einsum('bthd,chd->bthc', q_nope, w_uk)
    c_kv = rmsnorm(c_kv, g_kv)
    k_r = rope(k_r[:, :, None, :], q_pos)[:, :, 0, :]
    c_all = jnp.concatenate([pc.astype(c_kv.dtype), c_kv], axis=1)
    kr_all = jnp.concatenate([pkr.astype(k_r.dtype), k_r], axis=1)
    mla = over_query_blocks(
        lambda lb, rb, pb: mla_attend(lb, rb, pb, c_all, kr_all, k_pos, w_uv),
        (q_lat, q_rope), q_pos)

    mixed = jnp.concatenate([fox.reshape(B, T, FOX_W), mla.reshape(B, T, MLA_W)], axis=-1)
    x = x + mixed @ w_out

    u = rmsnorm(x, g_ffn) @ w_up
    up = jnp.concatenate([pconv.astype(u.dtype), u], axis=1)
    uc = conv_b + sum(conv_w[j] * up[:, j:j + T] for j in range(CONV_W))
    gate, val = jnp.split(uc, 2, axis=-1)
    x = x + (jax.nn.silu(gate) * val) @ w_down
    conv_new = up[:, -(CONV_W - 1):]
    return x, (k, v, logf, c_kv, k_r, conv_new)


def setup_inputs(seed: int = 0) -> dict:
    key = jax.random.key(seed)
    ks = jax.random.split(key, 32)
    f32 = jnp.float32
    nrm = lambda k, s: jax.random.normal(k, s, f32)
    return {
        "x_prompt": nrm(ks[0], (BATCH, SEQ, D_MODEL)),
        "x_sample": nrm(ks[1], (DEC_BATCH, DEC_SEQ, D_MODEL)),
        "cache_fox_k": nrm(ks[2], (DEPTH, DEC_BATCH, PAST_LEN, FOX_HEADS, FOX_DIM)),
        "cache_fox_v": nrm(ks[3], (DEPTH, DEC_BATCH, PAST_LEN, FOX_HEADS, FOX_DIM)),
        "cache_fox_logf": jax.nn.log_sigmoid(2.0 + 0.5 * nrm(ks[4], (DEPTH, DEC_BATCH, PAST_LEN, FOX_HEADS))),
        "cache_mla_latent": nrm(ks[5], (DEPTH, DEC_BATCH, PAST_LEN, KV_LORA)),
        "cache_mla_krope": nrm(ks[6], (DEPTH, DEC_BATCH, PAST_LEN, MLA_ROPE)),
        "state_ffn_conv": nrm(ks[7], (DEPTH, DEC_BATCH, CONV_W - 1, 2 * D_FF)),
        "attn_norm": 1.0 + 0.05 * nrm(ks[8], (DEPTH, D_MODEL)),
        "w_in": nrm(ks[9], (DEPTH, D_MODEL, D_IN)) * D_MODEL ** -0.5,
        "b_forget": 2.0 + 0.5 * nrm(ks[10], (DEPTH, FOX_HEADS)),
        "q_norm": 1.0 + 0.05 * nrm(ks[11], (DEPTH, Q_LORA)),
        "w_q_up": nrm(ks[12], (DEPTH, Q_LORA, MLA_HEADS * (MLA_NOPE + MLA_ROPE))) * Q_LORA ** -0.5,
        "kv_norm": 1.0 + 0.05 * nrm(ks[13], (DEPTH, KV_LORA)),
        "w_uk": nrm(ks[14], (DEPTH, KV_LORA, MLA_HEADS, MLA_NOPE)) * KV_LORA ** -0.5,
        "w_uv": nrm(ks[15], (DEPTH, KV_LORA, MLA_HEADS, MLA_VDIM)) * KV_LORA ** -0.5,
        "w_out": nrm(ks[16], (DEPTH, D_MIX, D_MODEL)) * D_MIX ** -0.5,
        "ffn_norm": 1.0 + 0.05 * nrm(ks[17], (DEPTH, D_MODEL)),
        "w_up": nrm(ks[18], (DEPTH, D_MODEL, 2 * D_FF)) * D_MODEL ** -0.5,
        "conv_w": nrm(ks[19], (DEPTH, CONV_W, 2 * D_FF)) * CONV_W ** -0.5,
        "conv_b": 0.01 * nrm(ks[20], (DEPTH, 2 * D_FF)),
        "w_down": nrm(ks[21], (DEPTH, D_FF, D_MODEL)) * D_FF ** -0.5,
        "final_norm": 1.0 + 0.05 * nrm(ks[22], (D_MODEL,)),
    }


def reference(x_prompt, x_sample, cache_fox_k, cache_fox_v, cache_fox_logf, cache_mla_latent,
              cache_mla_krope, state_ffn_conv, attn_norm, w_in, b_forget, q_norm, w_q_up, kv_norm,
              w_uk, w_uv, w_out, ffn_norm, w_up, conv_w, conv_b, w_down, final_norm):
    hp, hs = x_prompt, x_sample
    Bp = x_prompt.shape[0]
    dt = x_prompt.dtype
    new_p, new_s = [], []
    for l in range(DEPTH):
        lp = (attn_norm[l], w_in[l], b_forget[l], q_norm[l], w_q_up[l], kv_norm[l], w_uk[l], w_uv[l],
              w_out[l], ffn_norm[l], w_up[l], conv_w[l], conv_b[l], w_down[l])
        no_past = (jnp.zeros((Bp, 0, FOX_HEADS, FOX_DIM), dt), jnp.zeros((Bp, 0, FOX_HEADS, FOX_DIM), dt),
                   jnp.zeros((Bp, 0, FOX_HEADS), jnp.float32), jnp.zeros((Bp, 0, KV_LORA), dt),
                   jnp.zeros((Bp, 0, MLA_ROPE), dt), jnp.zeros((Bp, CONV_W - 1, 2 * D_FF), dt))
        hp, st_p = layer(hp, no_past, lp)
        past = (cache_fox_k[l], cache_fox_v[l], cache_fox_logf[l], cache_mla_latent[l],
                cache_mla_krope[l], state_ffn_conv[l])
        hs, st_s = layer(hs, past, lp)
        new_p.append(st_p)
        new_s.append(st_s)
    y_prompt = rmsnorm(hp, final_norm)
    y_sample = rmsnorm(hs, final_norm)
    fox_k_p, fox_v_p, fox_logf_p, mla_latent_p, mla_krope_p, ffn_conv_p = [jnp.stack(f) for f in zip(*new_p)]
    fox_k_s, fox_v_s, fox_logf_s, mla_latent_s, mla_krope_s, ffn_conv_s = [jnp.stack(f) for f in zip(*new_s)]
    return (y_prompt, y_sample,
            fox_k_p, fox_v_p, fox_logf_p, mla_latent_p, mla_krope_p, ffn_conv_p,
            fox_k_s, fox_v_s, fox_logf_s, mla_latent_s, mla_krope_s, ffn_conv_s)
```

```python
import functools

import numpy as np
import jax
import jax.numpy as jnp
from jax import lax
from jax.experimental import pallas as pl
from jax.experimental.pallas import tpu as pltpu

D_MODEL = 1024
CHUNK = 64
HEADS = 8
FOX_DIM = 64
MLA_NOPE = 64
MLA_ROPE = 32
MLA_VDIM = 64
Q_LORA = 384
KV_LORA = 256
D_FF = 2816
CONV_W = 3
ROPE_THETA = 10000.0
EPS = 1e-6
NEG = -1e30

FOX_W = HEADS * FOX_DIM
IN_SIZES = [FOX_W, FOX_W, FOX_W, HEADS, Q_LORA, KV_LORA, MLA_ROPE]
IN_SPLITS = [int(s) for s in np.cumsum(IN_SIZES)[:-1]]

LANES = 128
AUG = 16
N_PAIR = HEADS // 2
FF_CHUNK = 256
N_FF = D_FF // FF_CHUNK
VMEM_LIMIT = 56 * 1024 * 1024

C_Q, C_K, C_V = 0, 512, 1024
C_QC = 1536
C_CKV = 1920
C_KR = 2176
C_KRR = 2304
C_F = 2432
N_A = 2560

BF = jnp.bfloat16
F32 = jnp.float32


def _dot(a, b):
    return jnp.dot(a, b, preferred_element_type=F32)


def _dot_nt(a, b):
    return lax.dot_general(a, b, (((1,), (1,)), ((), ())), preferred_element_type=F32)


def _split3(x):
    hi = x.astype(BF)
    r1 = x - hi.astype(F32)
    mid = r1.astype(BF)
    lo = (r1 - mid.astype(F32)).astype(BF)
    return hi, mid, lo


def _log_sigmoid(x):
    return jnp.minimum(x, 0.0) - jnp.log1p(jnp.exp(-jnp.abs(x)))


def _params(n_axes):
    return pltpu.CompilerParams(dimension_semantics=("arbitrary",) * n_axes, vmem_limit_bytes=VMEM_LIMIT)


def _const_spec(shape):
    nd = len(shape)
    return pl.BlockSpec(shape, lambda *_: (0,) * nd, pipeline_mode=pl.Buffered(1))


def _proj_kernel(x_ref, off_ref, cos_ref, sin_ref, g_ref, wa_ref, bf_ref, gq_ref, gkv_ref, wq2_ref,
                 wuk_ref, place_ref, ones_ref, rep_ref,
                 k_ref, v_ref, logf_ref, lat_ref, kr_ref, qx_ref, kx_ref, vb_ref, mkey_ref, qlat_ref,
                 qrope_ref, carry_ref, *, tm, seq_len, mla_scale):
    x = x_ref[...]
    ms = jnp.mean(x * x, axis=-1, keepdims=True)
    hn = (x * lax.rsqrt(ms + EPS) * g_ref[...]).astype(BF)
    z = _dot(hn, wa_ref[...])

    zk = z[:, C_K:C_K + FOX_W]
    zv = z[:, C_V:C_V + FOX_W]
    k_ref[...] = zk
    v_ref[...] = zv
    vb_ref[...] = zv.astype(BF)

    lane = lax.broadcasted_iota(jnp.int32, (tm, LANES), 1)
    logf = _log_sigmoid(z[:, C_F:C_F + LANES] + bf_ref[...])
    logf = jnp.where(lane < HEADS, logf, 0.0)
    logf_ref[...] = logf[:, :HEADS]
    row = lax.broadcasted_iota(jnp.int32, (tm, tm), 0)
    col = lax.broadcasted_iota(jnp.int32, (tm, tm), 1)
    keep = col <= row
    if seq_len < tm:
        keep = keep & ((col // seq_len) == (row // seq_len))
    tri = jnp.where(keep, 1.0, 0.0).astype(BF)
    hi, mid, lo = _split3(logf)
    fcum = _dot(tri, hi) + _dot(tri, mid) + _dot(tri, lo) + off_ref[...]
    if seq_len > tm:
        @pl.when(pl.program_id(0) % (seq_len // tm) == 0)
        def _():
            carry_ref[...] = jnp.zeros_like(carry_ref)
        fcum = fcum + carry_ref[0:1, :]
        carry_ref[0:1, :] = fcum[tm - 1:tm, :]

    hi, mid, lo = _split3(fcum)
    comb = hi.astype(F32) + pltpu.roll(mid.astype(F32), HEADS, 1) + pltpu.roll(lo.astype(F32), 2 * HEADS, 1)
    aug = _dot(comb.astype(BF), place_ref[...]) + ones_ref[...]
    zq = z[:, C_Q:C_Q + FOX_W] * (FOX_DIM ** -0.5)
    for p in range(N_PAIR):
        lo_, hi_ = 2 * LANES * p, 2 * LANES * p + LANES
        qx_ref[:, lo_:hi_] = zq[:, LANES * p:LANES * (p + 1)].astype(BF)
        qx_ref[:, hi_:hi_ + LANES] = aug[:, LANES * p:LANES * (p + 1)].astype(BF)
        kx_ref[:, lo_:hi_] = zk[:, LANES * p:LANES * (p + 1)].astype(BF)
        kx_ref[:, hi_:hi_ + LANES] = aug[:, FOX_W + LANES * p:FOX_W + LANES * (p + 1)].astype(BF)

    zc = z[:, C_CKV:C_CKV + KV_LORA]
    ckv = zc * lax.rsqrt(jnp.mean(zc * zc, axis=-1, keepdims=True) + EPS) * gkv_ref[...]
    lat_ref[...] = ckv
    cos = cos_ref[...]
    sin = sin_ref[...]
    krb = z[:, C_KR:C_KR + LANES] * cos[:, :LANES] + z[:, C_KRR:C_KRR + LANES] * sin[:, :LANES]
    kr_ref[...] = krb[:, :MLA_ROPE]
    mkey_ref[:, :KV_LORA] = ckv.astype(BF)
    mkey_ref[:, KV_LORA:] = _dot(krb.astype(BF), rep_ref[...]).astype(BF)

    zqc = z[:, C_QC:C_QC + Q_LORA]
    qcn = (zqc * lax.rsqrt(jnp.mean(zqc * zqc, axis=-1, keepdims=True) + EPS) * gq_ref[...]).astype(BF)
    zq2 = _dot(qcn, wq2_ref[...])
    nr = HEADS * MLA_ROPE
    qr = zq2[:, FOX_W:FOX_W + nr] * cos + zq2[:, FOX_W + nr:FOX_W + 2 * nr] * sin
    qrope_ref[...] = (qr * mla_scale).astype(BF)
    for h in range(HEADS):
        g = h // 4
        qn = zq2[:, 256 * g:256 * (g + 1)].astype(BF)
        qlat_ref[h] = (_dot(qn, wuk_ref[h]) * mla_scale).astype(BF)


def _proj(x2d, off, cos, sin, g, wa, bfp, gq, gkv, wq2, wukp, place, ones, rep, *, seq_len, tm):
    n = x2d.shape[0]
    nt = n // tm
    if seq_len >= tm:
        tps = seq_len // tm
        tab_map = lambda i: (i % tps, 0)
    else:
        tab_map = lambda i: (0, 0)
    row = lambda w: pl.BlockSpec((tm, w), lambda i: (i, 0))
    out_shape = (
        jax.ShapeDtypeStruct((n, FOX_W), F32),
        jax.ShapeDtypeStruct((n, FOX_W), F32),
        jax.ShapeDtypeStruct((n, HEADS), F32),
        jax.ShapeDtypeStruct((n, KV_LORA), F32),
        jax.ShapeDtypeStruct((n, MLA_ROPE), F32),
        jax.ShapeDtypeStruct((n, 2 * FOX_W), BF),
        jax.ShapeDtypeStruct((n, 2 * FOX_W), BF),
        jax.ShapeDtypeStruct((n, FOX_W), BF),
        jax.ShapeDtypeStruct((n, 2 * KV_LORA), BF),
        jax.ShapeDtypeStruct((HEADS, n, KV_LORA), BF),
        jax.ShapeDtypeStruct((n, HEADS * MLA_ROPE), BF),
    )
    out_specs = (row(FOX_W), row(FOX_W), row(HEADS), row(KV_LORA), row(MLA_ROPE), row(2 * FOX_W),
                 row(2 * FOX_W), row(FOX_W), row(2 * KV_LORA),
                 pl.BlockSpec((HEADS, tm, KV_LORA), lambda i: (0, i, 0)), row(HEADS * MLA_ROPE))
    in_specs = [row(D_MODEL), row(LANES),
                pl.BlockSpec((tm, 2 * LANES), tab_map), pl.BlockSpec((tm, 2 * LANES), tab_map),
                _const_spec(g.shape), _const_spec(wa.shape), _const_spec(bfp.shape), _const_spec(gq.shape),
                _const_spec(gkv.shape), _const_spec(wq2.shape), _const_spec(wukp.shape),
                _const_spec(place.shape), _const_spec(ones.shape), _const_spec(rep.shape)]
    kern = functools.partial(_proj_kernel, tm=tm, seq_len=seq_len,
                             mla_scale=float((MLA_NOPE + MLA_ROPE) ** -0.5))
    return pl.pallas_call(
        kern, grid=(nt,), in_specs=in_specs, out_specs=out_specs, out_shape=out_shape,
        scratch_shapes=[pltpu.VMEM((8, LANES), F32)],
        compiler_params=_params(1), name="proj")(
            x2d, off, cos, sin, g, wa, bfp, gq, gkv, wq2, wukp, place, ones, rep)


def _fox_prep_kernel(plogf_ref, placek_ref, onesk_ref, ka_ref, tot_ref, carry_ref, *, tc, seq_out):
    c = pl.program_id(1)

    @pl.when(c == 0)
    def _():
        carry_ref[...] = jnp.zeros_like(carry_ref)

    x = plogf_ref[0]
    row = lax.broadcasted_iota(jnp.int32, (tc, tc), 0)
    col = lax.broadcasted_iota(jnp.int32, (tc, tc), 1)
    tri = jnp.where(col <= row, 1.0, 0.0).astype(BF)
    hi, mid, lo = _split3(x)
    f = _dot(tri, hi) + _dot(tri, mid) + _dot(tri, lo) + carry_ref[0:1, :]
    carry_ref[0:1, :] = f[tc - 1:tc, :]
    hi, mid, lo = _split3(f)
    ka = _dot(hi, placek_ref[0]) + _dot(mid, placek_ref[1]) + _dot(lo, placek_ref[2]) + onesk_ref[...]
    ka_ref[0] = ka.astype(BF)

    @pl.when(c == pl.num_programs(1) - 1)
    def _():
        tot_ref[0] = jnp.broadcast_to(f[tc - 1:tc, :], (seq_out, HEADS))


def _fox_prep(plogf, placek, onesk, *, seq_out, tc=256):
    b, p, _ = plogf.shape
    kern = functools.partial(_fox_prep_kernel, tc=tc, seq_out=seq_out)
    return pl.pallas_call(
        kern, grid=(b, p // tc),
        in_specs=[pl.BlockSpec((1, tc, HEADS), lambda i, c: (i, c, 0)),
                  _const_spec(placek.shape), _const_spec(onesk.shape)],
        out_specs=(pl.BlockSpec((1, tc, LANES), lambda i, c: (i, c, 0)),
                   pl.BlockSpec((1, seq_out, HEADS), lambda i, c: (i, 0, 0))),
        out_shape=(jax.ShapeDtypeStruct((b, p, LANES), BF), jax.ShapeDtypeStruct((b, seq_out, HEADS), F32)),
        scratch_shapes=[pltpu.VMEM((8, HEADS), F32)],
        compiler_params=_params(2), name="fox_prep")(plogf, placek, onesk)


def _online_softmax(s, m_ref, l_ref, h):
    m_prev = m_ref[h]
    m_new = jnp.maximum(m_prev, jnp.max(s, axis=-1, keepdims=True))
    p = jnp.exp(s - m_new)
    alpha = jnp.exp(m_prev - m_new)
    l_ref[h] = alpha * l_ref[h] + jnp.sum(p, axis=-1, keepdims=True)
    m_ref[h] = m_new
    return p.astype(BF), alpha


def _fox_init(qx, qpad_ref, m_ref, l_ref, acc_ref):
    tq = qx.shape[0]
    lane = lax.broadcasted_iota(jnp.int32, (tq, 2 * LANES), 1)
    for h in range(HEADS):
        p, hh = divmod(h, 2)
        blk = qx[:, 2 * LANES * p:2 * LANES * (p + 1)]
        keep = ((lane >= FOX_DIM * hh) & (lane < FOX_DIM * (hh + 1))) | (
            (lane >= LANES + AUG * h) & (lane < LANES + AUG * (h + 1)))
        qpad_ref[h] = jnp.where(keep, blk, jnp.zeros_like(blk))
    m_ref[...] = jnp.full(m_ref.shape, NEG, F32)
    l_ref[...] = jnp.zeros(l_ref.shape, F32)
    acc_ref[...] = jnp.zeros(acc_ref.shape, F32)


def _fox_pair_update(p, alphas, pvs, acc_ref):
    tq = pvs[0].shape[0]
    first = lax.broadcasted_iota(jnp.int32, (tq, LANES), 1) < FOX_DIM
    acc_ref[p] = acc_ref[p] * jnp.where(first, alphas[0], alphas[1]) + jnp.where(first, pvs[0], pvs[1])


def _fox_new_tile(kx, vb, qpad_ref, m_ref, l_ref, acc_ref, masked):
    tq = qpad_ref.shape[1]
    tk = kx.shape[0]
    if masked:
        causal = lax.broadcasted_iota(jnp.int32, (tq, tk), 1) <= lax.broadcasted_iota(jnp.int32, (tq, tk), 0)
    for p in range(N_PAIR):
        kxp = kx[:, 2 * LANES * p:2 * LANES * (p + 1)]
        vp = vb[:, LANES * p:LANES * (p + 1)]
        alphas, pvs = [], []
        for hh in range(2):
            h = 2 * p + hh
            s = _dot_nt(qpad_ref[h], kxp)
            if masked:
                s = jnp.where(causal, s, NEG)
            pb, alpha = _online_softmax(s, m_ref, l_ref, h)
            alphas.append(alpha)
            pvs.append(_dot(pb, vp))
        _fox_pair_update(p, alphas, pvs, acc_ref)


def _fox_finalize(o_ref, l_ref, acc_ref):
    tq = acc_ref.shape[1]
    first = lax.broadcasted_iota(jnp.int32, (tq, LANES), 1) < FOX_DIM
    for p in range(N_PAIR):
        l_sel = jnp.where(first, l_ref[2 * p], l_ref[2 * p + 1])
        o_ref[:, LANES * p:LANES * (p + 1)] = (acc_ref[p] / l_sel).astype(o_ref.dtype)


def _fox_prompt_kernel(qx_ref, kx_ref, vb_ref, o_ref, qpad_ref, m_ref, l_ref, acc_ref):
    i = pl.program_id(1)
    j = pl.program_id(2)

    @pl.when(j == 0)
    def _():
        _fox_init(qx_ref[...], qpad_ref, m_ref, l_ref, acc_ref)

    @pl.when(j < i)
    def _():
        _fox_new_tile(kx_ref[...], vb_ref[...], qpad_ref, m_ref, l_ref, acc_ref, False)

    @pl.when(j == i)
    def _():
        _fox_new_tile(kx_ref[...], vb_ref[...], qpad_ref, m_ref, l_ref, acc_ref, True)
        _fox_finalize(o_ref, l_ref, acc_ref)


def _fox_scratch(tq):
    return [pltpu.VMEM((HEADS, tq, 2 * LANES), BF), pltpu.VMEM((HEADS, tq, 1), F32),
            pltpu.VMEM((HEADS, tq, 1), F32), pltpu.VMEM((N_PAIR, tq, LANES), F32)]


def _fox_prompt(qx, kx, vb, *, batch, seq, t):
    nq = seq // t
    return pl.pallas_call(
        _fox_prompt_kernel, grid=(batch, nq, nq),
        in_specs=[pl.BlockSpec((t, 2 * FOX_W), lambda b, i, j: (b * nq + i, 0)),
                  pl.BlockSpec((t, 2 * FOX_W), lambda b, i, j: (b * nq + jnp.minimum(i, j), 0)),
                  pl.BlockSpec((t, FOX_W), lambda b, i, j: (b * nq + jnp.minimum(i, j), 0))],
        out_specs=pl.BlockSpec((t, FOX_W), lambda b, i, j: (b * nq + i, 0)),
        out_shape=jax.ShapeDtypeStruct((batch * seq, FOX_W), BF),
        scratch_shapes=_fox_scratch(t),
        compiler_params=_params(3), name="fox_prompt")(qx, kx, vb)


def _fox_sample_kernel(qx_ref, kx_ref, vb_ref, kc_ref, vc_ref, kac_ref, o_ref,
                       qpad_ref, m_ref, l_ref, acc_ref, qc_ref, qa_ref, *, n_cache):
    j = pl.program_id(1)
    tq = qx_ref.shape[0]

    @pl.when(j == 0)
    def _():
        qx = qx_ref[...]
        _fox_init(qx, qpad_ref, m_ref, l_ref, acc_ref)
        lane = lax.broadcasted_iota(jnp.int32, (tq, 2 * LANES), 1)
        lane_a = lax.broadcasted_iota(jnp.int32, (tq, LANES), 1)
        qa = qx[:, LANES:2 * LANES]
        for h in range(HEADS):
            g = h // 4
            qg = jnp.concatenate([qx[:, 2 * LANES * (2 * g):2 * LANES * (2 * g) + LANES],
                                  qx[:, 2 * LANES * (2 * g + 1):2 * LANES * (2 * g + 1) + LANES]], axis=1)
            qc_ref[h] = jnp.where((lane // FOX_DIM) == (h % 4), qg, jnp.zeros_like(qg))
            qa_ref[h] = jnp.where((lane_a // AUG) == h, qa, jnp.zeros_like(qa))

    @pl.when(j < n_cache)
    def _():
        ka = kac_ref[0]
        for p in range(N_PAIR):
            g = p // 2
            kcg = kc_ref[0, :, 2 * LANES * g:2 * LANES * (g + 1)].astype(BF)
            vp = vc_ref[0, :, LANES * p:LANES * (p + 1)].astype(BF)
            alphas, pvs = [], []
            for hh in range(2):
                h = 2 * p + hh
                s = _dot_nt(qc_ref[h], kcg) + _dot_nt(qa_ref[h], ka)
                pb, alpha = _online_softmax(s, m_ref, l_ref, h)
                alphas.append(alpha)
                pvs.append(_dot(pb, vp))
            _fox_pair_update(p, alphas, pvs, acc_ref)

    @pl.when(j == n_cache)
    def _():
        _fox_new_tile(kx_ref[...], vb_ref[...], qpad_ref, m_ref, l_ref, acc_ref, True)
        _fox_finalize(o_ref, l_ref, acc_ref)


def _fox_sample(qx, kx, vb, kc, vc, kac, *, batch, seq, tkc):
    past = kc.shape[1]
    n_cache = past // tkc
    cache_map = lambda b, j: (b, jnp.minimum(j, n_cache - 1), 0)
    kern = functools.partial(_fox_sample_kernel, n_cache=n_cache)
    return pl.pallas_call(
        kern, grid=(batch, n_cache + 1),
        in_specs=[pl.BlockSpec((seq, 2 * FOX_W), lambda b, j: (b, 0)),
                  pl.BlockSpec((seq, 2 * FOX_W), lambda b, j: (b, 0)),
                  pl.BlockSpec((seq, FOX_W), lambda b, j: (b, 0)),
                  pl.BlockSpec((1, tkc, FOX_W), cache_map),
                  pl.BlockSpec((1, tkc, FOX_W), cache_map),
                  pl.BlockSpec((1, tkc, LANES), cache_map)],
        out_specs=pl.BlockSpec((seq, FOX_W), lambda b, j: (b, 0)),
        out_shape=jax.ShapeDtypeStruct((batch * seq, FOX_W), BF),
        scratch_shapes=_fox_scratch(seq) + [pltpu.VMEM((HEADS, seq, 2 * LANES), BF),
                                            pltpu.VMEM((HEADS, seq, LANES), BF)],
        compiler_params=_params(2), name="fox_sample")(qx, kx, vb, kc, vc, kac)


def _mla_init(qlat_ref, qrope, qcat_ref, m_ref, l_ref, acc_ref):
    tq = qrope.shape[0]
    lane = lax.broadcasted_iota(jnp.int32, (tq, HEADS * MLA_ROPE), 1)
    for h in range(HEADS):
        qcat_ref[h * tq:(h + 1) * tq, :KV_LORA] = qlat_ref[h]
        qcat_ref[h * tq:(h + 1) * tq, KV_LORA:] = jnp.where((lane // MLA_ROPE) == h, qrope, jnp.zeros_like(qrope))
    m_ref[...] = jnp.full(m_ref.shape, NEG, F32)
    l_ref[...] = jnp.zeros(l_ref.shape, F32)
    acc_ref[...] = jnp.zeros(acc_ref.shape, F32)


def _chunk_mask(shape, tq, q0, k0):
    assert tq & (tq - 1) == 0 and CHUNK & (CHUNK - 1) == 0
    shift = CHUNK.bit_length() - 1
    qpos = q0 + (lax.broadcasted_iota(jnp.int32, shape, 0) & (tq - 1))
    kpos = k0 + lax.broadcasted_iota(jnp.int32, shape, 1)
    return lax.shift_right_logical(kpos, shift) <= lax.shift_right_logical(qpos, shift)


def _mla_update(s, val, m_ref, l_ref, acc_ref):
    m_prev = m_ref[...]
    m_new = jnp.maximum(m_prev, jnp.max(s, axis=-1, keepdims=True))
    p = jnp.exp(s - m_new)
    alpha = jnp.exp(m_prev - m_new)
    l_ref[...] = alpha * l_ref[...] + jnp.sum(p, axis=-1, keepdims=True)
    m_ref[...] = m_new
    acc_ref[...] = acc_ref[...] * alpha + _dot(p.astype(BF), val)


def _mla_finalize(o_ref, wuv_ref, l_ref, acc_ref, tq):
    olat = (acc_ref[...] / l_ref[...]).astype(BF)
    for g in range(2):
        out = None
        for h in range(4 * g, 4 * g + 4):
            part = _dot(olat[h * tq:(h + 1) * tq, :], wuv_ref[h])
            out = part if out is None else out + part
        o_ref[:, 256 * g:256 * (g + 1)] = out.astype(o_ref.dtype)


def _mla_prompt_kernel(qlat_ref, qrope_ref, mkey_ref, wuv_ref, o_ref, qcat_ref, m_ref, l_ref, acc_ref, *, tq, tk):
    i = pl.program_id(1)
    j = pl.program_id(2)
    last_j = (tq * (i + 1) - 1) // tk
    needs_mask = tk * (j + 1) > tq * i + CHUNK

    @pl.when(j == 0)
    def _():
        _mla_init(qlat_ref, qrope_ref[...], qcat_ref, m_ref, l_ref, acc_ref)

    def tile(masked):
        mk = mkey_ref[...]
        s = _dot_nt(qcat_ref[...], mk)
        if masked:
            s = jnp.where(_chunk_mask(s.shape, tq, tq * i, tk * j), s, NEG)
        _mla_update(s, mk[:, :KV_LORA], m_ref, l_ref, acc_ref)

    @pl.when((j <= last_j) & jnp.logical_not(needs_mask))
    def _():
        tile(False)

    @pl.when((j <= last_j) & needs_mask)
    def _():
        tile(True)

    @pl.when(j == last_j)
    def _():
        _mla_finalize(o_ref, wuv_ref, l_ref, acc_ref, tq)


def _mla_scratch(tq):
    return [pltpu.VMEM((HEADS * tq, 2 * KV_LORA), BF), pltpu.VMEM((HEADS * tq, 1), F32),
            pltpu.VMEM((HEADS * tq, 1), F32), pltpu.VMEM((HEADS * tq, KV_LORA), F32)]


def _mla_prompt(qlat, qrope, mkey, wuvp, *, batch, seq, tq, tk):
    nq = seq // tq
    nk = seq // tk
    nkb = seq // tk

    def key_map(b, i, j):
        return (b * nkb + jnp.minimum(j, (tq * (i + 1) - 1) // tk), 0)

    kern = functools.partial(_mla_prompt_kernel, tq=tq, tk=tk)
    return pl.pallas_call(
        kern, grid=(batch, nq, nk),
        in_specs=[pl.BlockSpec((HEADS, tq, KV_LORA), lambda b, i, j: (0, b * nq + i, 0)),
                  pl.BlockSpec((tq, HEADS * MLA_ROPE), lambda b, i, j: (b * nq + i, 0)),
                  pl.BlockSpec((tk, 2 * KV_LORA), key_map),
                  _const_spec(wuvp.shape)],
        out_specs=pl.BlockSpec((tq, FOX_W), lambda b, i, j: (b * nq + i, 0)),
        out_shape=jax.ShapeDtypeStruct((batch * seq, FOX_W), BF),
        scratch_shapes=_mla_scratch(tq),
        compiler_params=_params(3), name="mla_prompt")(qlat, qrope, mkey, wuvp)


def _mla_sample_kernel(qlat_ref, qrope_ref, mkey_ref, latc_ref, krc_ref, rep_ref, wuv_ref, o_ref,
                       qcat_ref, m_ref, l_ref, acc_ref, *, n_cache, tq, tkc, past):
    j = pl.program_id(1)

    @pl.when(j == 0)
    def _():
        _mla_init(qlat_ref, qrope_ref[...], qcat_ref, m_ref, l_ref, acc_ref)

    @pl.when(j < n_cache)
    def _():
        c = latc_ref[0].astype(BF)
        kr8 = _dot(krc_ref[0].astype(BF), rep_ref[...]).astype(BF)
        s = _dot_nt(qcat_ref[:, :KV_LORA], c) + _dot_nt(qcat_ref[:, KV_LORA:], kr8)
        _mla_update(s, c, m_ref, l_ref, acc_ref)

    @pl.when(j == n_cache)
    def _():
        mk = mkey_ref[...]
        s = _dot_nt(qcat_ref[...], mk)
        if past % CHUNK != 0 or tq > CHUNK:
            s = jnp.where(_chunk_mask(s.shape, tq, past, past), s, NEG)
        _mla_update(s, mk[:, :KV_LORA], m_ref, l_ref, acc_ref)
        _mla_finalize(o_ref, wuv_ref, l_ref, acc_ref, tq)


def _mla_sample(qlat, qrope, mkey, latc, krc, rep32, wuvp, *, batch, seq, tkc):
    past = latc.shape[1]
    n_cache = past // tkc
    cache_map = lambda b, j: (b, jnp.minimum(j, n_cache - 1), 0)
    kern = functools.partial(_mla_sample_kernel, n_cache=n_cache, tq=seq, tkc=tkc, past=past)
    return pl.pallas_call(
        kern, grid=(batch, n_cache + 1),
        in_specs=[pl.BlockSpec((HEADS, seq, KV_LORA), lambda b, j: (0, b, 0)),
                  pl.BlockSpec((seq, HEADS * MLA_ROPE), lambda b, j: (b, 0)),
                  pl.BlockSpec((seq, 2 * KV_LORA), lambda b, j: (b, 0)),
                  pl.BlockSpec((1, tkc, KV_LORA), cache_map),
                  pl.BlockSpec((1, tkc, MLA_ROPE), cache_map),
                  _const_spec(rep32.shape), _const_spec(wuvp.shape)],
        out_specs=pl.BlockSpec((seq, FOX_W), lambda b, j: (b, 0)),
        out_shape=jax.ShapeDtypeStruct((batch * seq, FOX_W), BF),
        scratch_shapes=_mla_scratch(seq),
        compiler_params=_params(2), name="mla_sample")(qlat, qrope, mkey, latc, krc, rep32, wuvp)


def _ffn_kernel(x_ref, fox_ref, mla_ref, pconv_ref, wo_ref, gf_ref, wup_ref, cw_ref, wdn_ref, gfin_ref,
                y_ref, cst_ref, hn_ref, acc_ref, carry_ref, *, tm, seq_len):
    n_seg = max(1, tm // seq_len)
    seg = tm // n_seg
    mixed = jnp.concatenate([fox_ref[...], mla_ref[...]], axis=1)
    x1 = x_ref[...] + _dot(mixed, wo_ref[...])
    hn_ref[...] = (x1 * lax.rsqrt(jnp.mean(x1 * x1, axis=-1, keepdims=True) + EPS) * gf_ref[...]).astype(BF)
    acc_ref[...] = x1

    if n_seg == 1:
        tps = seq_len // tm
        first = pl.program_id(0) % tps == 0

        @pl.when(first)
        def _():
            carry_ref[...] = pconv_ref[0]

    srow = lax.broadcasted_iota(jnp.int32, (tm, FF_CHUNK), 0) % seg

    def conv(u, c):
        if n_seg == 1:
            prev = [carry_ref[c]]
        else:
            prev = [pconv_ref[s, c] for s in range(n_seg)]
        e1 = jnp.concatenate([jnp.broadcast_to(pv[1:2, :], (seg, FF_CHUNK)) for pv in prev], axis=0)
        e2 = jnp.concatenate([jnp.broadcast_to(pv[0:1, :], (seg, FF_CHUNK)) for pv in prev], axis=0)
        u1 = jnp.where(srow == 0, e1, pltpu.roll(u, 1, 0))
        u2 = jnp.where(srow == 0, e2, jnp.where(srow == 1, e1, pltpu.roll(u, 2, 0)))
        cw = cw_ref[c]
        out = cw[3:4, :] + cw[0:1, :] * u2 + cw[1:2, :] * u1 + cw[2:3, :] * u
        for s in range(n_seg):
            cst_ref[s, c] = u[seg * (s + 1) - 2:seg * (s + 1), :]
        if n_seg == 1:
            carry_ref[c] = u[tm - 2:tm, :]
        return out

    def body(c, _):
        hn = hn_ref[...]
        gate = conv(_dot(hn, wup_ref[c]), c)
        val = conv(_dot(hn, wup_ref[N_FF + c]), N_FF + c)
        act = (gate * jax.nn.sigmoid(gate) * val).astype(BF)
        acc_ref[...] += _dot(act, wdn_ref[c])
        return 0

    lax.fori_loop(0, N_FF, body, 0)
    x2 = acc_ref[...]
    y_ref[...] = x2 * lax.rsqrt(jnp.mean(x2 * x2, axis=-1, keepdims=True) + EPS) * gfin_ref[...]


def _ffn(x2d, fox, mla, pconv, wo, gf, wup, cw, wdn, gfin, *, seq_len, tm):
    n = x2d.shape[0]
    nt = n // tm
    n_seq = n // seq_len
    n_seg = max(1, tm // seq_len)
    if n_seg == 1:
        tps = seq_len // tm
        st_map = lambda i: (i // tps, 0, 0, 0)
    else:
        st_map = lambda i: (i, 0, 0, 0)
    row = lambda w: pl.BlockSpec((tm, w), lambda i: (i, 0))
    kern = functools.partial(_ffn_kernel, tm=tm, seq_len=seq_len)
    return pl.pallas_call(
        kern, grid=(nt,),
        in_specs=[row(D_MODEL), row(FOX_W), row(FOX_W),
                  pl.BlockSpec((n_seg, 2 * N_FF, 2, FF_CHUNK), st_map),
                  _const_spec(wo.shape), _const_spec(gf.shape), _const_spec(wup.shape), _const_spec(cw.shape),
                  _const_spec(wdn.shape), _const_spec(gfin.shape)],
        out_specs=(row(D_MODEL), pl.BlockSpec((n_seg, 2 * N_FF, 2, FF_CHUNK), st_map)),
        out_shape=(jax.ShapeDtypeStruct((n, D_MODEL), F32),
                   jax.ShapeDtypeStruct((n_seq, 2 * N_FF, 2, FF_CHUNK), F32)),
        scratch_shapes=[pltpu.VMEM((tm, D_MODEL), BF), pltpu.VMEM((tm, D_MODEL), F32),
                        pltpu.VMEM((2 * N_FF, 2, FF_CHUNK), F32)],
        compiler_params=_params(1), name="ffn")(x2d, fox, mla, pconv, wo, gf, wup, cw, wdn, gfin)


def _rope_tables(pos):
    half = MLA_ROPE // 2
    inv = ROPE_THETA ** (-jnp.arange(half, dtype=F32) / half)
    ang = pos.astype(F32)[:, None] * inv[None, :]
    cos = jnp.tile(jnp.cos(ang), (1, 2 * HEADS))
    sin = jnp.tile(jnp.sin(ang), (1, 2 * HEADS))
    return cos, sin


def _placement():
    place = np.zeros((LANES, 2 * N_PAIR * LANES), np.float32)
    ones = np.zeros((1, 2 * N_PAIR * LANES), np.float32)
    placek = np.zeros((3, HEADS, LANES), np.float32)
    onesk = np.zeros((1, LANES), np.float32)
    for h in range(HEADS):
        for i in range(3):
            onesk[0, AUG * h + i] = 1.0
            placek[i, h, AUG * h + 3 + i] = -1.0
            for p in range(N_PAIR):
                place[HEADS * i + h, LANES * p + AUG * h + i] = 1.0
                ones[0, LANES * p + AUG * h + 3 + i] = 1.0
                ones[0, N_PAIR * LANES + LANES * p + AUG * h + i] = 1.0
                place[HEADS * i + h, N_PAIR * LANES + LANES * p + AUG * h + 3 + i] = -1.0
    rep = np.zeros((LANES, HEADS * MLA_ROPE), np.float32)
    rep32 = np.zeros((MLA_ROPE, HEADS * MLA_ROPE), np.float32)
    for h in range(HEADS):
        for r in range(MLA_ROPE):
            rep[r, MLA_ROPE * h + r] = 1.0
            rep32[r, MLA_ROPE * h + r] = 1.0
    return (jnp.asarray(place, BF), jnp.asarray(ones, F32), jnp.asarray(placek, BF), jnp.asarray(onesk, F32),
            jnp.asarray(rep, BF), jnp.asarray(rep32, BF))


def _rot_cols(w):
    half = MLA_ROPE // 2
    return jnp.concatenate([-w[..., half:], w[..., :half]], axis=-1)


def _pad_cols(w, width):
    return jnp.pad(w, ((0, 0), (0, width - w.shape[1])))


def _layer_weights(w_in, b_f, w_q_up, w_uk, w_uv, w_out, w_up, conv_w, conv_b, w_down):
    wq, wk, wv, wf, wqc, wckv, wkr = jnp.split(w_in, IN_SPLITS, axis=1)
    wa = jnp.concatenate([wq, wk, wv, wqc, wckv, _pad_cols(wkr, LANES), _pad_cols(_rot_cols(wkr), LANES),
                          _pad_cols(wf, LANES)], axis=1).astype(BF)
    bfp = _pad_cols(b_f[None, :], LANES).astype(F32)
    wq3 = w_q_up.reshape(Q_LORA, HEADS, MLA_NOPE + MLA_ROPE)
    wq_nope = wq3[:, :, :MLA_NOPE].reshape(Q_LORA, HEADS * MLA_NOPE)
    wq_rope = wq3[:, :, MLA_NOPE:]
    wq2 = jnp.concatenate([wq_nope, wq_rope.reshape(Q_LORA, -1), _rot_cols(wq_rope).reshape(Q_LORA, -1)],
                          axis=1).astype(BF)
    wukp = jnp.zeros((HEADS, 4 * MLA_NOPE, KV_LORA), F32)
    wuvp = jnp.zeros((HEADS, KV_LORA, 4 * MLA_VDIM), F32)
    for h in range(HEADS):
        o = MLA_NOPE * (h % 4)
        wukp = wukp.at[h, o:o + MLA_NOPE, :].set(w_uk[:, h, :].T)
        wuvp = wuvp.at[h, :, o:o + MLA_VDIM].set(w_uv[:, h, :])
    wup = w_up.reshape(D_MODEL, 2 * N_FF, FF_CHUNK).transpose(1, 0, 2).astype(BF)
    cw = jnp.concatenate([conv_w, conv_b[None, :], jnp.zeros((4, 2 * D_FF), F32)], axis=0)
    cw = cw.reshape(8, 2 * N_FF, FF_CHUNK).transpose(1, 0, 2)
    wdn = w_down.reshape(N_FF, FF_CHUNK, D_MODEL).astype(BF)
    return wa, bfp, wq2, wukp.astype(BF), wuvp.astype(BF), w_out.astype(BF), wup, cw, wdn


def _state_to_chunks(st):
    b = st.shape[0]
    return st.reshape(b, CONV_W - 1, 2 * N_FF, FF_CHUNK).transpose(0, 2, 1, 3)


def _chunks_to_state(st):
    b = st.shape[0]
    return st.transpose(0, 2, 1, 3).reshape(b, CONV_W - 1, 2 * D_FF)


def kernel(x_prompt, x_sample, cache_fox_k, cache_fox_v, cache_fox_logf, cache_mla_latent, cache_mla_krope,
           state_ffn_conv, attn_norm, w_in, b_forget, q_norm, w_q_up, kv_norm, w_uk, w_uv, w_out, ffn_norm,
           w_up, conv_w, conv_b, w_down, final_norm):
    assert attn_norm.shape[0] == 1, "single-layer stack"
    bp, tp, _ = x_prompt.shape
    bs, ts, _ = x_sample.shape
    past = cache_fox_k.shape[2]
    assert ts == CHUNK and past % CHUNK == 0

    wa, bfp, wq2, wukp, wuvp, wo, wup, cw, wdn = _layer_weights(
        w_in[0], b_forget[0], w_q_up[0], w_uk[0], w_uv[0], w_out[0], w_up[0], conv_w[0], conv_b[0], w_down[0])
    place, ones, placek, onesk, rep, rep32 = _placement()
    g_attn = attn_norm[0][None, :]
    g_q = q_norm[0][None, :]
    g_kv = kv_norm[0][None, :]
    g_ffn = ffn_norm[0][None, :]
    g_fin = final_norm[None, :]

    tm = 512
    outs = []
    for grp in ("prompt", "sample"):
        if grp == "prompt":
            x2d = x_prompt.reshape(bp * tp, D_MODEL)
            b, t = bp, tp
            cos, sin = _rope_tables(jnp.arange(tp, dtype=jnp.int32))
            off = jnp.zeros((bp * tp, LANES), F32)
            pconv = jnp.zeros((bp, 2 * N_FF, 2, FF_CHUNK), F32)
        else:
            x2d = x_sample.reshape(bs * ts, D_MODEL)
            b, t = bs, ts
            cos, sin = _rope_tables(past + jnp.arange(ts, dtype=jnp.int32))
            cos, sin = jnp.tile(cos, (tm // ts, 1)), jnp.tile(sin, (tm // ts, 1))
            kac, tot = _fox_prep(cache_fox_logf[0], placek, onesk, seq_out=ts)
            off = _pad_cols(tot.reshape(bs * ts, HEADS), LANES)
            pconv = _state_to_chunks(state_ffn_conv[0])

        (k, v, logf, lat, kr, qx, kx, vb, mkey, qlat, qrope) = _proj(
            x2d, off, cos, sin, g_attn, wa, bfp, g_q, g_kv, wq2, wukp, place, ones, rep, seq_len=t, tm=tm)

        if grp == "prompt":
            fox = _fox_prompt(qx, kx, vb, batch=b, seq=t, t=512)
            mla = _mla_prompt(qlat, qrope, mkey, wuvp, batch=b, seq=t, tq=128, tk=512)
        else:
            fox = _fox_sample(qx, kx, vb, cache_fox_k[0].reshape(bs, past, FOX_W),
                              cache_fox_v[0].reshape(bs, past, FOX_W), kac, batch=b, seq=t, tkc=512)
            mla = _mla_sample(qlat, qrope, mkey, cache_mla_latent[0], cache_mla_krope[0], rep32, wuvp,
                              batch=b, seq=t, tkc=512)

        y, cst = _ffn(x2d, fox, mla, pconv, wo, g_ffn, wup, cw, wdn, g_fin, seq_len=t, tm=tm)
        outs.append((y.reshape(b, t, D_MODEL),
                     k.reshape(1, b, t, HEADS, FOX_DIM), v.reshape(1, b, t, HEADS, FOX_DIM),
                     logf.reshape(1, b, t, HEADS), lat.reshape(1, b, t, KV_LORA), kr.reshape(1, b, t, MLA_ROPE),
                     _chunks_to_state(cst)[None]))
    (yp, *st_p), (ys, *st_s) = outs
    return (yp, ys, *st_p, *st_s)
```

```python
import functools

import numpy as np
import jax
import jax.numpy as jnp
from jax import lax
from jax.experimental import pallas as pl
from jax.experimental.pallas import tpu as pltpu

D_MODEL = 1024
CHUNK = 64
HEADS = 8
FOX_DIM = 64
MLA_NOPE = 64
MLA_ROPE = 32
MLA_VDIM = 64
Q_LORA = 384
KV_LORA = 256
D_FF = 2816
CONV_W = 3
ROPE_THETA = 10000.0
EPS = 1e-6
NEG = -1e30
LOG2E = 1.4426950408889634

FOX_W = HEADS * FOX_DIM
IN_SIZES = [FOX_W, FOX_W, FOX_W, HEADS, Q_LORA, KV_LORA, MLA_ROPE]
IN_SPLITS = [int(s) for s in np.cumsum(IN_SIZES)[:-1]]

LANES = 128
AUG = 16
N_PAIR = HEADS // 2
FF_CHUNK = 256
N_FF = D_FF // FF_CHUNK
VMEM_LIMIT = 56 * 1024 * 1024

C_Q, C_K, C_V = 0, 512, 1024
C_QC = 1536
C_CKV = 1920
C_KR = 2176
C_KRR = 2304
C_F = 2432
N_A = 2560

BF = jnp.bfloat16
F32 = jnp.float32


def _dot(a, b):
    return jnp.dot(a, b, preferred_element_type=F32)


def _dot_nt(a, b):
    return lax.dot_general(a, b, (((1,), (1,)), ((), ())), preferred_element_type=F32)


def _split3(x):
    hi = x.astype(BF)
    r1 = x - hi.astype(F32)
    mid = r1.astype(BF)
    lo = (r1 - mid.astype(F32)).astype(BF)
    return hi, mid, lo


def _log_sigmoid(x):
    return jnp.minimum(x, 0.0) - jnp.log1p(jnp.exp(-jnp.abs(x)))


def _lanes(x, n):
    if n % LANES == 0:
        return x if n == LANES else pltpu.repeat(x, n // LANES, 1)
    assert n < LANES
    return x[:, :n]


def _params(n_axes):
    return pltpu.CompilerParams(dimension_semantics=("arbitrary",) * n_axes, vmem_limit_bytes=VMEM_LIMIT)


def _const_spec(shape):
    nd = len(shape)
    return pl.BlockSpec(shape, lambda *_: (0,) * nd, pipeline_mode=pl.Buffered(1))


def _proj_kernel(x_ref, cos_ref, sin_ref, g_ref, wa_ref, bf_ref, gq_ref, gkv_ref, wq2_ref,
                 wuk_ref, place_ref, ones_ref, rep_ref,
                 k_ref, v_ref, logf_ref, lat_ref, kr_ref, qx_ref, kx_ref, vb_ref, mkey_ref, qlat_ref,
                 qrope_ref, fqs_ref, carry_ref, *, tm, seq_len, mla_scale):
    x = x_ref[...]
    ms = jnp.mean(x * x, axis=-1, keepdims=True)
    hn = (x * lax.rsqrt(ms + EPS) * g_ref[...]).astype(BF)
    z = _dot(hn, wa_ref[...])

    zk = z[:, C_K:C_K + FOX_W]
    zv = z[:, C_V:C_V + FOX_W]
    k_ref[...] = zk
    v_ref[...] = zv
    vb_ref[...] = zv.astype(BF)

    lane = lax.broadcasted_iota(jnp.int32, (tm, LANES), 1)
    logf = _log_sigmoid(z[:, C_F:C_F + LANES] + bf_ref[...])
    logf = jnp.where(lane < HEADS, logf, 0.0)
    logf_ref[...] = logf[:, :HEADS]
    row = lax.broadcasted_iota(jnp.int32, (tm, tm), 0)
    col = lax.broadcasted_iota(jnp.int32, (tm, tm), 1)
    keep = col <= row
    if seq_len < tm:
        keep = keep & ((col // seq_len) == (row // seq_len))
    tri = jnp.where(keep, 1.0, 0.0).astype(BF)
    hi, mid, lo = _split3(logf)
    fcum = _dot(tri, hi) + _dot(tri, mid) + _dot(tri, lo)
    if seq_len > tm:
        @pl.when(pl.program_id(0) % (seq_len // tm) == 0)
        def _():
            carry_ref[...] = jnp.zeros_like(carry_ref)
        fcum = fcum + carry_ref[0:1, :]
        carry_ref[0:1, :] = fcum[tm - 1:tm, :]

    fqs = fcum * LOG2E
    fqs_ref[...] = fqs
    hi, mid, lo = _split3(fqs)
    comb = hi.astype(F32) + pltpu.roll(mid.astype(F32), HEADS, 1) + pltpu.roll(lo.astype(F32), 2 * HEADS, 1)
    aug = _dot(comb.astype(BF), place_ref[...]) + ones_ref[...]
    zq = z[:, C_Q:C_Q + FOX_W] * (FOX_DIM ** -0.5 * LOG2E)
    for p in range(N_PAIR):
        lo_, hi_ = 2 * LANES * p, 2 * LANES * p + LANES
        qx_ref[:, lo_:hi_] = zq[:, LANES * p:LANES * (p + 1)].astype(BF)
        qx_ref[:, hi_:hi_ + LANES] = aug[:, LANES * p:LANES * (p + 1)].astype(BF)
        kx_ref[:, lo_:hi_] = zk[:, LANES * p:LANES * (p + 1)].astype(BF)
        kx_ref[:, hi_:hi_ + LANES] = aug[:, FOX_W + LANES * p:FOX_W + LANES * (p + 1)].astype(BF)

    zc = z[:, C_CKV:C_CKV + KV_LORA]
    ckv = zc * lax.rsqrt(jnp.mean(zc * zc, axis=-1, keepdims=True) + EPS) * gkv_ref[...]
    lat_ref[...] = ckv
    cos = cos_ref[...]
    sin = sin_ref[...]
    krb = z[:, C_KR:C_KR + LANES] * cos[:, :LANES] + z[:, C_KRR:C_KRR + LANES] * sin[:, :LANES]
    kr_ref[...] = krb[:, :MLA_ROPE]
    mkey_ref[:, :KV_LORA] = ckv.astype(BF)
    mkey_ref[:, KV_LORA:] = _dot(krb.astype(BF), rep_ref[...]).astype(BF)

    zqc = z[:, C_QC:C_QC + Q_LORA]
    qcn = (zqc * lax.rsqrt(jnp.mean(zqc * zqc, axis=-1, keepdims=True) + EPS) * gq_ref[...]).astype(BF)
    zq2 = _dot(qcn, wq2_ref[...])
    nr = HEADS * MLA_ROPE
    qr = zq2[:, FOX_W:FOX_W + nr] * cos + zq2[:, FOX_W + nr:FOX_W + 2 * nr] * sin
    qrope_ref[...] = (qr * mla_scale).astype(BF)
    for h in range(HEADS):
        g = h // 4
        qn = zq2[:, 256 * g:256 * (g + 1)].astype(BF)
        qlat_ref[h] = (_dot(qn, wuk_ref[h]) * mla_scale).astype(BF)


def _proj(x2d, cos, sin, g, wa, bfp, gq, gkv, wq2, wukp, place, ones, rep, *, seq_len, tm):
    n = x2d.shape[0]
    nt = n // tm
    if seq_len >= tm:
        tps = seq_len // tm
        tab_map = lambda i: (i % tps, 0)
    else:
        tab_map = lambda i: (0, 0)
    row = lambda w: pl.BlockSpec((tm, w), lambda i: (i, 0))
    out_shape = (
        jax.ShapeDtypeStruct((n, FOX_W), F32),
        jax.ShapeDtypeStruct((n, FOX_W), F32),
        jax.ShapeDtypeStruct((n, HEADS), F32),
        jax.ShapeDtypeStruct((n, KV_LORA), F32),
        jax.ShapeDtypeStruct((n, MLA_ROPE), F32),
        jax.ShapeDtypeStruct((n, 2 * FOX_W), BF),
        jax.ShapeDtypeStruct((n, 2 * FOX_W), BF),
        jax.ShapeDtypeStruct((n, FOX_W), BF),
        jax.ShapeDtypeStruct((n, 2 * KV_LORA), BF),
        jax.ShapeDtypeStruct((HEADS, n, KV_LORA), BF),
        jax.ShapeDtypeStruct((n, HEADS * MLA_ROPE), BF),
        jax.ShapeDtypeStruct((n, LANES), F32),
    )
    out_specs = (row(FOX_W), row(FOX_W), row(HEADS), row(KV_LORA), row(MLA_ROPE), row(2 * FOX_W),
                 row(2 * FOX_W), row(FOX_W), row(2 * KV_LORA),
                 pl.BlockSpec((HEADS, tm, KV_LORA), lambda i: (0, i, 0)), row(HEADS * MLA_ROPE), row(LANES))
    in_specs = [row(D_MODEL),
                pl.BlockSpec((tm, 2 * LANES), tab_map), pl.BlockSpec((tm, 2 * LANES), tab_map),
                _const_spec(g.shape), _const_spec(wa.shape), _const_spec(bfp.shape), _const_spec(gq.shape),
                _const_spec(gkv.shape), _const_spec(wq2.shape), _const_spec(wukp.shape),
                _const_spec(place.shape), _const_spec(ones.shape), _const_spec(rep.shape)]
    kern = functools.partial(_proj_kernel, tm=tm, seq_len=seq_len,
                             mla_scale=float((MLA_NOPE + MLA_ROPE) ** -0.5 * LOG2E))
    return pl.pallas_call(
        kern, grid=(nt,), in_specs=in_specs, out_specs=out_specs, out_shape=out_shape,
        scratch_shapes=[pltpu.VMEM((8, LANES), F32)],
        compiler_params=_params(1), name="proj")(
            x2d, cos, sin, g, wa, bfp, gq, gkv, wq2, wukp, place, ones, rep)


def _fox_prep_kernel(plt_ref, g_ref):
    x = plt_ref[0]
    n = x.shape[1]
    lane = lax.broadcasted_iota(jnp.int32, x.shape, 1)
    shift = 1
    while shift < n:
        x = x + jnp.where(lane >= shift, pltpu.roll(x, shift, 1), 0.0)
        shift *= 2
    g_ref[0] = (x - x[:, n - 1:n]) * LOG2E


def _fox_prep(plt):
    b, h, p = plt.shape
    return pl.pallas_call(
        _fox_prep_kernel, grid=(b,),
        in_specs=[pl.BlockSpec((1, h, p), lambda i: (i, 0, 0))],
        out_specs=pl.BlockSpec((1, h, p), lambda i: (i, 0, 0)),
        out_shape=jax.ShapeDtypeStruct((b, h, p), F32),
        compiler_params=_params(1), name="fox_prep")(plt)


def _softmax_step(s, m_prev):
    m_new = jnp.maximum(m_prev, jnp.max(s, axis=-1, keepdims=True))
    p = jnp.exp2(s - _lanes(m_new, s.shape[1]))
    alpha = jnp.exp2(m_prev - m_new)
    return p.astype(BF), alpha, m_new


def _fox_qpad(qx, h):
    p, hh = divmod(h, 2)
    lane = lax.broadcasted_iota(jnp.int32, (qx.shape[0], 2 * LANES), 1)
    blk = qx[:, 2 * LANES * p:2 * LANES * (p + 1)]
    keep = ((lane >= FOX_DIM * hh) & (lane < FOX_DIM * (hh + 1))) | (
        (lane >= LANES + AUG * h) & (lane < LANES + AUG * (h + 1)))
    return jnp.where(keep, blk, jnp.zeros_like(blk))


def _fox_prompt_tile(kx, vb, qpad_ref, m_ref, l_ref, acc_ref, masked):
    tq = qpad_ref.shape[1]
    tk = kx.shape[0]
    first = lax.broadcasted_iota(jnp.int32, (tq, LANES), 1) < FOX_DIM
    ones = jnp.ones((tk, LANES), BF)
    if masked:
        causal = lax.broadcasted_iota(jnp.int32, (tq, tk), 1) <= lax.broadcasted_iota(jnp.int32, (tq, tk), 0)
    for p in range(N_PAIR):
        kxp = kx[:, 2 * LANES * p:2 * LANES * (p + 1)]
        vpa = jnp.concatenate([vb[:, LANES * p:LANES * (p + 1)], ones], axis=1)
        alphas, pvs = [], []
        for hh in range(2):
            h = 2 * p + hh
            s = _dot_nt(qpad_ref[h], kxp)
            if masked:
                s = jnp.where(causal, s, NEG)
            pb, alpha, m_new = _softmax_step(s, m_ref[h])
            pv = _dot(pb, vpa)
            l_ref[h] = alpha * l_ref[h] + pv[:, LANES:]
            m_ref[h] = m_new
            alphas.append(alpha)
            pvs.append(pv[:, :LANES])
        acc_ref[p] = acc_ref[p] * jnp.where(first, alphas[0], alphas[1]) + jnp.where(first, pvs[0], pvs[1])


def _fox_prompt_kernel(qx_ref, kx_ref, vb_ref, o_ref, qpad_ref, m_ref, l_ref, acc_ref):
    i = pl.program_id(1)
    j = pl.program_id(2)

    @pl.when(j == 0)
    def _():
        qx = qx_ref[...]
        for h in range(HEADS):
            qpad_ref[h] = _fox_qpad(qx, h)
        m_ref[...] = jnp.full(m_ref.shape, NEG, F32)
        l_ref[...] = jnp.zeros(l_ref.shape, F32)
        acc_ref[...] = jnp.zeros(acc_ref.shape, F32)

    @pl.when(j < i)
    def _():
        _fox_prompt_tile(kx_ref[...], vb_ref[...], qpad_ref, m_ref, l_ref, acc_ref, False)

    @pl.when(j == i)
    def _():
        _fox_prompt_tile(kx_ref[...], vb_ref[...], qpad_ref, m_ref, l_ref, acc_ref, True)
        first = lax.broadcasted_iota(jnp.int32, (qpad_ref.shape[1], LANES), 1) < FOX_DIM
        for p in range(N_PAIR):
            l_sel = jnp.where(first, l_ref[2 * p], l_ref[2 * p + 1])
            o_ref[:, LANES * p:LANES * (p + 1)] = (acc_ref[p] / l_sel).astype(o_ref.dtype)


def _fox_prompt(qx, kx, vb, *, batch, seq, t):
    nq = seq // t
    return pl.pallas_call(
        _fox_prompt_kernel, grid=(batch, nq, nq),
        in_specs=[pl.BlockSpec((t, 2 * FOX_W), lambda b, i, j: (b * nq + i, 0)),
                  pl.BlockSpec((t, 2 * FOX_W), lambda b, i, j: (b * nq + jnp.minimum(i, j), 0)),
                  pl.BlockSpec((t, FOX_W), lambda b, i, j: (b * nq + jnp.minimum(i, j), 0))],
        out_specs=pl.BlockSpec((t, FOX_W), lambda b, i, j: (b * nq + i, 0)),
        out_shape=jax.ShapeDtypeStruct((batch * seq, FOX_W), BF),
        scratch_shapes=[pltpu.VMEM((HEADS, t, 2 * LANES), BF), pltpu.VMEM((HEADS, t, LANES), F32),
                        pltpu.VMEM((HEADS, t, LANES), F32), pltpu.VMEM((N_PAIR, t, LANES), F32)],
        compiler_params=_params(3), name="fox_prompt")(qx, kx, vb)


def _fox_sample_kernel(qx_ref, kx_ref, vb_ref, fqs_ref, kct_ref, vct_ref, gk_ref, o_ref,
                       qpad_ref, qh_ref, fq_ref, m_ref, acc_ref, *, n_cache):
    j = pl.program_id(1)
    tq = qx_ref.shape[0]
    first = lax.broadcasted_iota(jnp.int32, (tq, LANES), 1) < FOX_DIM

    @pl.when(j == 0)
    def _():
        qx = qx_ref[...]
        fqs = fqs_ref[...]
        for h in range(HEADS):
            p, hh = divmod(h, 2)
            qpad_ref[h] = _fox_qpad(qx, h)
            qh_ref[h] = qx[:, 2 * LANES * p + FOX_DIM * hh:2 * LANES * p + FOX_DIM * (hh + 1)]
            fq_ref[h] = jnp.broadcast_to(fqs[:, h:h + 1], (tq, LANES))
        m_ref[...] = jnp.full(m_ref.shape, NEG, F32)
        acc_ref[...] = jnp.zeros(acc_ref.shape, F32)

    @pl.when(j < n_cache)
    def _():
        tk = kct_ref.shape[3]
        ones = jnp.ones((FOX_DIM, tk), BF)
        for h in range(HEADS):
            s = _dot(qh_ref[h], kct_ref[0, h].astype(BF))
            s = s + _lanes(fq_ref[h], tk) - gk_ref[0, h:h + 1, :]
            pb, alpha, m_new = _softmax_step(s, m_ref[h])
            vta = jnp.concatenate([vct_ref[0, h].astype(BF), ones], axis=0)
            acc_ref[h] = acc_ref[h] * alpha + _dot_nt(pb, vta)
            m_ref[h] = m_new

    @pl.when(j == n_cache)
    def _():
        kx = kx_ref[...]
        vb = vb_ref[...]
        tk = kx.shape[0]
        ones = jnp.ones((tk, LANES), BF)
        causal = lax.broadcasted_iota(jnp.int32, (tq, tk), 1) <= lax.broadcasted_iota(jnp.int32, (tq, tk), 0)
        outs = []
        for h in range(HEADS):
            p, hh = divmod(h, 2)
            s = _dot_nt(qpad_ref[h], kx[:, 2 * LANES * p:2 * LANES * (p + 1)])
            s = jnp.where(causal, s, NEG)
            pb, alpha, _ = _softmax_step(s, m_ref[h])
            pv = _dot(pb, jnp.concatenate([vb[:, LANES * p:LANES * (p + 1)], ones], axis=1))
            o2 = pv[:, :LANES]
            if hh == 1:
                o2 = pltpu.roll(o2, FOX_DIM, 1)
            acc = acc_ref[h] * alpha + jnp.where(first, o2, pv[:, LANES:])
            outs.append((acc / pltpu.roll(acc, FOX_DIM, 1))[:, :FOX_DIM])
        o_ref[...] = jnp.concatenate(outs, axis=1).astype(o_ref.dtype)


def _fox_sample(qx, kx, vb, fqs, kct, vct, gk, *, batch, seq, tkc):
    past = kct.shape[3]
    n_cache = past // tkc
    kern = functools.partial(_fox_sample_kernel, n_cache=n_cache)
    row = lambda w: pl.BlockSpec((seq, w), lambda b, j: (b, 0))
    return pl.pallas_call(
        kern, grid=(batch, n_cache + 1),
        in_specs=[row(2 * FOX_W), row(2 * FOX_W), row(FOX_W), row(LANES),
                  pl.BlockSpec((1, HEADS, FOX_DIM, tkc), lambda b, j: (b, 0, 0, jnp.minimum(j, n_cache - 1))),
                  pl.BlockSpec((1, HEADS, FOX_DIM, tkc), lambda b, j: (b, 0, 0, jnp.minimum(j, n_cache - 1))),
                  pl.BlockSpec((1, HEADS, tkc), lambda b, j: (b, 0, jnp.minimum(j, n_cache - 1)))],
        out_specs=row(FOX_W),
        out_shape=jax.ShapeDtypeStruct((batch * seq, FOX_W), BF),
        scratch_shapes=[pltpu.VMEM((HEADS, seq, 2 * LANES), BF), pltpu.VMEM((HEADS, seq, FOX_DIM), BF),
                        pltpu.VMEM((HEADS, seq, LANES), F32), pltpu.VMEM((HEADS, seq, LANES), F32),
                        pltpu.VMEM((HEADS, seq, LANES), F32)],
        compiler_params=_params(2), name="fox_sample")(qx, kx, vb, fqs, kct, vct, gk)


def _chunk_mask(shape, tq, row0, q0, k0):
    assert tq & (tq - 1) == 0 and CHUNK & (CHUNK - 1) == 0
    shift = CHUNK.bit_length() - 1
    qpos = q0 + ((row0 + lax.broadcasted_iota(jnp.int32, shape, 0)) & (tq - 1))
    kpos = k0 + lax.broadcasted_iota(jnp.int32, shape, 1)
    return lax.shift_right_logical(kpos, shift) <= lax.shift_right_logical(qpos, shift)


def _mla_rows_update(s, val, m_ref, l_ref, acc_ref, r0, r1):
    m_prev = m_ref[r0:r1]
    m_new = jnp.maximum(m_prev, jnp.max(s, axis=-1, keepdims=True))
    p = jnp.exp2(s - _lanes(m_new, s.shape[1]))
    alpha = jnp.exp2(m_prev - m_new)
    l_ref[r0:r1] = alpha * l_ref[r0:r1] + jnp.sum(p, axis=-1, keepdims=True)
    m_ref[r0:r1] = m_new
    acc_ref[r0:r1] = acc_ref[r0:r1] * _lanes(alpha, KV_LORA) + _dot(p.astype(BF), val)


def _mla_init_stats(m_ref, l_ref, acc_ref):
    m_ref[...] = jnp.full(m_ref.shape, NEG, F32)
    l_ref[...] = jnp.zeros(l_ref.shape, F32)
    acc_ref[...] = jnp.zeros(acc_ref.shape, F32)


def _mla_finalize(o_ref, wuv_ref, l_ref, acc_ref, tq):
    olat = (acc_ref[...] / _lanes(l_ref[...], KV_LORA)).astype(BF)
    for g in range(2):
        out = None
        for h in range(4 * g, 4 * g + 4):
            part = _dot(olat[h * tq:(h + 1) * tq, :], wuv_ref[h])
            out = part if out is None else out + part
        o_ref[:, 256 * g:256 * (g + 1)] = out.astype(o_ref.dtype)


def _mla_prompt_kernel(qlat_ref, qrope_ref, mkey_ref, wuv_ref, o_ref, qcat_ref, m_ref, l_ref, acc_ref,
                       *, tq, tk, rc):
    i = pl.program_id(1)
    j = pl.program_id(2)
    last_j = (tq * (i + 1) - 1) // tk
    needs_mask = tk * (j + 1) > tq * i + CHUNK

    @pl.when(j == 0)
    def _():
        qrope = qrope_ref[...]
        lane = lax.broadcasted_iota(jnp.int32, (tq, HEADS * MLA_ROPE), 1)
        for h in range(HEADS):
            qcat_ref[h * tq:(h + 1) * tq, :KV_LORA] = qlat_ref[h]
            qcat_ref[h * tq:(h + 1) * tq, KV_LORA:] = jnp.where((lane // MLA_ROPE) == h, qrope,
                                                                jnp.zeros_like(qrope))
        _mla_init_stats(m_ref, l_ref, acc_ref)

    def tile(masked):
        mk = mkey_ref[...]
        for r0 in range(0, HEADS * tq, rc):
            s = _dot_nt(qcat_ref[r0:r0 + rc], mk)
            if masked:
                s = jnp.where(_chunk_mask(s.shape, tq, r0, tq * i, tk * j), s, NEG)
            _mla_rows_update(s, mk[:, :KV_LORA], m_ref, l_ref, acc_ref, r0, r0 + rc)

    @pl.when((j <= last_j) & jnp.logical_not(needs_mask))
    def _():
        tile(False)

    @pl.when((j <= last_j) & needs_mask)
    def _():
        tile(True)

    @pl.when(j == last_j)
    def _():
        _mla_finalize(o_ref, wuv_ref, l_ref, acc_ref, tq)


def _mla_scratch(rows):
    return [pltpu.VMEM((rows, LANES), F32), pltpu.VMEM((rows, LANES), F32), pltpu.VMEM((rows, KV_LORA), F32)]


def _mla_prompt(qlat, qrope, mkey, wuvp, *, batch, seq, tq, tk, rc):
    nq = seq // tq
    nk = seq // tk

    def key_map(b, i, j):
        return (b * nk + jnp.minimum(j, (tq * (i + 1) - 1) // tk), 0)

    kern = functools.partial(_mla_prompt_kernel, tq=tq, tk=tk, rc=rc)
    return pl.pallas_call(
        kern, grid=(batch, nq, nk),
        in_specs=[pl.BlockSpec((HEADS, tq, KV_LORA), lambda b, i, j: (0, b * nq + i, 0)),
                  pl.BlockSpec((tq, HEADS * MLA_ROPE), lambda b, i, j: (b * nq + i, 0)),
                  pl.BlockSpec((tk, 2 * KV_LORA), key_map),
                  _const_spec(wuvp.shape)],
        out_specs=pl.BlockSpec((tq, FOX_W), lambda b, i, j: (b * nq + i, 0)),
        out_shape=jax.ShapeDtypeStruct((batch * seq, FOX_W), BF),
        scratch_shapes=[pltpu.VMEM((HEADS * tq, 2 * KV_LORA), BF)] + _mla_scratch(HEADS * tq),
        compiler_params=_params(3), name="mla_prompt")(qlat, qrope, mkey, wuvp)


def _mla_sample_kernel(qlat_ref, qrope_ref, mkey_ref, latc_ref, krt_ref, wuv_ref, o_ref,
                       ql_ref, qr_ref, m_ref, l_ref, acc_ref, *, n_cache, tq, past):
    j = pl.program_id(1)
    rows = HEADS * tq

    @pl.when(j == 0)
    def _():
        qrope = qrope_ref[...]
        for h in range(HEADS):
            ql_ref[h * tq:(h + 1) * tq, :] = qlat_ref[h]
            qr_ref[h * tq:(h + 1) * tq, :] = qrope[:, MLA_ROPE * h:MLA_ROPE * (h + 1)]
        _mla_init_stats(m_ref, l_ref, acc_ref)

    @pl.when(j < n_cache)
    def _():
        c = latc_ref[0].astype(BF)
        s = _dot_nt(ql_ref[...], c) + _dot(qr_ref[...], krt_ref[0].astype(BF))
        _mla_rows_update(s, c, m_ref, l_ref, acc_ref, 0, rows)

    @pl.when(j == n_cache)
    def _():
        mk = mkey_ref[...]
        s = _dot_nt(ql_ref[...], mk[:, :KV_LORA]) + _dot_nt(qr_ref[...], mk[:, KV_LORA:KV_LORA + MLA_ROPE])
        if past % CHUNK != 0 or tq > CHUNK:
            s = jnp.where(_chunk_mask(s.shape, tq, 0, past, past), s, NEG)
        _mla_rows_update(s, mk[:, :KV_LORA], m_ref, l_ref, acc_ref, 0, rows)
        _mla_finalize(o_ref, wuv_ref, l_ref, acc_ref, tq)


def _mla_sample(qlat, qrope, mkey, latc, krt, wuvp, *, batch, seq, tkc):
    past = latc.shape[1]
    n_cache = past // tkc
    kern = functools.partial(_mla_sample_kernel, n_cache=n_cache, tq=seq, past=past)
    return pl.pallas_call(
        kern, grid=(batch, n_cache + 1),
        in_specs=[pl.BlockSpec((HEADS, seq, KV_LORA), lambda b, j: (0, b, 0)),
                  pl.BlockSpec((seq, HEADS * MLA_ROPE), lambda b, j: (b, 0)),
                  pl.BlockSpec((seq, 2 * KV_LORA), lambda b, j: (b, 0)),
                  pl.BlockSpec((1, tkc, KV_LORA), lambda b, j: (b, jnp.minimum(j, n_cache - 1), 0)),
                  pl.BlockSpec((1, MLA_ROPE, tkc), lambda b, j: (b, 0, jnp.minimum(j, n_cache - 1))),
                  _const_spec(wuvp.shape)],
        out_specs=pl.BlockSpec((seq, FOX_W), lambda b, j: (b, 0)),
        out_shape=jax.ShapeDtypeStruct((batch * seq, FOX_W), BF),
        scratch_shapes=[pltpu.VMEM((HEADS * seq, KV_LORA), BF), pltpu.VMEM((HEADS * seq, MLA_ROPE), BF)]
        + _mla_scratch(HEADS * seq),
        compiler_params=_params(2), name="mla_sample")(qlat, qrope, mkey, latc, krt, wuvp)


def _ffn_kernel(x_ref, fox_ref, mla_ref, pconv_ref, wo_ref, gf_ref, wup_ref, cw_ref, wdn_ref, gfin_ref,
                y_ref, cst_ref, hn_ref, acc_ref, carry_ref, *, tm, seq_len):
    n_seg = max(1, tm // seq_len)
    seg = tm // n_seg
    mixed = jnp.concatenate([fox_ref[...], mla_ref[...]], axis=1)
    x1 = x_ref[...] + _dot(mixed, wo_ref[...])
    hn_ref[...] = (x1 * lax.rsqrt(jnp.mean(x1 * x1, axis=-1, keepdims=True) + EPS) * gf_ref[...]).astype(BF)
    acc_ref[...] = x1

    if n_seg == 1:
        tps = seq_len // tm
        first = pl.program_id(0) % tps == 0

        @pl.when(first)
        def _():
            carry_ref[...] = pconv_ref[0]

    srow = lax.broadcasted_iota(jnp.int32, (tm, FF_CHUNK), 0) % seg

    def conv(u, c):
        if n_seg == 1:
            prev = [carry_ref[c]]
        else:
            prev = [pconv_ref[s, c] for s in range(n_seg)]
        e1 = jnp.concatenate([jnp.broadcast_to(pv[1:2, :], (seg, FF_CHUNK)) for pv in prev], axis=0)
        e2 = jnp.concatenate([jnp.broadcast_to(pv[0:1, :], (seg, FF_CHUNK)) for pv in prev], axis=0)
        u1 = jnp.where(srow == 0, e1, pltpu.roll(u, 1, 0))
        u2 = jnp.where(srow == 0, e2, jnp.where(srow == 1, e1, pltpu.roll(u, 2, 0)))
        cw = cw_ref[c]
        out = cw[3:4, :] + cw[0:1, :] * u2 + cw[1:2, :] * u1 + cw[2:3, :] * u
        for s in range(n_seg):
            cst_ref[s, c] = u[seg * (s + 1) - 2:seg * (s + 1), :]
        if n_seg == 1:
            carry_ref[c] = u[tm - 2:tm, :]
        return out

    def body(c, _):
        hn = hn_ref[...]
        gate = conv(_dot(hn, wup_ref[c]), c)
        val = conv(_dot(hn, wup_ref[N_FF + c]), N_FF + c)
        act = (gate * jax.nn.sigmoid(gate) * val).astype(BF)
        acc_ref[...] += _dot(act, wdn_ref[c])
        return 0

    lax.fori_loop(0, N_FF, body, 0)
    x2 = acc_ref[...]
    y_ref[...] = x2 * lax.rsqrt(jnp.mean(x2 * x2, axis=-1, keepdims=True) + EPS) * gfin_ref[...]


def _ffn(x2d, fox, mla, pconv, wo, gf, wup, cw, wdn, gfin, *, seq_len, tm):
    n = x2d.shape[0]
    nt = n // tm
    n_seq = n // seq_len
    n_seg = max(1, tm // seq_len)
    if n_seg == 1:
        tps = seq_len // tm
        st_map = lambda i: (i // tps, 0, 0, 0)
    else:
        st_map = lambda i: (i, 0, 0, 0)
    row = lambda w: pl.BlockSpec((tm, w), lambda i: (i, 0))
    kern = functools.partial(_ffn_kernel, tm=tm, seq_len=seq_len)
    return pl.pallas_call(
        kern, grid=(nt,),
        in_specs=[row(D_MODEL), row(FOX_W), row(FOX_W),
                  pl.BlockSpec((n_seg, 2 * N_FF, 2, FF_CHUNK), st_map),
                  _const_spec(wo.shape), _const_spec(gf.shape), _const_spec(wup.shape), _const_spec(cw.shape),
                  _const_spec(wdn.shape), _const_spec(gfin.shape)],
        out_specs=(row(D_MODEL), pl.BlockSpec((n_seg, 2 * N_FF, 2, FF_CHUNK), st_map)),
        out_shape=(jax.ShapeDtypeStruct((n, D_MODEL), F32),
                   jax.ShapeDtypeStruct((n_seq, 2 * N_FF, 2, FF_CHUNK), F32)),
        scratch_shapes=[pltpu.VMEM((tm, D_MODEL), BF), pltpu.VMEM((tm, D_MODEL), F32),
                        pltpu.VMEM((2 * N_FF, 2, FF_CHUNK), F32)],
        compiler_params=_params(1), name="ffn")(x2d, fox, mla, pconv, wo, gf, wup, cw, wdn, gfin)


def _rope_tables(pos):
    half = MLA_ROPE // 2
    inv = ROPE_THETA ** (-jnp.arange(half, dtype=F32) / half)
    ang = pos.astype(F32)[:, None] * inv[None, :]
    cos = jnp.tile(jnp.cos(ang), (1, 2 * HEADS))
    sin = jnp.tile(jnp.sin(ang), (1, 2 * HEADS))
    return cos, sin


def _placement():
    place = np.zeros((LANES, 2 * N_PAIR * LANES), np.float32)
    ones = np.zeros((1, 2 * N_PAIR * LANES), np.float32)
    for h in range(HEADS):
        for i in range(3):
            for p in range(N_PAIR):
                place[HEADS * i + h, LANES * p + AUG * h + i] = 1.0
                ones[0, LANES * p + AUG * h + 3 + i] = 1.0
                ones[0, N_PAIR * LANES + LANES * p + AUG * h + i] = 1.0
                place[HEADS * i + h, N_PAIR * LANES + LANES * p + AUG * h + 3 + i] = -1.0
    rep = np.zeros((LANES, HEADS * MLA_ROPE), np.float32)
    for h in range(HEADS):
        for r in range(MLA_ROPE):
            rep[r, MLA_ROPE * h + r] = 1.0
    return jnp.asarray(place, BF), jnp.asarray(ones, F32), jnp.asarray(rep, BF)


def _rot_cols(w):
    half = MLA_ROPE // 2
    return jnp.concatenate([-w[..., half:], w[..., :half]], axis=-1)


def _pad_cols(w, width):
    return jnp.pad(w, ((0, 0), (0, width - w.shape[1])))


def _layer_weights(w_in, b_f, w_q_up, w_uk, w_uv, w_out, w_up, conv_w, conv_b, w_down):
    wq, wk, wv, wf, wqc, wckv, wkr = jnp.split(w_in, IN_SPLITS, axis=1)
    wa = jnp.concatenate([wq, wk, wv, wqc, wckv, _pad_cols(wkr, LANES), _pad_cols(_rot_cols(wkr), LANES),
                          _pad_cols(wf, LANES)], axis=1).astype(BF)
    bfp = _pad_cols(b_f[None, :], LANES).astype(F32)
    wq3 = w_q_up.reshape(Q_LORA, HEADS, MLA_NOPE + MLA_ROPE)
    wq_nope = wq3[:, :, :MLA_NOPE].reshape(Q_LORA, HEADS * MLA_NOPE)
    wq_rope = wq3[:, :, MLA_NOPE:]
    wq2 = jnp.concatenate([wq_nope, wq_rope.reshape(Q_LORA, -1), _rot_cols(wq_rope).reshape(Q_LORA, -1)],
                          axis=1).astype(BF)
    wukp = jnp.zeros((HEADS, 4 * MLA_NOPE, KV_LORA), F32)
    wuvp = jnp.zeros((HEADS, KV_LORA, 4 * MLA_VDIM), F32)
    for h in range(HEADS):
        o = MLA_NOPE * (h % 4)
        wukp = wukp.at[h, o:o + MLA_NOPE, :].set(w_uk[:, h, :].T)
        wuvp = wuvp.at[h, :, o:o + MLA_VDIM].set(w_uv[:, h, :])
    wup = w_up.reshape(D_MODEL, 2 * N_FF, FF_CHUNK).transpose(1, 0, 2).astype(BF)
    cw = jnp.concatenate([conv_w, conv_b[None, :], jnp.zeros((4, 2 * D_FF), F32)], axis=0)
    cw = cw.reshape(8, 2 * N_FF, FF_CHUNK).transpose(1, 0, 2)
    wdn = w_down.reshape(N_FF, FF_CHUNK, D_MODEL).astype(BF)
    return wa, bfp, wq2, wukp.astype(BF), wuvp.astype(BF), w_out.astype(BF), wup, cw, wdn


def _state_to_chunks(st):
    b = st.shape[0]
    return st.reshape(b, CONV_W - 1, 2 * N_FF, FF_CHUNK).transpose(0, 2, 1, 3)


def _chunks_to_state(st):
    b = st.shape[0]
    return st.transpose(0, 2, 1, 3).reshape(b, CONV_W - 1, 2 * D_FF)


def kernel(x_prompt, x_sample, cache_fox_k, cache_fox_v, cache_fox_logf, cache_mla_latent, cache_mla_krope,
           state_ffn_conv, attn_norm, w_in, b_forget, q_norm, w_q_up, kv_norm, w_uk, w_uv, w_out, ffn_norm,
           w_up, conv_w, conv_b, w_down, final_norm):
    assert attn_norm.shape[0] == 1, "single-layer stack"
    bp, tp, _ = x_prompt.shape
    bs, ts, _ = x_sample.shape
    past = cache_fox_k.shape[2]
    assert ts == CHUNK and past % CHUNK == 0

    wa, bfp, wq2, wukp, wuvp, wo, wup, cw, wdn = _layer_weights(
        w_in[0], b_forget[0], w_q_up[0], w_uk[0], w_uv[0], w_out[0], w_up[0], conv_w[0], conv_b[0], w_down[0])
    place, ones, rep = _placement()
    g_attn = attn_norm[0][None, :]
    g_q = q_norm[0][None, :]
    g_kv = kv_norm[0][None, :]
    g_ffn = ffn_norm[0][None, :]
    g_fin = final_norm[None, :]

    tm = 512
    outs = []
    for grp in ("prompt", "sample"):
        if grp == "prompt":
            x2d = x_prompt.reshape(bp * tp, D_MODEL)
            b, t = bp, tp
            cos, sin = _rope_tables(jnp.arange(tp, dtype=jnp.int32))
            pconv = jnp.zeros((bp, 2 * N_FF, 2, FF_CHUNK), F32)
        else:
            x2d = x_sample.reshape(bs * ts, D_MODEL)
            b, t = bs, ts
            cos, sin = _rope_tables(past + jnp.arange(ts, dtype=jnp.int32))
            cos, sin = jnp.tile(cos, (tm // ts, 1)), jnp.tile(sin, (tm // ts, 1))
            pconv = _state_to_chunks(state_ffn_conv[0])

        (k, v, logf, lat, kr, qx, kx, vb, mkey, qlat, qrope, fqs) = _proj(
            x2d, cos, sin, g_attn, wa, bfp, g_q, g_kv, wq2, wukp, place, ones, rep, seq_len=t, tm=tm)

        if grp == "prompt":
            fox = _fox_prompt(qx, kx, vb, batch=b, seq=t, t=512)
            mla = _mla_prompt(qlat, qrope, mkey, wuvp, batch=b, seq=t, tq=128, tk=512, rc=256)
        else:
            gk = _fox_prep(jnp.transpose(cache_fox_logf[0], (0, 2, 1)))
            kct = jnp.transpose(cache_fox_k[0], (0, 2, 3, 1))
            vct = jnp.transpose(cache_fox_v[0], (0, 2, 3, 1))
            krt = jnp.transpose(cache_mla_krope[0], (0, 2, 1))
            fox = _fox_sample(qx, kx, vb, fqs, kct, vct, gk, batch=b, seq=t, tkc=512)
            mla = _mla_sample(qlat, qrope, mkey, cache_mla_latent[0], krt, wuvp, batch=b, seq=t, tkc=512)

        y, cst = _ffn(x2d, fox, mla, pconv, wo, g_ffn, wup, cw, wdn, g_fin, seq_len=t, tm=tm)
        outs.append((y.reshape(b, t, D_MODEL),
                     k.reshape(1, b, t, HEADS, FOX_DIM), v.reshape(1, b, t, HEADS, FOX_DIM),
                     logf.reshape(1, b, t, HEADS), lat.reshape(1, b, t, KV_LORA), kr.reshape(1, b, t, MLA_ROPE),
                     _chunks_to_state(cst)[None]))
    (yp, *st_p), (ys, *st_s) = outs
    return (yp, ys, *st_p, *st_s)
```

```python
import functools

import numpy as np
import jax
import jax.numpy as jnp
from jax import lax
from jax.experimental import pallas as pl
from jax.experimental.pallas import tpu as pltpu

D_MODEL = 1024
CHUNK = 64
HEADS = 8
FOX_DIM = 64
MLA_NOPE = 64
MLA_ROPE = 32
MLA_VDIM = 64
Q_LORA = 384
KV_LORA = 256
D_FF = 2816
CONV_W = 3
ROPE_THETA = 10000.0
EPS = 1e-6
NEG = -1e30
LOG2E = 1.4426950408889634

FOX_W = HEADS * FOX_DIM
IN_SIZES = [FOX_W, FOX_W, FOX_W, HEADS, Q_LORA, KV_LORA, MLA_ROPE]
IN_SPLITS = [int(s) for s in np.cumsum(IN_SIZES)[:-1]]

LANES = 128
AUG = 16
N_PAIR = HEADS // 2
FF_CHUNK = 256
N_FF = D_FF // FF_CHUNK
VMEM_LIMIT = 56 * 1024 * 1024

C_Q, C_K, C_V = 0, 512, 1024
C_QC = 1536
C_CKV = 1920
C_KR = 2176
C_KRR = 2304
C_F = 2432
N_A = 2560

BF = jnp.bfloat16
F32 = jnp.float32


def _dot(a, b):
    return jnp.dot(a, b, preferred_element_type=F32)


def _dot_nt(a, b):
    return lax.dot_general(a, b, (((1,), (1,)), ((), ())), preferred_element_type=F32)


def _split3(x):
    hi = x.astype(BF)
    r1 = x - hi.astype(F32)
    mid = r1.astype(BF)
    lo = (r1 - mid.astype(F32)).astype(BF)
    return hi, mid, lo


def _log_sigmoid(x):
    return jnp.minimum(x, 0.0) - jnp.log1p(jnp.exp(-jnp.abs(x)))


def _lanes(x, n):
    if n % LANES == 0:
        return x if n == LANES else pltpu.repeat(x, n // LANES, 1)
    assert n < LANES
    return x[:, :n]


def _params(n_axes):
    return pltpu.CompilerParams(dimension_semantics=("arbitrary",) * n_axes, vmem_limit_bytes=VMEM_LIMIT)


def _const_spec(shape):
    nd = len(shape)
    return pl.BlockSpec(shape, lambda *_: (0,) * nd, pipeline_mode=pl.Buffered(1))


def _proj_kernel(x_ref, cos_ref, sin_ref, g_ref, wa_ref, bf_ref, gq_ref, gkv_ref, wq2_ref,
                 wuk_ref, place_ref, ones_ref, rep_ref,
                 k_ref, v_ref, logf_ref, lat_ref, kr_ref, qx_ref, kx_ref, vb_ref, mkey_ref, qlat_ref,
                 qrope_ref, fqs_ref, carry_ref, *, tm, seq_len, mla_scale):
    x = x_ref[...]
    ms = jnp.mean(x * x, axis=-1, keepdims=True)
    hn = (x * lax.rsqrt(ms + EPS) * g_ref[...]).astype(BF)
    z = _dot(hn, wa_ref[...])

    zk = z[:, C_K:C_K + FOX_W]
    zv = z[:, C_V:C_V + FOX_W]
    k_ref[...] = zk
    v_ref[...] = zv
    vb_ref[...] = zv.astype(BF)

    lane = lax.broadcasted_iota(jnp.int32, (tm, LANES), 1)
    logf = _log_sigmoid(z[:, C_F:C_F + LANES] + bf_ref[...])
    logf = jnp.where(lane < HEADS, logf, 0.0)
    logf_ref[...] = logf[:, :HEADS]
    row = lax.broadcasted_iota(jnp.int32, (tm, tm), 0)
    col = lax.broadcasted_iota(jnp.int32, (tm, tm), 1)
    keep = col <= row
    if seq_len < tm:
        keep = keep & ((col // seq_len) == (row // seq_len))
    tri = jnp.where(keep, 1.0, 0.0).astype(BF)
    hi, mid, lo = _split3(logf)
    fcum = _dot(tri, hi) + _dot(tri, mid) + _dot(tri, lo)
    if seq_len > tm:
        @pl.when(pl.program_id(0) % (seq_len // tm) == 0)
        def _():
            carry_ref[...] = jnp.zeros_like(carry_ref)
        fcum = fcum + carry_ref[0:1, :]
        carry_ref[0:1, :] = fcum[tm - 1:tm, :]

    fqs = fcum * LOG2E
    fqs_ref[...] = fqs
    hi, mid, lo = _split3(fqs)
    comb = hi.astype(F32) + pltpu.roll(mid.astype(F32), HEADS, 1) + pltpu.roll(lo.astype(F32), 2 * HEADS, 1)
    aug = _dot(comb.astype(BF), place_ref[...]) + ones_ref[...]
    zq = z[:, C_Q:C_Q + FOX_W] * (FOX_DIM ** -0.5 * LOG2E)
    for p in range(N_PAIR):
        lo_, hi_ = 2 * LANES * p, 2 * LANES * p + LANES
        qx_ref[:, lo_:hi_] = zq[:, LANES * p:LANES * (p + 1)].astype(BF)
        qx_ref[:, hi_:hi_ + LANES] = aug[:, LANES * p:LANES * (p + 1)].astype(BF)
        kx_ref[:, lo_:hi_] = zk[:, LANES * p:LANES * (p + 1)].astype(BF)
        kx_ref[:, hi_:hi_ + LANES] = aug[:, FOX_W + LANES * p:FOX_W + LANES * (p + 1)].astype(BF)

    zc = z[:, C_CKV:C_CKV + KV_LORA]
    ckv = zc * lax.rsqrt(jnp.mean(zc * zc, axis=-1, keepdims=True) + EPS) * gkv_ref[...]
    lat_ref[...] = ckv
    cos = cos_ref[...]
    sin = sin_ref[...]
    krb = z[:, C_KR:C_KR + LANES] * cos[:, :LANES] + z[:, C_KRR:C_KRR + LANES] * sin[:, :LANES]
    kr_ref[...] = krb[:, :MLA_ROPE]
    mkey_ref[:, :KV_LORA] = ckv.astype(BF)
    mkey_ref[:, KV_LORA:] = _dot(krb.astype(BF), rep_ref[...]).astype(BF)

    zqc = z[:, C_QC:C_QC + Q_LORA]
    qcn = (zqc * lax.rsqrt(jnp.mean(zqc * zqc, axis=-1, keepdims=True) + EPS) * gq_ref[...]).astype(BF)
    zq2 = _dot(qcn, wq2_ref[...])
    nr = HEADS * MLA_ROPE
    qr = zq2[:, FOX_W:FOX_W + nr] * cos + zq2[:, FOX_W + nr:FOX_W + 2 * nr] * sin
    qrope_ref[...] = (qr * mla_scale).astype(BF)
    for h in range(HEADS):
        g = h // 4
        qn = zq2[:, 256 * g:256 * (g + 1)].astype(BF)
        qlat_ref[h] = (_dot(qn, wuk_ref[h]) * mla_scale).astype(BF)


def _proj(x2d, cos, sin, g, wa, bfp, gq, gkv, wq2, wukp, place, ones, rep, *, seq_len, tm):
    n = x2d.shape[0]
    nt = n // tm
    if seq_len >= tm:
        tps = seq_len // tm
        tab_map = lambda i: (i % tps, 0)
    else:
        tab_map = lambda i: (0, 0)
    row = lambda w: pl.BlockSpec((tm, w), lambda i: (i, 0))
    out_shape = (
        jax.ShapeDtypeStruct((n, FOX_W), F32),
        jax.ShapeDtypeStruct((n, FOX_W), F32),
        jax.ShapeDtypeStruct((n, HEADS), F32),
        jax.ShapeDtypeStruct((n, KV_LORA), F32),
        jax.ShapeDtypeStruct((n, MLA_ROPE), F32),
        jax.ShapeDtypeStruct((n, 2 * FOX_W), BF),
        jax.ShapeDtypeStruct((n, 2 * FOX_W), BF),
        jax.ShapeDtypeStruct((n, FOX_W), BF),
        jax.ShapeDtypeStruct((n, 2 * KV_LORA), BF),
        jax.ShapeDtypeStruct((HEADS, n, KV_LORA), BF),
        jax.ShapeDtypeStruct((n, HEADS * MLA_ROPE), BF),
        jax.ShapeDtypeStruct((n, LANES), F32),
    )
    out_specs = (row(FOX_W), row(FOX_W), row(HEADS), row(KV_LORA), row(MLA_ROPE), row(2 * FOX_W),
                 row(2 * FOX_W), row(FOX_W), row(2 * KV_LORA),
                 pl.BlockSpec((HEADS, tm, KV_LORA), lambda i: (0, i, 0)), row(HEADS * MLA_ROPE), row(LANES))
    in_specs = [row(D_MODEL),
                pl.BlockSpec((tm, 2 * LANES), tab_map), pl.BlockSpec((tm, 2 * LANES), tab_map),
                _const_spec(g.shape), _const_spec(wa.shape), _const_spec(bfp.shape), _const_spec(gq.shape),
                _const_spec(gkv.shape), _const_spec(wq2.shape), _const_spec(wukp.shape),
                _const_spec(place.shape), _const_spec(ones.shape), _const_spec(rep.shape)]
    kern = functools.partial(_proj_kernel, tm=tm, seq_len=seq_len,
                             mla_scale=float((MLA_NOPE + MLA_ROPE) ** -0.5 * LOG2E))
    return pl.pallas_call(
        kern, grid=(nt,), in_specs=in_specs, out_specs=out_specs, out_shape=out_shape,
        scratch_shapes=[pltpu.VMEM((8, LANES), F32)],
        compiler_params=_params(1), name="proj")(
            x2d, cos, sin, g, wa, bfp, gq, gkv, wq2, wukp, place, ones, rep)


def _fox_prep_kernel(plt_ref, g_ref):
    x = plt_ref[0]
    n = x.shape[1]
    lane = lax.broadcasted_iota(jnp.int32, x.shape, 1)
    shift = 1
    while shift < n:
        x = x + jnp.where(lane >= shift, pltpu.roll(x, shift, 1), 0.0)
        shift *= 2
    g_ref[0] = (x - x[:, n - 1:n]) * LOG2E


def _fox_prep(plt):
    b, h, p = plt.shape
    return pl.pallas_call(
        _fox_prep_kernel, grid=(b,),
        in_specs=[pl.BlockSpec((1, h, p), lambda i: (i, 0, 0))],
        out_specs=pl.BlockSpec((1, h, p), lambda i: (i, 0, 0)),
        out_shape=jax.ShapeDtypeStruct((b, h, p), F32),
        compiler_params=_params(1), name="fox_prep")(plt)


def _softmax_step(s, m_prev):
    m_new = jnp.maximum(m_prev, jnp.max(s, axis=-1, keepdims=True))
    p = jnp.exp2(s - _lanes(m_new, s.shape[1]))
    alpha = jnp.exp2(m_prev - m_new)
    return p.astype(BF), alpha, m_new


def _fox_qpad(qx, h):
    p, hh = divmod(h, 2)
    lane = lax.broadcasted_iota(jnp.int32, (qx.shape[0], 2 * LANES), 1)
    blk = qx[:, 2 * LANES * p:2 * LANES * (p + 1)]
    keep = ((lane >= FOX_DIM * hh) & (lane < FOX_DIM * (hh + 1))) | (
        (lane >= LANES + AUG * h) & (lane < LANES + AUG * (h + 1)))
    return jnp.where(keep, blk, jnp.zeros_like(blk))


def _fox_prompt_tile(kx, vb, qpad_ref, m_ref, l_ref, acc_ref, masked):
    tq = qpad_ref.shape[1]
    tk = kx.shape[0]
    first = lax.broadcasted_iota(jnp.int32, (tq, LANES), 1) < FOX_DIM
    ones = jnp.ones((tk, LANES), BF)
    if masked:
        causal = lax.broadcasted_iota(jnp.int32, (tq, tk), 1) <= lax.broadcasted_iota(jnp.int32, (tq, tk), 0)
    for p in range(N_PAIR):
        kxp = kx[:, 2 * LANES * p:2 * LANES * (p + 1)]
        vpa = jnp.concatenate([vb[:, LANES * p:LANES * (p + 1)], ones], axis=1)
        alphas, pvs = [], []
        for hh in range(2):
            h = 2 * p + hh
            s = _dot_nt(qpad_ref[h], kxp)
            if masked:
                s = jnp.where(causal, s, NEG)
            pb, alpha, m_new = _softmax_step(s, m_ref[h])
            pv = _dot(pb, vpa)
            l_ref[h] = alpha * l_ref[h] + pv[:, LANES:]
            m_ref[h] = m_new
            alphas.append(alpha)
            pvs.append(pv[:, :LANES])
        acc_ref[p] = acc_ref[p] * jnp.where(first, alphas[0], alphas[1]) + jnp.where(first, pvs[0], pvs[1])


def _fox_prompt_kernel(qx_ref, kx_ref, vb_ref, o_ref, qpad_ref, m_ref, l_ref, acc_ref):
    i = pl.program_id(1)
    j = pl.program_id(2)

    @pl.when(j == 0)
    def _():
        qx = qx_ref[...]
        for h in range(HEADS):
            qpad_ref[h] = _fox_qpad(qx, h)
        m_ref[...] = jnp.full(m_ref.shape, NEG, F32)
        l_ref[...] = jnp.zeros(l_ref.shape, F32)
        acc_ref[...] = jnp.zeros(acc_ref.shape, F32)

    @pl.when(j < i)
    def _():
        _fox_prompt_tile(kx_ref[...], vb_ref[...], qpad_ref, m_ref, l_ref, acc_ref, False)

    @pl.when(j == i)
    def _():
        _fox_prompt_tile(kx_ref[...], vb_ref[...], qpad_ref, m_ref, l_ref, acc_ref, True)
        first = lax.broadcasted_iota(jnp.int32, (qpad_ref.shape[1], LANES), 1) < FOX_DIM
        for p in range(N_PAIR):
            l_sel = jnp.where(first, l_ref[2 * p], l_ref[2 * p + 1])
            o_ref[:, LANES * p:LANES * (p + 1)] = (acc_ref[p] / l_sel).astype(o_ref.dtype)


def _fox_prompt(qx, kx, vb, *, batch, seq, t):
    nq = seq // t
    return pl.pallas_call(
        _fox_prompt_kernel, grid=(batch, nq, nq),
        in_specs=[pl.BlockSpec((t, 2 * FOX_W), lambda b, i, j: (b * nq + i, 0)),
                  pl.BlockSpec((t, 2 * FOX_W), lambda b, i, j: (b * nq + jnp.minimum(i, j), 0)),
                  pl.BlockSpec((t, FOX_W), lambda b, i, j: (b * nq + jnp.minimum(i, j), 0))],
        out_specs=pl.BlockSpec((t, FOX_W), lambda b, i, j: (b * nq + i, 0)),
        out_shape=jax.ShapeDtypeStruct((batch * seq, FOX_W), BF),
        scratch_shapes=[pltpu.VMEM((HEADS, t, 2 * LANES), BF), pltpu.VMEM((HEADS, t, LANES), F32),
                        pltpu.VMEM((HEADS, t, LANES), F32), pltpu.VMEM((N_PAIR, t, LANES), F32)],
        compiler_params=_params(3), name="fox_prompt")(qx, kx, vb)


def _fox_sample_kernel(qx_ref, kx_ref, vb_ref, fqs_ref, kct_ref, vct_ref, gk_ref, o_ref,
                       qbd_ref, qpad_ref, fq_ref, m_ref, l_ref, acc_ref, *, n_cache):
    j = pl.program_id(1)
    tq = qx_ref.shape[0]
    rows = HEADS * tq

    @pl.when(j == 0)
    def _():
        qx = qx_ref[...]
        fqs = fqs_ref[...]
        qc = jnp.concatenate([qx[:, 2 * LANES * p:2 * LANES * p + LANES] for p in range(N_PAIR)], axis=1)
        lane = lax.broadcasted_iota(jnp.int32, qc.shape, 1)
        for h in range(HEADS):
            qbd_ref[h * tq:(h + 1) * tq, :] = jnp.where((lane // FOX_DIM) == h, qc, jnp.zeros_like(qc))
            qpad_ref[h] = _fox_qpad(qx, h)
            fq_ref[h * tq:(h + 1) * tq, :] = jnp.broadcast_to(fqs[:, h:h + 1], (tq, LANES))
        m_ref[...] = jnp.full(m_ref.shape, NEG, F32)
        l_ref[...] = jnp.zeros(l_ref.shape, F32)
        acc_ref[...] = jnp.zeros(acc_ref.shape, F32)

    def update(s, pv_fn):
        m_prev = m_ref[...]
        m_new = jnp.maximum(m_prev, jnp.max(s, axis=-1, keepdims=True))
        p = jnp.exp2(s - _lanes(m_new, s.shape[1]))
        alpha = jnp.exp2(m_prev - m_new)
        l_ref[...] = alpha * l_ref[...] + jnp.sum(p, axis=-1, keepdims=True)
        m_ref[...] = m_new
        pb = p.astype(BF)
        for pr in range(N_PAIR):
            r0, r1 = 2 * tq * pr, 2 * tq * (pr + 1)
            acc_ref[r0:r1] = acc_ref[r0:r1] * alpha[r0:r1] + pv_fn(pr, pb[r0:r1])

    @pl.when(j < n_cache)
    def _():
        tk = kct_ref.shape[3]
        kt = kct_ref[0].reshape(HEADS * FOX_DIM, tk).astype(BF)
        gk = gk_ref[0]
        gk_rows = jnp.concatenate([jnp.broadcast_to(gk[h:h + 1, :], (tq, tk)) for h in range(HEADS)], axis=0)
        s = _dot(qbd_ref[...], kt) + (_lanes(fq_ref[...], tk) - gk_rows)
        update(s, lambda pr, pb: _dot_nt(
            pb, vct_ref[0, 2 * pr:2 * pr + 2].reshape(2 * FOX_DIM, tk).astype(BF)))

    @pl.when(j == n_cache)
    def _():
        kx = kx_ref[...]
        vb = vb_ref[...]
        tk = kx.shape[0]
        s = jnp.concatenate([_dot_nt(qpad_ref[h], kx[:, 2 * LANES * (h // 2):2 * LANES * (h // 2 + 1)])
                             for h in range(HEADS)], axis=0)
        row = lax.broadcasted_iota(jnp.int32, (rows, tk), 0) & (tq - 1)
        s = jnp.where(lax.broadcasted_iota(jnp.int32, (rows, tk), 1) <= row, s, NEG)
        update(s, lambda pr, pb: _dot(pb, vb[:, LANES * pr:LANES * (pr + 1)]))
        first = lax.broadcasted_iota(jnp.int32, (tq, LANES), 1) < FOX_DIM
        o = acc_ref[...] / l_ref[...]
        for pr in range(N_PAIR):
            o_ref[:, LANES * pr:LANES * (pr + 1)] = jnp.where(
                first, o[2 * tq * pr:2 * tq * pr + tq], o[2 * tq * pr + tq:2 * tq * (pr + 1)]).astype(o_ref.dtype)


def _fox_sample(qx, kx, vb, fqs, kct, vct, gk, *, batch, seq, tkc):
    past = kct.shape[3]
    n_cache = past // tkc
    kern = functools.partial(_fox_sample_kernel, n_cache=n_cache)
    row = lambda w: pl.BlockSpec((seq, w), lambda b, j: (b, 0))
    return pl.pallas_call(
        kern, grid=(batch, n_cache + 1),
        in_specs=[row(2 * FOX_W), row(2 * FOX_W), row(FOX_W), row(LANES),
                  pl.BlockSpec((1, HEADS, FOX_DIM, tkc), lambda b, j: (b, 0, 0, jnp.minimum(j, n_cache - 1))),
                  pl.BlockSpec((1, HEADS, FOX_DIM, tkc), lambda b, j: (b, 0, 0, jnp.minimum(j, n_cache - 1))),
                  pl.BlockSpec((1, HEADS, tkc), lambda b, j: (b, 0, jnp.minimum(j, n_cache - 1)))],
        out_specs=row(FOX_W),
        out_shape=jax.ShapeDtypeStruct((batch * seq, FOX_W), BF),
        scratch_shapes=[pltpu.VMEM((HEADS * seq, FOX_W), BF), pltpu.VMEM((HEADS, seq, 2 * LANES), BF)]
        + [pltpu.VMEM((HEADS * seq, LANES), F32)] * 4,
        compiler_params=_params(2), name="fox_sample")(qx, kx, vb, fqs, kct, vct, gk)


def _chunk_mask(shape, tq, row0, q0, k0):
    assert tq & (tq - 1) == 0 and CHUNK & (CHUNK - 1) == 0
    shift = CHUNK.bit_length() - 1
    qpos = q0 + ((row0 + lax.broadcasted_iota(jnp.int32, shape, 0)) & (tq - 1))
    kpos = k0 + lax.broadcasted_iota(jnp.int32, shape, 1)
    return lax.shift_right_logical(kpos, shift) <= lax.shift_right_logical(qpos, shift)


def _mla_rows_update(s, val, m_ref, l_ref, acc_ref, r0, r1):
    m_prev = m_ref[r0:r1]
    m_new = jnp.maximum(m_prev, jnp.max(s, axis=-1, keepdims=True))
    p = jnp.exp2(s - _lanes(m_new, s.shape[1]))
    alpha = jnp.exp2(m_prev - m_new)
    l_ref[r0:r1] = alpha * l_ref[r0:r1] + jnp.sum(p, axis=-1, keepdims=True)
    m_ref[r0:r1] = m_new
    acc_ref[r0:r1] = acc_ref[r0:r1] * _lanes(alpha, KV_LORA) + _dot(p.astype(BF), val)


def _mla_init_stats(m_ref, l_ref, acc_ref):
    m_ref[...] = jnp.full(m_ref.shape, NEG, F32)
    l_ref[...] = jnp.zeros(l_ref.shape, F32)
    acc_ref[...] = jnp.zeros(acc_ref.shape, F32)


def _mla_finalize(o_ref, wuv_ref, l_ref, acc_ref, tq):
    olat = (acc_ref[...] / _lanes(l_ref[...], KV_LORA)).astype(BF)
    for g in range(2):
        out = None
        for h in range(4 * g, 4 * g + 4):
            part = _dot(olat[h * tq:(h + 1) * tq, :], wuv_ref[h])
            out = part if out is None else out + part
        o_ref[:, 256 * g:256 * (g + 1)] = out.astype(o_ref.dtype)


def _mla_prompt_kernel(qlat_ref, qrope_ref, mkey_ref, wuv_ref, o_ref, qcat_ref, m_ref, l_ref, acc_ref,
                       *, tq, tk, rc):
    i = pl.program_id(1)
    j = pl.program_id(2)
    last_j = (tq * (i + 1) - 1) // tk
    needs_mask = tk * (j + 1) > tq * i + CHUNK

    @pl.when(j == 0)
    def _():
        qrope = qrope_ref[...]
        lane = lax.broadcasted_iota(jnp.int32, (tq, HEADS * MLA_ROPE), 1)
        for h in range(HEADS):
            qcat_ref[h * tq:(h + 1) * tq, :KV_LORA] = qlat_ref[h]
            qcat_ref[h * tq:(h + 1) * tq, KV_LORA:] = jnp.where((lane // MLA_ROPE) == h, qrope,
                                                                jnp.zeros_like(qrope))
        _mla_init_stats(m_ref, l_ref, acc_ref)

    def tile(masked):
        mk = mkey_ref[...]
        for r0 in range(0, HEADS * tq, rc):
            s = _dot_nt(qcat_ref[r0:r0 + rc], mk)
            if masked:
                s = jnp.where(_chunk_mask(s.shape, tq, r0, tq * i, tk * j), s, NEG)
            _mla_rows_update(s, mk[:, :KV_LORA], m_ref, l_ref, acc_ref, r0, r0 + rc)

    @pl.when((j <= last_j) & jnp.logical_not(needs_mask))
    def _():
        tile(False)

    @pl.when((j <= last_j) & needs_mask)
    def _():
        tile(True)

    @pl.when(j == last_j)
    def _():
        _mla_finalize(o_ref, wuv_ref, l_ref, acc_ref, tq)


def _mla_scratch(rows):
    return [pltpu.VMEM((rows, LANES), F32), pltpu.VMEM((rows, LANES), F32), pltpu.VMEM((rows, KV_LORA), F32)]


def _mla_prompt(qlat, qrope, mkey, wuvp, *, batch, seq, tq, tk, rc):
    nq = seq // tq
    nk = seq // tk

    def key_map(b, i, j):
        return (b * nk + jnp.minimum(j, (tq * (i + 1) - 1) // tk), 0)

    kern = functools.partial(_mla_prompt_kernel, tq=tq, tk=tk, rc=rc)
    return pl.pallas_call(
        kern, grid=(batch, nq, nk),
        in_specs=[pl.BlockSpec((HEADS, tq, KV_LORA), lambda b, i, j: (0, b * nq + i, 0)),
                  pl.BlockSpec((tq, HEADS * MLA_ROPE), lambda b, i, j: (b * nq + i, 0)),
                  pl.BlockSpec((tk, 2 * KV_LORA), key_map),
                  _const_spec(wuvp.shape)],
        out_specs=pl.BlockSpec((tq, FOX_W), lambda b, i, j: (b * nq + i, 0)),
        out_shape=jax.ShapeDtypeStruct((batch * seq, FOX_W), BF),
        scratch_shapes=[pltpu.VMEM((HEADS * tq, 2 * KV_LORA), BF)] + _mla_scratch(HEADS * tq),
        compiler_params=_params(3), name="mla_prompt")(qlat, qrope, mkey, wuvp)


def _mla_sample_kernel(qlat_ref, qrope_ref, mkey_ref, latc_ref, krt_ref, wuv_ref, o_ref,
                       ql_ref, qr_ref, m_ref, l_ref, acc_ref, *, n_cache, tq, past):
    j = pl.program_id(1)
    rows = HEADS * tq

    @pl.when(j == 0)
    def _():
        qrope = qrope_ref[...]
        for h in range(HEADS):
            ql_ref[h * tq:(h + 1) * tq, :] = qlat_ref[h]
            qr_ref[h * tq:(h + 1) * tq, :] = qrope[:, MLA_ROPE * h:MLA_ROPE * (h + 1)]
        _mla_init_stats(m_ref, l_ref, acc_ref)

    @pl.when(j < n_cache)
    def _():
        c = latc_ref[0].astype(BF)
        s = _dot_nt(ql_ref[...], c) + _dot(qr_ref[...], krt_ref[0].astype(BF))
        _mla_rows_update(s, c, m_ref, l_ref, acc_ref, 0, rows)

    @pl.when(j == n_cache)
    def _():
        mk = mkey_ref[...]
        s = _dot_nt(ql_ref[...], mk[:, :KV_LORA]) + _dot_nt(qr_ref[...], mk[:, KV_LORA:KV_LORA + MLA_ROPE])
        if past % CHUNK != 0 or tq > CHUNK:
            s = jnp.where(_chunk_mask(s.shape, tq, 0, past, past), s, NEG)
        _mla_rows_update(s, mk[:, :KV_LORA], m_ref, l_ref, acc_ref, 0, rows)
        _mla_finalize(o_ref, wuv_ref, l_ref, acc_ref, tq)


def _mla_sample(qlat, qrope, mkey, latc, krt, wuvp, *, batch, seq, tkc):
    past = latc.shape[1]
    n_cache = past // tkc
    kern = functools.partial(_mla_sample_kernel, n_cache=n_cache, tq=seq, past=past)
    return pl.pallas_call(
        kern, grid=(batch, n_cache + 1),
        in_specs=[pl.BlockSpec((HEADS, seq, KV_LORA), lambda b, j: (0, b, 0)),
                  pl.BlockSpec((seq, HEADS * MLA_ROPE), lambda b, j: (b, 0)),
                  pl.BlockSpec((seq, 2 * KV_LORA), lambda b, j: (b, 0)),
                  pl.BlockSpec((1, tkc, KV_LORA), lambda b, j: (b, jnp.minimum(j, n_cache - 1), 0)),
                  pl.BlockSpec((1, MLA_ROPE, tkc), lambda b, j: (b, 0, jnp.minimum(j, n_cache - 1))),
                  _const_spec(wuvp.shape)],
        out_specs=pl.BlockSpec((seq, FOX_W), lambda b, j: (b, 0)),
        out_shape=jax.ShapeDtypeStruct((batch * seq, FOX_W), BF),
        scratch_shapes=[pltpu.VMEM((HEADS * seq, KV_LORA), BF), pltpu.VMEM((HEADS * seq, MLA_ROPE), BF)]
        + _mla_scratch(HEADS * seq),
        compiler_params=_params(2), name="mla_sample")(qlat, qrope, mkey, latc, krt, wuvp)


def _ffn_kernel(x_ref, fox_ref, mla_ref, pconv_ref, wo_ref, gf_ref, wup_ref, cw_ref, wdn_ref, gfin_ref,
                y_ref, cst_ref, hn_ref, act_ref, ush_ref, carry_ref, *, tm, seq_len):
    n_seg = max(1, tm // seq_len)
    seg = tm // n_seg
    mixed = jnp.concatenate([fox_ref[...], mla_ref[...]], axis=1)
    x1 = x_ref[...] + _dot(mixed, wo_ref[...])
    hn_ref[...] = (x1 * lax.rsqrt(jnp.mean(x1 * x1, axis=-1, keepdims=True) + EPS) * gf_ref[...]).astype(BF)
    y_ref[...] = x1

    if n_seg == 1:
        @pl.when(pl.program_id(0) % (seq_len // tm) == 0)
        def _():
            carry_ref[...] = pconv_ref[0]

    def conv(u, c, slot):
        for s in range(n_seg):
            ush_ref[slot, s, 6:8, :] = carry_ref[c] if n_seg == 1 else pconv_ref[s, c]
            ush_ref[slot, s, 8:8 + seg, :] = u[seg * s:seg * (s + 1), :]
            cst_ref[s, c] = u[seg * (s + 1) - 2:seg * (s + 1), :]
        if n_seg == 1:
            carry_ref[c] = u[tm - 2:tm, :]
        u1 = jnp.concatenate([ush_ref[slot, s, 7:7 + seg, :] for s in range(n_seg)], axis=0)
        u2 = jnp.concatenate([ush_ref[slot, s, 6:6 + seg, :] for s in range(n_seg)], axis=0)
        cw = cw_ref[c]
        return cw[3:4, :] + cw[0:1, :] * u2 + cw[1:2, :] * u1 + cw[2:3, :] * u

    for c in range(N_FF):
        hn = hn_ref[...]
        gate = conv(_dot(hn, wup_ref[c]), c, 2 * (c % 2))
        val = conv(_dot(hn, wup_ref[N_FF + c]), N_FF + c, 2 * (c % 2) + 1)
        act_ref[:, FF_CHUNK * c:FF_CHUNK * (c + 1)] = (gate * jax.nn.sigmoid(gate) * val).astype(BF)

    x2 = y_ref[...] + _dot(act_ref[...], wdn_ref[...])
    y_ref[...] = x2 * lax.rsqrt(jnp.mean(x2 * x2, axis=-1, keepdims=True) + EPS) * gfin_ref[...]


def _ffn(x2d, fox, mla, pconv, wo, gf, wup, cw, wdn, gfin, *, seq_len, tm):
    n = x2d.shape[0]
    nt = n // tm
    n_seq = n // seq_len
    n_seg = max(1, tm // seq_len)
    if n_seg == 1:
        tps = seq_len // tm
        st_map = lambda i: (i // tps, 0, 0, 0)
    else:
        st_map = lambda i: (i, 0, 0, 0)
    row = lambda w: pl.BlockSpec((tm, w), lambda i: (i, 0))
    kern = functools.partial(_ffn_kernel, tm=tm, seq_len=seq_len)
    return pl.pallas_call(
        kern, grid=(nt,),
        in_specs=[row(D_MODEL), row(FOX_W), row(FOX_W),
                  pl.BlockSpec((n_seg, 2 * N_FF, 2, FF_CHUNK), st_map),
                  _const_spec(wo.shape), _const_spec(gf.shape), _const_spec(wup.shape), _const_spec(cw.shape),
                  _const_spec(wdn.shape), _const_spec(gfin.shape)],
        out_specs=(row(D_MODEL), pl.BlockSpec((n_seg, 2 * N_FF, 2, FF_CHUNK), st_map)),
        out_shape=(jax.ShapeDtypeStruct((n, D_MODEL), F32),
                   jax.ShapeDtypeStruct((n_seq, 2 * N_FF, 2, FF_CHUNK), F32)),
        scratch_shapes=[pltpu.VMEM((tm, D_MODEL), BF), pltpu.VMEM((tm, D_FF), BF),
                        pltpu.VMEM((4, n_seg, 8 + tm // n_seg, FF_CHUNK), F32),
                        pltpu.VMEM((2 * N_FF, 2, FF_CHUNK), F32)],
        compiler_params=_params(1), name="ffn")(x2d, fox, mla, pconv, wo, gf, wup, cw, wdn, gfin)


def _rope_tables(pos):
    half = MLA_ROPE // 2
    inv = ROPE_THETA ** (-jnp.arange(half, dtype=F32) / half)
    ang = pos.astype(F32)[:, None] * inv[None, :]
    cos = jnp.tile(jnp.cos(ang), (1, 2 * HEADS))
    sin = jnp.tile(jnp.sin(ang), (1, 2 * HEADS))
    return cos, sin


def _placement():
    place = np.zeros((LANES, 2 * N_PAIR * LANES), np.float32)
    ones = np.zeros((1, 2 * N_PAIR * LANES), np.float32)
    for h in range(HEADS):
        for i in range(3):
            for p in range(N_PAIR):
                place[HEADS * i + h, LANES * p + AUG * h + i] = 1.0
                ones[0, LANES * p + AUG * h + 3 + i] = 1.0
                ones[0, N_PAIR * LANES + LANES * p + AUG * h + i] = 1.0
                place[HEADS * i + h, N_PAIR * LANES + LANES * p + AUG * h + 3 + i] = -1.0
    rep = np.zeros((LANES, HEADS * MLA_ROPE), np.float32)
    for h in range(HEADS):
        for r in range(MLA_ROPE):
            rep[r, MLA_ROPE * h + r] = 1.0
    return jnp.asarray(place, BF), jnp.asarray(ones, F32), jnp.asarray(rep, BF)


def _rot_cols(w):
    half = MLA_ROPE // 2
    return jnp.concatenate([-w[..., half:], w[..., :half]], axis=-1)


def _pad_cols(w, width):
    return jnp.pad(w, ((0, 0), (0, width - w.shape[1])))


def _layer_weights(w_in, b_f, w_q_up, w_uk, w_uv, w_out, w_up, conv_w, conv_b, w_down):
    wq, wk, wv, wf, wqc, wckv, wkr = jnp.split(w_in, IN_SPLITS, axis=1)
    wa = jnp.concatenate([wq, wk, wv, wqc, wckv, _pad_cols(wkr, LANES), _pad_cols(_rot_cols(wkr), LANES),
                          _pad_cols(wf, LANES)], axis=1).astype(BF)
    bfp = _pad_cols(b_f[None, :], LANES).astype(F32)
    wq3 = w_q_up.reshape(Q_LORA, HEADS, MLA_NOPE + MLA_ROPE)
    wq_nope = wq3[:, :, :MLA_NOPE].reshape(Q_LORA, HEADS * MLA_NOPE)
    wq_rope = wq3[:, :, MLA_NOPE:]
    wq2 = jnp.concatenate([wq_nope, wq_rope.reshape(Q_LORA, -1), _rot_cols(wq_rope).reshape(Q_LORA, -1)],
                          axis=1).astype(BF)
    wukp = jnp.zeros((HEADS, 4 * MLA_NOPE, KV_LORA), F32)
    wuvp = jnp.zeros((HEADS, KV_LORA, 4 * MLA_VDIM), F32)
    for h in range(HEADS):
        o = MLA_NOPE * (h % 4)
        wukp = wukp.at[h, o:o + MLA_NOPE, :].set(w_uk[:, h, :].T)
        wuvp = wuvp.at[h, :, o:o + MLA_VDIM].set(w_uv[:, h, :])
    wup = w_up.reshape(D_MODEL, 2 * N_FF, FF_CHUNK).transpose(1, 0, 2).astype(BF)
    cw = jnp.concatenate([conv_w, conv_b[None, :], jnp.zeros((4, 2 * D_FF), F32)], axis=0)
    cw = cw.reshape(8, 2 * N_FF, FF_CHUNK).transpose(1, 0, 2)
    wdn = w_down.astype(BF)
    return wa, bfp, wq2, wukp.astype(BF), wuvp.astype(BF), w_out.astype(BF), wup, cw, wdn


def _state_to_chunks(st):
    b = st.shape[0]
    return st.reshape(b, CONV_W - 1, 2 * N_FF, FF_CHUNK).transpose(0, 2, 1, 3)


def _chunks_to_state(st):
    b = st.shape[0]
    return st.transpose(0, 2, 1, 3).reshape(b, CONV_W - 1, 2 * D_FF)


def kernel(x_prompt, x_sample, cache_fox_k, cache_fox_v, cache_fox_logf, cache_mla_latent, cache_mla_krope,
           state_ffn_conv, attn_norm, w_in, b_forget, q_norm, w_q_up, kv_norm, w_uk, w_uv, w_out, ffn_norm,
           w_up, conv_w, conv_b, w_down, final_norm):
    assert attn_norm.shape[0] == 1, "single-layer stack"
    bp, tp, _ = x_prompt.shape
    bs, ts, _ = x_sample.shape
    past = cache_fox_k.shape[2]
    assert ts == CHUNK and past % CHUNK == 0

    wa, bfp, wq2, wukp, wuvp, wo, wup, cw, wdn = _layer_weights(
        w_in[0], b_forget[0], w_q_up[0], w_uk[0], w_uv[0], w_out[0], w_up[0], conv_w[0], conv_b[0], w_down[0])
    place, ones, rep = _placement()
    g_attn = attn_norm[0][None, :]
    g_q = q_norm[0][None, :]
    g_kv = kv_norm[0][None, :]
    g_ffn = ffn_norm[0][None, :]
    g_fin = final_norm[None, :]

    tm = 512
    outs = []
    for grp in ("prompt", "sample"):
        if grp == "prompt":
            x2d = x_prompt.reshape(bp * tp, D_MODEL)
            b, t = bp, tp
            cos, sin = _rope_tables(jnp.arange(tp, dtype=jnp.int32))
            pconv = jnp.zeros((bp, 2 * N_FF, 2, FF_CHUNK), F32)
        else:
            x2d = x_sample.reshape(bs * ts, D_MODEL)
            b, t = bs, ts
            cos, sin = _rope_tables(past + jnp.arange(ts, dtype=jnp.int32))
            cos, sin = jnp.tile(cos, (tm // ts, 1)), jnp.tile(sin, (tm // ts, 1))
            pconv = _state_to_chunks(state_ffn_conv[0])

        (k, v, logf, lat, kr, qx, kx, vb, mkey, qlat, qrope, fqs) = _proj(
            x2d, cos, sin, g_attn, wa, bfp, g_q, g_kv, wq2, wukp, place, ones, rep, seq_len=t, tm=tm)

        if grp == "prompt":
            fox = _fox_prompt(qx, kx, vb, batch=b, seq=t, t=512)
            mla = _mla_prompt(qlat, qrope, mkey, wuvp, batch=b, seq=t, tq=512, tk=512, rc=512)
        else:
            gk = _fox_prep(jnp.transpose(cache_fox_logf[0], (0, 2, 1)))
            kct = jnp.transpose(cache_fox_k[0], (0, 2, 3, 1))
            vct = jnp.transpose(cache_fox_v[0], (0, 2, 3, 1))
            krt = jnp.transpose(cache_mla_krope[0], (0, 2, 1))
            fox = _fox_sample(qx, kx, vb, fqs, kct, vct, gk, batch=b, seq=t, tkc=512)
            mla = _mla_sample(qlat, qrope, mkey, cache_mla_latent[0], krt, wuvp, batch=b, seq=t, tkc=512)

        y, cst = _ffn(x2d, fox, mla, pconv, wo, g_ffn, wup, cw, wdn, g_fin, seq_len=t, tm=tm)
        outs.append((y.reshape(b, t, D_MODEL),
                     k.reshape(1, b, t, HEADS, FOX_DIM), v.reshape(1, b, t, HEADS, FOX_DIM),
                     logf.reshape(1, b, t, HEADS), lat.reshape(1, b, t, KV_LORA), kr.reshape(1, b, t, MLA_ROPE),
                     _chunks_to_state(cst)[None]))
    (yp, *st_p), (ys, *st_s) = outs
    return (yp, ys, *st_p, *st_s)
```

```python
import functools

import numpy as np
import jax
import jax.numpy as jnp
from jax import lax
from jax.experimental import pallas as pl
from jax.experimental.pallas import tpu as pltpu

D_MODEL = 1024
CHUNK = 64
HEADS = 8
FOX_DIM = 64
MLA_NOPE = 64
MLA_ROPE = 32
MLA_VDIM = 64
Q_LORA = 384
KV_LORA = 256
D_FF = 2816
CONV_W = 3
ROPE_THETA = 10000.0
EPS = 1e-6
NEG = -1e30
LOG2E = 1.4426950408889634

FOX_W = HEADS * FOX_DIM
IN_SIZES = [FOX_W, FOX_W, FOX_W, HEADS, Q_LORA, KV_LORA, MLA_ROPE]
IN_SPLITS = [int(s) for s in np.cumsum(IN_SIZES)[:-1]]

LANES = 128
AUG = 16
N_PAIR = HEADS // 2
FF_CHUNK = 256
N_FF = D_FF // FF_CHUNK
VMEM_LIMIT = 56 * 1024 * 1024

C_Q, C_K, C_V = 0, 512, 1024
C_QC = 1536
C_CKV = 1920
C_KR = 2176
C_KRR = 2304
C_F = 2432
N_A = 2560

BF = jnp.bfloat16
F32 = jnp.float32


def _dot(a, b):
    return jnp.dot(a, b, preferred_element_type=F32)


def _dot_nt(a, b):
    return lax.dot_general(a, b, (((1,), (1,)), ((), ())), preferred_element_type=F32)


def _split3(x):
    hi = x.astype(BF)
    r1 = x - hi.astype(F32)
    mid = r1.astype(BF)
    lo = (r1 - mid.astype(F32)).astype(BF)
    return hi, mid, lo


def _log_sigmoid(x):
    return jnp.minimum(x, 0.0) - jnp.log1p(jnp.exp(-jnp.abs(x)))


def _lanes(x, n):
    if n % LANES == 0:
        return x if n == LANES else jnp.concatenate([x] * (n // LANES), axis=1)
    assert n < LANES
    return x[:, :n]


def _params(n_axes):
    return pltpu.CompilerParams(dimension_semantics=("arbitrary",) * n_axes, vmem_limit_bytes=VMEM_LIMIT)


def _const_spec(shape):
    nd = len(shape)
    return pl.BlockSpec(shape, lambda *_: (0,) * nd, pipeline_mode=pl.Buffered(1))


def _proj_kernel(x_ref, cos_ref, sin_ref, g_ref, wa_ref, bf_ref, gq_ref, gkv_ref, wq2_ref,
                 wuk_ref, place_ref, ones_ref, rep_ref,
                 k_ref, v_ref, logf_ref, lat_ref, kr_ref, qx_ref, kx_ref, vb_ref, mkey_ref, qlat_ref,
                 qrope_ref, fqs_ref, carry_ref, *, tm, seq_len, mla_scale):
    x = x_ref[...]
    ms = jnp.mean(x * x, axis=-1, keepdims=True)
    hn = (x * lax.rsqrt(ms + EPS) * g_ref[...]).astype(BF)
    z = _dot(hn, wa_ref[...])

    zk = z[:, C_K:C_K + FOX_W]
    zv = z[:, C_V:C_V + FOX_W]
    k_ref[...] = zk
    v_ref[...] = zv
    vb_ref[...] = zv.astype(BF)

    lane = lax.broadcasted_iota(jnp.int32, (tm, LANES), 1)
    logf = _log_sigmoid(z[:, C_F:C_F + LANES] + bf_ref[...])
    logf = jnp.where(lane < HEADS, logf, 0.0)
    logf_ref[...] = logf[:, :HEADS]
    row = lax.broadcasted_iota(jnp.int32, (tm, tm), 0)
    col = lax.broadcasted_iota(jnp.int32, (tm, tm), 1)
    keep = col <= row
    if seq_len < tm:
        keep = keep & ((col // seq_len) == (row // seq_len))
    tri = jnp.where(keep, 1.0, 0.0).astype(BF)
    hi, mid, lo = _split3(logf)
    fcum = _dot(tri, hi) + _dot(tri, mid) + _dot(tri, lo)
    if seq_len > tm:
        @pl.when(pl.program_id(0) % (seq_len // tm) == 0)
        def _():
            carry_ref[...] = jnp.zeros_like(carry_ref)
        fcum = fcum + carry_ref[0:1, :]
        carry_ref[0:1, :] = fcum[tm - 1:tm, :]

    fqs = fcum * LOG2E
    fqs_ref[...] = fqs
    hi, mid, lo = _split3(fqs)
    comb = hi.astype(F32) + pltpu.roll(mid.astype(F32), HEADS, 1) + pltpu.roll(lo.astype(F32), 2 * HEADS, 1)
    aug = _dot(comb.astype(BF), place_ref[...]) + ones_ref[...]
    zq = z[:, C_Q:C_Q + FOX_W] * (FOX_DIM ** -0.5 * LOG2E)
    for p in range(N_PAIR):
        lo_, hi_ = 2 * LANES * p, 2 * LANES * p + LANES
        qx_ref[:, lo_:hi_] = zq[:, LANES * p:LANES * (p + 1)].astype(BF)
        qx_ref[:, hi_:hi_ + LANES] = aug[:, LANES * p:LANES * (p + 1)].astype(BF)
        kx_ref[:, lo_:hi_] = zk[:, LANES * p:LANES * (p + 1)].astype(BF)
        kx_ref[:, hi_:hi_ + LANES] = aug[:, FOX_W + LANES * p:FOX_W + LANES * (p + 1)].astype(BF)

    zc = z[:, C_CKV:C_CKV + KV_LORA]
    ckv = zc * lax.rsqrt(jnp.mean(zc * zc, axis=-1, keepdims=True) + EPS) * gkv_ref[...]
    lat_ref[...] = ckv
    cos = cos_ref[...]
    sin = sin_ref[...]
    krb = z[:, C_KR:C_KR + LANES] * cos[:, :LANES] + z[:, C_KRR:C_KRR + LANES] * sin[:, :LANES]
    kr_ref[...] = krb[:, :MLA_ROPE]
    mkey_ref[:, :KV_LORA] = ckv.astype(BF)
    mkey_ref[:, KV_LORA:] = _dot(krb.astype(BF), rep_ref[...]).astype(BF)

    zqc = z[:, C_QC:C_QC + Q_LORA]
    qcn = (zqc * lax.rsqrt(jnp.mean(zqc * zqc, axis=-1, keepdims=True) + EPS) * gq_ref[...]).astype(BF)
    zq2 = _dot(qcn, wq2_ref[...])
    nr = HEADS * MLA_ROPE
    qr = zq2[:, FOX_W:FOX_W + nr] * cos + zq2[:, FOX_W + nr:FOX_W + 2 * nr] * sin
    qrope_ref[...] = (qr * mla_scale).astype(BF)
    for h in range(HEADS):
        g = h // 4
        qn = zq2[:, 256 * g:256 * (g + 1)].astype(BF)
        qlat_ref[h] = (_dot(qn, wuk_ref[h]) * mla_scale).astype(BF)


def _proj(x2d, cos, sin, g, wa, bfp, gq, gkv, wq2, wukp, place, ones, rep, *, seq_len, tm):
    n = x2d.shape[0]
    nt = n // tm
    if seq_len >= tm:
        tps = seq_len // tm
        tab_map = lambda i: (i % tps, 0)
    else:
        tab_map = lambda i: (0, 0)
    row = lambda w: pl.BlockSpec((tm, w), lambda i: (i, 0))
    out_shape = (
        jax.ShapeDtypeStruct((n, FOX_W), F32),
        jax.ShapeDtypeStruct((n, FOX_W), F32),
        jax.ShapeDtypeStruct((n, HEADS), F32),
        jax.ShapeDtypeStruct((n, KV_LORA), F32),
        jax.ShapeDtypeStruct((n, MLA_ROPE), F32),
        jax.ShapeDtypeStruct((n, 2 * FOX_W), BF),
        jax.ShapeDtypeStruct((n, 2 * FOX_W), BF),
        jax.ShapeDtypeStruct((n, FOX_W), BF),
        jax.ShapeDtypeStruct((n, 2 * KV_LORA), BF),
        jax.ShapeDtypeStruct((HEADS, n, KV_LORA), BF),
        jax.ShapeDtypeStruct((n, HEADS * MLA_ROPE), BF),
        jax.ShapeDtypeStruct((n, LANES), F32),
    )
    out_specs = (row(FOX_W), row(FOX_W), row(HEADS), row(KV_LORA), row(MLA_ROPE), row(2 * FOX_W),
                 row(2 * FOX_W), row(FOX_W), row(2 * KV_LORA),
                 pl.BlockSpec((HEADS, tm, KV_LORA), lambda i: (0, i, 0)), row(HEADS * MLA_ROPE), row(LANES))
    in_specs = [row(D_MODEL),
                pl.BlockSpec((tm, 2 * LANES), tab_map), pl.BlockSpec((tm, 2 * LANES), tab_map),
                _const_spec(g.shape), _const_spec(wa.shape), _const_spec(bfp.shape), _const_spec(gq.shape),
                _const_spec(gkv.shape), _const_spec(wq2.shape), _const_spec(wukp.shape),
                _const_spec(place.shape), _const_spec(ones.shape), _const_spec(rep.shape)]
    kern = functools.partial(_proj_kernel, tm=tm, seq_len=seq_len,
                             mla_scale=float((MLA_NOPE + MLA_ROPE) ** -0.5 * LOG2E))
    return pl.pallas_call(
        kern, grid=(nt,), in_specs=in_specs, out_specs=out_specs, out_shape=out_shape,
        scratch_shapes=[pltpu.VMEM((8, LANES), F32)],
        compiler_params=_params(1), name="proj")(
            x2d, cos, sin, g, wa, bfp, gq, gkv, wq2, wukp, place, ones, rep)


def _fox_prep_kernel(plt_ref, g_ref):
    x = plt_ref[0]
    n = x.shape[1]
    lane = lax.broadcasted_iota(jnp.int32, x.shape, 1)
    shift = 1
    while shift < n:
        x = x + jnp.where(lane >= shift, pltpu.roll(x, shift, 1), 0.0)
        shift *= 2
    g_ref[0] = (x - x[:, n - 1:n]) * LOG2E


def _fox_prep(plt):
    b, h, p = plt.shape
    return pl.pallas_call(
        _fox_prep_kernel, grid=(b,),
        in_specs=[pl.BlockSpec((1, h, p), lambda i: (i, 0, 0))],
        out_specs=pl.BlockSpec((1, h, p), lambda i: (i, 0, 0)),
        out_shape=jax.ShapeDtypeStruct((b, h, p), F32),
        compiler_params=_params(1), name="fox_prep")(plt)


def _softmax_step(s, m_prev):
    m_new = jnp.maximum(m_prev, jnp.max(s, axis=-1, keepdims=True))
    p = jnp.exp2(s - _lanes(m_new, s.shape[1]))
    alpha = jnp.exp2(m_prev - m_new)
    return p.astype(BF), alpha, m_new


def _fox_qpad(qx, h):
    p, hh = divmod(h, 2)
    lane = lax.broadcasted_iota(jnp.int32, (qx.shape[0], 2 * LANES), 1)
    blk = qx[:, 2 * LANES * p:2 * LANES * (p + 1)]
    keep = ((lane >= FOX_DIM * hh) & (lane < FOX_DIM * (hh + 1))) | (
        (lane >= LANES + AUG * h) & (lane < LANES + AUG * (h + 1)))
    return jnp.where(keep, blk, jnp.zeros_like(blk))


def _fox_prompt_tile(kx, vb, qpad_ref, m_ref, l_ref, acc_ref, masked):
    tq = qpad_ref.shape[1]
    tk = kx.shape[0]
    first = lax.broadcasted_iota(jnp.int32, (tq, LANES), 1) < FOX_DIM
    ones = jnp.ones((tk, LANES), BF)
    if masked:
        causal = lax.broadcasted_iota(jnp.int32, (tq, tk), 1) <= lax.broadcasted_iota(jnp.int32, (tq, tk), 0)
    for p in range(N_PAIR):
        kxp = kx[:, 2 * LANES * p:2 * LANES * (p + 1)]
        vpa = jnp.concatenate([vb[:, LANES * p:LANES * (p + 1)], ones], axis=1)
        alphas, pvs = [], []
        for hh in range(2):
            h = 2 * p + hh
            s = _dot_nt(qpad_ref[h], kxp)
            if masked:
                s = jnp.where(causal, s, NEG)
            pb, alpha, m_new = _softmax_step(s, m_ref[h])
            pv = _dot(pb, vpa)
            l_ref[h] = alpha * l_ref[h] + pv[:, LANES:]
            m_ref[h] = m_new
            alphas.append(alpha)
            pvs.append(pv[:, :LANES])
        acc_ref[p] = acc_ref[p] * jnp.where(first, alphas[0], alphas[1]) + jnp.where(first, pvs[0], pvs[1])


def _fox_prompt_kernel(qx_ref, kx_ref, vb_ref, o_ref, qpad_ref, m_ref, l_ref, acc_ref):
    i = pl.program_id(1)
    j = pl.program_id(2)

    @pl.when(j == 0)
    def _():
        qx = qx_ref[...]
        for h in range(HEADS):
            qpad_ref[h] = _fox_qpad(qx, h)
        m_ref[...] = jnp.full(m_ref.shape, NEG, F32)
        l_ref[...] = jnp.zeros(l_ref.shape, F32)
        acc_ref[...] = jnp.zeros(acc_ref.shape, F32)

    @pl.when(j < i)
    def _():
        _fox_prompt_tile(kx_ref[...], vb_ref[...], qpad_ref, m_ref, l_ref, acc_ref, False)

    @pl.when(j == i)
    def _():
        _fox_prompt_tile(kx_ref[...], vb_ref[...], qpad_ref, m_ref, l_ref, acc_ref, True)
        first = lax.broadcasted_iota(jnp.int32, (qpad_ref.shape[1], LANES), 1) < FOX_DIM
        for p in range(N_PAIR):
            l_sel = jnp.where(first, l_ref[2 * p], l_ref[2 * p + 1])
            o_ref[:, LANES * p:LANES * (p + 1)] = (acc_ref[p] / l_sel).astype(o_ref.dtype)


def _fox_prompt(qx, kx, vb, *, batch, seq, t):
    nq = seq // t
    return pl.pallas_call(
        _fox_prompt_kernel, grid=(batch, nq, nq),
        in_specs=[pl.BlockSpec((t, 2 * FOX_W), lambda b, i, j: (b * nq + i, 0)),
                  pl.BlockSpec((t, 2 * FOX_W), lambda b, i, j: (b * nq + jnp.minimum(i, j), 0)),
                  pl.BlockSpec((t, FOX_W), lambda b, i, j: (b * nq + jnp.minimum(i, j), 0))],
        out_specs=pl.BlockSpec((t, FOX_W), lambda b, i, j: (b * nq + i, 0)),
        out_shape=jax.ShapeDtypeStruct((batch * seq, FOX_W), BF),
        scratch_shapes=[pltpu.VMEM((HEADS, t, 2 * LANES), BF), pltpu.VMEM((HEADS, t, LANES), F32),
                        pltpu.VMEM((HEADS, t, LANES), F32), pltpu.VMEM((N_PAIR, t, LANES), F32)],
        compiler_params=_params(3), name="fox_prompt")(qx, kx, vb)


def _fox_sample_kernel(qx_ref, kx_ref, vb_ref, fqs_ref, kct_ref, vct_ref, gk_ref, o_ref,
                       qbd_ref, qpad_ref, fq_ref, m_ref, l_ref, acc_ref, *, n_cache):
    j = pl.program_id(1)
    tq = qx_ref.shape[0]
    rows = HEADS * tq

    def update(s, pv_fn):
        m_prev = m_ref[...]
        m_new = jnp.maximum(m_prev, jnp.max(s, axis=-1, keepdims=True))
        p = jnp.exp2(s - _lanes(m_new, s.shape[1]))
        alpha = jnp.exp2(m_prev - m_new)
        l_ref[...] = alpha * l_ref[...] + jnp.sum(p, axis=-1, keepdims=True)
        m_ref[...] = m_new
        pb = p.astype(BF)
        for pr in range(N_PAIR):
            r0, r1 = 2 * tq * pr, 2 * tq * (pr + 1)
            acc_ref[r0:r1] = acc_ref[r0:r1] * alpha[r0:r1] + pv_fn(pr, pb[r0:r1])

    @pl.when(j == 0)
    def _():
        qx = qx_ref[...]
        fqs = fqs_ref[...]
        qc = jnp.concatenate([qx[:, 2 * LANES * p:2 * LANES * p + LANES] for p in range(N_PAIR)], axis=1)
        lane = lax.broadcasted_iota(jnp.int32, qc.shape, 1)
        for h in range(HEADS):
            qbd_ref[h * tq:(h + 1) * tq, :] = jnp.where((lane // FOX_DIM) == h, qc, jnp.zeros_like(qc))
            qpad_ref[h] = _fox_qpad(qx, h)
            fq_ref[h * tq:(h + 1) * tq, :] = jnp.broadcast_to(fqs[:, h:h + 1], (tq, LANES))
        m_ref[...] = jnp.full(m_ref.shape, NEG, F32)
        l_ref[...] = jnp.zeros(l_ref.shape, F32)
        acc_ref[...] = jnp.zeros(acc_ref.shape, F32)
        kx = kx_ref[...]
        vb = vb_ref[...]
        tk = kx.shape[0]
        s = jnp.concatenate([_dot_nt(qpad_ref[h], kx[:, 2 * LANES * (h // 2):2 * LANES * (h // 2 + 1)])
                             for h in range(HEADS)], axis=0)
        row = lax.broadcasted_iota(jnp.int32, (rows, tk), 0) & (tq - 1)
        s = jnp.where(lax.broadcasted_iota(jnp.int32, (rows, tk), 1) <= row, s, NEG)
        update(s, lambda pr, pb: _dot(pb, vb[:, LANES * pr:LANES * (pr + 1)]))

    @pl.when(j > 0)
    def _():
        tk = kct_ref.shape[3]
        kt = kct_ref[0].reshape(HEADS * FOX_DIM, tk).astype(BF)
        gk = gk_ref[0]
        gk_rows = jnp.concatenate([jnp.broadcast_to(gk[h:h + 1, :], (tq, tk)) for h in range(HEADS)], axis=0)
        s = _dot(qbd_ref[...], kt) + (_lanes(fq_ref[...], tk) - gk_rows)
        update(s, lambda pr, pb: _dot_nt(
            pb, vct_ref[0, 2 * pr:2 * pr + 2].reshape(2 * FOX_DIM, tk).astype(BF)))

    @pl.when(j == n_cache)
    def _():
        first = lax.broadcasted_iota(jnp.int32, (tq, LANES), 1) < FOX_DIM
        o = acc_ref[...] / l_ref[...]
        for pr in range(N_PAIR):
            o_ref[:, LANES * pr:LANES * (pr + 1)] = jnp.where(
                first, o[2 * tq * pr:2 * tq * pr + tq], o[2 * tq * pr + tq:2 * tq * (pr + 1)]).astype(o_ref.dtype)


def _fox_sample(qx, kx, vb, fqs, kct, vct, gk, *, batch, seq, tkc):
    past = kct.shape[3]
    n_cache = past // tkc
    kern = functools.partial(_fox_sample_kernel, n_cache=n_cache)
    row = lambda w: pl.BlockSpec((seq, w), lambda b, j: (b, 0))
    return pl.pallas_call(
        kern, grid=(batch, n_cache + 1),
        in_specs=[row(2 * FOX_W), row(2 * FOX_W), row(FOX_W), row(LANES),
                  pl.BlockSpec((1, HEADS, FOX_DIM, tkc), lambda b, j: (b, 0, 0, jnp.maximum(j - 1, 0))),
                  pl.BlockSpec((1, HEADS, FOX_DIM, tkc), lambda b, j: (b, 0, 0, jnp.maximum(j - 1, 0))),
                  pl.BlockSpec((1, HEADS, tkc), lambda b, j: (b, 0, jnp.maximum(j - 1, 0)))],
        out_specs=row(FOX_W),
        out_shape=jax.ShapeDtypeStruct((batch * seq, FOX_W), BF),
        scratch_shapes=[pltpu.VMEM((HEADS * seq, FOX_W), BF), pltpu.VMEM((HEADS, seq, 2 * LANES), BF)]
        + [pltpu.VMEM((HEADS * seq, LANES), F32)] * 4,
        compiler_params=_params(2), name="fox_sample")(qx, kx, vb, fqs, kct, vct, gk)


def _chunk_mask(shape, tq, row0, q0, k0):
    assert tq & (tq - 1) == 0 and CHUNK & (CHUNK - 1) == 0
    shift = CHUNK.bit_length() - 1
    qpos = q0 + ((row0 + lax.broadcasted_iota(jnp.int32, shape, 0)) & (tq - 1))
    kpos = k0 + lax.broadcasted_iota(jnp.int32, shape, 1)
    return lax.shift_right_logical(kpos, shift) <= lax.shift_right_logical(qpos, shift)


def _mla_rows_update(s, val, m_ref, l_ref, acc_ref, r0, r1):
    m_prev = m_ref[r0:r1]
    m_new = jnp.maximum(m_prev, jnp.max(s, axis=-1, keepdims=True))
    p = jnp.exp2(s - _lanes(m_new, s.shape[1]))
    alpha = jnp.exp2(m_prev - m_new)
    l_ref[r0:r1] = alpha * l_ref[r0:r1] + jnp.sum(p, axis=-1, keepdims=True)
    m_ref[r0:r1] = m_new
    acc_ref[r0:r1] = acc_ref[r0:r1] * _lanes(alpha, KV_LORA) + _dot(p.astype(BF), val)


def _mla_init_stats(m_ref, l_ref, acc_ref):
    m_ref[...] = jnp.full(m_ref.shape, NEG, F32)
    l_ref[...] = jnp.zeros(l_ref.shape, F32)
    acc_ref[...] = jnp.zeros(acc_ref.shape, F32)


def _mla_finalize(o_ref, wuv_ref, l_ref, acc_ref, tq):
    olat = (acc_ref[...] / _lanes(l_ref[...], KV_LORA)).astype(BF)
    for g in range(2):
        out = None
        for h in range(4 * g, 4 * g + 4):
            part = _dot(olat[h * tq:(h + 1) * tq, :], wuv_ref[h])
            out = part if out is None else out + part
        o_ref[:, 256 * g:256 * (g + 1)] = out.astype(o_ref.dtype)


def _mla_prompt_kernel(qlat_ref, qrope_ref, mkey_ref, wuv_ref, o_ref, qcat_ref, m_ref, l_ref, acc_ref,
                       *, tq, tk, rc):
    i = pl.program_id(1)
    j = pl.program_id(2)
    last_j = (tq * (i + 1) - 1) // tk
    needs_mask = tk * (j + 1) > tq * i + CHUNK

    @pl.when(j == 0)
    def _():
        qrope = qrope_ref[...]
        lane = lax.broadcasted_iota(jnp.int32, (tq, HEADS * MLA_ROPE), 1)
        for h in range(HEADS):
            qcat_ref[h * tq:(h + 1) * tq, :KV_LORA] = qlat_ref[h]
            qcat_ref[h * tq:(h + 1) * tq, KV_LORA:] = jnp.where((lane // MLA_ROPE) == h, qrope,
                                                                jnp.zeros_like(qrope))
        _mla_init_stats(m_ref, l_ref, acc_ref)

    def tile(masked):
        mk = mkey_ref[...]
        for r0 in range(0, HEADS * tq, rc):
            s = _dot_nt(qcat_ref[r0:r0 + rc], mk)
            if masked:
                s = jnp.where(_chunk_mask(s.shape, tq, r0, tq * i, tk * j), s, NEG)
            _mla_rows_update(s, mk[:, :KV_LORA], m_ref, l_ref, acc_ref, r0, r0 + rc)

    @pl.when((j <= last_j) & jnp.logical_not(needs_mask))
    def _():
        tile(False)

    @pl.when((j <= last_j) & needs_mask)
    def _():
        tile(True)

    @pl.when(j == last_j)
    def _():
        _mla_finalize(o_ref, wuv_ref, l_ref, acc_ref, tq)


def _mla_scratch(rows):
    return [pltpu.VMEM((rows, LANES), F32), pltpu.VMEM((rows, LANES), F32), pltpu.VMEM((rows, KV_LORA), F32)]


def _mla_prompt(qlat, qrope, mkey, wuvp, *, batch, seq, tq, tk, rc):
    nq = seq // tq
    nk = seq // tk

    def key_map(b, i, j):
        return (b * nk + jnp.minimum(j, (tq * (i + 1) - 1) // tk), 0)

    kern = functools.partial(_mla_prompt_kernel, tq=tq, tk=tk, rc=rc)
    return pl.pallas_call(
        kern, grid=(batch, nq, nk),
        in_specs=[pl.BlockSpec((HEADS, tq, KV_LORA), lambda b, i, j: (0, b * nq + i, 0)),
                  pl.BlockSpec((tq, HEADS * MLA_ROPE), lambda b, i, j: (b * nq + i, 0)),
                  pl.BlockSpec((tk, 2 * KV_LORA), key_map),
                  _const_spec(wuvp.shape)],
        out_specs=pl.BlockSpec((tq, FOX_W), lambda b, i, j: (b * nq + i, 0)),
        out_shape=jax.ShapeDtypeStruct((batch * seq, FOX_W), BF),
        scratch_shapes=[pltpu.VMEM((HEADS * tq, 2 * KV_LORA), BF)] + _mla_scratch(HEADS * tq),
        compiler_params=_params(3), name="mla_prompt")(qlat, qrope, mkey, wuvp)


def _mla_sample_kernel(qlat_ref, qrope_ref, mkey_ref, latc_ref, krt_ref, wuv_ref, o_ref,
                       ql_ref, qr_ref, m_ref, l_ref, acc_ref, *, n_cache, tq, past, rc):
    j = pl.program_id(1)
    rows = HEADS * tq

    @pl.when(j == 0)
    def _():
        qrope = qrope_ref[...]
        for h in range(HEADS):
            ql_ref[h * tq:(h + 1) * tq, :] = qlat_ref[h]
            qr_ref[h * tq:(h + 1) * tq, :] = qrope[:, MLA_ROPE * h:MLA_ROPE * (h + 1)]
        _mla_init_stats(m_ref, l_ref, acc_ref)
        mk = mkey_ref[...]
        s = _dot_nt(ql_ref[...], mk[:, :KV_LORA]) + _dot_nt(qr_ref[...], mk[:, KV_LORA:KV_LORA + MLA_ROPE])
        if past % CHUNK != 0 or tq > CHUNK:
            s = jnp.where(_chunk_mask(s.shape, tq, 0, past, past), s, NEG)
        _mla_rows_update(s, mk[:, :KV_LORA], m_ref, l_ref, acc_ref, 0, rows)

    @pl.when(j > 0)
    def _():
        c = latc_ref[0].astype(BF)
        krt = krt_ref[0].astype(BF)
        for r0 in range(0, rows, rc):
            s = _dot_nt(ql_ref[r0:r0 + rc], c) + _dot(qr_ref[r0:r0 + rc], krt)
            _mla_rows_update(s, c, m_ref, l_ref, acc_ref, r0, r0 + rc)

    @pl.when(j == n_cache)
    def _():
        _mla_finalize(o_ref, wuv_ref, l_ref, acc_ref, tq)


def _mla_sample(qlat, qrope, mkey, latc, krt, wuvp, *, batch, seq, tkc, rc):
    past = latc.shape[1]
    n_cache = past // tkc
    kern = functools.partial(_mla_sample_kernel, n_cache=n_cache, tq=seq, past=past, rc=rc)
    return pl.pallas_call(
        kern, grid=(batch, n_cache + 1),
        in_specs=[pl.BlockSpec((HEADS, seq, KV_LORA), lambda b, j: (0, b, 0)),
                  pl.BlockSpec((seq, HEADS * MLA_ROPE), lambda b, j: (b, 0)),
                  pl.BlockSpec((seq, 2 * KV_LORA), lambda b, j: (b, 0)),
                  pl.BlockSpec((1, tkc, KV_LORA), lambda b, j: (b, jnp.maximum(j - 1, 0), 0)),
                  pl.BlockSpec((1, MLA_ROPE, tkc), lambda b, j: (b, 0, jnp.maximum(j - 1, 0))),
                  _const_spec(wuvp.shape)],
        out_specs=pl.BlockSpec((seq, FOX_W), lambda b, j: (b, 0)),
        out_shape=jax.ShapeDtypeStruct((batch * seq, FOX_W), BF),
        scratch_shapes=[pltpu.VMEM((HEADS * seq, KV_LORA), BF), pltpu.VMEM((HEADS * seq, MLA_ROPE), BF)]
        + _mla_scratch(HEADS * seq),
        compiler_params=_params(2), name="mla_sample")(qlat, qrope, mkey, latc, krt, wuvp)


def _ffn_kernel(x_ref, fox_ref, mla_ref, pconv_ref, wo_ref, gf_ref, wup_ref, cw_ref, wdn_ref, gfin_ref,
                y_ref, cst_ref, hn_ref, act_ref, ush_ref, carry_ref, *, tm, seq_len):
    n_seg = max(1, tm // seq_len)
    seg = tm // n_seg
    mixed = jnp.concatenate([fox_ref[...], mla_ref[...]], axis=1)
    x1 = x_ref[...] + _dot(mixed, wo_ref[...])
    hn_ref[...] = (x1 * lax.rsqrt(jnp.mean(x1 * x1, axis=-1, keepdims=True) + EPS) * gf_ref[...]).astype(BF)
    y_ref[...] = x1

    if n_seg == 1:
        @pl.when(pl.program_id(0) % (seq_len // tm) == 0)
        def _():
            carry_ref[...] = pconv_ref[0]

    def conv(u, col, slot):
        cols = slice(col, col + FF_CHUNK)
        for s in range(n_seg):
            ush_ref[slot, s, 6:8, :] = carry_ref[:, cols] if n_seg == 1 else pconv_ref[s, :, cols]
            ush_ref[slot, s, 8:8 + seg, :] = u[seg * s:seg * (s + 1), :]
            cst_ref[s, :, cols] = u[seg * (s + 1) - 2:seg * (s + 1), :]
        if n_seg == 1:
            carry_ref[:, cols] = u[tm - 2:tm, :]
        u1 = jnp.concatenate([ush_ref[slot, s, 7:7 + seg, :] for s in range(n_seg)], axis=0)
        u2 = jnp.concatenate([ush_ref[slot, s, 6:6 + seg, :] for s in range(n_seg)], axis=0)
        cw = cw_ref[:, cols]
        return cw[3:4, :] + cw[0:1, :] * u2 + cw[1:2, :] * u1 + cw[2:3, :] * u

    for c in range(N_FF):
        hn = hn_ref[...]
        g0, v0 = FF_CHUNK * c, D_FF + FF_CHUNK * c
        gate = conv(_dot(hn, wup_ref[:, g0:g0 + FF_CHUNK]), g0, 2 * (c % 2))
        val = conv(_dot(hn, wup_ref[:, v0:v0 + FF_CHUNK]), v0, 2 * (c % 2) + 1)
        act_ref[:, FF_CHUNK * c:FF_CHUNK * (c + 1)] = (gate * jax.nn.sigmoid(gate) * val).astype(BF)

    x2 = y_ref[...] + _dot(act_ref[...], wdn_ref[...])
    y_ref[...] = x2 * lax.rsqrt(jnp.mean(x2 * x2, axis=-1, keepdims=True) + EPS) * gfin_ref[...]


def _ffn(x2d, fox, mla, pconv, wo, gf, wup, cw, wdn, gfin, *, seq_len, tm):
    n = x2d.shape[0]
    nt = n // tm
    n_seq = n // seq_len
    n_seg = max(1, tm // seq_len)
    if n_seg == 1:
        tps = seq_len // tm
        st_map = lambda i: (i // tps, 0, 0)
    else:
        st_map = lambda i: (i, 0, 0)
    row = lambda w: pl.BlockSpec((tm, w), lambda i: (i, 0))
    kern = functools.partial(_ffn_kernel, tm=tm, seq_len=seq_len)
    return pl.pallas_call(
        kern, grid=(nt,),
        in_specs=[row(D_MODEL), row(FOX_W), row(FOX_W),
                  pl.BlockSpec((n_seg, CONV_W - 1, 2 * D_FF), st_map),
                  _const_spec(wo.shape), _const_spec(gf.shape), _const_spec(wup.shape), _const_spec(cw.shape),
                  _const_spec(wdn.shape), _const_spec(gfin.shape)],
        out_specs=(row(D_MODEL), pl.BlockSpec((n_seg, CONV_W - 1, 2 * D_FF), st_map)),
        out_shape=(jax.ShapeDtypeStruct((n, D_MODEL), F32),
                   jax.ShapeDtypeStruct((n_seq, CONV_W - 1, 2 * D_FF), F32)),
        scratch_shapes=[pltpu.VMEM((tm, D_MODEL), BF), pltpu.VMEM((tm, D_FF), BF),
                        pltpu.VMEM((4, n_seg, 8 + tm // n_seg, FF_CHUNK), F32),
                        pltpu.VMEM((CONV_W - 1, 2 * D_FF), F32)],
        compiler_params=_params(1), name="ffn")(x2d, fox, mla, pconv, wo, gf, wup, cw, wdn, gfin)


def _rope_tables(pos):
    half = MLA_ROPE // 2
    inv = ROPE_THETA ** (-jnp.arange(half, dtype=F32) / half)
    ang = pos.astype(F32)[:, None] * inv[None, :]
    cos = jnp.tile(jnp.cos(ang), (1, 2 * HEADS))
    sin = jnp.tile(jnp.sin(ang), (1, 2 * HEADS))
    return cos, sin


def _placement():
    place = np.zeros((LANES, 2 * N_PAIR * LANES), np.float32)
    ones = np.zeros((1, 2 * N_PAIR * LANES), np.float32)
    for h in range(HEADS):
        for i in range(3):
            for p in range(N_PAIR):
                place[HEADS * i + h, LANES * p + AUG * h + i] = 1.0
                ones[0, LANES * p + AUG * h + 3 + i] = 1.0
                ones[0, N_PAIR * LANES + LANES * p + AUG * h + i] = 1.0
                place[HEADS * i + h, N_PAIR * LANES + LANES * p + AUG * h + 3 + i] = -1.0
    rep = np.zeros((LANES, HEADS * MLA_ROPE), np.float32)
    for h in range(HEADS):
        for r in range(MLA_ROPE):
            rep[r, MLA_ROPE * h + r] = 1.0
    return jnp.asarray(place, BF), jnp.asarray(ones, F32), jnp.asarray(rep, BF)


def _rot_cols(w):
    half = MLA_ROPE // 2
    return jnp.concatenate([-w[..., half:], w[..., :half]], axis=-1)


def _pad_cols(w, width):
    return jnp.pad(w, ((0, 0), (0, width - w.shape[1])))


def _layer_weights(w_in, b_f, w_q_up, w_uk, w_uv, w_out, w_up, conv_w, conv_b, w_down):
    wq, wk, wv, wf, wqc, wckv, wkr = jnp.split(w_in, IN_SPLITS, axis=1)
    wa = jnp.concatenate([wq, wk, wv, wqc, wckv, _pad_cols(wkr, LANES), _pad_cols(_rot_cols(wkr), LANES),
                          _pad_cols(wf, LANES)], axis=1).astype(BF)
    bfp = _pad_cols(b_f[None, :], LANES).astype(F32)
    wq3 = w_q_up.reshape(Q_LORA, HEADS, MLA_NOPE + MLA_ROPE)
    wq_nope = wq3[:, :, :MLA_NOPE].reshape(Q_LORA, HEADS * MLA_NOPE)
    wq_rope = wq3[:, :, MLA_NOPE:]
    wq2 = jnp.concatenate([wq_nope, wq_rope.reshape(Q_LORA, -1), _rot_cols(wq_rope).reshape(Q_LORA, -1)],
                          axis=1).astype(BF)
    wukp = jnp.zeros((HEADS, 4 * MLA_NOPE, KV_LORA), F32)
    wuvp = jnp.zeros((HEADS, KV_LORA, 4 * MLA_VDIM), F32)
    for h in range(HEADS):
        o = MLA_NOPE * (h % 4)
        wukp = wukp.at[h, o:o + MLA_NOPE, :].set(w_uk[:, h, :].T)
        wuvp = wuvp.at[h, :, o:o + MLA_VDIM].set(w_uv[:, h, :])
    wup = w_up.astype(BF)
    cw = jnp.concatenate([conv_w, conv_b[None, :]], axis=0)
    wdn = w_down.astype(BF)
    return wa, bfp, wq2, wukp.astype(BF), wuvp.astype(BF), w_out.astype(BF), wup, cw, wdn


def kernel(x_prompt, x_sample, cache_fox_k, cache_fox_v, cache_fox_logf, cache_mla_latent, cache_mla_krope,
           state_ffn_conv, attn_norm, w_in, b_forget, q_norm, w_q_up, kv_norm, w_uk, w_uv, w_out, ffn_norm,
           w_up, conv_w, conv_b, w_down, final_norm):
    assert attn_norm.shape[0] == 1, "single-layer stack"
    bp, tp, _ = x_prompt.shape
    bs, ts, _ = x_sample.shape
    past = cache_fox_k.shape[2]
    assert ts == CHUNK and past % CHUNK == 0

    wa, bfp, wq2, wukp, wuvp, wo, wup, cw, wdn = _layer_weights(
        w_in[0], b_forget[0], w_q_up[0], w_uk[0], w_uv[0], w_out[0], w_up[0], conv_w[0], conv_b[0], w_down[0])
    place, ones, rep = _placement()
    g_attn = attn_norm[0][None, :]
    g_q = q_norm[0][None, :]
    g_kv = kv_norm[0][None, :]
    g_ffn = ffn_norm[0][None, :]
    g_fin = final_norm[None, :]

    tm = 512
    outs = []
    for grp in ("prompt", "sample"):
        if grp == "prompt":
            x2d = x_prompt.reshape(bp * tp, D_MODEL)
            b, t = bp, tp
            cos, sin = _rope_tables(jnp.arange(tp, dtype=jnp.int32))
            pconv = jnp.zeros((bp, CONV_W - 1, 2 * D_FF), F32)
        else:
            x2d = x_sample.reshape(bs * ts, D_MODEL)
            b, t = bs, ts
            cos, sin = _rope_tables(past + jnp.arange(ts, dtype=jnp.int32))
            cos, sin = jnp.tile(cos, (tm // ts, 1)), jnp.tile(sin, (tm // ts, 1))
            pconv = state_ffn_conv[0]

        (k, v, logf, lat, kr, qx, kx, vb, mkey, qlat, qrope, fqs) = _proj(
            x2d, cos, sin, g_attn, wa, bfp, g_q, g_kv, wq2, wukp, place, ones, rep, seq_len=t, tm=tm)

        if grp == "prompt":
            fox = _fox_prompt(qx, kx, vb, batch=b, seq=t, t=512)
            mla = _mla_prompt(qlat, qrope, mkey, wuvp, batch=b, seq=t, tq=512, tk=512, rc=512)
        else:
            gk = _fox_prep(jnp.transpose(cache_fox_logf[0], (0, 2, 1)))
            kct = jnp.transpose(cache_fox_k[0], (0, 2, 3, 1))
            vct = jnp.transpose(cache_fox_v[0], (0, 2, 3, 1))
            krt = jnp.transpose(cache_mla_krope[0], (0, 2, 1))
            fox = _fox_sample(qx, kx, vb, fqs, kct, vct, gk, batch=b, seq=t, tkc=min(past, 1024))
            mla = _mla_sample(qlat, qrope, mkey, cache_mla_latent[0], krt, wuvp, batch=b, seq=t,
                              tkc=min(past, 1024), rc=256)

        y, cst = _ffn(x2d, fox, mla, pconv, wo, g_ffn, wup, cw, wdn, g_fin, seq_len=t, tm=tm)
        outs.append((y.reshape(b, t, D_MODEL),
                     k.reshape(1, b, t, HEADS, FOX_DIM), v.reshape(1, b, t, HEADS, FOX_DIM),
                     logf.reshape(1, b, t, HEADS), lat.reshape(1, b, t, KV_LORA), kr.reshape(1, b, t, MLA_ROPE),
                     cst[None]))
    (yp, *st_p), (ys, *st_s) = outs
    return (yp, ys, *st_p, *st_s)
```

```python
import functools

import numpy as np
import jax
import jax.numpy as jnp
from jax import lax
from jax.experimental import pallas as pl
from jax.experimental.pallas import tpu as pltpu

D_MODEL = 1024
CHUNK = 64
HEADS = 8
FOX_DIM = 64
MLA_NOPE = 64
MLA_ROPE = 32
MLA_VDIM = 64
Q_LORA = 384
KV_LORA = 256
D_FF = 2816
CONV_W = 3
ROPE_THETA = 10000.0
EPS = 1e-6
NEG = -1e30
LOG2E = 1.4426950408889634

FOX_W = HEADS * FOX_DIM
IN_SIZES = [FOX_W, FOX_W, FOX_W, HEADS, Q_LORA, KV_LORA, MLA_ROPE]
IN_SPLITS = [int(s) for s in np.cumsum(IN_SIZES)[:-1]]

LANES = 128
AUG = 16
N_PAIR = HEADS // 2
FF_CHUNK = 256
N_FF = D_FF // FF_CHUNK
VMEM_LIMIT = 56 * 1024 * 1024

C_Q, C_K, C_V = 0, 512, 1024
C_QC = 1536
C_CKV = 1920
C_KR = 2176
C_KRR = 2304
C_F = 2432
N_A = 2560

BF = jnp.bfloat16
F32 = jnp.float32


def _dot(a, b):
    return jnp.dot(a, b, preferred_element_type=F32)


def _dot_nt(a, b):
    return lax.dot_general(a, b, (((1,), (1,)), ((), ())), preferred_element_type=F32)


def _split3(x):
    hi = x.astype(BF)
    r1 = x - hi.astype(F32)
    mid = r1.astype(BF)
    lo = (r1 - mid.astype(F32)).astype(BF)
    return hi, mid, lo


def _log_sigmoid(x):
    return jnp.minimum(x, 0.0) - jnp.log1p(jnp.exp(-jnp.abs(x)))


def _lanes(x, n):
    if n % LANES == 0:
        return x if n == LANES else jnp.concatenate([x] * (n // LANES), axis=1)
    assert n < LANES
    return x[:, :n]


def _params(n_axes):
    return pltpu.CompilerParams(dimension_semantics=("arbitrary",) * n_axes, vmem_limit_bytes=VMEM_LIMIT)


def _const_spec(shape):
    nd = len(shape)
    return pl.BlockSpec(shape, lambda *_: (0,) * nd, pipeline_mode=pl.Buffered(1))


def _proj_kernel(x_ref, cos_ref, sin_ref, g_ref, wa_ref, bf_ref, gq_ref, gkv_ref, wq2_ref,
                 wuk_ref, place_ref, ones_ref, rep_ref,
                 k_ref, v_ref, logf_ref, lat_ref, kr_ref, qx_ref, kx_ref, vb_ref, mkey_ref, qlat_ref,
                 qrope_ref, fqs_ref, carry_ref, *, tm, seq_len, mla_scale):
    x = x_ref[...]
    ms = jnp.mean(x * x, axis=-1, keepdims=True)
    hn = (x * lax.rsqrt(ms + EPS) * g_ref[...]).astype(BF)
    z = _dot(hn, wa_ref[...])

    zk = z[:, C_K:C_K + FOX_W]
    zv = z[:, C_V:C_V + FOX_W]
    for h in range(HEADS):
        k_ref[pl.ds(h, tm, stride=HEADS), :] = zk[:, FOX_DIM * h:FOX_DIM * (h + 1)]
        v_ref[pl.ds(h, tm, stride=HEADS), :] = zv[:, FOX_DIM * h:FOX_DIM * (h + 1)]
    vb_ref[...] = zv.astype(BF)

    lane = lax.broadcasted_iota(jnp.int32, (tm, LANES), 1)
    logf = _log_sigmoid(z[:, C_F:C_F + LANES] + bf_ref[...])
    logf = jnp.where(lane < HEADS, logf, 0.0)
    logf_ref[...] = logf[:, :HEADS]
    row = lax.broadcasted_iota(jnp.int32, (tm, tm), 0)
    col = lax.broadcasted_iota(jnp.int32, (tm, tm), 1)
    keep = col <= row
    if seq_len < tm:
        keep = keep & ((col // seq_len) == (row // seq_len))
    tri = jnp.where(keep, 1.0, 0.0).astype(BF)
    hi, mid, lo = _split3(logf)
    fcum = _dot(tri, hi) + _dot(tri, mid) + _dot(tri, lo)
    if seq_len > tm:
        @pl.when(pl.program_id(0) % (seq_len // tm) == 0)
        def _():
            carry_ref[...] = jnp.zeros_like(carry_ref)
        fcum = fcum + carry_ref[0:1, :]
        carry_ref[0:1, :] = fcum[tm - 1:tm, :]

    fqs = fcum * LOG2E
    fqs_ref[...] = fqs
    hi, mid, lo = _split3(fqs)
    comb = hi.astype(F32) + pltpu.roll(mid.astype(F32), HEADS, 1) + pltpu.roll(lo.astype(F32), 2 * HEADS, 1)
    aug = _dot(comb.astype(BF), place_ref[...]) + ones_ref[...]
    zq = z[:, C_Q:C_Q + FOX_W] * (FOX_DIM ** -0.5 * LOG2E)
    for p in range(N_PAIR):
        lo_, hi_ = 2 * LANES * p, 2 * LANES * p + LANES
        qx_ref[:, lo_:hi_] = zq[:, LANES * p:LANES * (p + 1)].astype(BF)
        qx_ref[:, hi_:hi_ + LANES] = aug[:, LANES * p:LANES * (p + 1)].astype(BF)
        kx_ref[:, lo_:hi_] = zk[:, LANES * p:LANES * (p + 1)].astype(BF)
        kx_ref[:, hi_:hi_ + LANES] = aug[:, FOX_W + LANES * p:FOX_W + LANES * (p + 1)].astype(BF)

    zc = z[:, C_CKV:C_CKV + KV_LORA]
    ckv = zc * lax.rsqrt(jnp.mean(zc * zc, axis=-1, keepdims=True) + EPS) * gkv_ref[...]
    lat_ref[...] = ckv
    cos = cos_ref[...]
    sin = sin_ref[...]
    krb = z[:, C_KR:C_KR + LANES] * cos[:, :LANES] + z[:, C_KRR:C_KRR + LANES] * sin[:, :LANES]
    kr_ref[...] = krb[:, :MLA_ROPE]
    mkey_ref[:, :KV_LORA] = ckv.astype(BF)
    mkey_ref[:, KV_LORA:] = _dot(krb.astype(BF), rep_ref[...]).astype(BF)

    zqc = z[:, C_QC:C_QC + Q_LORA]
    qcn = (zqc * lax.rsqrt(jnp.mean(zqc * zqc, axis=-1, keepdims=True) + EPS) * gq_ref[...]).astype(BF)
    zq2 = _dot(qcn, wq2_ref[...])
    nr = HEADS * MLA_ROPE
    qr = zq2[:, FOX_W:FOX_W + nr] * cos + zq2[:, FOX_W + nr:FOX_W + 2 * nr] * sin
    qrope_ref[...] = (qr * mla_scale).astype(BF)
    for h in range(HEADS):
        g = h // 4
        qn = zq2[:, 256 * g:256 * (g + 1)].astype(BF)
        qlat_ref[h] = (_dot(qn, wuk_ref[h]) * mla_scale).astype(BF)


def _proj(x2d, cos, sin, g, wa, bfp, gq, gkv, wq2, wukp, place, ones, rep, *, seq_len, tm):
    n = x2d.shape[0]
    nt = n // tm
    if seq_len >= tm:
        tps = seq_len // tm
        tab_map = lambda i: (i % tps, 0)
    else:
        tab_map = lambda i: (0, 0)
    row = lambda w: pl.BlockSpec((tm, w), lambda i: (i, 0))
    out_shape = (
        jax.ShapeDtypeStruct((n * HEADS, FOX_DIM), F32),
        jax.ShapeDtypeStruct((n * HEADS, FOX_DIM), F32),
        jax.ShapeDtypeStruct((n, HEADS), F32),
        jax.ShapeDtypeStruct((n, KV_LORA), F32),
        jax.ShapeDtypeStruct((n, MLA_ROPE), F32),
        jax.ShapeDtypeStruct((n, 2 * FOX_W), BF),
        jax.ShapeDtypeStruct((n, 2 * FOX_W), BF),
        jax.ShapeDtypeStruct((n, FOX_W), BF),
        jax.ShapeDtypeStruct((n, 2 * KV_LORA), BF),
        jax.ShapeDtypeStruct((HEADS, n, KV_LORA), BF),
        jax.ShapeDtypeStruct((n, HEADS * MLA_ROPE), BF),
        jax.ShapeDtypeStruct((n, LANES), F32),
    )
    kv_spec = pl.BlockSpec((tm * HEADS, FOX_DIM), lambda i: (i, 0))
    out_specs = (kv_spec, kv_spec, row(HEADS), row(KV_LORA), row(MLA_ROPE), row(2 * FOX_W),
                 row(2 * FOX_W), row(FOX_W), row(2 * KV_LORA),
                 pl.BlockSpec((HEADS, tm, KV_LORA), lambda i: (0, i, 0)), row(HEADS * MLA_ROPE), row(LANES))
    in_specs = [row(D_MODEL),
                pl.BlockSpec((tm, 2 * LANES), tab_map), pl.BlockSpec((tm, 2 * LANES), tab_map),
                _const_spec(g.shape), _const_spec(wa.shape), _const_spec(bfp.shape), _const_spec(gq.shape),
                _const_spec(gkv.shape), _const_spec(wq2.shape), _const_spec(wukp.shape),
                _const_spec(place.shape), _const_spec(ones.shape), _const_spec(rep.shape)]
    kern = functools.partial(_proj_kernel, tm=tm, seq_len=seq_len,
                             mla_scale=float((MLA_NOPE + MLA_ROPE) ** -0.5 * LOG2E))
    return pl.pallas_call(
        kern, grid=(nt,), in_specs=in_specs, out_specs=out_specs, out_shape=out_shape,
        scratch_shapes=[pltpu.VMEM((8, LANES), F32)],
        compiler_params=_params(1), name="proj")(
            x2d, cos, sin, g, wa, bfp, gq, gkv, wq2, wukp, place, ones, rep)


def _fox_prep_kernel(plt_ref, g_ref):
    x = plt_ref[0]
    n = x.shape[1]
    lane = lax.broadcasted_iota(jnp.int32, x.shape, 1)
    shift = 1
    while shift < n:
        x = x + jnp.where(lane >= shift, pltpu.roll(x, shift, 1), 0.0)
        shift *= 2
    g_ref[0] = (x - x[:, n - 1:n]) * LOG2E


def _fox_prep(plt):
    b, h, p = plt.shape
    return pl.pallas_call(
        _fox_prep_kernel, grid=(b,),
        in_specs=[pl.BlockSpec((1, h, p), lambda i: (i, 0, 0))],
        out_specs=pl.BlockSpec((1, h, p), lambda i: (i, 0, 0)),
        out_shape=jax.ShapeDtypeStruct((b, h, p), F32),
        compiler_params=_params(1), name="fox_prep")(plt)


def _softmax_step(s, m_prev):
    m_new = jnp.maximum(m_prev, jnp.max(s, axis=-1, keepdims=True))
    p = jnp.exp2(s - _lanes(m_new, s.shape[1]))
    alpha = jnp.exp2(m_prev - m_new)
    return p.astype(BF), alpha, m_new


def _fox_qpad(qx, h):
    p, hh = divmod(h, 2)
    lane = lax.broadcasted_iota(jnp.int32, (qx.shape[0], 2 * LANES), 1)
    blk = qx[:, 2 * LANES * p:2 * LANES * (p + 1)]
    keep = ((lane >= FOX_DIM * hh) & (lane < FOX_DIM * (hh + 1))) | (
        (lane >= LANES + AUG * h) & (lane < LANES + AUG * (h + 1)))
    return jnp.where(keep, blk, jnp.zeros_like(blk))


def _fox_prompt_tile(kx, vb, qpad_ref, m_ref, l_ref, acc_ref, masked):
    tq = qpad_ref.shape[1]
    tk = kx.shape[0]
    first = lax.broadcasted_iota(jnp.int32, (tq, LANES), 1) < FOX_DIM
    ones = jnp.ones((tk, LANES), BF)
    if masked:
        causal = lax.broadcasted_iota(jnp.int32, (tq, tk), 1) <= lax.broadcasted_iota(jnp.int32, (tq, tk), 0)
    def scores(h):
        return _dot_nt(qpad_ref[h], kx[:, 2 * LANES * (h // 2):2 * LANES * (h // 2 + 1)])

    s_next = scores(0)
    for p in range(N_PAIR):
        vpa = jnp.concatenate([vb[:, LANES * p:LANES * (p + 1)], ones], axis=1)
        alphas, pvs = [], []
        for hh in range(2):
            h = 2 * p + hh
            s = s_next
            if h + 1 < HEADS:
                s_next = scores(h + 1)
            if masked:
                s = jnp.where(causal, s, NEG)
            pb, alpha, m_new = _softmax_step(s, m_ref[h])
            pv = _dot(pb, vpa)
            l_ref[h] = alpha * l_ref[h] + pv[:, LANES:]
            m_ref[h] = m_new
            alphas.append(alpha)
            pvs.append(pv[:, :LANES])
        acc_ref[p] = acc_ref[p] * jnp.where(first, alphas[0], alphas[1]) + jnp.where(first, pvs[0], pvs[1])


def _fox_prompt_kernel(qx_ref, kx_ref, vb_ref, o_ref, qpad_ref, m_ref, l_ref, acc_ref):
    i = pl.program_id(1)
    j = pl.program_id(2)

    @pl.when(j == 0)
    def _():
        qx = qx_ref[...]
        for h in range(HEADS):
            qpad_ref[h] = _fox_qpad(qx, h)
        m_ref[...] = jnp.full(m_ref.shape, NEG, F32)
        l_ref[...] = jnp.zeros(l_ref.shape, F32)
        acc_ref[...] = jnp.zeros(acc_ref.shape, F32)

    @pl.when(j < i)
    def _():
        _fox_prompt_tile(kx_ref[...], vb_ref[...], qpad_ref, m_ref, l_ref, acc_ref, False)

    @pl.when(j == i)
    def _():
        _fox_prompt_tile(kx_ref[...], vb_ref[...], qpad_ref, m_ref, l_ref, acc_ref, True)
        first = lax.broadcasted_iota(jnp.int32, (qpad_ref.shape[1], LANES), 1) < FOX_DIM
        for p in range(N_PAIR):
            l_sel = jnp.where(first, l_ref[2 * p], l_ref[2 * p + 1])
            o_ref[:, LANES * p:LANES * (p + 1)] = (acc_ref[p] / l_sel).astype(o_ref.dtype)


def _fox_prompt(qx, kx, vb, *, batch, seq, t):
    nq = seq // t
    return pl.pallas_call(
        _fox_prompt_kernel, grid=(batch, nq, nq),
        in_specs=[pl.BlockSpec((t, 2 * FOX_W), lambda b, i, j: (b * nq + i, 0)),
                  pl.BlockSpec((t, 2 * FOX_W), lambda b, i, j: (b * nq + jnp.minimum(i, j), 0)),
                  pl.BlockSpec((t, FOX_W), lambda b, i, j: (b * nq + jnp.minimum(i, j), 0))],
        out_specs=pl.BlockSpec((t, FOX_W), lambda b, i, j: (b * nq + i, 0)),
        out_shape=jax.ShapeDtypeStruct((batch * seq, FOX_W), BF),
        scratch_shapes=[pltpu.VMEM((HEADS, t, 2 * LANES), BF), pltpu.VMEM((HEADS, t, LANES), F32),
                        pltpu.VMEM((HEADS, t, LANES), F32), pltpu.VMEM((N_PAIR, t, LANES), F32)],
        compiler_params=_params(3), name="fox_prompt")(qx, kx, vb)


def _fox_sample_kernel(qx_ref, kx_ref, vb_ref, fqs_ref, kct_ref, vct_ref, gk_ref, o_ref,
                       qbd_ref, qpad_ref, fq_ref, m_ref, l_ref, acc_ref, *, n_cache):
    j = pl.program_id(1)
    tq = qx_ref.shape[0]
    rows = HEADS * tq

    def update(s, pv_fn):
        m_prev = m_ref[...]
        m_new = jnp.maximum(m_prev, jnp.max(s, axis=-1, keepdims=True))
        p = jnp.exp2(s - _lanes(m_new, s.shape[1]))
        alpha = jnp.exp2(m_prev - m_new)
        l_ref[...] = alpha * l_ref[...] + jnp.sum(p, axis=-1, keepdims=True)
        m_ref[...] = m_new
        pb = p.astype(BF)
        for pr in range(N_PAIR):
            r0, r1 = 2 * tq * pr, 2 * tq * (pr + 1)
            acc_ref[r0:r1] = acc_ref[r0:r1] * alpha[r0:r1] + pv_fn(pr, pb[r0:r1])

    @pl.when(j == 0)
    def _():
        qx = qx_ref[...]
        fqs = fqs_ref[...]
        qc = jnp.concatenate([qx[:, 2 * LANES * p:2 * LANES * p + LANES] for p in range(N_PAIR)], axis=1)
        lane = lax.broadcasted_iota(jnp.int32, qc.shape, 1)
        for h in range(HEADS):
            qbd_ref[h * tq:(h + 1) * tq, :] = jnp.where((lane // FOX_DIM) == h, qc, jnp.zeros_like(qc))
            qpad_ref[h] = _fox_qpad(qx, h)
            fq_ref[h * tq:(h + 1) * tq, :] = jnp.broadcast_to(fqs[:, h:h + 1], (tq, LANES))
        m_ref[...] = jnp.full(m_ref.shape, NEG, F32)
        l_ref[...] = jnp.zeros(l_ref.shape, F32)
        acc_ref[...] = jnp.zeros(acc_ref.shape, F32)
        kx = kx_ref[...]
        vb = vb_ref[...]
        tk = kx.shape[0]
        s = jnp.concatenate([_dot_nt(qpad_ref[h], kx[:, 2 * LANES * (h // 2):2 * LANES * (h // 2 + 1)])
                             for h in range(HEADS)], axis=0)
        row = lax.broadcasted_iota(jnp.int32, (rows, tk), 0) & (tq - 1)
        s = jnp.where(lax.broadcasted_iota(jnp.int32, (rows, tk), 1) <= row, s, NEG)
        update(s, lambda pr, pb: _dot(pb, vb[:, LANES * pr:LANES * (pr + 1)]))

    @pl.when(j > 0)
    def _():
        tk = kct_ref.shape[3]
        kt = kct_ref[0].reshape(HEADS * FOX_DIM, tk).astype(BF)
        gk = gk_ref[0]
        gk_rows = jnp.concatenate([jnp.broadcast_to(gk[h:h + 1, :], (tq, tk)) for h in range(HEADS)], axis=0)
        s = _dot(qbd_ref[...], kt) + (_lanes(fq_ref[...], tk) - gk_rows)
        update(s, lambda pr, pb: _dot_nt(
            pb, vct_ref[0, 2 * pr:2 * pr + 2].reshape(2 * FOX_DIM, tk).astype(BF)))

    @pl.when(j == n_cache)
    def _():
        first = lax.broadcasted_iota(jnp.int32, (tq, LANES), 1) < FOX_DIM
        o = acc_ref[...] / l_ref[...]
        for pr in range(N_PAIR):
            o_ref[:, LANES * pr:LANES * (pr + 1)] = jnp.where(
                first, o[2 * tq * pr:2 * tq * pr + tq], o[2 * tq * pr + tq:2 * tq * (pr + 1)]).astype(o_ref.dtype)


def _fox_sample(qx, kx, vb, fqs, kct, vct, gk, *, batch, seq, tkc):
    past = kct.shape[3]
    n_cache = past // tkc
    kern = functools.partial(_fox_sample_kernel, n_cache=n_cache)
    row = lambda w: pl.BlockSpec((seq, w), lambda b, j: (b, 0))
    return pl.pallas_call(
        kern, grid=(batch, n_cache + 1),
        in_specs=[row(2 * FOX_W), row(2 * FOX_W), row(FOX_W), row(LANES),
                  pl.BlockSpec((1, HEADS, FOX_DIM, tkc), lambda b, j: (b, 0, 0, jnp.maximum(j - 1, 0))),
                  pl.BlockSpec((1, HEADS, FOX_DIM, tkc), lambda b, j: (b, 0, 0, jnp.maximum(j - 1, 0))),
                  pl.BlockSpec((1, HEADS, tkc), lambda b, j: (b, 0, jnp.maximum(j - 1, 0)))],
        out_specs=row(FOX_W),
        out_shape=jax.ShapeDtypeStruct((batch * seq, FOX_W), BF),
        scratch_shapes=[pltpu.VMEM((HEADS * seq, FOX_W), BF), pltpu.VMEM((HEADS, seq, 2 * LANES), BF)]
        + [pltpu.VMEM((HEADS * seq, LANES), F32)] * 4,
        compiler_params=_params(2), name="fox_sample")(qx, kx, vb, fqs, kct, vct, gk)


def _chunk_mask(shape, tq, row0, q0, k0):
    assert tq & (tq - 1) == 0 and CHUNK & (CHUNK - 1) == 0
    shift = CHUNK.bit_length() - 1
    qpos = q0 + ((row0 + lax.broadcasted_iota(jnp.int32, shape, 0)) & (tq - 1))
    kpos = k0 + lax.broadcasted_iota(jnp.int32, shape, 1)
    return lax.shift_right_logical(kpos, shift) <= lax.shift_right_logical(qpos, shift)


def _mla_rows_update(s, val, m_ref, l_ref, acc_ref, r0, r1):
    m_prev = m_ref[r0:r1]
    m_new = jnp.maximum(m_prev, jnp.max(s, axis=-1, keepdims=True))
    p = jnp.exp2(s - _lanes(m_new, s.shape[1]))
    alpha = jnp.exp2(m_prev - m_new)
    l_ref[r0:r1] = alpha * l_ref[r0:r1] + jnp.sum(p, axis=-1, keepdims=True)
    m_ref[r0:r1] = m_new
    acc_ref[r0:r1] = acc_ref[r0:r1] * _lanes(alpha, KV_LORA) + _dot(p.astype(BF), val)


def _mla_init_stats(m_ref, l_ref, acc_ref):
    m_ref[...] = jnp.full(m_ref.shape, NEG, F32)
    l_ref[...] = jnp.zeros(l_ref.shape, F32)
    acc_ref[...] = jnp.zeros(acc_ref.shape, F32)


def _mla_finalize(o_ref, wuv_ref, l_ref, acc_ref, tq):
    olat = (acc_ref[...] / _lanes(l_ref[...], KV_LORA)).astype(BF)
    for g in range(2):
        out = None
        for h in range(4 * g, 4 * g + 4):
            part = _dot(olat[h * tq:(h + 1) * tq, :], wuv_ref[h])
            out = part if out is None else out + part
        o_ref[:, 256 * g:256 * (g + 1)] = out.astype(o_ref.dtype)


def _mla_prompt_kernel(qlat_ref, qrope_ref, mkey_ref, wuv_ref, o_ref, qcat_ref, m_ref, l_ref, acc_ref,
                       *, tq, tk, rc):
    i = pl.program_id(1)
    j = pl.program_id(2)
    last_j = (tq * (i + 1) - 1) // tk
    needs_mask = tk * (j + 1) > tq * i + CHUNK

    @pl.when(j == 0)
    def _():
        qrope = qrope_ref[...]
        lane = lax.broadcasted_iota(jnp.int32, (tq, HEADS * MLA_ROPE), 1)
        for h in range(HEADS):
            qcat_ref[h * tq:(h + 1) * tq, :KV_LORA] = qlat_ref[h]
            qcat_ref[h * tq:(h + 1) * tq, KV_LORA:] = jnp.where((lane // MLA_ROPE) == h, qrope,
                                                                jnp.zeros_like(qrope))
        _mla_init_stats(m_ref, l_ref, acc_ref)

    def tile(masked):
        mk = mkey_ref[...]
        starts = list(range(0, HEADS * tq, rc))
        ahead = 1
        pending =[_dot_nt(qcat_ref[r:r + rc], mk) for r in starts[:ahead]]
        for n, r0 in enumerate(starts):
            s = pending.pop(0)
            if n + ahead < len(starts):
                r = starts[n + ahead]
                pending.append(_dot_nt(qcat_ref[r:r + rc], mk))
            if masked:
                s = jnp.where(_chunk_mask(s.shape, tq, r0, tq * i, tk * j), s, NEG)
            _mla_rows_update(s, mk[:, :KV_LORA], m_ref, l_ref, acc_ref, r0, r0 + rc)

    @pl.when((j <= last_j) & jnp.logical_not(needs_mask))
    def _():
        tile(False)

    @pl.when((j <= last_j) & needs_mask)
    def _():
        tile(True)

    @pl.when(j == last_j)
    def _():
        _mla_finalize(o_ref, wuv_ref, l_ref, acc_ref, tq)


def _mla_scratch(rows):
    return [pltpu.VMEM((rows, LANES), F32), pltpu.VMEM((rows, LANES), F32), pltpu.VMEM((rows, KV_LORA), F32)]


def _mla_prompt(qlat, qrope, mkey, wuvp, *, batch, seq, tq, tk, rc):
    nq = seq // tq
    nk = seq // tk

    def key_map(b, i, j):
        return (b * nk + jnp.minimum(j, (tq * (i + 1) - 1) // tk), 0)

    kern = functools.partial(_mla_prompt_kernel, tq=tq, tk=tk, rc=rc)
    return pl.pallas_call(
        kern, grid=(batch, nq, nk),
        in_specs=[pl.BlockSpec((HEADS, tq, KV_LORA), lambda b, i, j: (0, b * nq + i, 0)),
                  pl.BlockSpec((tq, HEADS * MLA_ROPE), lambda b, i, j: (b * nq + i, 0)),
                  pl.BlockSpec((tk, 2 * KV_LORA), key_map),
                  _const_spec(wuvp.shape)],
        out_specs=pl.BlockSpec((tq, FOX_W), lambda b, i, j: (b * nq + i, 0)),
        out_shape=jax.ShapeDtypeStruct((batch * seq, FOX_W), BF),
        scratch_shapes=[pltpu.VMEM((HEADS * tq, 2 * KV_LORA), BF)] + _mla_scratch(HEADS * tq),
        compiler_params=_params(3), name="mla_prompt")(qlat, qrope, mkey, wuvp)


def _mla_sample_kernel(qlat_ref, qrope_ref, mkey_ref, latc_ref, krt_ref, wuv_ref, o_ref,
                       ql_ref, qr_ref, m_ref, l_ref, acc_ref, *, n_cache, tq, past, rc):
    j = pl.program_id(1)
    rows = HEADS * tq

    @pl.when(j == 0)
    def _():
        qrope = qrope_ref[...]
        for h in range(HEADS):
            ql_ref[h * tq:(h + 1) * tq, :] = qlat_ref[h]
            qr_ref[h * tq:(h + 1) * tq, :] = qrope[:, MLA_ROPE * h:MLA_ROPE * (h + 1)]
        _mla_init_stats(m_ref, l_ref, acc_ref)
        mk = mkey_ref[...]
        s = _dot_nt(ql_ref[...], mk[:, :KV_LORA]) + _dot_nt(qr_ref[...], mk[:, KV_LORA:KV_LORA + MLA_ROPE])
        if past % CHUNK != 0 or tq > CHUNK:
            s = jnp.where(_chunk_mask(s.shape, tq, 0, past, past), s, NEG)
        _mla_rows_update(s, mk[:, :KV_LORA], m_ref, l_ref, acc_ref, 0, rows)

    @pl.when(j > 0)
    def _():
        c = latc_ref[0].astype(BF)
        krt = krt_ref[0].astype(BF)
        def scores(r0):
            return _dot_nt(ql_ref[r0:r0 + rc], c) + _dot(qr_ref[r0:r0 + rc], krt)

        s_next = scores(0)
        for r0 in range(0, rows, rc):
            s = s_next
            if r0 + rc < rows:
                s_next = scores(r0 + rc)
            _mla_rows_update(s, c, m_ref, l_ref, acc_ref, r0, r0 + rc)

    @pl.when(j == n_cache)
    def _():
        _mla_finalize(o_ref, wuv_ref, l_ref, acc_ref, tq)


def _mla_sample(qlat, qrope, mkey, latc, krt, wuvp, *, batch, seq, tkc, rc):
    past = latc.shape[1]
    n_cache = past // tkc
    kern = functools.partial(_mla_sample_kernel, n_cache=n_cache, tq=seq, past=past, rc=rc)
    return pl.pallas_call(
        kern, grid=(batch, n_cache + 1),
        in_specs=[pl.BlockSpec((HEADS, seq, KV_LORA), lambda b, j: (0, b, 0)),
                  pl.BlockSpec((seq, HEADS * MLA_ROPE), lambda b, j: (b, 0)),
                  pl.BlockSpec((seq, 2 * KV_LORA), lambda b, j: (b, 0)),
                  pl.BlockSpec((1, tkc, KV_LORA), lambda b, j: (b, jnp.maximum(j - 1, 0), 0)),
                  pl.BlockSpec((1, MLA_ROPE, tkc), lambda b, j: (b, 0, jnp.maximum(j - 1, 0))),
                  _const_spec(wuvp.shape)],
        out_specs=pl.BlockSpec((seq, FOX_W), lambda b, j: (b, 0)),
        out_shape=jax.ShapeDtypeStruct((batch * seq, FOX_W), BF),
        scratch_shapes=[pltpu.VMEM((HEADS * seq, KV_LORA), BF), pltpu.VMEM((HEADS * seq, MLA_ROPE), BF)]
        + _mla_scratch(HEADS * seq),
        compiler_params=_params(2), name="mla_sample")(qlat, qrope, mkey, latc, krt, wuvp)


def _ffn_kernel(x_ref, fox_ref, mla_ref, pconv_ref, wo_ref, gf_ref, wup_ref, cw_ref, wdn_ref, gfin_ref,
                y_ref, cst_ref, hn_ref, act_ref, ush_ref, carry_ref, *, tm, seq_len):
    n_seg = max(1, tm // seq_len)
    seg = tm // n_seg
    mixed = jnp.concatenate([fox_ref[...], mla_ref[...]], axis=1)
    x1 = x_ref[...] + _dot(mixed, wo_ref[...])
    hn_ref[...] = (x1 * lax.rsqrt(jnp.mean(x1 * x1, axis=-1, keepdims=True) + EPS) * gf_ref[...]).astype(BF)
    y_ref[...] = x1

    if n_seg == 1:
        @pl.when(pl.program_id(0) % (seq_len // tm) == 0)
        def _():
            carry_ref[...] = pconv_ref[0]

    def conv(u, col, slot):
        cols = slice(col, col + FF_CHUNK)
        for s in range(n_seg):
            ush_ref[slot, s, 6:8, :] = carry_ref[:, cols] if n_seg == 1 else pconv_ref[s, :, cols]
            ush_ref[slot, s, 8:8 + seg, :] = u[seg * s:seg * (s + 1), :]
            cst_ref[s, :, cols] = u[seg * (s + 1) - 2:seg * (s + 1), :]
        if n_seg == 1:
            carry_ref[:, cols] = u[tm - 2:tm, :]
        u1 = jnp.concatenate([ush_ref[slot, s, 7:7 + seg, :] for s in range(n_seg)], axis=0)
        u2 = jnp.concatenate([ush_ref[slot, s, 6:6 + seg, :] for s in range(n_seg)], axis=0)
        cw = cw_ref[:, cols]
        return cw[3:4, :] + cw[0:1, :] * u2 + cw[1:2, :] * u1 + cw[2:3, :] * u

    def up(c):
        hn = hn_ref[...]
        g0, v0 = FF_CHUNK * c, D_FF + FF_CHUNK * c
        return _dot(hn, wup_ref[:, g0:g0 + FF_CHUNK]), _dot(hn, wup_ref[:, v0:v0 + FF_CHUNK])

    u_next = up(0)
    for c in range(N_FF):
        ug, uv = u_next
        if c + 1 < N_FF:
            u_next = up(c + 1)
        gate = conv(ug, FF_CHUNK * c, 2 * (c % 2))
        val = conv(uv, D_FF + FF_CHUNK * c, 2 * (c % 2) + 1)
        act_ref[:, FF_CHUNK * c:FF_CHUNK * (c + 1)] = (gate * jax.nn.sigmoid(gate) * val).astype(BF)

    x2 = y_ref[...] + _dot(act_ref[...], wdn_ref[...])
    y_ref[...] = x2 * lax.rsqrt(jnp.mean(x2 * x2, axis=-1, keepdims=True) + EPS) * gfin_ref[...]


def _ffn(x2d, fox, mla, pconv, wo, gf, wup, cw, wdn, gfin, *, seq_len, tm):
    n = x2d.shape[0]
    nt = n // tm
    n_seq = n // seq_len
    n_seg = max(1, tm // seq_len)
    if n_seg == 1:
        tps = seq_len // tm
        st_map = lambda i: (i // tps, 0, 0)
    else:
        st_map = lambda i: (i, 0, 0)
    row = lambda w: pl.BlockSpec((tm, w), lambda i: (i, 0))
    kern = functools.partial(_ffn_kernel, tm=tm, seq_len=seq_len)
    return pl.pallas_call(
        kern, grid=(nt,),
        in_specs=[row(D_MODEL), row(FOX_W), row(FOX_W),
                  pl.BlockSpec((n_seg, CONV_W - 1, 2 * D_FF), st_map),
                  _const_spec(wo.shape), _const_spec(gf.shape), _const_spec(wup.shape), _const_spec(cw.shape),
                  _const_spec(wdn.shape), _const_spec(gfin.shape)],
        out_specs=(row(D_MODEL), pl.BlockSpec((n_seg, CONV_W - 1, 2 * D_FF), st_map)),
        out_shape=(jax.ShapeDtypeStruct((n, D_MODEL), F32),
                   jax.ShapeDtypeStruct((n_seq, CONV_W - 1, 2 * D_FF), F32)),
        scratch_shapes=[pltpu.VMEM((tm, D_MODEL), BF), pltpu.VMEM((tm, D_FF), BF),
                        pltpu.VMEM((4, n_seg, 8 + tm // n_seg, FF_CHUNK), F32),
                        pltpu.VMEM((CONV_W - 1, 2 * D_FF), F32)],
        compiler_params=_params(1), name="ffn")(x2d, fox, mla, pconv, wo, gf, wup, cw, wdn, gfin)


def _rope_tables(pos):
    half = MLA_ROPE // 2
    inv = ROPE_THETA ** (-jnp.arange(half, dtype=F32) / half)
    ang = pos.astype(F32)[:, None] * inv[None, :]
    cos = jnp.tile(jnp.cos(ang), (1, 2 * HEADS))
    sin = jnp.tile(jnp.sin(ang), (1, 2 * HEADS))
    return cos, sin


def _placement():
    place = np.zeros((LANES, 2 * N_PAIR * LANES), np.float32)
    ones = np.zeros((1, 2 * N_PAIR * LANES), np.float32)
    for h in range(HEADS):
        for i in range(3):
            for p in range(N_PAIR):
                place[HEADS * i + h, LANES * p + AUG * h + i] = 1.0
                ones[0, LANES * p + AUG * h + 3 + i] = 1.0
                ones[0, N_PAIR * LANES + LANES * p + AUG * h + i] = 1.0
                place[HEADS * i + h, N_PAIR * LANES + LANES * p + AUG * h + 3 + i] = -1.0
    rep = np.zeros((LANES, HEADS * MLA_ROPE), np.float32)
    for h in range(HEADS):
        for r in range(MLA_ROPE):
            rep[r, MLA_ROPE * h + r] = 1.0
    return jnp.asarray(place, BF), jnp.asarray(ones, F32), jnp.asarray(rep, BF)


def _rot_cols(w):
    half = MLA_ROPE // 2
    return jnp.concatenate([-w[..., half:], w[..., :half]], axis=-1)


def _pad_cols(w, width):
    return jnp.pad(w, ((0, 0), (0, width - w.shape[1])))


def _layer_weights(w_in, b_f, w_q_up, w_uk, w_uv, w_out, w_up, conv_w, conv_b, w_down):
    wq, wk, wv, wf, wqc, wckv, wkr = jnp.split(w_in, IN_SPLITS, axis=1)
    wa = jnp.concatenate([wq, wk, wv, wqc, wckv, _pad_cols(wkr, LANES), _pad_cols(_rot_cols(wkr), LANES),
                          _pad_cols(wf, LANES)], axis=1).astype(BF)
    bfp = _pad_cols(b_f[None, :], LANES).astype(F32)
    wq3 = w_q_up.reshape(Q_LORA, HEADS, MLA_NOPE + MLA_ROPE)
    wq_nope = wq3[:, :, :MLA_NOPE].reshape(Q_LORA, HEADS * MLA_NOPE)
    wq_rope = wq3[:, :, MLA_NOPE:]
    wq2 = jnp.concatenate([wq_nope, wq_rope.reshape(Q_LORA, -1), _rot_cols(wq_rope).reshape(Q_LORA, -1)],
                          axis=1).astype(BF)
    wukp = jnp.zeros((HEADS, 4 * MLA_NOPE, KV_LORA), F32)
    wuvp = jnp.zeros((HEADS, KV_LORA, 4 * MLA_VDIM), F32)
    for h in range(HEADS):
        o = MLA_NOPE * (h % 4)
        wukp = wukp.at[h, o:o + MLA_NOPE, :].set(w_uk[:, h, :].T)
        wuvp = wuvp.at[h, :, o:o + MLA_VDIM].set(w_uv[:, h, :])
    wup = w_up.astype(BF)
    cw = jnp.concatenate([conv_w, conv_b[None, :]], axis=0)
    wdn = w_down.astype(BF)
    return wa, bfp, wq2, wukp.astype(BF), wuvp.astype(BF), w_out.astype(BF), wup, cw, wdn


def kernel(x_prompt, x_sample, cache_fox_k, cache_fox_v, cache_fox_logf, cache_mla_latent, cache_mla_krope,
           state_ffn_conv, attn_norm, w_in, b_forget, q_norm, w_q_up, kv_norm, w_uk, w_uv, w_out, ffn_norm,
           w_up, conv_w, conv_b, w_down, final_norm):
    assert attn_norm.shape[0] == 1, "single-layer stack"
    bp, tp, _ = x_prompt.shape
    bs, ts, _ = x_sample.shape
    past = cache_fox_k.shape[2]
    assert ts == CHUNK and past % CHUNK == 0

    wa, bfp, wq2, wukp, wuvp, wo, wup, cw, wdn = _layer_weights(
        w_in[0], b_forget[0], w_q_up[0], w_uk[0], w_uv[0], w_out[0], w_up[0], conv_w[0], conv_b[0], w_down[0])
    place, ones, rep = _placement()
    g_attn = attn_norm[0][None, :]
    g_q = q_norm[0][None, :]
    g_kv = kv_norm[0][None, :]
    g_ffn = ffn_norm[0][None, :]
    g_fin = final_norm[None, :]

    tm = 512
    outs = []
    for grp in ("prompt", "sample"):
        if grp == "prompt":
            x2d = x_prompt.reshape(bp * tp, D_MODEL)
            b, t = bp, tp
            cos, sin = _rope_tables(jnp.arange(tp, dtype=jnp.int32))
            pconv = jnp.zeros((bp, CONV_W - 1, 2 * D_FF), F32)
        else:
            x2d = x_sample.reshape(bs * ts, D_MODEL)
            b, t = bs, ts
            cos, sin = _rope_tables(past + jnp.arange(ts, dtype=jnp.int32))
            cos, sin = jnp.tile(cos, (tm // ts, 1)), jnp.tile(sin, (tm // ts, 1))
            pconv = state_ffn_conv[0]

        (k, v, logf, lat, kr, qx, kx, vb, mkey, qlat, qrope, fqs) = _proj(
            x2d, cos, sin, g_attn, wa, bfp, g_q, g_kv, wq2, wukp, place, ones, rep, seq_len=t, tm=tm)

        if grp == "prompt":
            fox = _fox_prompt(qx, kx, vb, batch=b, seq=t, t=512)
            mla = _mla_prompt(qlat, qrope, mkey, wuvp, batch=b, seq=t, tq=512, tk=512, rc=512)
        else:
            gk = _fox_prep(jnp.transpose(cache_fox_logf[0], (0, 2, 1)))
            kct = jnp.transpose(cache_fox_k[0], (0, 2, 3, 1))
            vct = jnp.transpose(cache_fox_v[0], (0, 2, 3, 1))
            krt = jnp.transpose(cache_mla_krope[0], (0, 2, 1))
            fox = _fox_sample(qx, kx, vb, fqs, kct, vct, gk, batch=b, seq=t, tkc=min(past, 1024))
            mla = _mla_sample(qlat, qrope, mkey, cache_mla_latent[0], krt, wuvp, batch=b, seq=t,
                              tkc=min(past, 1024), rc=256)

        y, cst = _ffn(x2d, fox, mla, pconv, wo, g_ffn, wup, cw, wdn, g_fin, seq_len=t, tm=tm)
        outs.append((y.reshape(b, t, D_MODEL),
                     k.reshape(1, b, t, HEADS, FOX_DIM), v.reshape(1, b, t, HEADS, FOX_DIM),
                     logf.reshape(1, b, t, HEADS), lat.reshape(1, b, t, KV_LORA), kr.reshape(1, b, t, MLA_ROPE),
                     cst[None]))
    (yp, *st_p), (ys, *st_s) = outs
    return (yp, ys, *st_p, *st_s)
```

```python
import functools

import numpy as np
import jax
import jax.numpy as jnp
from jax import lax
from jax.experimental import pallas as pl
from jax.experimental.pallas import tpu as pltpu

D_MODEL = 1024
CHUNK = 64
HEADS = 8
FOX_DIM = 64
MLA_NOPE = 64
MLA_ROPE = 32
MLA_VDIM = 64
Q_LORA = 384
KV_LORA = 256
D_FF = 2816
CONV_W = 3
ROPE_THETA = 10000.0
EPS = 1e-6
NEG = -1e30
LOG2E = 1.4426950408889634

FOX_W = HEADS * FOX_DIM
IN_SIZES = [FOX_W, FOX_W, FOX_W, HEADS, Q_LORA, KV_LORA, MLA_ROPE]
IN_SPLITS = [int(s) for s in np.cumsum(IN_SIZES)[:-1]]

LANES = 128
AUG = 16
N_PAIR = HEADS // 2
FF_CHUNK = 256
N_FF = D_FF // FF_CHUNK
VMEM_LIMIT = 56 * 1024 * 1024

C_Q, C_K, C_V = 0, 512, 1024
C_QC = 1536
C_CKV = 1920
C_KR = 2176
C_KRR = 2304
C_F = 2432
N_A = 2560

BF = jnp.bfloat16
F32 = jnp.float32


def _dot(a, b):
    return jnp.dot(a, b, preferred_element_type=F32)


def _dot_nt(a, b):
    return lax.dot_general(a, b, (((1,), (1,)), ((), ())), preferred_element_type=F32)


def _split3(x):
    hi = x.astype(BF)
    r1 = x - hi.astype(F32)
    mid = r1.astype(BF)
    lo = (r1 - mid.astype(F32)).astype(BF)
    return hi, mid, lo


def _log_sigmoid(x):
    return jnp.minimum(x, 0.0) - jnp.log1p(jnp.exp(-jnp.abs(x)))


def _lanes(x, n):
    if n % LANES == 0:
        return x if n == LANES else jnp.concatenate([x] * (n // LANES), axis=1)
    assert n < LANES
    return x[:, :n]


def _params(n_axes):
    return pltpu.CompilerParams(dimension_semantics=("arbitrary",) * n_axes, vmem_limit_bytes=VMEM_LIMIT)


def _const_spec(shape):
    nd = len(shape)
    return pl.BlockSpec(shape, lambda *_: (0,) * nd, pipeline_mode=pl.Buffered(1))


def _proj_kernel(x_ref, cos_ref, sin_ref, g_ref, wa_ref, bf_ref, gq_ref, gkv_ref, wq2_ref,
                 wuk_ref, place_ref, ones_ref, rep_ref,
                 k_ref, v_ref, logf_ref, lat_ref, kr_ref, qx_ref, kx_ref, vb_ref, mkey_ref, qlat_ref,
                 qrope_ref, fqs_ref, carry_ref, *, tm, seq_len, mla_scale):
    x = x_ref[...]
    ms = jnp.mean(x * x, axis=-1, keepdims=True)
    hn = (x * lax.rsqrt(ms + EPS) * g_ref[...]).astype(BF)
    z = _dot(hn, wa_ref[...])

    zk = z[:, C_K:C_K + FOX_W]
    zv = z[:, C_V:C_V + FOX_W]
    for h in range(HEADS):
        k_ref[pl.ds(h, tm, stride=HEADS), :] = zk[:, FOX_DIM * h:FOX_DIM * (h + 1)]
        v_ref[pl.ds(h, tm, stride=HEADS), :] = zv[:, FOX_DIM * h:FOX_DIM * (h + 1)]
    vb_ref[...] = zv.astype(BF)

    lane = lax.broadcasted_iota(jnp.int32, (tm, LANES), 1)
    logf = _log_sigmoid(z[:, C_F:C_F + LANES] + bf_ref[...])
    logf = jnp.where(lane < HEADS, logf, 0.0)
    if seq_len >= tm:
        logf_ref[0] = logf.T[:HEADS, :]
    else:
        logf_ref[...] = logf[:, :HEADS]
    row = lax.broadcasted_iota(jnp.int32, (tm, tm), 0)
    col = lax.broadcasted_iota(jnp.int32, (tm, tm), 1)
    keep = col <= row
    if seq_len < tm:
        keep = keep & ((col // seq_len) == (row // seq_len))
    tri = jnp.where(keep, 1.0, 0.0).astype(BF)
    hi, mid, lo = _split3(logf)
    fcum = _dot(tri, hi) + _dot(tri, mid) + _dot(tri, lo)
    if seq_len > tm:
        @pl.when(pl.program_id(0) % (seq_len // tm) == 0)
        def _():
            carry_ref[...] = jnp.zeros_like(carry_ref)
        fcum = fcum + carry_ref[0:1, :]
        carry_ref[0:1, :] = fcum[tm - 1:tm, :]

    fqs = fcum * LOG2E
    fqs_ref[...] = fqs
    hi, mid, lo = _split3(fqs)
    comb = hi.astype(F32) + pltpu.roll(mid.astype(F32), HEADS, 1) + pltpu.roll(lo.astype(F32), 2 * HEADS, 1)
    aug = _dot(comb.astype(BF), place_ref[...]) + ones_ref[...]
    zq = z[:, C_Q:C_Q + FOX_W] * (FOX_DIM ** -0.5 * LOG2E)
    for p in range(N_PAIR):
        lo_, hi_ = 2 * LANES * p, 2 * LANES * p + LANES
        qx_ref[:, lo_:hi_] = zq[:, LANES * p:LANES * (p + 1)].astype(BF)
        qx_ref[:, hi_:hi_ + LANES] = aug[:, LANES * p:LANES * (p + 1)].astype(BF)
        kx_ref[:, lo_:hi_] = zk[:, LANES * p:LANES * (p + 1)].astype(BF)
        kx_ref[:, hi_:hi_ + LANES] = aug[:, FOX_W + LANES * p:FOX_W + LANES * (p + 1)].astype(BF)

    zc = z[:, C_CKV:C_CKV + KV_LORA]
    ckv = zc * lax.rsqrt(jnp.mean(zc * zc, axis=-1, keepdims=True) + EPS) * gkv_ref[...]
    lat_ref[...] = ckv
    cos = cos_ref[...]
    sin = sin_ref[...]
    krb = z[:, C_KR:C_KR + LANES] * cos[:, :LANES] + z[:, C_KRR:C_KRR + LANES] * sin[:, :LANES]
    if seq_len >= tm:
        kr_ref[0] = krb.T[:MLA_ROPE, :]
    else:
        kr_ref[...] = krb[:, :MLA_ROPE]
    mkey_ref[:, :KV_LORA] = ckv.astype(BF)
    mkey_ref[:, KV_LORA:] = _dot(krb.astype(BF), rep_ref[...]).astype(BF)

    zqc = z[:, C_QC:C_QC + Q_LORA]
    qcn = (zqc * lax.rsqrt(jnp.mean(zqc * zqc, axis=-1, keepdims=True) + EPS) * gq_ref[...]).astype(BF)
    zq2 = _dot(qcn, wq2_ref[...])
    nr = HEADS * MLA_ROPE
    qr = zq2[:, FOX_W:FOX_W + nr] * cos + zq2[:, FOX_W + nr:FOX_W + 2 * nr] * sin
    qrope_ref[...] = (qr * mla_scale).astype(BF)
    for h in range(HEADS):
        g = h // 4
        qn = zq2[:, 256 * g:256 * (g + 1)].astype(BF)
        qlat_ref[h] = (_dot(qn, wuk_ref[h]) * mla_scale).astype(BF)


def _proj(x2d, cos, sin, g, wa, bfp, gq, gkv, wq2, wukp, place, ones, rep, *, seq_len, tm):
    n = x2d.shape[0]
    nt = n // tm
    if seq_len >= tm:
        tps = seq_len // tm
        tab_map = lambda i: (i % tps, 0)
    else:
        tab_map = lambda i: (0, 0)
    row = lambda w: pl.BlockSpec((tm, w), lambda i: (i, 0))
    if seq_len >= tm:
        small = lambda w: jax.ShapeDtypeStruct((n // seq_len, w, seq_len), F32)
        small_spec = lambda w: pl.BlockSpec((1, w, tm), lambda i: (i // tps, 0, i % tps))
    else:
        small = lambda w: jax.ShapeDtypeStruct((n, w), F32)
        small_spec = row
    out_shape = (
        jax.ShapeDtypeStruct((n * HEADS, FOX_DIM), F32),
        jax.ShapeDtypeStruct((n * HEADS, FOX_DIM), F32),
        small(HEADS),
        jax.ShapeDtypeStruct((n, KV_LORA), F32),
        small(MLA_ROPE),
        jax.ShapeDtypeStruct((n, 2 * FOX_W), BF),
        jax.ShapeDtypeStruct((n, 2 * FOX_W), BF),
        jax.ShapeDtypeStruct((n, FOX_W), BF),
        jax.ShapeDtypeStruct((n, 2 * KV_LORA), BF),
        jax.ShapeDtypeStruct((HEADS, n, KV_LORA), BF),
        jax.ShapeDtypeStruct((n, HEADS * MLA_ROPE), BF),
        jax.ShapeDtypeStruct((n, LANES), F32),
    )
    kv_spec = pl.BlockSpec((tm * HEADS, FOX_DIM), lambda i: (i, 0))
    out_specs = (kv_spec, kv_spec, small_spec(HEADS), row(KV_LORA), small_spec(MLA_ROPE), row(2 * FOX_W),
                 row(2 * FOX_W), row(FOX_W), row(2 * KV_LORA),
                 pl.BlockSpec((HEADS, tm, KV_LORA), lambda i: (0, i, 0)), row(HEADS * MLA_ROPE), row(LANES))
    in_specs = [row(D_MODEL),
                pl.BlockSpec((tm, 2 * LANES), tab_map), pl.BlockSpec((tm, 2 * LANES), tab_map),
                _const_spec(g.shape), _const_spec(wa.shape), _const_spec(bfp.shape), _const_spec(gq.shape),
                _const_spec(gkv.shape), _const_spec(wq2.shape), _const_spec(wukp.shape),
                _const_spec(place.shape), _const_spec(ones.shape), _const_spec(rep.shape)]
    kern = functools.partial(_proj_kernel, tm=tm, seq_len=seq_len,
                             mla_scale=float((MLA_NOPE + MLA_ROPE) ** -0.5 * LOG2E))
    return pl.pallas_call(
        kern, grid=(nt,), in_specs=in_specs, out_specs=out_specs, out_shape=out_shape,
        scratch_shapes=[pltpu.VMEM((8, LANES), F32)],
        compiler_params=_params(1), name="proj")(
            x2d, cos, sin, g, wa, bfp, gq, gkv, wq2, wukp, place, ones, rep)


def _fox_prep_kernel(plt_ref, g_ref):
    x = plt_ref[0]
    n = x.shape[1]
    lane = lax.broadcasted_iota(jnp.int32, x.shape, 1)
    shift = 1
    while shift < n:
        x = x + jnp.where(lane >= shift, pltpu.roll(x, shift, 1), 0.0)
        shift *= 2
    g_ref[0] = (x - x[:, n - 1:n]) * LOG2E


def _fox_prep(plt):
    b, h, p = plt.shape
    return pl.pallas_call(
        _fox_prep_kernel, grid=(b,),
        in_specs=[pl.BlockSpec((1, h, p), lambda i: (i, 0, 0))],
        out_specs=pl.BlockSpec((1, h, p), lambda i: (i, 0, 0)),
        out_shape=jax.ShapeDtypeStruct((b, h, p), F32),
        compiler_params=_params(1), name="fox_prep")(plt)


def _softmax_step(s, m_prev):
    m_new = jnp.maximum(m_prev, jnp.max(s, axis=-1, keepdims=True))
    p = jnp.exp2(s - _lanes(m_new, s.shape[1]))
    alpha = jnp.exp2(m_prev - m_new)
    return p.astype(BF), alpha, m_new


def _fox_qpad(qx, h):
    p, hh = divmod(h, 2)
    lane = lax.broadcasted_iota(jnp.int32, (qx.shape[0], 2 * LANES), 1)
    blk = qx[:, 2 * LANES * p:2 * LANES * (p + 1)]
    keep = ((lane >= FOX_DIM * hh) & (lane < FOX_DIM * (hh + 1))) | (
        (lane >= LANES + AUG * h) & (lane < LANES + AUG * (h + 1)))
    return jnp.where(keep, blk, jnp.zeros_like(blk))


def _fox_prompt_tile(kx, vb, qpad_ref, m_ref, l_ref, acc_ref, masked):
    tq = qpad_ref.shape[1]
    tk = kx.shape[0]
    first = lax.broadcasted_iota(jnp.int32, (tq, LANES), 1) < FOX_DIM
    ones = jnp.ones((tk, LANES), BF)
    if masked:
        causal = lax.broadcasted_iota(jnp.int32, (tq, tk), 1) <= lax.broadcasted_iota(jnp.int32, (tq, tk), 0)
    def scores(h):
        return _dot_nt(qpad_ref[h], kx[:, 2 * LANES * (h // 2):2 * LANES * (h // 2 + 1)])

    s_next = scores(0)
    for p in range(N_PAIR):
        vpa = jnp.concatenate([vb[:, LANES * p:LANES * (p + 1)], ones], axis=1)
        alphas, pvs = [], []
        for hh in range(2):
            h = 2 * p + hh
            s = s_next
            if h + 1 < HEADS:
                s_next = scores(h + 1)
            if masked:
                s = jnp.where(causal, s, NEG)
            pb, alpha, m_new = _softmax_step(s, m_ref[h])
            pv = _dot(pb, vpa)
            l_ref[h] = alpha * l_ref[h] + pv[:, LANES:]
            m_ref[h] = m_new
            alphas.append(alpha)
            pvs.append(pv[:, :LANES])
        acc_ref[p] = acc_ref[p] * jnp.where(first, alphas[0], alphas[1]) + jnp.where(first, pvs[0], pvs[1])


def _fox_prompt_kernel(qx_ref, kx_ref, vb_ref, o_ref, qpad_ref, m_ref, l_ref, acc_ref):
    i = pl.program_id(1)
    j = pl.program_id(2)

    @pl.when(j == 0)
    def _():
        qx = qx_ref[...]
        for h in range(HEADS):
            qpad_ref[h] = _fox_qpad(qx, h)
        m_ref[...] = jnp.full(m_ref.shape, NEG, F32)
        l_ref[...] = jnp.zeros(l_ref.shape, F32)
        acc_ref[...] = jnp.zeros(acc_ref.shape, F32)

    @pl.when(j < i)
    def _():
        _fox_prompt_tile(kx_ref[...], vb_ref[...], qpad_ref, m_ref, l_ref, acc_ref, False)

    @pl.when(j == i)
    def _():
        _fox_prompt_tile(kx_ref[...], vb_ref[...], qpad_ref, m_ref, l_ref, acc_ref, True)
        first = lax.broadcasted_iota(jnp.int32, (qpad_ref.shape[1], LANES), 1) < FOX_DIM
        for p in range(N_PAIR):
            l_sel = jnp.where(first, l_ref[2 * p], l_ref[2 * p + 1])
            o_ref[:, LANES * p:LANES * (p + 1)] = (acc_ref[p] / l_sel).astype(o_ref.dtype)


def _fox_prompt(qx, kx, vb, *, batch, seq, t):
    nq = seq // t
    return pl.pallas_call(
        _fox_prompt_kernel, grid=(batch, nq, nq),
        in_specs=[pl.BlockSpec((t, 2 * FOX_W), lambda b, i, j: (b * nq + i, 0)),
                  pl.BlockSpec((t, 2 * FOX_W), lambda b, i, j: (b * nq + jnp.minimum(i, j), 0)),
                  pl.BlockSpec((t, FOX_W), lambda b, i, j: (b * nq + jnp.minimum(i, j), 0))],
        out_specs=pl.BlockSpec((t, FOX_W), lambda b, i, j: (b * nq + i, 0)),
        out_shape=jax.ShapeDtypeStruct((batch * seq, FOX_W), BF),
        scratch_shapes=[pltpu.VMEM((HEADS, t, 2 * LANES), BF), pltpu.VMEM((HEADS, t, LANES), F32),
                        pltpu.VMEM((HEADS, t, LANES), F32), pltpu.VMEM((N_PAIR, t, LANES), F32)],
        compiler_params=_params(3), name="fox_prompt")(qx, kx, vb)


def _fox_sample_kernel(qx_ref, kx_ref, vb_ref, fqs_ref, kct_ref, vct_ref, gk_ref, o_ref,
                       qbd_ref, qpad_ref, fq_ref, m_ref, l_ref, acc_ref, *, n_cache):
    j = pl.program_id(1)
    tq = qx_ref.shape[0]
    rows = HEADS * tq

    def update(s, pv_fn, g0=0, g1=2):
        a, b = 4 * tq * g0, 4 * tq * g1
        m_prev = m_ref[a:b]
        m_new = jnp.maximum(m_prev, jnp.max(s, axis=-1, keepdims=True))
        p = jnp.exp2(s - _lanes(m_new, s.shape[1]))
        alpha = jnp.exp2(m_prev - m_new)
        l_ref[a:b] = alpha * l_ref[a:b] + jnp.sum(p, axis=-1, keepdims=True)
        m_ref[a:b] = m_new
        pb = p.astype(BF)
        for pr in range(2 * g0, 2 * g1):
            r0, r1 = 2 * tq * pr, 2 * tq * (pr + 1)
            acc_ref[r0:r1] = acc_ref[r0:r1] * alpha[r0 - a:r1 - a] + pv_fn(pr, pb[r0 - a:r1 - a])

    @pl.when(j == 0)
    def _():
        qx = qx_ref[...]
        fqs = fqs_ref[...]
        qc = jnp.concatenate([qx[:, 2 * LANES * p:2 * LANES * p + LANES] for p in range(N_PAIR)], axis=1)
        lane = lax.broadcasted_iota(jnp.int32, qc.shape, 1)
        for h in range(HEADS):
            qbd_ref[h * tq:(h + 1) * tq, :] = jnp.where((lane // FOX_DIM) == h, qc, jnp.zeros_like(qc))
            qpad_ref[h] = _fox_qpad(qx, h)
            fq_ref[h * tq:(h + 1) * tq, :] = jnp.broadcast_to(fqs[:, h:h + 1], (tq, LANES))
        m_ref[...] = jnp.full(m_ref.shape, NEG, F32)
        l_ref[...] = jnp.zeros(l_ref.shape, F32)
        acc_ref[...] = jnp.zeros(acc_ref.shape, F32)
        kx = kx_ref[...]
        vb = vb_ref[...]
        tk = kx.shape[0]
        s = jnp.concatenate([_dot_nt(qpad_ref[h], kx[:, 2 * LANES * (h // 2):2 * LANES * (h // 2 + 1)])
                             for h in range(HEADS)], axis=0)
        row = lax.broadcasted_iota(jnp.int32, (rows, tk), 0) & (tq - 1)
        s = jnp.where(lax.broadcasted_iota(jnp.int32, (rows, tk), 1) <= row, s, NEG)
        update(s, lambda pr, pb: _dot(pb, vb[:, LANES * pr:LANES * (pr + 1)]))

    @pl.when(j > 0)
    def _():
        tk = kct_ref.shape[3]
        gk = gk_ref[0]

        def scores(g):
            kt = kct_ref[0, 4 * g:4 * g + 4].reshape(4 * FOX_DIM, tk).astype(BF)
            gk_rows = jnp.concatenate([jnp.broadcast_to(gk[h:h + 1, :], (tq, tk)) for h in range(4 * g, 4 * g + 4)],
                                      axis=0)
            qg = qbd_ref[4 * tq * g:4 * tq * (g + 1), 4 * FOX_DIM * g:4 * FOX_DIM * (g + 1)]
            return _dot(qg, kt) + (_lanes(fq_ref[4 * tq * g:4 * tq * (g + 1)], tk) - gk_rows)

        def pv(pr, pb):
            return _dot_nt(pb, vct_ref[0, 2 * pr:2 * pr + 2].reshape(2 * FOX_DIM, tk).astype(BF))

        s0 = scores(0)
        s1 = scores(1)
        update(s0, pv, 0, 1)
        update(s1, pv, 1, 2)

    @pl.when(j == n_cache)
    def _():
        first = lax.broadcasted_iota(jnp.int32, (tq, LANES), 1) < FOX_DIM
        o = acc_ref[...] / l_ref[...]
        for pr in range(N_PAIR):
            o_ref[:, LANES * pr:LANES * (pr + 1)] = jnp.where(
                first, o[2 * tq * pr:2 * tq * pr + tq], o[2 * tq * pr + tq:2 * tq * (pr + 1)]).astype(o_ref.dtype)


def _fox_sample(qx, kx, vb, fqs, kct, vct, gk, *, batch, seq, tkc):
    past = kct.shape[3]
    n_cache = past // tkc
    kern = functools.partial(_fox_sample_kernel, n_cache=n_cache)
    row = lambda w: pl.BlockSpec((seq, w), lambda b, j: (b, 0))
    return pl.pallas_call(
        kern, grid=(batch, n_cache + 1),
        in_specs=[row(2 * FOX_W), row(2 * FOX_W), row(FOX_W), row(LANES),
                  pl.BlockSpec((1, HEADS, FOX_DIM, tkc), lambda b, j: (b, 0, 0, jnp.maximum(j - 1, 0))),
                  pl.BlockSpec((1, HEADS, FOX_DIM, tkc), lambda b, j: (b, 0, 0, jnp.maximum(j - 1, 0))),
                  pl.BlockSpec((1, HEADS, tkc), lambda b, j: (b, 0, jnp.maximum(j - 1, 0)))],
        out_specs=row(FOX_W),
        out_shape=jax.ShapeDtypeStruct((batch * seq, FOX_W), BF),
        scratch_shapes=[pltpu.VMEM((HEADS * seq, FOX_W), BF), pltpu.VMEM((HEADS, seq, 2 * LANES), BF)]
        + [pltpu.VMEM((HEADS * seq, LANES), F32)] * 4,
        compiler_params=_params(2), name="fox_sample")(qx, kx, vb, fqs, kct, vct, gk)


def _chunk_mask(shape, tq, row0, q0, k0):
    assert tq & (tq - 1) == 0 and CHUNK & (CHUNK - 1) == 0
    shift = CHUNK.bit_length() - 1
    qpos = q0 + ((row0 + lax.broadcasted_iota(jnp.int32, shape, 0)) & (tq - 1))
    kpos = k0 + lax.broadcasted_iota(jnp.int32, shape, 1)
    return lax.shift_right_logical(kpos, shift) <= lax.shift_right_logical(qpos, shift)


def _mla_rows_update(s, val, m_ref, l_ref, acc_ref, r0, r1):
    m_prev = m_ref[r0:r1]
    m_new = jnp.maximum(m_prev, jnp.max(s, axis=-1, keepdims=True))
    p = jnp.exp2(s - _lanes(m_new, s.shape[1]))
    alpha = jnp.exp2(m_prev - m_new)
    l_ref[r0:r1] = alpha * l_ref[r0:r1] + jnp.sum(p, axis=-1, keepdims=True)
    m_ref[r0:r1] = m_new
    acc_ref[r0:r1] = acc_ref[r0:r1] * _lanes(alpha, KV_LORA) + _dot(p.astype(BF), val)


def _mla_init_stats(m_ref, l_ref, acc_ref):
    m_ref[...] = jnp.full(m_ref.shape, NEG, F32)
    l_ref[...] = jnp.zeros(l_ref.shape, F32)
    acc_ref[...] = jnp.zeros(acc_ref.shape, F32)


def _mla_finalize(o_ref, wuv_ref, l_ref, acc_ref, tq):
    olat = (acc_ref[...] / _lanes(l_ref[...], KV_LORA)).astype(BF)
    for g in range(2):
        out = None
        for h in range(4 * g, 4 * g + 4):
            part = _dot(olat[h * tq:(h + 1) * tq, :], wuv_ref[h])
            out = part if out is None else out + part
        o_ref[:, 256 * g:256 * (g + 1)] = out.astype(o_ref.dtype)


def _mla_prompt_kernel(qlat_ref, qrope_ref, mkey_ref, wuv_ref, o_ref, qcat_ref, m_ref, l_ref, acc_ref,
                       *, tq, tk, rc):
    i = pl.program_id(1)
    j = pl.program_id(2)
    last_j = (tq * (i + 1) - 1) // tk
    needs_mask = tk * (j + 1) > tq * i + CHUNK

    @pl.when(j == 0)
    def _():
        qrope = qrope_ref[...]
        lane = lax.broadcasted_iota(jnp.int32, (tq, HEADS * MLA_ROPE), 1)
        for h in range(HEADS):
            qcat_ref[h * tq:(h + 1) * tq, :KV_LORA] = qlat_ref[h]
            qcat_ref[h * tq:(h + 1) * tq, KV_LORA:] = jnp.where((lane // MLA_ROPE) == h, qrope,
                                                                jnp.zeros_like(qrope))
        _mla_init_stats(m_ref, l_ref, acc_ref)

    def tile(masked):
        mk = mkey_ref[...]
        starts = list(range(0, HEADS * tq, rc))
        ahead = 1
        pending =[_dot_nt(qcat_ref[r:r + rc], mk) for r in starts[:ahead]]
        for n, r0 in enumerate(starts):
            s = pending.pop(0)
            if n + ahead < len(starts):
                r = starts[n + ahead]
                pending.append(_dot_nt(qcat_ref[r:r + rc], mk))
            if masked:
                s = jnp.where(_chunk_mask(s.shape, tq, r0, tq * i, tk * j), s, NEG)
            _mla_rows_update(s, mk[:, :KV_LORA], m_ref, l_ref, acc_ref, r0, r0 + rc)

    @pl.when((j <= last_j) & jnp.logical_not(needs_mask))
    def _():
        tile(False)

    @pl.when((j <= last_j) & needs_mask)
    def _():
        tile(True)

    @pl.when(j == last_j)
    def _():
        _mla_finalize(o_ref, wuv_ref, l_ref, acc_ref, tq)


def _mla_scratch(rows):
    return [pltpu.VMEM((rows, LANES), F32), pltpu.VMEM((rows, LANES), F32), pltpu.VMEM((rows, KV_LORA), F32)]


def _mla_prompt(qlat, qrope, mkey, wuvp, *, batch, seq, tq, tk, rc):
    nq = seq // tq
    nk = seq // tk

    def key_map(b, i, j):
        return (b * nk + jnp.minimum(j, (tq * (i + 1) - 1) // tk), 0)

    kern = functools.partial(_mla_prompt_kernel, tq=tq, tk=tk, rc=rc)
    return pl.pallas_call(
        kern, grid=(batch, nq, nk),
        in_specs=[pl.BlockSpec((HEADS, tq, KV_LORA), lambda b, i, j: (0, b * nq + i, 0)),
                  pl.BlockSpec((tq, HEADS * MLA_ROPE), lambda b, i, j: (b * nq + i, 0)),
                  pl.BlockSpec((tk, 2 * KV_LORA), key_map),
                  _const_spec(wuvp.shape)],
        out_specs=pl.BlockSpec((tq, FOX_W), lambda b, i, j: (b * nq + i, 0)),
        out_shape=jax.ShapeDtypeStruct((batch * seq, FOX_W), BF),
        scratch_shapes=[pltpu.VMEM((HEADS * tq, 2 * KV_LORA), BF)] + _mla_scratch(HEADS * tq),
        compiler_params=_params(3), name="mla_prompt")(qlat, qrope, mkey, wuvp)


def _mla_sample_kernel(qlat_ref, qrope_ref, mkey_ref, latc_ref, krt_ref, wuv_ref, o_ref,
                       ql_ref, qr_ref, m_ref, l_ref, acc_ref, *, n_cache, tq, past, rc):
    j = pl.program_id(1)
    rows = HEADS * tq

    @pl.when(j == 0)
    def _():
        qrope = qrope_ref[...]
        for h in range(HEADS):
            ql_ref[h * tq:(h + 1) * tq, :] = qlat_ref[h]
            qr_ref[h * tq:(h + 1) * tq, :] = qrope[:, MLA_ROPE * h:MLA_ROPE * (h + 1)]
        _mla_init_stats(m_ref, l_ref, acc_ref)
        mk = mkey_ref[...]
        s = _dot_nt(ql_ref[...], mk[:, :KV_LORA]) + _dot_nt(qr_ref[...], mk[:, KV_LORA:KV_LORA + MLA_ROPE])
        if past % CHUNK != 0 or tq > CHUNK:
            s = jnp.where(_chunk_mask(s.shape, tq, 0, past, past), s, NEG)
        _mla_rows_update(s, mk[:, :KV_LORA], m_ref, l_ref, acc_ref, 0, rows)

    @pl.when(j > 0)
    def _():
        c = latc_ref[0].astype(BF)
        krt = krt_ref[0].astype(BF)
        def scores(r0):
            return _dot_nt(ql_ref[r0:r0 + rc], c) + _dot(qr_ref[r0:r0 + rc], krt)

        s_next = scores(0)
        for r0 in range(0, rows, rc):
            s = s_next
            if r0 + rc < rows:
                s_next = scores(r0 + rc)
            _mla_rows_update(s, c, m_ref, l_ref, acc_ref, r0, r0 + rc)

    @pl.when(j == n_cache)
    def _():
        _mla_finalize(o_ref, wuv_ref, l_ref, acc_ref, tq)


def _mla_sample(qlat, qrope, mkey, latc, krt, wuvp, *, batch, seq, tkc, rc):
    past = latc.shape[1]
    n_cache = past // tkc
    kern = functools.partial(_mla_sample_kernel, n_cache=n_cache, tq=seq, past=past, rc=rc)
    return pl.pallas_call(
        kern, grid=(batch, n_cache + 1),
        in_specs=[pl.BlockSpec((HEADS, seq, KV_LORA), lambda b, j: (0, b, 0)),
                  pl.BlockSpec((seq, HEADS * MLA_ROPE), lambda b, j: (b, 0)),
                  pl.BlockSpec((seq, 2 * KV_LORA), lambda b, j: (b, 0)),
                  pl.BlockSpec((1, tkc, KV_LORA), lambda b, j: (b, jnp.maximum(j - 1, 0), 0)),
                  pl.BlockSpec((1, MLA_ROPE, tkc), lambda b, j: (b, 0, jnp.maximum(j - 1, 0))),
                  _const_spec(wuvp.shape)],
        out_specs=pl.BlockSpec((seq, FOX_W), lambda b, j: (b, 0)),
        out_shape=jax.ShapeDtypeStruct((batch * seq, FOX_W), BF),
        scratch_shapes=[pltpu.VMEM((HEADS * seq, KV_LORA), BF), pltpu.VMEM((HEADS * seq, MLA_ROPE), BF)]
        + _mla_scratch(HEADS * seq),
        compiler_params=_params(2), name="mla_sample")(qlat, qrope, mkey, latc, krt, wuvp)


def _ffn_kernel(x_ref, fox_ref, mla_ref, pconv_ref, wo_ref, gf_ref, wup_ref, cw_ref, wdn_ref, gfin_ref,
                y_ref, cst_ref, hn_ref, act_ref, ush_ref, carry_ref, *, tm, seq_len):
    n_seg = max(1, tm // seq_len)
    seg = tm // n_seg
    mixed = jnp.concatenate([fox_ref[...], mla_ref[...]], axis=1)
    x1 = x_ref[...] + _dot(mixed, wo_ref[...])
    hn_ref[...] = (x1 * lax.rsqrt(jnp.mean(x1 * x1, axis=-1, keepdims=True) + EPS) * gf_ref[...]).astype(BF)
    y_ref[...] = x1

    if n_seg == 1:
        @pl.when(pl.program_id(0) % (seq_len // tm) == 0)
        def _():
            carry_ref[...] = pconv_ref[0]

    def conv(u, col, slot):
        cols = slice(col, col + FF_CHUNK)
        for s in range(n_seg):
            ush_ref[slot, s, 6:8, :] = carry_ref[:, cols] if n_seg == 1 else pconv_ref[s, :, cols]
            ush_ref[slot, s, 8:8 + seg, :] = u[seg * s:seg * (s + 1), :]
            cst_ref[s, :, cols] = u[seg * (s + 1) - 2:seg * (s + 1), :]
        if n_seg == 1:
            carry_ref[:, cols] = u[tm - 2:tm, :]
        u1 = jnp.concatenate([ush_ref[slot, s, 7:7 + seg, :] for s in range(n_seg)], axis=0)
        u2 = jnp.concatenate([ush_ref[slot, s, 6:6 + seg, :] for s in range(n_seg)], axis=0)
        cw = cw_ref[:, cols]
        return cw[3:4, :] + cw[0:1, :] * u2 + cw[1:2, :] * u1 + cw[2:3, :] * u

    def up(c):
        hn = hn_ref[...]
        g0, v0 = FF_CHUNK * c, D_FF + FF_CHUNK * c
        return _dot(hn, wup_ref[:, g0:g0 + FF_CHUNK]), _dot(hn, wup_ref[:, v0:v0 + FF_CHUNK])

    u_next = up(0)
    for c in range(N_FF):
        ug, uv = u_next
        if c + 1 < N_FF:
            u_next = up(c + 1)
        gate = conv(ug, FF_CHUNK * c, 2 * (c % 2))
        val = conv(uv, D_FF + FF_CHUNK * c, 2 * (c % 2) + 1)
        act_ref[:, FF_CHUNK * c:FF_CHUNK * (c + 1)] = (gate * jax.nn.sigmoid(gate) * val).astype(BF)

    x2 = y_ref[...] + _dot(act_ref[...], wdn_ref[...])
    y_ref[...] = x2 * lax.rsqrt(jnp.mean(x2 * x2, axis=-1, keepdims=True) + EPS) * gfin_ref[...]


def _ffn(x2d, fox, mla, pconv, wo, gf, wup, cw, wdn, gfin, *, seq_len, tm):
    n = x2d.shape[0]
    nt = n // tm
    n_seq = n // seq_len
    n_seg = max(1, tm // seq_len)
    if n_seg == 1:
        tps = seq_len // tm
        st_map = lambda i: (i // tps, 0, 0)
    else:
        st_map = lambda i: (i, 0, 0)
    row = lambda w: pl.BlockSpec((tm, w), lambda i: (i, 0))
    kern = functools.partial(_ffn_kernel, tm=tm, seq_len=seq_len)
    return pl.pallas_call(
        kern, grid=(nt,),
        in_specs=[row(D_MODEL), row(FOX_W), row(FOX_W),
                  pl.BlockSpec((n_seg, CONV_W - 1, 2 * D_FF), st_map),
                  _const_spec(wo.shape), _const_spec(gf.shape), _const_spec(wup.shape), _const_spec(cw.shape),
                  _const_spec(wdn.shape), _const_spec(gfin.shape)],
        out_specs=(row(D_MODEL), pl.BlockSpec((n_seg, CONV_W - 1, 2 * D_FF), st_map)),
        out_shape=(jax.ShapeDtypeStruct((n, D_MODEL), F32),
                   jax.ShapeDtypeStruct((n_seq, CONV_W - 1, 2 * D_FF), F32)),
        scratch_shapes=[pltpu.VMEM((tm, D_MODEL), BF), pltpu.VMEM((tm, D_FF), BF),
                        pltpu.VMEM((4, n_seg, 8 + tm // n_seg, FF_CHUNK), F32),
                        pltpu.VMEM((CONV_W - 1, 2 * D_FF), F32)],
        compiler_params=_params(1), name="ffn")(x2d, fox, mla, pconv, wo, gf, wup, cw, wdn, gfin)


def _rope_tables(pos):
    half = MLA_ROPE // 2
    inv = ROPE_THETA ** (-jnp.arange(half, dtype=F32) / half)
    ang = pos.astype(F32)[:, None] * inv[None, :]
    cos = jnp.tile(jnp.cos(ang), (1, 2 * HEADS))
    sin = jnp.tile(jnp.sin(ang), (1, 2 * HEADS))
    return cos, sin


def _placement():
    place = np.zeros((LANES, 2 * N_PAIR * LANES), np.float32)
    ones = np.zeros((1, 2 * N_PAIR * LANES), np.float32)
    for h in range(HEADS):
        for i in range(3):
            for p in range(N_PAIR):
                place[HEADS * i + h, LANES * p + AUG * h + i] = 1.0
                ones[0, LANES * p + AUG * h + 3 + i] = 1.0
                ones[0, N_PAIR * LANES + LANES * p + AUG * h + i] = 1.0
                place[HEADS * i + h, N_PAIR * LANES + LANES * p + AUG * h + 3 + i] = -1.0
    rep = np.zeros((LANES, HEADS * MLA_ROPE), np.float32)
    for h in range(HEADS):
        for r in range(MLA_ROPE):
            rep[r, MLA_ROPE * h + r] = 1.0
    return jnp.asarray(place, BF), jnp.asarray(ones, F32), jnp.asarray(rep, BF)


def _rot_cols(w):
    half = MLA_ROPE // 2
    return jnp.concatenate([-w[..., half:], w[..., :half]], axis=-1)


def _pad_cols(w, width):
    return jnp.pad(w, ((0, 0), (0, width - w.shape[1])))


def _layer_weights(w_in, b_f, w_q_up, w_uk, w_uv, w_out, w_up, conv_w, conv_b, w_down):
    wq, wk, wv, wf, wqc, wckv, wkr = jnp.split(w_in, IN_SPLITS, axis=1)
    wa = jnp.concatenate([wq, wk, wv, wqc, wckv, _pad_cols(wkr, LANES), _pad_cols(_rot_cols(wkr), LANES),
                          _pad_cols(wf, LANES)], axis=1).astype(BF)
    bfp = _pad_cols(b_f[None, :], LANES).astype(F32)
    wq3 = w_q_up.reshape(Q_LORA, HEADS, MLA_NOPE + MLA_ROPE)
    wq_nope = wq3[:, :, :MLA_NOPE].reshape(Q_LORA, HEADS * MLA_NOPE)
    wq_rope = wq3[:, :, MLA_NOPE:]
    wq2 = jnp.concatenate([wq_nope, wq_rope.reshape(Q_LORA, -1), _rot_cols(wq_rope).reshape(Q_LORA, -1)],
                          axis=1).astype(BF)
    eye = jnp.eye(4, dtype=F32)
    wk = jnp.transpose(w_uk, (1, 2, 0)).reshape(2, 4, 1, MLA_NOPE, KV_LORA)
    wukp = (wk * eye[None, :, :, None, None]).reshape(HEADS, 4 * MLA_NOPE, KV_LORA)
    wv = jnp.transpose(w_uv, (1, 0, 2)).reshape(2, 4, KV_LORA, 1, MLA_VDIM)
    wuvp = (wv * eye[None, :, None, :, None]).reshape(HEADS, KV_LORA, 4 * MLA_VDIM)
    wup = w_up.astype(BF)
    cw = jnp.concatenate([conv_w, conv_b[None, :]], axis=0)
    wdn = w_down.astype(BF)
    return wa, bfp, wq2, wukp.astype(BF), wuvp.astype(BF), w_out.astype(BF), wup, cw, wdn


def kernel(x_prompt, x_sample, cache_fox_k, cache_fox_v, cache_fox_logf, cache_mla_latent, cache_mla_krope,
           state_ffn_conv, attn_norm, w_in, b_forget, q_norm, w_q_up, kv_norm, w_uk, w_uv, w_out, ffn_norm,
           w_up, conv_w, conv_b, w_down, final_norm):
    assert attn_norm.shape[0] == 1, "single-layer stack"
    bp, tp, _ = x_prompt.shape
    bs, ts, _ = x_sample.shape
    past = cache_fox_k.shape[2]
    assert ts == CHUNK and past % CHUNK == 0

    wa, bfp, wq2, wukp, wuvp, wo, wup, cw, wdn = _layer_weights(
        w_in[0], b_forget[0], w_q_up[0], w_uk[0], w_uv[0], w_out[0], w_up[0], conv_w[0], conv_b[0], w_down[0])
    place, ones, rep = _placement()
    g_attn = attn_norm[0][None, :]
    g_q = q_norm[0][None, :]
    g_kv = kv_norm[0][None, :]
    g_ffn = ffn_norm[0][None, :]
    g_fin = final_norm[None, :]

    tm = 512
    outs = []
    for grp in ("prompt", "sample"):
        if grp == "prompt":
            x2d = x_prompt.reshape(bp * tp, D_MODEL)
            b, t = bp, tp
            cos, sin = _rope_tables(jnp.arange(tp, dtype=jnp.int32))
            pconv = jnp.zeros((bp, CONV_W - 1, 2 * D_FF), F32)
        else:
            x2d = x_sample.reshape(bs * ts, D_MODEL)
            b, t = bs, ts
            cos, sin = _rope_tables(past + jnp.arange(ts, dtype=jnp.int32))
            cos, sin = jnp.tile(cos, (tm // ts, 1)), jnp.tile(sin, (tm // ts, 1))
            pconv = state_ffn_conv[0]

        (k, v, logf, lat, kr, qx, kx, vb, mkey, qlat, qrope, fqs) = _proj(
            x2d, cos, sin, g_attn, wa, bfp, g_q, g_kv, wq2, wukp, place, ones, rep, seq_len=t, tm=tm)

        if grp == "prompt":
            fox = _fox_prompt(qx, kx, vb, batch=b, seq=t, t=512)
            mla = _mla_prompt(qlat, qrope, mkey, wuvp, batch=b, seq=t, tq=512, tk=512, rc=512)
        else:
            gk = _fox_prep(jnp.transpose(cache_fox_logf[0], (0, 2, 1)))
            kct = jnp.transpose(cache_fox_k[0], (0, 2, 3, 1))
            vct = jnp.transpose(cache_fox_v[0], (0, 2, 3, 1))
            krt = jnp.transpose(cache_mla_krope[0], (0, 2, 1))
            fox = _fox_sample(qx, kx, vb, fqs, kct, vct, gk, batch=b, seq=t, tkc=min(past, 2048))
            mla = _mla_sample(qlat, qrope, mkey, cache_mla_latent[0], krt, wuvp, batch=b, seq=t,
                              tkc=min(past, 1024), rc=256)

        def small_state(a, w, b=b, t=t):
            return (jnp.transpose(a, (0, 2, 1)) if a.ndim == 3 else a.reshape(b, t, w))[None]

        y, cst = _ffn(x2d, fox, mla, pconv, wo, g_ffn, wup, cw, wdn, g_fin, seq_len=t, tm=tm)
        outs.append((y.reshape(b, t, D_MODEL),
                     k.reshape(1, b, t, HEADS, FOX_DIM), v.reshape(1, b, t, HEADS, FOX_DIM),
                     small_state(logf, HEADS), lat.reshape(1, b, t, KV_LORA), small_state(kr, MLA_ROPE),
                     cst[None]))
    (yp, *st_p), (ys, *st_s) = outs
    return (yp, ys, *st_p, *st_s)
```

```python
import functools

import numpy as np
import jax
import jax.numpy as jnp
from jax import lax
from jax.experimental import pallas as pl
from jax.experimental.pallas import tpu as pltpu

D_MODEL = 1024
CHUNK = 64
HEADS = 8
FOX_DIM = 64
MLA_NOPE = 64
MLA_ROPE = 32
MLA_VDIM = 64
Q_LORA = 384
KV_LORA = 256
D_FF = 2816
CONV_W = 3
ROPE_THETA = 10000.0
EPS = 1e-6
NEG = -1e30
LOG2E = 1.4426950408889634

FOX_W = HEADS * FOX_DIM
IN_SIZES = [FOX_W, FOX_W, FOX_W, HEADS, Q_LORA, KV_LORA, MLA_ROPE]
IN_SPLITS = [int(s) for s in np.cumsum(IN_SIZES)[:-1]]

LANES = 128
AUG = 16
N_PAIR = HEADS // 2
FF_CHUNK = 256
N_FF = D_FF // FF_CHUNK
VMEM_LIMIT = 56 * 1024 * 1024

C_Q, C_K, C_V = 0, 512, 1024
C_QC = 1536
C_CKV = 1920
C_KR = 2176
C_KRR = 2304
C_F = 2432
N_A = 2560

BF = jnp.bfloat16
F32 = jnp.float32


def _dot(a, b):
    return jnp.dot(a, b, preferred_element_type=F32)


def _dot_nt(a, b):
    return lax.dot_general(a, b, (((1,), (1,)), ((), ())), preferred_element_type=F32)


def _split3(x):
    hi = x.astype(BF)
    r1 = x - hi.astype(F32)
    mid = r1.astype(BF)
    lo = (r1 - mid.astype(F32)).astype(BF)
    return hi, mid, lo


def _log_sigmoid(x):
    return jnp.minimum(x, 0.0) - jnp.log1p(jnp.exp(-jnp.abs(x)))


def _lanes(x, n):
    if n % LANES == 0:
        return x if n == LANES else jnp.concatenate([x] * (n // LANES), axis=1)
    assert n < LANES
    return x[:, :n]


def _params(n_axes):
    return pltpu.CompilerParams(dimension_semantics=("arbitrary",) * n_axes, vmem_limit_bytes=VMEM_LIMIT)


def _const_spec(shape):
    nd = len(shape)
    return pl.BlockSpec(shape, lambda *_: (0,) * nd, pipeline_mode=pl.Buffered(1))


def _proj_kernel(x_ref, cos_ref, sin_ref, cosm_ref, sinm_ref, g_ref, wa_ref, bf_ref, gq_ref, gkv_ref,
                 w1_ref, w2_ref, w3_ref, place_ref, ones_ref,
                 k_ref, v_ref, logf_ref, lat_ref, kr_ref, qx_ref, kx_ref, vb_ref, o1_ref, o2_ref, o3_ref,
                 fqs_ref, carry_ref, *, tm, seq_len, mla_scale, absorbed):
    x = x_ref[...]
    ms = jnp.mean(x * x, axis=-1, keepdims=True)
    hn = (x * lax.rsqrt(ms + EPS) * g_ref[...]).astype(BF)
    z = _dot(hn, wa_ref[...])

    zk = z[:, C_K:C_K + FOX_W]
    zv = z[:, C_V:C_V + FOX_W]
    for h in range(HEADS):
        k_ref[pl.ds(h, tm, stride=HEADS), :] = zk[:, FOX_DIM * h:FOX_DIM * (h + 1)]
        v_ref[pl.ds(h, tm, stride=HEADS), :] = zv[:, FOX_DIM * h:FOX_DIM * (h + 1)]
    vb_ref[...] = zv.astype(BF)

    lane = lax.broadcasted_iota(jnp.int32, (tm, LANES), 1)
    logf = _log_sigmoid(z[:, C_F:C_F + LANES] + bf_ref[...])
    logf = jnp.where(lane < HEADS, logf, 0.0)
    if seq_len >= tm:
        logf_ref[0] = logf.T[:HEADS, :]
    else:
        logf_ref[...] = logf[:, :HEADS]
    row = lax.broadcasted_iota(jnp.int32, (tm, tm), 0)
    col = lax.broadcasted_iota(jnp.int32, (tm, tm), 1)
    keep = col <= row
    if seq_len < tm:
        keep = keep & ((col // seq_len) == (row // seq_len))
    tri = jnp.where(keep, 1.0, 0.0).astype(BF)
    hi, mid, lo = _split3(logf)
    fcum = _dot(tri, hi) + _dot(tri, mid) + _dot(tri, lo)
    if seq_len > tm:
        @pl.when(pl.program_id(0) % (seq_len // tm) == 0)
        def _():
            carry_ref[...] = jnp.zeros_like(carry_ref)
        fcum = fcum + carry_ref[0:1, :]
        carry_ref[0:1, :] = fcum[tm - 1:tm, :]

    fqs = fcum * LOG2E
    fqs_ref[...] = fqs
    hi, mid, lo = _split3(fqs)
    comb = hi.astype(F32) + pltpu.roll(mid.astype(F32), HEADS, 1) + pltpu.roll(lo.astype(F32), 2 * HEADS, 1)
    aug = _dot(comb.astype(BF), place_ref[...]) + ones_ref[...]
    zq = z[:, C_Q:C_Q + FOX_W] * (FOX_DIM ** -0.5 * LOG2E)
    for p in range(N_PAIR):
        lo_, hi_ = 2 * LANES * p, 2 * LANES * p + LANES
        qx_ref[:, lo_:hi_] = zq[:, LANES * p:LANES * (p + 1)].astype(BF)
        qx_ref[:, hi_:hi_ + LANES] = aug[:, LANES * p:LANES * (p + 1)].astype(BF)
        kx_ref[:, lo_:hi_] = zk[:, LANES * p:LANES * (p + 1)].astype(BF)
        kx_ref[:, hi_:hi_ + LANES] = aug[:, FOX_W + LANES * p:FOX_W + LANES * (p + 1)].astype(BF)

    zc = z[:, C_CKV:C_CKV + KV_LORA]
    ckv = zc * lax.rsqrt(jnp.mean(zc * zc, axis=-1, keepdims=True) + EPS) * gkv_ref[...]
    lat_ref[...] = ckv
    cos = cos_ref[...]
    sin = sin_ref[...]
    krb = z[:, C_KR:C_KR + LANES] * cos[:, :LANES] + z[:, C_KRR:C_KRR + LANES] * sin[:, :LANES]
    if seq_len >= tm:
        kr_ref[0] = krb.T[:MLA_ROPE, :]
    else:
        kr_ref[...] = krb[:, :MLA_ROPE]
    cb = ckv.astype(BF)

    zqc = z[:, C_QC:C_QC + Q_LORA]
    qcn = (zqc * lax.rsqrt(jnp.mean(zqc * zqc, axis=-1, keepdims=True) + EPS) * gq_ref[...]).astype(BF)
    zq2 = _dot(qcn, w1_ref[...])
    if absorbed:
        mkey_ref, qlat_ref, qrope_ref = o1_ref, o2_ref, o3_ref
        mkey_ref[:, :KV_LORA] = cb
        mkey_ref[:, KV_LORA:] = _dot(krb.astype(BF), w3_ref[...]).astype(BF)
        nr = HEADS * MLA_ROPE
        qr = zq2[:, FOX_W:FOX_W + nr] * cos + zq2[:, FOX_W + nr:FOX_W + 2 * nr] * sin
        qrope_ref[...] = (qr * mla_scale).astype(BF)
        for h in range(HEADS):
            g = h // 4
            qn = zq2[:, 256 * g:256 * (g + 1)].astype(BF)
            qlat_ref[h] = (_dot(qn, w2_ref[h]) * mla_scale).astype(BF)
    else:
        qm_ref, km_ref, vm_ref = o1_ref, o2_ref, o3_ref
        hw = HEADS * LANES
        cosm = jnp.concatenate([cosm_ref[...]] * HEADS, axis=1)
        sinm = jnp.concatenate([sinm_ref[...]] * HEADS, axis=1)
        qm_ref[...] = ((zq2[:, :hw] * cosm + zq2[:, hw:] * sinm) * mla_scale).astype(BF)
        krs = pltpu.roll(krb, MLA_NOPE, 1)
        km_ref[...] = (_dot(cb, w2_ref[...]) + jnp.concatenate([krs] * HEADS, axis=1)).astype(BF)
        vm_ref[...] = _dot(cb, w3_ref[...]).astype(BF)


def _proj(x2d, cos, sin, cosm, sinm, g, wa, bfp, gq, gkv, w1, w2, w3, place, ones, *, seq_len, tm, absorbed):
    n = x2d.shape[0]
    nt = n // tm
    if seq_len >= tm:
        tps = seq_len // tm
        tab_map = lambda i: (i % tps, 0)
    else:
        tab_map = lambda i: (0, 0)
    row = lambda w: pl.BlockSpec((tm, w), lambda i: (i, 0))
    if seq_len >= tm:
        small = lambda w: jax.ShapeDtypeStruct((n // seq_len, w, seq_len), F32)
        small_spec = lambda w: pl.BlockSpec((1, w, tm), lambda i: (i // tps, 0, i % tps))
    else:
        small = lambda w: jax.ShapeDtypeStruct((n, w), F32)
        small_spec = row
    if absorbed:
        mla_shapes = (jax.ShapeDtypeStruct((n, 2 * KV_LORA), BF), jax.ShapeDtypeStruct((HEADS, n, KV_LORA), BF),
                      jax.ShapeDtypeStruct((n, HEADS * MLA_ROPE), BF))
        mla_specs = (row(2 * KV_LORA), pl.BlockSpec((HEADS, tm, KV_LORA), lambda i: (0, i, 0)),
                     row(HEADS * MLA_ROPE))
    else:
        mla_shapes = (jax.ShapeDtypeStruct((n, HEADS * LANES), BF), jax.ShapeDtypeStruct((n, HEADS * LANES), BF),
                      jax.ShapeDtypeStruct((n, HEADS * MLA_VDIM), BF))
        mla_specs = (row(HEADS * LANES), row(HEADS * LANES), row(HEADS * MLA_VDIM))
    out_shape = (
        jax.ShapeDtypeStruct((n * HEADS, FOX_DIM), F32),
        jax.ShapeDtypeStruct((n * HEADS, FOX_DIM), F32),
        small(HEADS),
        jax.ShapeDtypeStruct((n, KV_LORA), F32),
        small(MLA_ROPE),
        jax.ShapeDtypeStruct((n, 2 * FOX_W), BF),
        jax.ShapeDtypeStruct((n, 2 * FOX_W), BF),
        jax.ShapeDtypeStruct((n, FOX_W), BF),
        *mla_shapes,
        jax.ShapeDtypeStruct((n, LANES), F32),
    )
    kv_spec = pl.BlockSpec((tm * HEADS, FOX_DIM), lambda i: (i, 0))
    out_specs = (kv_spec, kv_spec, small_spec(HEADS), row(KV_LORA), small_spec(MLA_ROPE), row(2 * FOX_W),
                 row(2 * FOX_W), row(FOX_W), *mla_specs, row(LANES))
    in_specs = [row(D_MODEL),
                pl.BlockSpec((tm, 2 * LANES), tab_map), pl.BlockSpec((tm, 2 * LANES), tab_map),
                pl.BlockSpec((tm, LANES), tab_map), pl.BlockSpec((tm, LANES), tab_map),
                _const_spec(g.shape), _const_spec(wa.shape), _const_spec(bfp.shape), _const_spec(gq.shape),
                _const_spec(gkv.shape), _const_spec(w1.shape), _const_spec(w2.shape), _const_spec(w3.shape),
                _const_spec(place.shape), _const_spec(ones.shape)]
    kern = functools.partial(_proj_kernel, tm=tm, seq_len=seq_len, absorbed=absorbed,
                             mla_scale=float((MLA_NOPE + MLA_ROPE) ** -0.5 * LOG2E))
    return pl.pallas_call(
        kern, grid=(nt,), in_specs=in_specs, out_specs=out_specs, out_shape=out_shape,
        scratch_shapes=[pltpu.VMEM((8, LANES), F32)],
        compiler_params=_params(1), name="proj")(
            x2d, cos, sin, cosm, sinm, g, wa, bfp, gq, gkv, w1, w2, w3, place, ones)


def _fox_prep_kernel(plt_ref, g_ref):
    x = plt_ref[0]
    n = x.shape[1]
    lane = lax.broadcasted_iota(jnp.int32, x.shape, 1)
    shift = 1
    while shift < n:
        x = x + jnp.where(lane >= shift, pltpu.roll(x, shift, 1), 0.0)
        shift *= 2
    g_ref[0] = (x - x[:, n - 1:n]) * LOG2E


def _fox_prep(plt):
    b, h, p = plt.shape
    return pl.pallas_call(
        _fox_prep_kernel, grid=(b,),
        in_specs=[pl.BlockSpec((1, h, p), lambda i: (i, 0, 0))],
        out_specs=pl.BlockSpec((1, h, p), lambda i: (i, 0, 0)),
        out_shape=jax.ShapeDtypeStruct((b, h, p), F32),
        compiler_params=_params(1), name="fox_prep")(plt)


def _softmax_step(s, m_prev):
    m_new = jnp.maximum(m_prev, jnp.max(s, axis=-1, keepdims=True))
    p = jnp.exp2(s - _lanes(m_new, s.shape[1]))
    alpha = jnp.exp2(m_prev - m_new)
    return p.astype(BF), alpha, m_new


def _fox_qpad(qx, h, kind="fox"):
    p, hh = divmod(h, 2)
    lane = lax.broadcasted_iota(jnp.int32, (qx.shape[0], 2 * LANES), 1)
    blk = qx[:, 2 * LANES * p:2 * LANES * (p + 1)]
    if kind == "fox":
        keep = ((lane >= FOX_DIM * hh) & (lane < FOX_DIM * (hh + 1))) | (
            (lane >= LANES + AUG * h) & (lane < LANES + AUG * (h + 1)))
    else:
        keep = (lane >= LANES * hh) & (lane < LANES * (hh + 1))
    return jnp.where(keep, blk, jnp.zeros_like(blk))


def _fox_prompt_tile(kx, vb, qpad_ref, m_ref, l_ref, acc_ref, masked, kind):
    tq = qpad_ref.shape[1]
    tk = kx.shape[0]
    first = lax.broadcasted_iota(jnp.int32, (tq, LANES), 1) < FOX_DIM
    ones = jnp.ones((tk, LANES), BF)
    if masked:
        col = lax.broadcasted_iota(jnp.int32, (tq, tk), 1)
        row = lax.broadcasted_iota(jnp.int32, (tq, tk), 0)
        if kind == "mla":
            shift = CHUNK.bit_length() - 1
            col, row = lax.shift_right_logical(col, shift), lax.shift_right_logical(row, shift)
        causal = col <= row
    def scores(h):
        return _dot_nt(qpad_ref[h], kx[:, 2 * LANES * (h // 2):2 * LANES * (h // 2 + 1)])

    s_next = scores(0)
    for p in range(N_PAIR):
        vpa = jnp.concatenate([vb[:, LANES * p:LANES * (p + 1)], ones], axis=1)
        alphas, pvs = [], []
        for hh in range(2):
            h = 2 * p + hh
            s = s_next
            if h + 1 < HEADS:
                s_next = scores(h + 1)
            if masked:
                s = jnp.where(causal, s, NEG)
            pb, alpha, m_new = _softmax_step(s, m_ref[h])
            pv = _dot(pb, vpa)
            l_ref[h] = alpha * l_ref[h] + pv[:, LANES:]
            m_ref[h] = m_new
            alphas.append(alpha)
            pvs.append(pv[:, :LANES])
        acc_ref[p] = acc_ref[p] * jnp.where(first, alphas[0], alphas[1]) + jnp.where(first, pvs[0], pvs[1])


def _fox_prompt_kernel(qx_ref, kx_ref, vb_ref, o_ref, qpad_ref, m_ref, l_ref, acc_ref, *, kind):
    i = pl.program_id(1)
    j = pl.program_id(2)

    @pl.when(j == 0)
    def _():
        qx = qx_ref[...]
        for h in range(HEADS):
            qpad_ref[h] = _fox_qpad(qx, h, kind)
        m_ref[...] = jnp.full(m_ref.shape, NEG, F32)
        l_ref[...] = jnp.zeros(l_ref.shape, F32)
        acc_ref[...] = jnp.zeros(acc_ref.shape, F32)

    @pl.when(j < i)
    def _():
        _fox_prompt_tile(kx_ref[...], vb_ref[...], qpad_ref, m_ref, l_ref, acc_ref, False, kind)

    @pl.when(j == i)
    def _():
        _fox_prompt_tile(kx_ref[...], vb_ref[...], qpad_ref, m_ref, l_ref, acc_ref, True, kind)
        first = lax.broadcasted_iota(jnp.int32, (qpad_ref.shape[1], LANES), 1) < FOX_DIM
        for p in range(N_PAIR):
            l_sel = jnp.where(first, l_ref[2 * p], l_ref[2 * p + 1])
            o_ref[:, LANES * p:LANES * (p + 1)] = (acc_ref[p] / l_sel).astype(o_ref.dtype)


def _fox_prompt(qx, kx, vb, *, batch, seq, t, kind):
    nq = seq // t
    assert CHUNK & (CHUNK - 1) == 0 and t % CHUNK == 0
    return pl.pallas_call(
        functools.partial(_fox_prompt_kernel, kind=kind), grid=(batch, nq, nq),
        in_specs=[pl.BlockSpec((t, 2 * FOX_W), lambda b, i, j: (b * nq + i, 0)),
                  pl.BlockSpec((t, 2 * FOX_W), lambda b, i, j: (b * nq + jnp.minimum(i, j), 0)),
                  pl.BlockSpec((t, FOX_W), lambda b, i, j: (b * nq + jnp.minimum(i, j), 0))],
        out_specs=pl.BlockSpec((t, FOX_W), lambda b, i, j: (b * nq + i, 0)),
        out_shape=jax.ShapeDtypeStruct((batch * seq, FOX_W), BF),
        scratch_shapes=[pltpu.VMEM((HEADS, t, 2 * LANES), BF), pltpu.VMEM((HEADS, t, LANES), F32),
                        pltpu.VMEM((HEADS, t, LANES), F32), pltpu.VMEM((N_PAIR, t, LANES), F32)],
        compiler_params=_params(3), name=kind + "_prompt")(qx, kx, vb)


def _fox_sample_kernel(qx_ref, kx_ref, vb_ref, fqs_ref, kct_ref, vct_ref, gk_ref, o_ref,
                       qbd_ref, qpad_ref, fq_ref, m_ref, l_ref, acc_ref, *, n_cache):
    j = pl.program_id(1)
    tq = qx_ref.shape[0]
    rows = HEADS * tq

    def update(s, pv_fn, g0=0, g1=2):
        a, b = 4 * tq * g0, 4 * tq * g1
        m_prev = m_ref[a:b]
        m_new = jnp.maximum(m_prev, jnp.max(s, axis=-1, keepdims=True))
        p = jnp.exp2(s - _lanes(m_new, s.shape[1]))
        alpha = jnp.exp2(m_prev - m_new)
        l_ref[a:b] = alpha * l_ref[a:b] + jnp.sum(p, axis=-1, keepdims=True)
        m_ref[a:b] = m_new
        pb = p.astype(BF)
        for pr in range(2 * g0, 2 * g1):
            r0, r1 = 2 * tq * pr, 2 * tq * (pr + 1)
            acc_ref[r0:r1] = acc_ref[r0:r1] * alpha[r0 - a:r1 - a] + pv_fn(pr, pb[r0 - a:r1 - a])

    @pl.when(j == 0)
    def _():
        qx = qx_ref[...]
        fqs = fqs_ref[...]
        qc = jnp.concatenate([qx[:, 2 * LANES * p:2 * LANES * p + LANES] for p in range(N_PAIR)], axis=1)
        lane = lax.broadcasted_iota(jnp.int32, qc.shape, 1)
        for h in range(HEADS):
            qbd_ref[h * tq:(h + 1) * tq, :] = jnp.where((lane // FOX_DIM) == h, qc, jnp.zeros_like(qc))
            qpad_ref[h] = _fox_qpad(qx, h)
            fq_ref[h * tq:(h + 1) * tq, :] = jnp.broadcast_to(fqs[:, h:h + 1], (tq, LANES))
        m_ref[...] = jnp.full(m_ref.shape, NEG, F32)
        l_ref[...] = jnp.zeros(l_ref.shape, F32)
        acc_ref[...] = jnp.zeros(acc_ref.shape, F32)
        kx = kx_ref[...]
        vb = vb_ref[...]
        tk = kx.shape[0]
        s = jnp.concatenate([_dot_nt(qpad_ref[h], kx[:, 2 * LANES * (h // 2):2 * LANES * (h // 2 + 1)])
                             for h in range(HEADS)], axis=0)
        row = lax.broadcasted_iota(jnp.int32, (rows, tk), 0) & (tq - 1)
        s = jnp.where(lax.broadcasted_iota(jnp.int32, (rows, tk), 1) <= row, s, NEG)
        update(s, lambda pr, pb: _dot(pb, vb[:, LANES * pr:LANES * (pr + 1)]))

    @pl.when(j > 0)
    def _():
        tk = kct_ref.shape[3]
        gk = gk_ref[0]

        def scores(g):
            kt = kct_ref[0, 4 * g:4 * g + 4].reshape(4 * FOX_DIM, tk).astype(BF)
            gk_rows = jnp.concatenate([jnp.broadcast_to(gk[h:h + 1, :], (tq, tk)) for h in range(4 * g, 4 * g + 4)],
                                      axis=0)
            qg = qbd_ref[4 * tq * g:4 * tq * (g + 1), 4 * FOX_DIM * g:4 * FOX_DIM * (g + 1)]
            return _dot(qg, kt) + (_lanes(fq_ref[4 * tq * g:4 * tq * (g + 1)], tk) - gk_rows)

        def pv(pr, pb):
            return _dot_nt(pb, vct_ref[0, 2 * pr:2 * pr + 2].reshape(2 * FOX_DIM, tk).astype(BF))

        s0 = scores(0)
        s1 = scores(1)
        update(s0, pv, 0, 1)
        update(s1, pv, 1, 2)

    @pl.when(j == n_cache)
    def _():
        first = lax.broadcasted_iota(jnp.int32, (tq, LANES), 1) < FOX_DIM
        o = acc_ref[...] / l_ref[...]
        for pr in range(N_PAIR):
            o_ref[:, LANES * pr:LANES * (pr + 1)] = jnp.where(
                first, o[2 * tq * pr:2 * tq * pr + tq], o[2 * tq * pr + tq:2 * tq * (pr + 1)]).astype(o_ref.dtype)


def _fox_sample(qx, kx, vb, fqs, kct, vct, gk, *, batch, seq, tkc):
    past = kct.shape[3]
    n_cache = past // tkc
    kern = functools.partial(_fox_sample_kernel, n_cache=n_cache)
    row = lambda w: pl.BlockSpec((seq, w), lambda b, j: (b, 0))
    return pl.pallas_call(
        kern, grid=(batch, n_cache + 1),
        in_specs=[row(2 * FOX_W), row(2 * FOX_W), row(FOX_W), row(LANES),
                  pl.BlockSpec((1, HEADS, FOX_DIM, tkc), lambda b, j: (b, 0, 0, jnp.maximum(j - 1, 0))),
                  pl.BlockSpec((1, HEADS, FOX_DIM, tkc), lambda b, j: (b, 0, 0, jnp.maximum(j - 1, 0))),
                  pl.BlockSpec((1, HEADS, tkc), lambda b, j: (b, 0, jnp.maximum(j - 1, 0)))],
        out_specs=row(FOX_W),
        out_shape=jax.ShapeDtypeStruct((batch * seq, FOX_W), BF),
        scratch_shapes=[pltpu.VMEM((HEADS * seq, FOX_W), BF), pltpu.VMEM((HEADS, seq, 2 * LANES), BF)]
        + [pltpu.VMEM((HEADS * seq, LANES), F32)] * 4,
        compiler_params=_params(2), name="fox_sample")(qx, kx, vb, fqs, kct, vct, gk)


def _chunk_mask(shape, tq, row0, q0, k0):
    assert tq & (tq - 1) == 0 and CHUNK & (CHUNK - 1) == 0
    shift = CHUNK.bit_length() - 1
    qpos = q0 + ((row0 + lax.broadcasted_iota(jnp.int32, shape, 0)) & (tq - 1))
    kpos = k0 + lax.broadcasted_iota(jnp.int32, shape, 1)
    return lax.shift_right_logical(kpos, shift) <= lax.shift_right_logical(qpos, shift)


def _mla_rows_update(s, val, m_ref, l_ref, acc_ref, r0, r1):
    m_prev = m_ref[r0:r1]
    m_new = jnp.maximum(m_prev, jnp.max(s, axis=-1, keepdims=True))
    p = jnp.exp2(s - _lanes(m_new, s.shape[1]))
    alpha = jnp.exp2(m_prev - m_new)
    l_ref[r0:r1] = alpha * l_ref[r0:r1] + jnp.sum(p, axis=-1, keepdims=True)
    m_ref[r0:r1] = m_new
    acc_ref[r0:r1] = acc_ref[r0:r1] * _lanes(alpha, KV_LORA) + _dot(p.astype(BF), val)


def _mla_init_stats(m_ref, l_ref, acc_ref):
    m_ref[...] = jnp.full(m_ref.shape, NEG, F32)
    l_ref[...] = jnp.zeros(l_ref.shape, F32)
    acc_ref[...] = jnp.zeros(acc_ref.shape, F32)


def _mla_finalize(o_ref, wuv_ref, l_ref, acc_ref, tq):
    olat = (acc_ref[...] / _lanes(l_ref[...], KV_LORA)).astype(BF)
    for g in range(2):
        out = None
        for h in range(4 * g, 4 * g + 4):
            part = _dot(olat[h * tq:(h + 1) * tq, :], wuv_ref[h])
            out = part if out is None else out + part
        o_ref[:, 256 * g:256 * (g + 1)] = out.astype(o_ref.dtype)


def _mla_scratch(rows):
    return [pltpu.VMEM((rows, LANES), F32), pltpu.VMEM((rows, LANES), F32), pltpu.VMEM((rows, KV_LORA), F32)]


def _mla_sample_kernel(qlat_ref, qrope_ref, mkey_ref, latc_ref, krt_ref, wuv_ref, o_ref,
                       ql_ref, qr_ref, m_ref, l_ref, acc_ref, *, n_cache, tq, past, rc):
    j = pl.program_id(1)
    rows = HEADS * tq

    @pl.when(j == 0)
    def _():
        qrope = qrope_ref[...]
        for h in range(HEADS):
            ql_ref[h * tq:(h + 1) * tq, :] = qlat_ref[h]
            qr_ref[h * tq:(h + 1) * tq, :] = qrope[:, MLA_ROPE * h:MLA_ROPE * (h + 1)]
        _mla_init_stats(m_ref, l_ref, acc_ref)
        mk = mkey_ref[...]
        s = _dot_nt(ql_ref[...], mk[:, :KV_LORA]) + _dot_nt(qr_ref[...], mk[:, KV_LORA:KV_LORA + MLA_ROPE])
        if past % CHUNK != 0 or tq > CHUNK:
            s = jnp.where(_chunk_mask(s.shape, tq, 0, past, past), s, NEG)
        _mla_rows_update(s, mk[:, :KV_LORA], m_ref, l_ref, acc_ref, 0, rows)

    @pl.when(j > 0)
    def _():
        c = latc_ref[0].astype(BF)
        krt = krt_ref[0].astype(BF)
        def scores(r0):
            return _dot_nt(ql_ref[r0:r0 + rc], c) + _dot(qr_ref[r0:r0 + rc], krt)

        s_next = scores(0)
        for r0 in range(0, rows, rc):
            s = s_next
            if r0 + rc < rows:
                s_next = scores(r0 + rc)
            _mla_rows_update(s, c, m_ref, l_ref, acc_ref, r0, r0 + rc)

    @pl.when(j == n_cache)
    def _():
        _mla_finalize(o_ref, wuv_ref, l_ref, acc_ref, tq)


def _mla_sample(qlat, qrope, mkey, latc, krt, wuvp, *, batch, seq, tkc, rc):
    past = latc.shape[1]
    n_cache = past // tkc
    kern = functools.partial(_mla_sample_kernel, n_cache=n_cache, tq=seq, past=past, rc=rc)
    return pl.pallas_call(
        kern, grid=(batch, n_cache + 1),
        in_specs=[pl.BlockSpec((HEADS, seq, KV_LORA), lambda b, j: (0, b, 0)),
                  pl.BlockSpec((seq, HEADS * MLA_ROPE), lambda b, j: (b, 0)),
                  pl.BlockSpec((seq, 2 * KV_LORA), lambda b, j: (b, 0)),
                  pl.BlockSpec((1, tkc, KV_LORA), lambda b, j: (b, jnp.maximum(j - 1, 0), 0)),
                  pl.BlockSpec((1, MLA_ROPE, tkc), lambda b, j: (b, 0, jnp.maximum(j - 1, 0))),
                  _const_spec(wuvp.shape)],
        out_specs=pl.BlockSpec((seq, FOX_W), lambda b, j: (b, 0)),
        out_shape=jax.ShapeDtypeStruct((batch * seq, FOX_W), BF),
        scratch_shapes=[pltpu.VMEM((HEADS * seq, KV_LORA), BF), pltpu.VMEM((HEADS * seq, MLA_ROPE), BF)]
        + _mla_scratch(HEADS * seq),
        compiler_params=_params(2), name="mla_sample")(qlat, qrope, mkey, latc, krt, wuvp)


def _ffn_kernel(x_ref, fox_ref, mla_ref, pconv_ref, wo_ref, gf_ref, wup_ref, cw_ref, wdn_ref, gfin_ref,
                y_ref, cst_ref, hn_ref, act_ref, ush_ref, carry_ref, *, tm, seq_len):
    n_seg = max(1, tm // seq_len)
    seg = tm // n_seg
    mixed = jnp.concatenate([fox_ref[...], mla_ref[...]], axis=1)
    x1 = x_ref[...] + _dot(mixed, wo_ref[...])
    hn_ref[...] = (x1 * lax.rsqrt(jnp.mean(x1 * x1, axis=-1, keepdims=True) + EPS) * gf_ref[...]).astype(BF)
    y_ref[...] = x1

    if n_seg == 1:
        @pl.when(pl.program_id(0) % (seq_len // tm) == 0)
        def _():
            carry_ref[...] = pconv_ref[0]

    def conv(u, col, slot):
        cols = slice(col, col + FF_CHUNK)
        for s in range(n_seg):
            ush_ref[slot, s, 6:8, :] = carry_ref[:, cols] if n_seg == 1 else pconv_ref[s, :, cols]
            ush_ref[slot, s, 8:8 + seg, :] = u[seg * s:seg * (s + 1), :]
            cst_ref[s, :, cols] = u[seg * (s + 1) - 2:seg * (s + 1), :]
        if n_seg == 1:
            carry_ref[:, cols] = u[tm - 2:tm, :]
        u1 = jnp.concatenate([ush_ref[slot, s, 7:7 + seg, :] for s in range(n_seg)], axis=0)
        u2 = jnp.concatenate([ush_ref[slot, s, 6:6 + seg, :] for s in range(n_seg)], axis=0)
        cw = cw_ref[:, cols]
        return cw[3:4, :] + cw[0:1, :] * u2 + cw[1:2, :] * u1 + cw[2:3, :] * u

    def up(c):
        hn = hn_ref[...]
        g0, v0 = FF_CHUNK * c, D_FF + FF_CHUNK * c
        return _dot(hn, wup_ref[:, g0:g0 + FF_CHUNK]), _dot(hn, wup_ref[:, v0:v0 + FF_CHUNK])

    u_next = up(0)
    for c in range(N_FF):
        ug, uv = u_next
        if c + 1 < N_FF:
            u_next = up(c + 1)
        gate = conv(ug, FF_CHUNK * c, 2 * (c % 2))
        val = conv(uv, D_FF + FF_CHUNK * c, 2 * (c % 2) + 1)
        act_ref[:, FF_CHUNK * c:FF_CHUNK * (c + 1)] = (gate * jax.nn.sigmoid(gate) * val).astype(BF)

    x2 = y_ref[...] + _dot(act_ref[...], wdn_ref[...])
    y_ref[...] = x2 * lax.rsqrt(jnp.mean(x2 * x2, axis=-1, keepdims=True) + EPS) * gfin_ref[...]


def _ffn(x2d, fox, mla, pconv, wo, gf, wup, cw, wdn, gfin, *, seq_len, tm):
    n = x2d.shape[0]
    nt = n // tm
    n_seq = n // seq_len
    n_seg = max(1, tm // seq_len)
    if n_seg == 1:
        tps = seq_len // tm
        st_map = lambda i: (i // tps, 0, 0)
    else:
        st_map = lambda i: (i, 0, 0)
    row = lambda w: pl.BlockSpec((tm, w), lambda i: (i, 0))
    kern = functools.partial(_ffn_kernel, tm=tm, seq_len=seq_len)
    return pl.pallas_call(
        kern, grid=(nt,),
        in_specs=[row(D_MODEL), row(FOX_W), row(FOX_W),
                  pl.BlockSpec((n_seg, CONV_W - 1, 2 * D_FF), st_map),
                  _const_spec(wo.shape), _const_spec(gf.shape), _const_spec(wup.shape), _const_spec(cw.shape),
                  _const_spec(wdn.shape), _const_spec(gfin.shape)],
        out_specs=(row(D_MODEL), pl.BlockSpec((n_seg, CONV_W - 1, 2 * D_FF), st_map)),
        out_shape=(jax.ShapeDtypeStruct((n, D_MODEL), F32),
                   jax.ShapeDtypeStruct((n_seq, CONV_W - 1, 2 * D_FF), F32)),
        scratch_shapes=[pltpu.VMEM((tm, D_MODEL), BF), pltpu.VMEM((tm, D_FF), BF),
                        pltpu.VMEM((4, n_seg, 8 + tm // n_seg, FF_CHUNK), F32),
                        pltpu.VMEM((CONV_W - 1, 2 * D_FF), F32)],
        compiler_params=_params(1), name="ffn")(x2d, fox, mla, pconv, wo, gf, wup, cw, wdn, gfin)


def _rope_tables(pos):
    half = MLA_ROPE // 2
    inv = ROPE_THETA ** (-jnp.arange(half, dtype=F32) / half)
    ang = pos.astype(F32)[:, None] * inv[None, :]
    cos = jnp.tile(jnp.cos(ang), (1, 2 * HEADS))
    sin = jnp.tile(jnp.sin(ang), (1, 2 * HEADS))
    ones = jnp.ones((pos.shape[0], MLA_NOPE), F32)
    zeros = jnp.zeros((pos.shape[0], LANES - MLA_NOPE - MLA_ROPE), F32)
    cosm = jnp.concatenate([ones, cos[:, :MLA_ROPE], zeros], axis=1)
    sinm = jnp.concatenate([0.0 * ones, sin[:, :MLA_ROPE], zeros], axis=1)
    return cos, sin, cosm, sinm


def _placement():
    place = np.zeros((LANES, 2 * N_PAIR * LANES), np.float32)
    ones = np.zeros((1, 2 * N_PAIR * LANES), np.float32)
    for h in range(HEADS):
        for i in range(3):
            for p in range(N_PAIR):
                place[HEADS * i + h, LANES * p + AUG * h + i] = 1.0
                ones[0, LANES * p + AUG * h + 3 + i] = 1.0
                ones[0, N_PAIR * LANES + LANES * p + AUG * h + i] = 1.0
                place[HEADS * i + h, N_PAIR * LANES + LANES * p + AUG * h + 3 + i] = -1.0
    rep = np.zeros((LANES, HEADS * MLA_ROPE), np.float32)
    for h in range(HEADS):
        for r in range(MLA_ROPE):
            rep[r, MLA_ROPE * h + r] = 1.0
    return jnp.asarray(place, BF), jnp.asarray(ones, F32), jnp.asarray(rep, BF)


def _rot_cols(w):
    half = MLA_ROPE // 2
    return jnp.concatenate([-w[..., half:], w[..., :half]], axis=-1)


def _pad_cols(w, width):
    return jnp.pad(w, ((0, 0), (0, width - w.shape[1])))


def _layer_weights(w_in, b_f, w_q_up, w_uk, w_uv, w_out, w_up, conv_w, conv_b, w_down):
    wq, wk, wv, wf, wqc, wckv, wkr = jnp.split(w_in, IN_SPLITS, axis=1)
    wa = jnp.concatenate([wq, wk, wv, wqc, wckv, _pad_cols(wkr, LANES), _pad_cols(_rot_cols(wkr), LANES),
                          _pad_cols(wf, LANES)], axis=1).astype(BF)
    bfp = _pad_cols(b_f[None, :], LANES).astype(F32)
    wq3 = w_q_up.reshape(Q_LORA, HEADS, MLA_NOPE + MLA_ROPE)
    wq_nope = wq3[:, :, :MLA_NOPE].reshape(Q_LORA, HEADS * MLA_NOPE)
    wq_rope = wq3[:, :, MLA_NOPE:]
    wq2 = jnp.concatenate([wq_nope, wq_rope.reshape(Q_LORA, -1), _rot_cols(wq_rope).reshape(Q_LORA, -1)],
                          axis=1).astype(BF)
    eye = jnp.eye(4, dtype=F32)
    wk = jnp.transpose(w_uk, (1, 2, 0)).reshape(2, 4, 1, MLA_NOPE, KV_LORA)
    wukp = (wk * eye[None, :, :, None, None]).reshape(HEADS, 4 * MLA_NOPE, KV_LORA)
    wv = jnp.transpose(w_uv, (1, 0, 2)).reshape(2, 4, KV_LORA, 1, MLA_VDIM)
    wuvp = (wv * eye[None, :, None, :, None]).reshape(HEADS, KV_LORA, 4 * MLA_VDIM)
    wup = w_up.astype(BF)
    cw = jnp.concatenate([conv_w, conv_b[None, :]], axis=0)
    wdn = w_down.astype(BF)
    zpad = ((0, 0), (0, 0), (0, LANES - MLA_NOPE - MLA_ROPE))
    wq_a = jnp.pad(wq3, zpad).reshape(Q_LORA, HEADS * LANES)
    wq_b = jnp.pad(jnp.concatenate([jnp.zeros_like(wq3[:, :, :MLA_NOPE]), _rot_cols(wq_rope)], axis=2), zpad)
    wqm = jnp.concatenate([wq_a, wq_b.reshape(Q_LORA, HEADS * LANES)], axis=1).astype(BF)
    wkm = jnp.pad(w_uk, ((0, 0), (0, 0), (0, LANES - MLA_NOPE))).reshape(KV_LORA, HEADS * LANES).astype(BF)
    wvm = w_uv.reshape(KV_LORA, HEADS * MLA_VDIM).astype(BF)
    return wa, bfp, wq2, wukp.astype(BF), wuvp.astype(BF), w_out.astype(BF), wup, cw, wdn, wqm, wkm, wvm


def kernel(x_prompt, x_sample, cache_fox_k, cache_fox_v, cache_fox_logf, cache_mla_latent, cache_mla_krope,
           state_ffn_conv, attn_norm, w_in, b_forget, q_norm, w_q_up, kv_norm, w_uk, w_uv, w_out, ffn_norm,
           w_up, conv_w, conv_b, w_down, final_norm):
    assert attn_norm.shape[0] == 1, "single-layer stack"
    bp, tp, _ = x_prompt.shape
    bs, ts, _ = x_sample.shape
    past = cache_fox_k.shape[2]
    assert ts == CHUNK and past % CHUNK == 0

    wa, bfp, wq2, wukp, wuvp, wo, wup, cw, wdn, wqm, wkm, wvm = _layer_weights(
        w_in[0], b_forget[0], w_q_up[0], w_uk[0], w_uv[0], w_out[0], w_up[0], conv_w[0], conv_b[0], w_down[0])
    place, ones, rep = _placement()
    g_attn = attn_norm[0][None, :]
    g_q = q_norm[0][None, :]
    g_kv = kv_norm[0][None, :]
    g_ffn = ffn_norm[0][None, :]
    g_fin = final_norm[None, :]

    tm = 512
    outs = []
    for grp in ("prompt", "sample"):
        if grp == "prompt":
            x2d = x_prompt.reshape(bp * tp, D_MODEL)
            b, t = bp, tp
            tabs = _rope_tables(jnp.arange(tp, dtype=jnp.int32))
            pconv = jnp.zeros((bp, CONV_W - 1, 2 * D_FF), F32)
        else:
            x2d = x_sample.reshape(bs * ts, D_MODEL)
            b, t = bs, ts
            tabs = [jnp.tile(a, (tm // ts, 1)) for a in _rope_tables(past + jnp.arange(ts, dtype=jnp.int32))]
            pconv = state_ffn_conv[0]

        absorbed = grp == "sample"
        wm = (wq2, wukp, rep) if absorbed else (wqm, wkm, wvm)
        (k, v, logf, lat, kr, qx, kx, vb, m1, m2, m3, fqs) = _proj(
            x2d, *tabs, g_attn, wa, bfp, g_q, g_kv, *wm, place, ones, seq_len=t, tm=tm, absorbed=absorbed)

        if grp == "prompt":
            fox = _fox_prompt(qx, kx, vb, batch=b, seq=t, t=512, kind="fox")
            mla = _fox_prompt(m1, m2, m3, batch=b, seq=t, t=512, kind="mla")
        else:
            mkey, qlat, qrope = m1, m2, m3
            gk = _fox_prep(jnp.transpose(cache_fox_logf[0], (0, 2, 1)))
            kct = jnp.transpose(cache_fox_k[0], (0, 2, 3, 1))
            vct = jnp.transpose(cache_fox_v[0], (0, 2, 3, 1))
            krt = jnp.transpose(cache_mla_krope[0], (0, 2, 1))
            fox = _fox_sample(qx, kx, vb, fqs, kct, vct, gk, batch=b, seq=t, tkc=min(past, 2048))
            mla = _mla_sample(qlat, qrope, mkey, cache_mla_latent[0], krt, wuvp, batch=b, seq=t,
                              tkc=min(past, 1024), rc=256)

        def small_state(a, w, b=b, t=t):
            return (jnp.transpose(a, (0, 2, 1)) if a.ndim == 3 else a.reshape(b, t, w))[None]

        y, cst = _ffn(x2d, fox, mla, pconv, wo, g_ffn, wup, cw, wdn, g_fin, seq_len=t, tm=tm)
        outs.append((y.reshape(b, t, D_MODEL),
                     k.reshape(1, b, t, HEADS, FOX_DIM), v.reshape(1, b, t, HEADS, FOX_DIM),
                     small_state(logf, HEADS), lat.reshape(1, b, t, KV_LORA), small_state(kr, MLA_ROPE),
                     cst[None]))
    (yp, *st_p), (ys, *st_s) = outs
    return (yp, ys, *st_p, *st_s)
```

```python
import functools

import numpy as np
import jax
import jax.numpy as jnp
from jax import lax
from jax.experimental import pallas as pl
from jax.experimental.pallas import tpu as pltpu

D_MODEL = 1024
CHUNK = 64
HEADS = 8
FOX_DIM = 64
MLA_NOPE = 64
MLA_ROPE = 32
MLA_VDIM = 64
Q_LORA = 384
KV_LORA = 256
D_FF = 2816
CONV_W = 3
ROPE_THETA = 10000.0
EPS = 1e-6
NEG = -1e30
LOG2E = 1.4426950408889634

FOX_W = HEADS * FOX_DIM
IN_SIZES = [FOX_W, FOX_W, FOX_W, HEADS, Q_LORA, KV_LORA, MLA_ROPE]
IN_SPLITS = [int(s) for s in np.cumsum(IN_SIZES)[:-1]]

LANES = 128
AUG = 16
N_PAIR = HEADS // 2
FF_CHUNK = 256
N_FF = D_FF // FF_CHUNK
VMEM_LIMIT = 56 * 1024 * 1024

C_Q, C_K, C_V = 0, 512, 1024
C_QC = 1536
C_CKV = 1920
C_KR = 2176
C_KRR = 2304
C_F = 2432
N_A = 2560

BF = jnp.bfloat16
F32 = jnp.float32


def _dot(a, b):
    return jnp.dot(a, b, preferred_element_type=F32)


def _dot_nt(a, b):
    return lax.dot_general(a, b, (((1,), (1,)), ((), ())), preferred_element_type=F32)


def _split3(x):
    hi = x.astype(BF)
    r1 = x - hi.astype(F32)
    mid = r1.astype(BF)
    lo = (r1 - mid.astype(F32)).astype(BF)
    return hi, mid, lo


def _log_sigmoid(x):
    return jnp.minimum(x, 0.0) - jnp.log1p(jnp.exp(-jnp.abs(x)))


def _lanes(x, n):
    if n % LANES == 0:
        return x if n == LANES else jnp.concatenate([x] * (n // LANES), axis=1)
    assert n < LANES
    return x[:, :n]


def _params(n_axes):
    return pltpu.CompilerParams(dimension_semantics=("arbitrary",) * n_axes, vmem_limit_bytes=VMEM_LIMIT)


def _const_spec(shape):
    nd = len(shape)
    return pl.BlockSpec(shape, lambda *_: (0,) * nd, pipeline_mode=pl.Buffered(1))


def _proj_kernel(x_ref, cos_ref, sin_ref, cosm_ref, sinm_ref, g_ref, wa_ref, bf_ref, gq_ref, gkv_ref,
                 w1_ref, w2_ref, w3_ref, place_ref, ones_ref,
                 k_ref, v_ref, logf_ref, lat_ref, kr_ref, qx_ref, kx_ref, vb_ref, o1_ref, o2_ref, o3_ref,
                 fqs_ref, carry_ref, *, tm, seq_len, mla_scale, absorbed):
    x = x_ref[...]
    ms = jnp.mean(x * x, axis=-1, keepdims=True)
    hn = (x * lax.rsqrt(ms + EPS) * g_ref[...]).astype(BF)
    z = _dot(hn, wa_ref[...])

    zk = z[:, C_K:C_K + FOX_W]
    zv = z[:, C_V:C_V + FOX_W]
    for h in range(HEADS):
        k_ref[pl.ds(h, tm, stride=HEADS), :] = zk[:, FOX_DIM * h:FOX_DIM * (h + 1)]
        v_ref[pl.ds(h, tm, stride=HEADS), :] = zv[:, FOX_DIM * h:FOX_DIM * (h + 1)]
    vb_ref[...] = zv.astype(BF)

    lane = lax.broadcasted_iota(jnp.int32, (tm, LANES), 1)
    logf = _log_sigmoid(z[:, C_F:C_F + LANES] + bf_ref[...])
    logf = jnp.where(lane < HEADS, logf, 0.0)
    if seq_len >= tm:
        logf_ref[0] = logf.T[:HEADS, :]
    else:
        logf_ref[...] = logf[:, :HEADS]
    row = lax.broadcasted_iota(jnp.int32, (tm, tm), 0)
    col = lax.broadcasted_iota(jnp.int32, (tm, tm), 1)
    keep = col <= row
    if seq_len < tm:
        keep = keep & ((col // seq_len) == (row // seq_len))
    tri = jnp.where(keep, 1.0, 0.0).astype(BF)
    hi, mid, lo = _split3(logf)
    fcum = _dot(tri, hi) + _dot(tri, mid) + _dot(tri, lo)
    if seq_len > tm:
        @pl.when(pl.program_id(0) % (seq_len // tm) == 0)
        def _():
            carry_ref[...] = jnp.zeros_like(carry_ref)
        fcum = fcum + carry_ref[0:1, :]
        carry_ref[0:1, :] = fcum[tm - 1:tm, :]

    fqs = fcum * LOG2E
    fqs_ref[...] = fqs
    hi, mid, lo = _split3(fqs)
    comb = hi.astype(F32) + pltpu.roll(mid.astype(F32), HEADS, 1) + pltpu.roll(lo.astype(F32), 2 * HEADS, 1)
    aug = _dot(comb.astype(BF), place_ref[...]) + ones_ref[...]
    zq = z[:, C_Q:C_Q + FOX_W] * (FOX_DIM ** -0.5 * LOG2E)
    for p in range(N_PAIR):
        lo_, hi_ = 2 * LANES * p, 2 * LANES * p + LANES
        qx_ref[:, lo_:hi_] = zq[:, LANES * p:LANES * (p + 1)].astype(BF)
        qx_ref[:, hi_:hi_ + LANES] = aug[:, LANES * p:LANES * (p + 1)].astype(BF)
        kx_ref[:, lo_:hi_] = zk[:, LANES * p:LANES * (p + 1)].astype(BF)
        kx_ref[:, hi_:hi_ + LANES] = aug[:, FOX_W + LANES * p:FOX_W + LANES * (p + 1)].astype(BF)

    zc = z[:, C_CKV:C_CKV + KV_LORA]
    ckv = zc * lax.rsqrt(jnp.mean(zc * zc, axis=-1, keepdims=True) + EPS) * gkv_ref[...]
    lat_ref[...] = ckv
    cos = cos_ref[...]
    sin = sin_ref[...]
    krb = z[:, C_KR:C_KR + LANES] * cos[:, :LANES] + z[:, C_KRR:C_KRR + LANES] * sin[:, :LANES]
    if seq_len >= tm:
        kr_ref[0] = krb.T[:MLA_ROPE, :]
    else:
        kr_ref[...] = krb[:, :MLA_ROPE]
    cb = ckv.astype(BF)

    zqc = z[:, C_QC:C_QC + Q_LORA]
    qcn = (zqc * lax.rsqrt(jnp.mean(zqc * zqc, axis=-1, keepdims=True) + EPS) * gq_ref[...]).astype(BF)
    zq2 = _dot(qcn, w1_ref[...])
    if absorbed:
        mkey_ref, qlat_ref, qrope_ref = o1_ref, o2_ref, o3_ref
        mkey_ref[:, :KV_LORA] = cb
        mkey_ref[:, KV_LORA:] = _dot(krb.astype(BF), w3_ref[...]).astype(BF)
        nr = HEADS * MLA_ROPE
        qr = zq2[:, FOX_W:FOX_W + nr] * cos + zq2[:, FOX_W + nr:FOX_W + 2 * nr] * sin
        qrope_ref[...] = (qr * mla_scale).astype(BF)
        for h in range(HEADS):
            g = h // 4
            qn = zq2[:, 256 * g:256 * (g + 1)].astype(BF)
            qlat_ref[h] = (_dot(qn, w2_ref[h]) * mla_scale).astype(BF)
    else:
        qm_ref, km_ref, vm_ref = o1_ref, o2_ref, o3_ref
        hw = HEADS * LANES
        cosm = jnp.concatenate([cosm_ref[...]] * HEADS, axis=1)
        sinm = jnp.concatenate([sinm_ref[...]] * HEADS, axis=1)
        qm_ref[...] = ((zq2[:, :hw] * cosm + zq2[:, hw:] * sinm) * mla_scale).astype(BF)
        krs = pltpu.roll(krb, MLA_NOPE, 1)
        km_ref[...] = (_dot(cb, w2_ref[...]) + jnp.concatenate([krs] * HEADS, axis=1)).astype(BF)
        vm_ref[...] = _dot(cb, w3_ref[...]).astype(BF)


def _proj(x2d, cos, sin, cosm, sinm, g, wa, bfp, gq, gkv, w1, w2, w3, place, ones, *, seq_len, tm, absorbed):
    n = x2d.shape[0]
    nt = n // tm
    if seq_len >= tm:
        tps = seq_len // tm
        tab_map = lambda i: (i % tps, 0)
    else:
        tab_map = lambda i: (0, 0)
    row = lambda w: pl.BlockSpec((tm, w), lambda i: (i, 0))
    if seq_len >= tm:
        small = lambda w: jax.ShapeDtypeStruct((n // seq_len, w, seq_len), F32)
        small_spec = lambda w: pl.BlockSpec((1, w, tm), lambda i: (i // tps, 0, i % tps))
    else:
        small = lambda w: jax.ShapeDtypeStruct((n, w), F32)
        small_spec = row
    if absorbed:
        mla_shapes = (jax.ShapeDtypeStruct((n, 2 * KV_LORA), BF), jax.ShapeDtypeStruct((HEADS, n, KV_LORA), BF),
                      jax.ShapeDtypeStruct((n, HEADS * MLA_ROPE), BF))
        mla_specs = (row(2 * KV_LORA), pl.BlockSpec((HEADS, tm, KV_LORA), lambda i: (0, i, 0)),
                     row(HEADS * MLA_ROPE))
    else:
        mla_shapes = (jax.ShapeDtypeStruct((n, HEADS * LANES), BF), jax.ShapeDtypeStruct((n, HEADS * LANES), BF),
                      jax.ShapeDtypeStruct((n, HEADS * MLA_VDIM), BF))
        mla_specs = (row(HEADS * LANES), row(HEADS * LANES), row(HEADS * MLA_VDIM))
    out_shape = (
        jax.ShapeDtypeStruct((n * HEADS, FOX_DIM), F32),
        jax.ShapeDtypeStruct((n * HEADS, FOX_DIM), F32),
        small(HEADS),
        jax.ShapeDtypeStruct((n, KV_LORA), F32),
        small(MLA_ROPE),
        jax.ShapeDtypeStruct((n, 2 * FOX_W), BF),
        jax.ShapeDtypeStruct((n, 2 * FOX_W), BF),
        jax.ShapeDtypeStruct((n, FOX_W), BF),
        *mla_shapes,
        jax.ShapeDtypeStruct((n, LANES), F32),
    )
    kv_spec = pl.BlockSpec((tm * HEADS, FOX_DIM), lambda i: (i, 0))
    out_specs = (kv_spec, kv_spec, small_spec(HEADS), row(KV_LORA), small_spec(MLA_ROPE), row(2 * FOX_W),
                 row(2 * FOX_W), row(FOX_W), *mla_specs, row(LANES))
    in_specs = [row(D_MODEL),
                pl.BlockSpec((tm, 2 * LANES), tab_map), pl.BlockSpec((tm, 2 * LANES), tab_map),
                pl.BlockSpec((tm, LANES), tab_map), pl.BlockSpec((tm, LANES), tab_map),
                _const_spec(g.shape), _const_spec(wa.shape), _const_spec(bfp.shape), _const_spec(gq.shape),
                _const_spec(gkv.shape), _const_spec(w1.shape), _const_spec(w2.shape), _const_spec(w3.shape),
                _const_spec(place.shape), _const_spec(ones.shape)]
    kern = functools.partial(_proj_kernel, tm=tm, seq_len=seq_len, absorbed=absorbed,
                             mla_scale=float((MLA_NOPE + MLA_ROPE) ** -0.5 * LOG2E))
    return pl.pallas_call(
        kern, grid=(nt,), in_specs=in_specs, out_specs=out_specs, out_shape=out_shape,
        scratch_shapes=[pltpu.VMEM((8, LANES), F32)],
        compiler_params=_params(1), name="proj")(
            x2d, cos, sin, cosm, sinm, g, wa, bfp, gq, gkv, w1, w2, w3, place, ones)


def _fox_prep_kernel(plt_ref, g_ref):
    nb, h, n = plt_ref.shape
    x = plt_ref[...].reshape(nb * h, n)
    lane = lax.broadcasted_iota(jnp.int32, x.shape, 1)
    shift = 1
    while shift < n:
        x = x + jnp.where(lane >= shift, pltpu.roll(x, shift, 1), 0.0)
        shift *= 2
    g_ref[...] = ((x - x[:, n - 1:n]) * LOG2E).reshape(nb, h, n)


def _fox_prep(plt, nb=4):
    b, h, p = plt.shape
    nb = nb if b % nb == 0 else 1
    return pl.pallas_call(
        _fox_prep_kernel, grid=(b // nb,),
        in_specs=[pl.BlockSpec((nb, h, p), lambda i: (i, 0, 0))],
        out_specs=pl.BlockSpec((nb, h, p), lambda i: (i, 0, 0)),
        out_shape=jax.ShapeDtypeStruct((b, h, p), F32),
        compiler_params=_params(1), name="fox_prep")(plt)


def _softmax_step(s, m_prev):
    m_new = jnp.maximum(m_prev, jnp.max(s, axis=-1, keepdims=True))
    p = jnp.exp2(s - _lanes(m_new, s.shape[1]))
    alpha = jnp.exp2(m_prev - m_new)
    return p.astype(BF), alpha, m_new


def _fox_qpad(qx, h, kind="fox"):
    p, hh = divmod(h, 2)
    lane = lax.broadcasted_iota(jnp.int32, (qx.shape[0], 2 * LANES), 1)
    blk = qx[:, 2 * LANES * p:2 * LANES * (p + 1)]
    if kind == "fox":
        keep = ((lane >= FOX_DIM * hh) & (lane < FOX_DIM * (hh + 1))) | (
            (lane >= LANES + AUG * h) & (lane < LANES + AUG * (h + 1)))
    else:
        keep = (lane >= LANES * hh) & (lane < LANES * (hh + 1))
    return jnp.where(keep, blk, jnp.zeros_like(blk))


def _fox_prompt_tile(kx, vb, qpad_ref, m_ref, l_ref, acc_ref, masked, kind):
    tq = qpad_ref.shape[1]
    tk = kx.shape[0]
    first = lax.broadcasted_iota(jnp.int32, (tq, LANES), 1) < FOX_DIM
    ones = jnp.ones((tk, LANES), BF)
    if masked:
        col = lax.broadcasted_iota(jnp.int32, (tq, tk), 1)
        row = lax.broadcasted_iota(jnp.int32, (tq, tk), 0)
        if kind == "mla":
            shift = CHUNK.bit_length() - 1
            col, row = lax.shift_right_logical(col, shift), lax.shift_right_logical(row, shift)
        causal = col <= row
    def scores(h):
        return _dot_nt(qpad_ref[h], kx[:, 2 * LANES * (h // 2):2 * LANES * (h // 2 + 1)])

    s_next = scores(0)
    for p in range(N_PAIR):
        vpa = jnp.concatenate([vb[:, LANES * p:LANES * (p + 1)], ones], axis=1)
        alphas, pvs = [], []
        for hh in range(2):
            h = 2 * p + hh
            s = s_next
            if h + 1 < HEADS:
                s_next = scores(h + 1)
            if masked:
                s = jnp.where(causal, s, NEG)
            pb, alpha, m_new = _softmax_step(s, m_ref[h])
            pv = _dot(pb, vpa)
            l_ref[h] = alpha * l_ref[h] + pv[:, LANES:]
            m_ref[h] = m_new
            alphas.append(alpha)
            pvs.append(pv[:, :LANES])
        acc_ref[p] = acc_ref[p] * jnp.where(first, alphas[0], alphas[1]) + jnp.where(first, pvs[0], pvs[1])


def _fox_prompt_kernel(qx_ref, kx_ref, vb_ref, o_ref, qpad_ref, m_ref, l_ref, acc_ref, *, kind):
    i = pl.program_id(1)
    j = pl.program_id(2)

    @pl.when(j == 0)
    def _():
        qx = qx_ref[...]
        for h in range(HEADS):
            qpad_ref[h] = _fox_qpad(qx, h, kind)
        m_ref[...] = jnp.full(m_ref.shape, NEG, F32)
        l_ref[...] = jnp.zeros(l_ref.shape, F32)
        acc_ref[...] = jnp.zeros(acc_ref.shape, F32)

    @pl.when(j < i)
    def _():
        _fox_prompt_tile(kx_ref[...], vb_ref[...], qpad_ref, m_ref, l_ref, acc_ref, False, kind)

    @pl.when(j == i)
    def _():
        _fox_prompt_tile(kx_ref[...], vb_ref[...], qpad_ref, m_ref, l_ref, acc_ref, True, kind)
        first = lax.broadcasted_iota(jnp.int32, (qpad_ref.shape[1], LANES), 1) < FOX_DIM
        for p in range(N_PAIR):
            l_sel = jnp.where(first, l_ref[2 * p], l_ref[2 * p + 1])
            o_ref[:, LANES * p:LANES * (p + 1)] = (acc_ref[p] / l_sel).astype(o_ref.dtype)


def _fox_prompt(qx, kx, vb, *, batch, seq, t, kind):
    nq = seq // t
    assert CHUNK & (CHUNK - 1) == 0 and t % CHUNK == 0
    return pl.pallas_call(
        functools.partial(_fox_prompt_kernel, kind=kind), grid=(batch, nq, nq),
        in_specs=[pl.BlockSpec((t, 2 * FOX_W), lambda b, i, j: (b * nq + i, 0)),
                  pl.BlockSpec((t, 2 * FOX_W), lambda b, i, j: (b * nq + jnp.minimum(i, j), 0)),
                  pl.BlockSpec((t, FOX_W), lambda b, i, j: (b * nq + jnp.minimum(i, j), 0))],
        out_specs=pl.BlockSpec((t, FOX_W), lambda b, i, j: (b * nq + i, 0)),
        out_shape=jax.ShapeDtypeStruct((batch * seq, FOX_W), BF),
        scratch_shapes=[pltpu.VMEM((HEADS, t, 2 * LANES), BF), pltpu.VMEM((HEADS, t, LANES), F32),
                        pltpu.VMEM((HEADS, t, LANES), F32), pltpu.VMEM((N_PAIR, t, LANES), F32)],
        compiler_params=_params(3), name=kind + "_prompt")(qx, kx, vb)


def _fox_sample_kernel(qx_ref, kx_ref, vb_ref, fqs_ref, kct_ref, vct_ref, gk_ref, o_ref,
                       qbd_ref, qpad_ref, fq_ref, m_ref, l_ref, acc_ref, *, n_cache):
    j = pl.program_id(1)
    tq = qx_ref.shape[0]
    rows = HEADS * tq

    def update(s, pv_fn, g0=0, g1=2):
        a, b = 4 * tq * g0, 4 * tq * g1
        m_prev = m_ref[a:b]
        m_new = jnp.maximum(m_prev, jnp.max(s, axis=-1, keepdims=True))
        p = jnp.exp2(s - _lanes(m_new, s.shape[1]))
        alpha = jnp.exp2(m_prev - m_new)
        l_ref[a:b] = alpha * l_ref[a:b] + jnp.sum(p, axis=-1, keepdims=True)
        m_ref[a:b] = m_new
        pb = p.astype(BF)
        for pr in range(2 * g0, 2 * g1):
            r0, r1 = 2 * tq * pr, 2 * tq * (pr + 1)
            acc_ref[r0:r1] = acc_ref[r0:r1] * alpha[r0 - a:r1 - a] + pv_fn(pr, pb[r0 - a:r1 - a])

    @pl.when(j == 0)
    def _():
        qx = qx_ref[...]
        fqs = fqs_ref[...]
        qc = jnp.concatenate([qx[:, 2 * LANES * p:2 * LANES * p + LANES] for p in range(N_PAIR)], axis=1)
        lane = lax.broadcasted_iota(jnp.int32, qc.shape, 1)
        for h in range(HEADS):
            qbd_ref[h * tq:(h + 1) * tq, :] = jnp.where((lane // FOX_DIM) == h, qc, jnp.zeros_like(qc))
            qpad_ref[h] = _fox_qpad(qx, h)
            fq_ref[h * tq:(h + 1) * tq, :] = jnp.broadcast_to(fqs[:, h:h + 1], (tq, LANES))
        m_ref[...] = jnp.full(m_ref.shape, NEG, F32)
        l_ref[...] = jnp.zeros(l_ref.shape, F32)
        acc_ref[...] = jnp.zeros(acc_ref.shape, F32)
        kx = kx_ref[...]
        vb = vb_ref[...]
        tk = kx.shape[0]
        s = jnp.concatenate([_dot_nt(qpad_ref[h], kx[:, 2 * LANES * (h // 2):2 * LANES * (h // 2 + 1)])
                             for h in range(HEADS)], axis=0)
        row = lax.broadcasted_iota(jnp.int32, (rows, tk), 0) & (tq - 1)
        s = jnp.where(lax.broadcasted_iota(jnp.int32, (rows, tk), 1) <= row, s, NEG)
        update(s, lambda pr, pb: _dot(pb, vb[:, LANES * pr:LANES * (pr + 1)]))

    @pl.when(j > 0)
    def _():
        tk = kct_ref.shape[3]
        gk = gk_ref[0]

        def scores(g):
            kt = kct_ref[0, 4 * g:4 * g + 4].reshape(4 * FOX_DIM, tk).astype(BF)
            gk_rows = jnp.concatenate([jnp.broadcast_to(gk[h:h + 1, :], (tq, tk)) for h in range(4 * g, 4 * g + 4)],
                                      axis=0)
            qg = qbd_ref[4 * tq * g:4 * tq * (g + 1), 4 * FOX_DIM * g:4 * FOX_DIM * (g + 1)]
            return _dot(qg, kt) + (_lanes(fq_ref[4 * tq * g:4 * tq * (g + 1)], tk) - gk_rows)

        def pv(pr, pb):
            return _dot_nt(pb, vct_ref[0, 2 * pr:2 * pr + 2].reshape(2 * FOX_DIM, tk).astype(BF))

        s0 = scores(0)
        s1 = scores(1)
        update(s0, pv, 0, 1)
        update(s1, pv, 1, 2)

    @pl.when(j == n_cache)
    def _():
        first = lax.broadcasted_iota(jnp.int32, (tq, LANES), 1) < FOX_DIM
        o = acc_ref[...] / l_ref[...]
        for pr in range(N_PAIR):
            o_ref[:, LANES * pr:LANES * (pr + 1)] = jnp.where(
                first, o[2 * tq * pr:2 * tq * pr + tq], o[2 * tq * pr + tq:2 * tq * (pr + 1)]).astype(o_ref.dtype)


def _fox_sample(qx, kx, vb, fqs, kct, vct, gk, *, batch, seq, tkc):
    past = kct.shape[3]
    n_cache = past // tkc
    kern = functools.partial(_fox_sample_kernel, n_cache=n_cache)
    row = lambda w: pl.BlockSpec((seq, w), lambda b, j: (b, 0))
    return pl.pallas_call(
        kern, grid=(batch, n_cache + 1),
        in_specs=[row(2 * FOX_W), row(2 * FOX_W), row(FOX_W), row(LANES),
                  pl.BlockSpec((1, HEADS, FOX_DIM, tkc), lambda b, j: (b, 0, 0, jnp.maximum(j - 1, 0))),
                  pl.BlockSpec((1, HEADS, FOX_DIM, tkc), lambda b, j: (b, 0, 0, jnp.maximum(j - 1, 0))),
                  pl.BlockSpec((1, HEADS, tkc), lambda b, j: (b, 0, jnp.maximum(j - 1, 0)))],
        out_specs=row(FOX_W),
        out_shape=jax.ShapeDtypeStruct((batch * seq, FOX_W), BF),
        scratch_shapes=[pltpu.VMEM((HEADS * seq, FOX_W), BF), pltpu.VMEM((HEADS, seq, 2 * LANES), BF)]
        + [pltpu.VMEM((HEADS * seq, LANES), F32)] * 4,
        compiler_params=_params(2), name="fox_sample")(qx, kx, vb, fqs, kct, vct, gk)


def _chunk_mask(shape, tq, row0, q0, k0):
    assert tq & (tq - 1) == 0 and CHUNK & (CHUNK - 1) == 0
    shift = CHUNK.bit_length() - 1
    qpos = q0 + ((row0 + lax.broadcasted_iota(jnp.int32, shape, 0)) & (tq - 1))
    kpos = k0 + lax.broadcasted_iota(jnp.int32, shape, 1)
    return lax.shift_right_logical(kpos, shift) <= lax.shift_right_logical(qpos, shift)


def _mla_rows_update(s, val, m_ref, l_ref, acc_ref, r0, r1):
    m_prev = m_ref[r0:r1]
    m_new = jnp.maximum(m_prev, jnp.max(s, axis=-1, keepdims=True))
    p = jnp.exp2(s - _lanes(m_new, s.shape[1]))
    alpha = jnp.exp2(m_prev - m_new)
    l_ref[r0:r1] = alpha * l_ref[r0:r1] + jnp.sum(p, axis=-1, keepdims=True)
    m_ref[r0:r1] = m_new
    acc_ref[r0:r1] = acc_ref[r0:r1] * _lanes(alpha, KV_LORA) + _dot(p.astype(BF), val)


def _mla_init_stats(m_ref, l_ref, acc_ref):
    m_ref[...] = jnp.full(m_ref.shape, NEG, F32)
    l_ref[...] = jnp.zeros(l_ref.shape, F32)
    acc_ref[...] = jnp.zeros(acc_ref.shape, F32)


def _mla_finalize(o_ref, wuv_ref, l_ref, acc_ref, tq):
    olat = (acc_ref[...] / _lanes(l_ref[...], KV_LORA)).astype(BF)
    for g in range(2):
        out = None
        for h in range(4 * g, 4 * g + 4):
            part = _dot(olat[h * tq:(h + 1) * tq, :], wuv_ref[h])
            out = part if out is None else out + part
        o_ref[:, 256 * g:256 * (g + 1)] = out.astype(o_ref.dtype)


def _mla_scratch(rows):
    return [pltpu.VMEM((rows, LANES), F32), pltpu.VMEM((rows, LANES), F32), pltpu.VMEM((rows, KV_LORA), F32)]


def _mla_sample_kernel(qlat_ref, qrope_ref, mkey_ref, latc_ref, krt_ref, wuv_ref, o_ref,
                       ql_ref, qr_ref, m_ref, l_ref, acc_ref, *, n_cache, tq, past, rc):
    j = pl.program_id(1)
    rows = HEADS * tq

    @pl.when(j == 0)
    def _():
        qrope = qrope_ref[...]
        for h in range(HEADS):
            ql_ref[h * tq:(h + 1) * tq, :] = qlat_ref[h]
            qr_ref[h * tq:(h + 1) * tq, :] = qrope[:, MLA_ROPE * h:MLA_ROPE * (h + 1)]
        _mla_init_stats(m_ref, l_ref, acc_ref)
        mk = mkey_ref[...]
        s = _dot_nt(ql_ref[...], mk[:, :KV_LORA]) + _dot_nt(qr_ref[...], mk[:, KV_LORA:KV_LORA + MLA_ROPE])
        if past % CHUNK != 0 or tq > CHUNK:
            s = jnp.where(_chunk_mask(s.shape, tq, 0, past, past), s, NEG)
        _mla_rows_update(s, mk[:, :KV_LORA], m_ref, l_ref, acc_ref, 0, rows)

    @pl.when(j > 0)
    def _():
        c = latc_ref[0].astype(BF)
        krt = krt_ref[0].astype(BF)
        def scores(r0):
            return _dot_nt(ql_ref[r0:r0 + rc], c) + _dot(qr_ref[r0:r0 + rc], krt)

        s_next = scores(0)
        for r0 in range(0, rows, rc):
            s = s_next
            if r0 + rc < rows:
                s_next = scores(r0 + rc)
            _mla_rows_update(s, c, m_ref, l_ref, acc_ref, r0, r0 + rc)

    @pl.when(j == n_cache)
    def _():
        _mla_finalize(o_ref, wuv_ref, l_ref, acc_ref, tq)


def _mla_sample(qlat, qrope, mkey, latc, krt, wuvp, *, batch, seq, tkc, rc):
    past = latc.shape[1]
    n_cache = past // tkc
    kern = functools.partial(_mla_sample_kernel, n_cache=n_cache, tq=seq, past=past, rc=rc)
    return pl.pallas_call(
        kern, grid=(batch, n_cache + 1),
        in_specs=[pl.BlockSpec((HEADS, seq, KV_LORA), lambda b, j: (0, b, 0)),
                  pl.BlockSpec((seq, HEADS * MLA_ROPE), lambda b, j: (b, 0)),
                  pl.BlockSpec((seq, 2 * KV_LORA), lambda b, j: (b, 0)),
                  pl.BlockSpec((1, tkc, KV_LORA), lambda b, j: (b, jnp.maximum(j - 1, 0), 0)),
                  pl.BlockSpec((1, MLA_ROPE, tkc), lambda b, j: (b, 0, jnp.maximum(j - 1, 0))),
                  _const_spec(wuvp.shape)],
        out_specs=pl.BlockSpec((seq, FOX_W), lambda b, j: (b, 0)),
        out_shape=jax.ShapeDtypeStruct((batch * seq, FOX_W), BF),
        scratch_shapes=[pltpu.VMEM((HEADS * seq, KV_LORA), BF), pltpu.VMEM((HEADS * seq, MLA_ROPE), BF)]
        + _mla_scratch(HEADS * seq),
        compiler_params=_params(2), name="mla_sample")(qlat, qrope, mkey, latc, krt, wuvp)


def _ffn_kernel(x_ref, fox_ref, mla_ref, pconv_ref, wo_ref, gf_ref, wup_ref, cw_ref, wdn_ref, gfin_ref,
                y_ref, cst_ref, hn_ref, act_ref, ush_ref, carry_ref, *, tm, seq_len):
    n_seg = max(1, tm // seq_len)
    seg = tm // n_seg
    mixed = jnp.concatenate([fox_ref[...], mla_ref[...]], axis=1)
    x1 = x_ref[...] + _dot(mixed, wo_ref[...])
    hn_ref[...] = (x1 * lax.rsqrt(jnp.mean(x1 * x1, axis=-1, keepdims=True) + EPS) * gf_ref[...]).astype(BF)
    y_ref[...] = x1

    if n_seg == 1:
        @pl.when(pl.program_id(0) % (seq_len // tm) == 0)
        def _():
            carry_ref[...] = pconv_ref[0]

    def conv(u, col, slot):
        cols = slice(col, col + FF_CHUNK)
        for s in range(n_seg):
            ush_ref[slot, s, 6:8, :] = carry_ref[:, cols] if n_seg == 1 else pconv_ref[s, :, cols]
            ush_ref[slot, s, 8:8 + seg, :] = u[seg * s:seg * (s + 1), :]
            cst_ref[s, :, cols] = u[seg * (s + 1) - 2:seg * (s + 1), :]
        if n_seg == 1:
            carry_ref[:, cols] = u[tm - 2:tm, :]
        u1 = jnp.concatenate([ush_ref[slot, s, 7:7 + seg, :] for s in range(n_seg)], axis=0)
        u2 = jnp.concatenate([ush_ref[slot, s, 6:6 + seg, :] for s in range(n_seg)], axis=0)
        cw = cw_ref[:, cols]
        return cw[3:4, :] + cw[0:1, :] * u2 + cw[1:2, :] * u1 + cw[2:3, :] * u

    def up(c):
        hn = hn_ref[...]
        g0, v0 = FF_CHUNK * c, D_FF + FF_CHUNK * c
        return _dot(hn, wup_ref[:, g0:g0 + FF_CHUNK]), _dot(hn, wup_ref[:, v0:v0 + FF_CHUNK])

    u_next = up(0)
    for c in range(N_FF):
        ug, uv = u_next
        if c + 1 < N_FF:
            u_next = up(c + 1)
        gate = conv(ug, FF_CHUNK * c, 2 * (c % 2))
        val = conv(uv, D_FF + FF_CHUNK * c, 2 * (c % 2) + 1)
        act_ref[:, FF_CHUNK * c:FF_CHUNK * (c + 1)] = (gate * jax.nn.sigmoid(gate) * val).astype(BF)

    x2 = y_ref[...] + _dot(act_ref[...], wdn_ref[...])
    y_ref[...] = x2 * lax.rsqrt(jnp.mean(x2 * x2, axis=-1, keepdims=True) + EPS) * gfin_ref[...]


def _ffn(x2d, fox, mla, pconv, wo, gf, wup, cw, wdn, gfin, *, seq_len, tm):
    n = x2d.shape[0]
    nt = n // tm
    n_seq = n // seq_len
    n_seg = max(1, tm // seq_len)
    if n_seg == 1:
        tps = seq_len // tm
        st_map = lambda i: (i // tps, 0, 0)
    else:
        st_map = lambda i: (i, 0, 0)
    row = lambda w: pl.BlockSpec((tm, w), lambda i: (i, 0))
    kern = functools.partial(_ffn_kernel, tm=tm, seq_len=seq_len)
    return pl.pallas_call(
        kern, grid=(nt,),
        in_specs=[row(D_MODEL), row(FOX_W), row(FOX_W),
                  pl.BlockSpec((n_seg, CONV_W - 1, 2 * D_FF), st_map),
                  _const_spec(wo.shape), _const_spec(gf.shape), _const_spec(wup.shape), _const_spec(cw.shape),
                  _const_spec(wdn.shape), _const_spec(gfin.shape)],
        out_specs=(row(D_MODEL), pl.BlockSpec((n_seg, CONV_W - 1, 2 * D_FF), st_map)),
        out_shape=(jax.ShapeDtypeStruct((n, D_MODEL), F32),
                   jax.ShapeDtypeStruct((n_seq, CONV_W - 1, 2 * D_FF), F32)),
        scratch_shapes=[pltpu.VMEM((tm, D_MODEL), BF), pltpu.VMEM((tm, D_FF), BF),
                        pltpu.VMEM((4, n_seg, 8 + tm // n_seg, FF_CHUNK), F32),
                        pltpu.VMEM((CONV_W - 1, 2 * D_FF), F32)],
        compiler_params=_params(1), name="ffn")(x2d, fox, mla, pconv, wo, gf, wup, cw, wdn, gfin)


def _rope_tables(start, n, reps=1):
    half = MLA_ROPE // 2
    inv = ROPE_THETA ** (-np.arange(half, dtype=np.float64) / half)
    ang = (start + np.arange(n, dtype=np.float64))[:, None] * inv[None, :]
    cos = np.tile(np.cos(ang), (1, 2 * HEADS))
    sin = np.tile(np.sin(ang), (1, 2 * HEADS))
    ones = np.ones((n, MLA_NOPE))
    zeros = np.zeros((n, LANES - MLA_NOPE - MLA_ROPE))
    cosm = np.concatenate([ones, cos[:, :MLA_ROPE], zeros], axis=1)
    sinm = np.concatenate([0.0 * ones, sin[:, :MLA_ROPE], zeros], axis=1)
    return [jnp.asarray(np.tile(a, (reps, 1)), F32) for a in (cos, sin, cosm, sinm)]


def _placement():
    place = np.zeros((LANES, 2 * N_PAIR * LANES), np.float32)
    ones = np.zeros((1, 2 * N_PAIR * LANES), np.float32)
    for h in range(HEADS):
        for i in range(3):
            for p in range(N_PAIR):
                place[HEADS * i + h, LANES * p + AUG * h + i] = 1.0
                ones[0, LANES * p + AUG * h + 3 + i] = 1.0
                ones[0, N_PAIR * LANES + LANES * p + AUG * h + i] = 1.0
                place[HEADS * i + h, N_PAIR * LANES + LANES * p + AUG * h + 3 + i] = -1.0
    rep = np.zeros((LANES, HEADS * MLA_ROPE), np.float32)
    for h in range(HEADS):
        for r in range(MLA_ROPE):
            rep[r, MLA_ROPE * h + r] = 1.0
    return jnp.asarray(place, BF), jnp.asarray(ones, F32), jnp.asarray(rep, BF)


def _rot_cols(w):
    half = MLA_ROPE // 2
    return jnp.concatenate([-w[..., half:], w[..., :half]], axis=-1)


def _pad_cols(w, width):
    return jnp.pad(w, ((0, 0), (0, width - w.shape[1])))


def _layer_weights(w_in, b_f, w_q_up, w_uk, w_uv, w_out, w_up, conv_w, conv_b, w_down):
    wq, wk, wv, wf, wqc, wckv, wkr = jnp.split(w_in, IN_SPLITS, axis=1)
    wa = jnp.concatenate([wq, wk, wv, wqc, wckv, _pad_cols(wkr, LANES), _pad_cols(_rot_cols(wkr), LANES),
                          _pad_cols(wf, LANES)], axis=1).astype(BF)
    bfp = _pad_cols(b_f[None, :], LANES).astype(F32)
    wq3 = w_q_up.reshape(Q_LORA, HEADS, MLA_NOPE + MLA_ROPE)
    wq_nope = wq3[:, :, :MLA_NOPE].reshape(Q_LORA, HEADS * MLA_NOPE)
    wq_rope = wq3[:, :, MLA_NOPE:]
    wq2 = jnp.concatenate([wq_nope, wq_rope.reshape(Q_LORA, -1), _rot_cols(wq_rope).reshape(Q_LORA, -1)],
                          axis=1).astype(BF)
    eye = jnp.eye(4, dtype=F32)
    wk = jnp.transpose(w_uk, (1, 2, 0)).reshape(2, 4, 1, MLA_NOPE, KV_LORA)
    wukp = (wk * eye[None, :, :, None, None]).reshape(HEADS, 4 * MLA_NOPE, KV_LORA)
    wv = jnp.transpose(w_uv, (1, 0, 2)).reshape(2, 4, KV_LORA, 1, MLA_VDIM)
    wuvp = (wv * eye[None, :, None, :, None]).reshape(HEADS, KV_LORA, 4 * MLA_VDIM)
    wup = w_up.astype(BF)
    cw = jnp.concatenate([conv_w, conv_b[None, :]], axis=0)
    wdn = w_down.astype(BF)
    zpad = ((0, 0), (0, 0), (0, LANES - MLA_NOPE - MLA_ROPE))
    wq_a = jnp.pad(wq3, zpad).reshape(Q_LORA, HEADS * LANES)
    wq_b = jnp.pad(jnp.concatenate([jnp.zeros_like(wq3[:, :, :MLA_NOPE]), _rot_cols(wq_rope)], axis=2), zpad)
    wqm = jnp.concatenate([wq_a, wq_b.reshape(Q_LORA, HEADS * LANES)], axis=1).astype(BF)
    wkm = jnp.pad(w_uk, ((0, 0), (0, 0), (0, LANES - MLA_NOPE))).reshape(KV_LORA, HEADS * LANES).astype(BF)
    wvm = w_uv.reshape(KV_LORA, HEADS * MLA_VDIM).astype(BF)
    return wa, bfp, wq2, wukp.astype(BF), wuvp.astype(BF), w_out.astype(BF), wup, cw, wdn, wqm, wkm, wvm


def kernel(x_prompt, x_sample, cache_fox_k, cache_fox_v, cache_fox_logf, cache_mla_latent, cache_mla_krope,
           state_ffn_conv, attn_norm, w_in, b_forget, q_norm, w_q_up, kv_norm, w_uk, w_uv, w_out, ffn_norm,
           w_up, conv_w, conv_b, w_down, final_norm):
    assert attn_norm.shape[0] == 1, "single-layer stack"
    bp, tp, _ = x_prompt.shape
    bs, ts, _ = x_sample.shape
    past = cache_fox_k.shape[2]
    assert ts == CHUNK and past % CHUNK == 0

    wa, bfp, wq2, wukp, wuvp, wo, wup, cw, wdn, wqm, wkm, wvm = _layer_weights(
        w_in[0], b_forget[0], w_q_up[0], w_uk[0], w_uv[0], w_out[0], w_up[0], conv_w[0], conv_b[0], w_down[0])
    place, ones, rep = _placement()
    g_attn = attn_norm[0][None, :]
    g_q = q_norm[0][None, :]
    g_kv = kv_norm[0][None, :]
    g_ffn = ffn_norm[0][None, :]
    g_fin = final_norm[None, :]

    tm = 512
    outs = []
    for grp in ("prompt", "sample"):
        if grp == "prompt":
            x2d = x_prompt.reshape(bp * tp, D_MODEL)
            b, t = bp, tp
            tabs = _rope_tables(0, tp)
            pconv = jnp.zeros((bp, CONV_W - 1, 2 * D_FF), F32)
        else:
            x2d = x_sample.reshape(bs * ts, D_MODEL)
            b, t = bs, ts
            tabs = _rope_tables(past, ts, reps=tm // ts)
            pconv = state_ffn_conv[0]

        absorbed = grp == "sample"
        wm = (wq2, wukp, rep) if absorbed else (wqm, wkm, wvm)
        (k, v, logf, lat, kr, qx, kx, vb, m1, m2, m3, fqs) = _proj(
            x2d, *tabs, g_attn, wa, bfp, g_q, g_kv, *wm, place, ones, seq_len=t, tm=tm, absorbed=absorbed)

        if grp == "prompt":
            fox = _fox_prompt(qx, kx, vb, batch=b, seq=t, t=512, kind="fox")
            mla = _fox_prompt(m1, m2, m3, batch=b, seq=t, t=512, kind="mla")
        else:
            mkey, qlat, qrope = m1, m2, m3
            gk = _fox_prep(jnp.transpose(cache_fox_logf[0], (0, 2, 1)))
            kct = jnp.transpose(cache_fox_k[0], (0, 2, 3, 1))
            vct = jnp.transpose(cache_fox_v[0], (0, 2, 3, 1))
            krt = jnp.transpose(cache_mla_krope[0], (0, 2, 1))
            fox = _fox_sample(qx, kx, vb, fqs, kct, vct, gk, batch=b, seq=t, tkc=min(past, 2048))
            mla = _mla_sample(qlat, qrope, mkey, cache_mla_latent[0], krt, wuvp, batch=b, seq=t,
                              tkc=min(past, 2048), rc=256)

        def small_state(a, w, b=b, t=t):
            return (jnp.transpose(a, (0, 2, 1)) if a.ndim == 3 else a.reshape(b, t, w))[None]

        y, cst = _ffn(x2d, fox, mla, pconv, wo, g_ffn, wup, cw, wdn, g_fin, seq_len=t, tm=tm)
        outs.append((y.reshape(b, t, D_MODEL),
                     k.reshape(1, b, t, HEADS, FOX_DIM), v.reshape(1, b, t, HEADS, FOX_DIM),
                     small_state(logf, HEADS), lat.reshape(1, b, t, KV_LORA), small_state(kr, MLA_ROPE),
                     cst[None]))
    (yp, *st_p), (ys, *st_s) = outs
    return (yp, ys, *st_p, *st_s)
```

```python
import functools

import numpy as np
import jax
import jax.numpy as jnp
from jax import lax
from jax.experimental import pallas as pl
from jax.experimental.pallas import tpu as pltpu

D_MODEL = 1024
CHUNK = 64
HEADS = 8
FOX_DIM = 64
MLA_NOPE = 64
MLA_ROPE = 32
MLA_VDIM = 64
Q_LORA = 384
KV_LORA = 256
D_FF = 2816
CONV_W = 3
ROPE_THETA = 10000.0
EPS = 1e-6
NEG = -1e30
LOG2E = 1.4426950408889634

FOX_W = HEADS * FOX_DIM
IN_SIZES = [FOX_W, FOX_W, FOX_W, HEADS, Q_LORA, KV_LORA, MLA_ROPE]
IN_SPLITS = [int(s) for s in np.cumsum(IN_SIZES)[:-1]]

LANES = 128
AUG = 16
N_PAIR = HEADS // 2
FF_CHUNK = 256
N_FF = D_FF // FF_CHUNK
VMEM_LIMIT = 56 * 1024 * 1024

C_Q, C_K, C_V = 0, 512, 1024
C_QC = 1536
C_CKV = 1920
C_KR = 2176
C_KRR = 2304
C_F = 2432
N_A = 2560

BF = jnp.bfloat16
F32 = jnp.float32


def _dot(a, b):
    return jnp.dot(a, b, preferred_element_type=F32)


def _dot_nt(a, b):
    return lax.dot_general(a, b, (((1,), (1,)), ((), ())), preferred_element_type=F32)


def _split3(x):
    hi = x.astype(BF)
    r1 = x - hi.astype(F32)
    mid = r1.astype(BF)
    lo = (r1 - mid.astype(F32)).astype(BF)
    return hi, mid, lo


def _log_sigmoid(x):
    return jnp.minimum(x, 0.0) - jnp.log1p(jnp.exp(-jnp.abs(x)))


def _lanes(x, n):
    if n % LANES == 0:
        return x if n == LANES else jnp.concatenate([x] * (n // LANES), axis=1)
    assert n < LANES
    return x[:, :n]


def _params(n_axes):
    return pltpu.CompilerParams(dimension_semantics=("arbitrary",) * n_axes, vmem_limit_bytes=VMEM_LIMIT)


def _const_spec(shape):
    nd = len(shape)
    return pl.BlockSpec(shape, lambda *_: (0,) * nd, pipeline_mode=pl.Buffered(1))


def _proj_kernel(x_ref, cos_ref, sin_ref, cosm_ref, sinm_ref, g_ref, wa_ref, bf_ref, gq_ref, gkv_ref,
                 w1_ref, w2_ref, w3_ref, place_ref, ones_ref,
                 k_ref, v_ref, logf_ref, lat_ref, kr_ref, qx_ref, kx_ref, vb_ref, o1_ref, o2_ref, o3_ref,
                 fqs_ref, carry_ref, *, tm, seq_len, mla_scale, absorbed):
    x = x_ref[...]
    ms = jnp.mean(x * x, axis=-1, keepdims=True)
    hn = (x * lax.rsqrt(ms + EPS) * g_ref[...]).astype(BF)
    zs = _dot(hn, wa_ref[:, C_QC:])
    z = _dot(hn, wa_ref[:, :C_QC])

    lane = lax.broadcasted_iota(jnp.int32, (tm, LANES), 1)
    logf = _log_sigmoid(zs[:, C_F - C_QC:C_F - C_QC + LANES] + bf_ref[...])
    logf = jnp.where(lane < HEADS, logf, 0.0)
    logf_t = logf.T[:HEADS, :]
    if seq_len >= tm:
        logf_ref[0] = logf_t
    else:
        logf_ref[...] = logf[:, :HEADS]
    seg = min(seq_len, tm)
    assert seg & (seg - 1) == 0
    pos = lax.broadcasted_iota(jnp.int32, (HEADS, tm), 1) & (seg - 1)
    fc = logf_t
    shift = 1
    while shift < seg:
        fc = fc + jnp.where(pos >= shift, pltpu.roll(fc, shift, 1), 0.0)
        shift *= 2
    if seq_len > tm:
        @pl.when(pl.program_id(0) % (seq_len // tm) == 0)
        def _():
            carry_ref[...] = jnp.zeros_like(carry_ref)
        fc = fc + carry_ref[:, 0:1]
        carry_ref[...] = jnp.broadcast_to(fc[:, tm - 1:tm], carry_ref.shape)
    fcum = jnp.concatenate([fc, jnp.zeros((LANES - HEADS, tm), F32)], axis=0).T

    zc = zs[:, C_CKV - C_QC:C_CKV - C_QC + KV_LORA]
    ckv = zc * lax.rsqrt(jnp.mean(zc * zc, axis=-1, keepdims=True) + EPS) * gkv_ref[...]
    lat_ref[...] = ckv
    cos = cos_ref[...]
    sin = sin_ref[...]
    krb = (zs[:, C_KR - C_QC:C_KR - C_QC + LANES] * cos[:, :LANES]
           + zs[:, C_KRR - C_QC:C_KRR - C_QC + LANES] * sin[:, :LANES])
    if seq_len >= tm:
        kr_ref[0] = krb.T[:MLA_ROPE, :]
    else:
        kr_ref[...] = krb[:, :MLA_ROPE]
    cb = ckv.astype(BF)

    zqc = zs[:, :Q_LORA]
    qcn = (zqc * lax.rsqrt(jnp.mean(zqc * zqc, axis=-1, keepdims=True) + EPS) * gq_ref[...]).astype(BF)
    zq2 = _dot(qcn, w1_ref[...])
    if absorbed:
        mkey_ref, qlat_ref, qrope_ref = o1_ref, o2_ref, o3_ref
        mkey_ref[:, :KV_LORA] = cb
        mkey_ref[:, KV_LORA:] = _dot(krb.astype(BF), w3_ref[...]).astype(BF)
        nr = HEADS * MLA_ROPE
        qr = zq2[:, FOX_W:FOX_W + nr] * cos + zq2[:, FOX_W + nr:FOX_W + 2 * nr] * sin
        qrope_ref[...] = (qr * mla_scale).astype(BF)
        for h in range(HEADS):
            g = h // 4
            qn = zq2[:, 256 * g:256 * (g + 1)].astype(BF)
            qlat_ref[h] = (_dot(qn, w2_ref[h]) * mla_scale).astype(BF)
    else:
        qm_ref, km_ref, vm_ref = o1_ref, o2_ref, o3_ref
        hw = HEADS * LANES
        cosm = jnp.concatenate([cosm_ref[...]] * HEADS, axis=1)
        sinm = jnp.concatenate([sinm_ref[...]] * HEADS, axis=1)
        qm_ref[...] = ((zq2[:, :hw] * cosm + zq2[:, hw:] * sinm) * mla_scale).astype(BF)
        krs = pltpu.roll(krb, MLA_NOPE, 1)
        km_ref[...] = (_dot(cb, w2_ref[...]) + jnp.concatenate([krs] * HEADS, axis=1)).astype(BF)
        vm_ref[...] = _dot(cb, w3_ref[...]).astype(BF)

    zk = z[:, C_K:C_K + FOX_W]
    zv = z[:, C_V:C_V + FOX_W]
    for h in range(HEADS):
        k_ref[pl.ds(h, tm, stride=HEADS), :] = zk[:, FOX_DIM * h:FOX_DIM * (h + 1)]
        v_ref[pl.ds(h, tm, stride=HEADS), :] = zv[:, FOX_DIM * h:FOX_DIM * (h + 1)]
    vb_ref[...] = zv.astype(BF)

    fqs = fcum * LOG2E
    fqs_ref[...] = fqs
    hi, mid, lo = _split3(fqs)
    comb = hi.astype(F32) + pltpu.roll(mid.astype(F32), HEADS, 1) + pltpu.roll(lo.astype(F32), 2 * HEADS, 1)
    aug = _dot(comb.astype(BF), place_ref[...]) + ones_ref[...]
    zq = z[:, C_Q:C_Q + FOX_W] * (FOX_DIM ** -0.5 * LOG2E)
    for p in range(N_PAIR):
        lo_, hi_ = 2 * LANES * p, 2 * LANES * p + LANES
        qx_ref[:, lo_:hi_] = zq[:, LANES * p:LANES * (p + 1)].astype(BF)
        qx_ref[:, hi_:hi_ + LANES] = aug[:, LANES * p:LANES * (p + 1)].astype(BF)
        kx_ref[:, lo_:hi_] = zk[:, LANES * p:LANES * (p + 1)].astype(BF)
        kx_ref[:, hi_:hi_ + LANES] = aug[:, FOX_W + LANES * p:FOX_W + LANES * (p + 1)].astype(BF)


def _proj(x2d, cos, sin, cosm, sinm, g, wa, bfp, gq, gkv, w1, w2, w3, place, ones, *, seq_len, tm, absorbed):
    n = x2d.shape[0]
    nt = n // tm
    if seq_len >= tm:
        tps = seq_len // tm
        tab_map = lambda i: (i % tps, 0)
    else:
        tab_map = lambda i: (0, 0)
    row = lambda w: pl.BlockSpec((tm, w), lambda i: (i, 0))
    if seq_len >= tm:
        small = lambda w: jax.ShapeDtypeStruct((n // seq_len, w, seq_len), F32)
        small_spec = lambda w: pl.BlockSpec((1, w, tm), lambda i: (i // tps, 0, i % tps))
    else:
        small = lambda w: jax.ShapeDtypeStruct((n, w), F32)
        small_spec = row
    if absorbed:
        mla_shapes = (jax.ShapeDtypeStruct((n, 2 * KV_LORA), BF), jax.ShapeDtypeStruct((HEADS, n, KV_LORA), BF),
                      jax.ShapeDtypeStruct((n, HEADS * MLA_ROPE), BF))
        mla_specs = (row(2 * KV_LORA), pl.BlockSpec((HEADS, tm, KV_LORA), lambda i: (0, i, 0)),
                     row(HEADS * MLA_ROPE))
    else:
        mla_shapes = (jax.ShapeDtypeStruct((n, HEADS * LANES), BF), jax.ShapeDtypeStruct((n, HEADS * LANES), BF),
                      jax.ShapeDtypeStruct((n, HEADS * MLA_VDIM), BF))
        mla_specs = (row(HEADS * LANES), row(HEADS * LANES), row(HEADS * MLA_VDIM))
    out_shape = (
        jax.ShapeDtypeStruct((n * HEADS, FOX_DIM), F32),
        jax.ShapeDtypeStruct((n * HEADS, FOX_DIM), F32),
        small(HEADS),
        jax.ShapeDtypeStruct((n, KV_LORA), F32),
        small(MLA_ROPE),
        jax.ShapeDtypeStruct((n, 2 * FOX_W), BF),
        jax.ShapeDtypeStruct((n, 2 * FOX_W), BF),
        jax.ShapeDtypeStruct((n, FOX_W), BF),
        *mla_shapes,
        jax.ShapeDtypeStruct((n, LANES), F32),
    )
    kv_spec = pl.BlockSpec((tm * HEADS, FOX_DIM), lambda i: (i, 0))
    out_specs = (kv_spec, kv_spec, small_spec(HEADS), row(KV_LORA), small_spec(MLA_ROPE), row(2 * FOX_W),
                 row(2 * FOX_W), row(FOX_W), *mla_specs, row(LANES))
    in_specs = [row(D_MODEL),
                pl.BlockSpec((tm, 2 * LANES), tab_map), pl.BlockSpec((tm, 2 * LANES), tab_map),
                pl.BlockSpec((tm, LANES), tab_map), pl.BlockSpec((tm, LANES), tab_map),
                _const_spec(g.shape), _const_spec(wa.shape), _const_spec(bfp.shape), _const_spec(gq.shape),
                _const_spec(gkv.shape), _const_spec(w1.shape), _const_spec(w2.shape), _const_spec(w3.shape),
                _const_spec(place.shape), _const_spec(ones.shape)]
    kern = functools.partial(_proj_kernel, tm=tm, seq_len=seq_len, absorbed=absorbed,
                             mla_scale=float((MLA_NOPE + MLA_ROPE) ** -0.5 * LOG2E))
    return pl.pallas_call(
        kern, grid=(nt,), in_specs=in_specs, out_specs=out_specs, out_shape=out_shape,
        scratch_shapes=[pltpu.VMEM((8, LANES), F32)],
        compiler_params=_params(1), name="proj")(
            x2d, cos, sin, cosm, sinm, g, wa, bfp, gq, gkv, w1, w2, w3, place, ones)


def _fox_prep_kernel(plt_ref, g_ref):
    nb, h, n = plt_ref.shape
    x = plt_ref[...].reshape(nb * h, n)
    lane = lax.broadcasted_iota(jnp.int32, x.shape, 1)
    shift = 1
    while shift < n:
        x = x + jnp.where(lane >= shift, pltpu.roll(x, shift, 1), 0.0)
        shift *= 2
    g_ref[...] = ((x - x[:, n - 1:n]) * LOG2E).reshape(nb, h, n)


def _fox_prep(plt, nb=4):
    b, h, p = plt.shape
    nb = nb if b % nb == 0 else 1
    return pl.pallas_call(
        _fox_prep_kernel, grid=(b // nb,),
        in_specs=[pl.BlockSpec((nb, h, p), lambda i: (i, 0, 0))],
        out_specs=pl.BlockSpec((nb, h, p), lambda i: (i, 0, 0)),
        out_shape=jax.ShapeDtypeStruct((b, h, p), F32),
        compiler_params=_params(1), name="fox_prep")(plt)


def _softmax_step(s, m_prev):
    m_new = jnp.maximum(m_prev, jnp.max(s, axis=-1, keepdims=True))
    p = jnp.exp2(s - _lanes(m_new, s.shape[1]))
    alpha = jnp.exp2(m_prev - m_new)
    return p.astype(BF), alpha, m_new


def _fox_qpad(qx, h, kind="fox"):
    p, hh = divmod(h, 2)
    lane = lax.broadcasted_iota(jnp.int32, (qx.shape[0], 2 * LANES), 1)
    blk = qx[:, 2 * LANES * p:2 * LANES * (p + 1)]
    if kind == "fox":
        keep = ((lane >= FOX_DIM * hh) & (lane < FOX_DIM * (hh + 1))) | (
            (lane >= LANES + AUG * h) & (lane < LANES + AUG * (h + 1)))
    else:
        keep = (lane >= LANES * hh) & (lane < LANES * (hh + 1))
    return jnp.where(keep, blk, jnp.zeros_like(blk))


def _fox_rows_step(s, vpa, m_ref, l_ref, h, r0, r1):
    pb, alpha, m_new = _softmax_step(s, m_ref[h, r0:r1])
    pv = _dot(pb, vpa)
    l_ref[h, r0:r1] = alpha * l_ref[h, r0:r1] + pv[:, LANES:]
    m_ref[h, r0:r1] = m_new
    return alpha, pv[:, :LANES]


def _fox_pair_acc(acc_ref, p, r0, r1, a, b):
    first = lax.broadcasted_iota(jnp.int32, (r1 - r0, LANES), 1) < FOX_DIM
    acc_ref[p, r0:r1] = acc_ref[p, r0:r1] * jnp.where(first, a[0], b[0]) + jnp.where(first, a[1], b[1])


def _fox_prompt_tile(kx, vb, qpad_ref, m_ref, l_ref, acc_ref):
    tq = qpad_ref.shape[1]
    ones = jnp.ones((kx.shape[0], LANES), BF)

    def scores(h):
        return _dot_nt(qpad_ref[h], kx[:, 2 * LANES * (h // 2):2 * LANES * (h // 2 + 1)])

    s_next = scores(0)
    for p in range(N_PAIR):
        vpa = jnp.concatenate([vb[:, LANES * p:LANES * (p + 1)], ones], axis=1)
        res = []
        for hh in range(2):
            h = 2 * p + hh
            s = s_next
            if h + 1 < HEADS:
                s_next = scores(h + 1)
            res.append(_fox_rows_step(s, vpa, m_ref, l_ref, h, 0, tq))
        _fox_pair_acc(acc_ref, p, 0, tq, *res)


def _fox_diag_tile(kx, vb, qpad_ref, m_ref, l_ref, acc_ref, kind):
    tq = qpad_ref.shape[1]
    half = tq // 2
    ones = jnp.ones((tq, LANES), BF)
    col = lax.broadcasted_iota(jnp.int32, (half, half), 1)
    row = lax.broadcasted_iota(jnp.int32, (half, half), 0)
    if kind == "mla":
        shift = CHUNK.bit_length() - 1
        col, row = lax.shift_right_logical(col, shift), lax.shift_right_logical(row, shift)
    tri = col <= row

    def scores(h):
        kxp = kx[:, 2 * LANES * (h // 2):2 * LANES * (h // 2 + 1)]
        q = qpad_ref[h]
        return _dot_nt(q, kxp[:half]), _dot_nt(q[half:], kxp[half:])

    s_next = scores(0)
    for p in range(N_PAIR):
        vpa = jnp.concatenate([vb[:, LANES * p:LANES * (p + 1)], ones], axis=1)
        top, bot = [], []
        for hh in range(2):
            h = 2 * p + hh
            s_a, s_b = s_next
            if h + 1 < HEADS:
                s_next = scores(h + 1)
            s_top = jnp.where(tri, s_a[:half], NEG)
            s_bot = jnp.concatenate([s_a[half:], jnp.where(tri, s_b, NEG)], axis=1)
            top.append(_fox_rows_step(s_top, vpa[:half], m_ref, l_ref, h, 0, half))
            bot.append(_fox_rows_step(s_bot, vpa, m_ref, l_ref, h, half, tq))
        _fox_pair_acc(acc_ref, p, 0, half, *top)
        _fox_pair_acc(acc_ref, p, half, tq, *bot)


def _fox_prompt_kernel(qx_ref, kx_ref, vb_ref, o_ref, qpad_ref, m_ref, l_ref, acc_ref, *, kind):
    i = pl.program_id(1)
    j = pl.program_id(2)

    @pl.when(j == 0)
    def _():
        qx = qx_ref[...]
        for h in range(HEADS):
            qpad_ref[h] = _fox_qpad(qx, h, kind)
        m_ref[...] = jnp.full(m_ref.shape, NEG, F32)
        l_ref[...] = jnp.zeros(l_ref.shape, F32)
        acc_ref[...] = jnp.zeros(acc_ref.shape, F32)

    @pl.when(j < i)
    def _():
        _fox_prompt_tile(kx_ref[...], vb_ref[...], qpad_ref, m_ref, l_ref, acc_ref)

    @pl.when(j == i)
    def _():
        _fox_diag_tile(kx_ref[...], vb_ref[...], qpad_ref, m_ref, l_ref, acc_ref, kind)
        first = lax.broadcasted_iota(jnp.int32, (qpad_ref.shape[1], LANES), 1) < FOX_DIM
        for p in range(N_PAIR):
            l_sel = jnp.where(first, l_ref[2 * p], l_ref[2 * p + 1])
            o_ref[:, LANES * p:LANES * (p + 1)] = (acc_ref[p] / l_sel).astype(o_ref.dtype)


def _fox_prompt(qx, kx, vb, *, batch, seq, t, kind):
    nq = seq // t
    assert CHUNK & (CHUNK - 1) == 0 and t % CHUNK == 0
    return pl.pallas_call(
        functools.partial(_fox_prompt_kernel, kind=kind), grid=(batch, nq, nq),
        in_specs=[pl.BlockSpec((t, 2 * FOX_W), lambda b, i, j: (b * nq + i, 0)),
                  pl.BlockSpec((t, 2 * FOX_W), lambda b, i, j: (b * nq + jnp.minimum(i, j), 0)),
                  pl.BlockSpec((t, FOX_W), lambda b, i, j: (b * nq + jnp.minimum(i, j), 0))],
        out_specs=pl.BlockSpec((t, FOX_W), lambda b, i, j: (b * nq + i, 0)),
        out_shape=jax.ShapeDtypeStruct((batch * seq, FOX_W), BF),
        scratch_shapes=[pltpu.VMEM((HEADS, t, 2 * LANES), BF), pltpu.VMEM((HEADS, t, LANES), F32),
                        pltpu.VMEM((HEADS, t, LANES), F32), pltpu.VMEM((N_PAIR, t, LANES), F32)],
        compiler_params=_params(3), name=kind + "_prompt")(qx, kx, vb)


def _fox_sample_kernel(qx_ref, kx_ref, vb_ref, fqs_ref, kct_ref, vct_ref, gk_ref, o_ref,
                       qbd_ref, qpad_ref, fq_ref, m_ref, l_ref, acc_ref, *, n_cache):
    j = pl.program_id(1)
    tq = qx_ref.shape[0]
    rows = HEADS * tq

    def update(s, pv_fn, g0=0, g1=2):
        a, b = 4 * tq * g0, 4 * tq * g1
        m_prev = m_ref[a:b]
        m_new = jnp.maximum(m_prev, jnp.max(s, axis=-1, keepdims=True))
        p = jnp.exp2(s - _lanes(m_new, s.shape[1]))
        alpha = jnp.exp2(m_prev - m_new)
        l_ref[a:b] = alpha * l_ref[a:b] + jnp.sum(p, axis=-1, keepdims=True)
        m_ref[a:b] = m_new
        pb = p.astype(BF)
        for pr in range(2 * g0, 2 * g1):
            r0, r1 = 2 * tq * pr, 2 * tq * (pr + 1)
            acc_ref[r0:r1] = acc_ref[r0:r1] * alpha[r0 - a:r1 - a] + pv_fn(pr, pb[r0 - a:r1 - a])

    @pl.when(j == 0)
    def _():
        qx = qx_ref[...]
        fqs = fqs_ref[...]
        qc = jnp.concatenate([qx[:, 2 * LANES * p:2 * LANES * p + LANES] for p in range(N_PAIR)], axis=1)
        lane = lax.broadcasted_iota(jnp.int32, qc.shape, 1)
        for h in range(HEADS):
            qbd_ref[h * tq:(h + 1) * tq, :] = jnp.where((lane // FOX_DIM) == h, qc, jnp.zeros_like(qc))
            qpad_ref[h] = _fox_qpad(qx, h)
            fq_ref[h * tq:(h + 1) * tq, :] = jnp.broadcast_to(fqs[:, h:h + 1], (tq, LANES))
        m_ref[...] = jnp.full(m_ref.shape, NEG, F32)
        l_ref[...] = jnp.zeros(l_ref.shape, F32)
        acc_ref[...] = jnp.zeros(acc_ref.shape, F32)
        kx = kx_ref[...]
        vb = vb_ref[...]
        tk = kx.shape[0]
        s = jnp.concatenate([_dot_nt(qpad_ref[h], kx[:, 2 * LANES * (h // 2):2 * LANES * (h // 2 + 1)])
                             for h in range(HEADS)], axis=0)
        row = lax.broadcasted_iota(jnp.int32, (rows, tk), 0) & (tq - 1)
        s = jnp.where(lax.broadcasted_iota(jnp.int32, (rows, tk), 1) <= row, s, NEG)
        update(s, lambda pr, pb: _dot(pb, vb[:, LANES * pr:LANES * (pr + 1)]))

    @pl.when(j > 0)
    def _():
        tk = kct_ref.shape[3]
        gk = gk_ref[0]

        def scores(g):
            kt = kct_ref[0, 4 * g:4 * g + 4].reshape(4 * FOX_DIM, tk).astype(BF)
            gk_rows = jnp.concatenate([jnp.broadcast_to(gk[h:h + 1, :], (tq, tk)) for h in range(4 * g, 4 * g + 4)],
                                      axis=0)
            qg = qbd_ref[4 * tq * g:4 * tq * (g + 1), 4 * FOX_DIM * g:4 * FOX_DIM * (g + 1)]
            return _dot(qg, kt) + (_lanes(fq_ref[4 * tq * g:4 * tq * (g + 1)], tk) - gk_rows)

        def pv(pr, pb):
            return _dot_nt(pb, vct_ref[0, 2 * pr:2 * pr + 2].reshape(2 * FOX_DIM, tk).astype(BF))

        s0 = scores(0)
        s1 = scores(1)
        update(s0, pv, 0, 1)
        update(s1, pv, 1, 2)

    @pl.when(j == n_cache)
    def _():
        first = lax.broadcasted_iota(jnp.int32, (tq, LANES), 1) < FOX_DIM
        o = acc_ref[...] / l_ref[...]
        for pr in range(N_PAIR):
            o_ref[:, LANES * pr:LANES * (pr + 1)] = jnp.where(
                first, o[2 * tq * pr:2 * tq * pr + tq], o[2 * tq * pr + tq:2 * tq * (pr + 1)]).astype(o_ref.dtype)


def _fox_sample(qx, kx, vb, fqs, kct, vct, gk, *, batch, seq, tkc):
    past = kct.shape[3]
    n_cache = past // tkc
    kern = functools.partial(_fox_sample_kernel, n_cache=n_cache)
    row = lambda w: pl.BlockSpec((seq, w), lambda b, j: (b, 0))
    return pl.pallas_call(
        kern, grid=(batch, n_cache + 1),
        in_specs=[row(2 * FOX_W), row(2 * FOX_W), row(FOX_W), row(LANES),
                  pl.BlockSpec((1, HEADS, FOX_DIM, tkc), lambda b, j: (b, 0, 0, jnp.maximum(j - 1, 0))),
                  pl.BlockSpec((1, HEADS, FOX_DIM, tkc), lambda b, j: (b, 0, 0, jnp.maximum(j - 1, 0))),
                  pl.BlockSpec((1, HEADS, tkc), lambda b, j: (b, 0, jnp.maximum(j - 1, 0)))],
        out_specs=row(FOX_W),
        out_shape=jax.ShapeDtypeStruct((batch * seq, FOX_W), BF),
        scratch_shapes=[pltpu.VMEM((HEADS * seq, FOX_W), BF), pltpu.VMEM((HEADS, seq, 2 * LANES), BF)]
        + [pltpu.VMEM((HEADS * seq, LANES), F32)] * 4,
        compiler_params=_params(2), name="fox_sample")(qx, kx, vb, fqs, kct, vct, gk)


def _chunk_mask(shape, tq, row0, q0, k0):
    assert tq & (tq - 1) == 0 and CHUNK & (CHUNK - 1) == 0
    shift = CHUNK.bit_length() - 1
    qpos = q0 + ((row0 + lax.broadcasted_iota(jnp.int32, shape, 0)) & (tq - 1))
    kpos = k0 + lax.broadcasted_iota(jnp.int32, shape, 1)
    return lax.shift_right_logical(kpos, shift) <= lax.shift_right_logical(qpos, shift)


def _mla_rows_update(s, val, m_ref, l_ref, acc_ref, r0, r1):
    m_prev = m_ref[r0:r1]
    m_new = jnp.maximum(m_prev, jnp.max(s, axis=-1, keepdims=True))
    p = jnp.exp2(s - _lanes(m_new, s.shape[1]))
    alpha = jnp.exp2(m_prev - m_new)
    l_ref[r0:r1] = alpha * l_ref[r0:r1] + jnp.sum(p, axis=-1, keepdims=True)
    m_ref[r0:r1] = m_new
    acc_ref[r0:r1] = acc_ref[r0:r1] * _lanes(alpha, KV_LORA) + _dot(p.astype(BF), val)


def _mla_init_stats(m_ref, l_ref, acc_ref):
    m_ref[...] = jnp.full(m_ref.shape, NEG, F32)
    l_ref[...] = jnp.zeros(l_ref.shape, F32)
    acc_ref[...] = jnp.zeros(acc_ref.shape, F32)


def _mla_finalize(o_ref, wuv_ref, l_ref, acc_ref, tq):
    olat = (acc_ref[...] / _lanes(l_ref[...], KV_LORA)).astype(BF)
    for g in range(2):
        out = None
        for h in range(4 * g, 4 * g + 4):
            part = _dot(olat[h * tq:(h + 1) * tq, :], wuv_ref[h])
            out = part if out is None else out + part
        o_ref[:, 256 * g:256 * (g + 1)] = out.astype(o_ref.dtype)


def _mla_scratch(rows):
    return [pltpu.VMEM((rows, LANES), F32), pltpu.VMEM((rows, LANES), F32), pltpu.VMEM((rows, KV_LORA), F32)]


def _mla_sample_kernel(qlat_ref, qrope_ref, mkey_ref, latc_ref, krt_ref, wuv_ref, o_ref,
                       ql_ref, qr_ref, m_ref, l_ref, acc_ref, *, n_cache, tq, past, rc):
    j = pl.program_id(1)
    rows = HEADS * tq

    @pl.when(j == 0)
    def _():
        qrope = qrope_ref[...]
        for h in range(HEADS):
            ql_ref[h * tq:(h + 1) * tq, :] = qlat_ref[h]
            qr_ref[h * tq:(h + 1) * tq, :] = qrope[:, MLA_ROPE * h:MLA_ROPE * (h + 1)]
        _mla_init_stats(m_ref, l_ref, acc_ref)
        mk = mkey_ref[...]
        s = _dot_nt(ql_ref[...], mk[:, :KV_LORA]) + _dot_nt(qr_ref[...], mk[:, KV_LORA:KV_LORA + MLA_ROPE])
        if past % CHUNK != 0 or tq > CHUNK:
            s = jnp.where(_chunk_mask(s.shape, tq, 0, past, past), s, NEG)
        _mla_rows_update(s, mk[:, :KV_LORA], m_ref, l_ref, acc_ref, 0, rows)

    @pl.when(j > 0)
    def _():
        c = latc_ref[0].astype(BF)
        krt = krt_ref[0].astype(BF)
        def scores(r0):
            return _dot_nt(ql_ref[r0:r0 + rc], c) + _dot(qr_ref[r0:r0 + rc], krt)

        s_next = scores(0)
        for r0 in range(0, rows, rc):
            s = s_next
            if r0 + rc < rows:
                s_next = scores(r0 + rc)
            _mla_rows_update(s, c, m_ref, l_ref, acc_ref, r0, r0 + rc)

    @pl.when(j == n_cache)
    def _():
        _mla_finalize(o_ref, wuv_ref, l_ref, acc_ref, tq)


def _mla_sample(qlat, qrope, mkey, latc, krt, wuvp, *, batch, seq, tkc, rc):
    past = latc.shape[1]
    n_cache = past // tkc
    kern = functools.partial(_mla_sample_kernel, n_cache=n_cache, tq=seq, past=past, rc=rc)
    return pl.pallas_call(
        kern, grid=(batch, n_cache + 1),
        in_specs=[pl.BlockSpec((HEADS, seq, KV_LORA), lambda b, j: (0, b, 0)),
                  pl.BlockSpec((seq, HEADS * MLA_ROPE), lambda b, j: (b, 0)),
                  pl.BlockSpec((seq, 2 * KV_LORA), lambda b, j: (b, 0)),
                  pl.BlockSpec((1, tkc, KV_LORA), lambda b, j: (b, jnp.maximum(j - 1, 0), 0)),
                  pl.BlockSpec((1, MLA_ROPE, tkc), lambda b, j: (b, 0, jnp.maximum(j - 1, 0))),
                  _const_spec(wuvp.shape)],
        out_specs=pl.BlockSpec((seq, FOX_W), lambda b, j: (b, 0)),
        out_shape=jax.ShapeDtypeStruct((batch * seq, FOX_W), BF),
        scratch_shapes=[pltpu.VMEM((HEADS * seq, KV_LORA), BF), pltpu.VMEM((HEADS * seq, MLA_ROPE), BF)]
        + _mla_scratch(HEADS * seq),
        compiler_params=_params(2), name="mla_sample")(qlat, qrope, mkey, latc, krt, wuvp)


def _ffn_kernel(x_ref, fox_ref, mla_ref, pconv_ref, wo_ref, gf_ref, wup_ref, cw_ref, wdn_ref, gfin_ref,
                y_ref, cst_ref, hn_ref, act_ref, ush_ref, carry_ref, *, tm, seq_len):
    n_seg = max(1, tm // seq_len)
    seg = tm // n_seg
    mixed = jnp.concatenate([fox_ref[...], mla_ref[...]], axis=1)
    x1 = x_ref[...] + _dot(mixed, wo_ref[...])
    hn_ref[...] = (x1 * lax.rsqrt(jnp.mean(x1 * x1, axis=-1, keepdims=True) + EPS) * gf_ref[...]).astype(BF)
    y_ref[...] = x1

    if n_seg == 1:
        @pl.when(pl.program_id(0) % (seq_len // tm) == 0)
        def _():
            carry_ref[...] = pconv_ref[0]

    def conv(u, col, slot):
        cols = slice(col, col + FF_CHUNK)
        for s in range(n_seg):
            ush_ref[slot, s, 6:8, :] = carry_ref[:, cols] if n_seg == 1 else pconv_ref[s, :, cols]
            ush_ref[slot, s, 8:8 + seg, :] = u[seg * s:seg * (s + 1), :]
            cst_ref[s, :, cols] = u[seg * (s + 1) - 2:seg * (s + 1), :]
        if n_seg == 1:
            carry_ref[:, cols] = u[tm - 2:tm, :]
        u1 = jnp.concatenate([ush_ref[slot, s, 7:7 + seg, :] for s in range(n_seg)], axis=0)
        u2 = jnp.concatenate([ush_ref[slot, s, 6:6 + seg, :] for s in range(n_seg)], axis=0)
        cw = cw_ref[:, cols]
        return cw[3:4, :] + cw[0:1, :] * u2 + cw[1:2, :] * u1 + cw[2:3, :] * u

    def up(c):
        hn = hn_ref[...]
        g0, v0 = FF_CHUNK * c, D_FF + FF_CHUNK * c
        return _dot(hn, wup_ref[:, g0:g0 + FF_CHUNK]), _dot(hn, wup_ref[:, v0:v0 + FF_CHUNK])

    u_next = up(0)
    for c in range(N_FF):
        ug, uv = u_next
        if c + 1 < N_FF:
            u_next = up(c + 1)
        gate = conv(ug, FF_CHUNK * c, 2 * (c % 2))
        val = conv(uv, D_FF + FF_CHUNK * c, 2 * (c % 2) + 1)
        act_ref[:, FF_CHUNK * c:FF_CHUNK * (c + 1)] = (gate * jax.nn.sigmoid(gate) * val).astype(BF)

    x2 = y_ref[...] + _dot(act_ref[...], wdn_ref[...])
    y_ref[...] = x2 * lax.rsqrt(jnp.mean(x2 * x2, axis=-1, keepdims=True) + EPS) * gfin_ref[...]


def _ffn(x2d, fox, mla, pconv, wo, gf, wup, cw, wdn, gfin, *, seq_len, tm):
    n = x2d.shape[0]
    nt = n // tm
    n_seq = n // seq_len
    n_seg = max(1, tm // seq_len)
    if n_seg == 1:
        tps = seq_len // tm
        st_map = lambda i: (i // tps, 0, 0)
    else:
        st_map = lambda i: (i, 0, 0)
    row = lambda w: pl.BlockSpec((tm, w), lambda i: (i, 0))
    kern = functools.partial(_ffn_kernel, tm=tm, seq_len=seq_len)
    return pl.pallas_call(
        kern, grid=(nt,),
        in_specs=[row(D_MODEL), row(FOX_W), row(FOX_W),
                  pl.BlockSpec((n_seg, CONV_W - 1, 2 * D_FF), st_map),
                  _const_spec(wo.shape), _const_spec(gf.shape), _const_spec(wup.shape), _const_spec(cw.shape),
                  _const_spec(wdn.shape), _const_spec(gfin.shape)],
        out_specs=(row(D_MODEL), pl.BlockSpec((n_seg, CONV_W - 1, 2 * D_FF), st_map)),
        out_shape=(jax.ShapeDtypeStruct((n, D_MODEL), F32),
                   jax.ShapeDtypeStruct((n_seq, CONV_W - 1, 2 * D_FF), F32)),
        scratch_shapes=[pltpu.VMEM((tm, D_MODEL), BF), pltpu.VMEM((tm, D_FF), BF),
                        pltpu.VMEM((4, n_seg, 8 + tm // n_seg, FF_CHUNK), F32),
                        pltpu.VMEM((CONV_W - 1, 2 * D_FF), F32)],
        compiler_params=_params(1), name="ffn")(x2d, fox, mla, pconv, wo, gf, wup, cw, wdn, gfin)


def _rope_tables(start, n, reps=1):
    half = MLA_ROPE // 2
    inv = ROPE_THETA ** (-np.arange(half, dtype=np.float64) / half)
    ang = (start + np.arange(n, dtype=np.float64))[:, None] * inv[None, :]
    cos = np.tile(np.cos(ang), (1, 2 * HEADS))
    sin = np.tile(np.sin(ang), (1, 2 * HEADS))
    ones = np.ones((n, MLA_NOPE))
    zeros = np.zeros((n, LANES - MLA_NOPE - MLA_ROPE))
    cosm = np.concatenate([ones, cos[:, :MLA_ROPE], zeros], axis=1)
    sinm = np.concatenate([0.0 * ones, sin[:, :MLA_ROPE], zeros], axis=1)
    return [jnp.asarray(np.tile(a, (reps, 1)), F32) for a in (cos, sin, cosm, sinm)]


def _placement():
    place = np.zeros((LANES, 2 * N_PAIR * LANES), np.float32)
    ones = np.zeros((1, 2 * N_PAIR * LANES), np.float32)
    for h in range(HEADS):
        for i in range(3):
            for p in range(N_PAIR):
                place[HEADS * i + h, LANES * p + AUG * h + i] = 1.0
                ones[0, LANES * p + AUG * h + 3 + i] = 1.0
                ones[0, N_PAIR * LANES + LANES * p + AUG * h + i] = 1.0
                place[HEADS * i + h, N_PAIR * LANES + LANES * p + AUG * h + 3 + i] = -1.0
    rep = np.zeros((LANES, HEADS * MLA_ROPE), np.float32)
    for h in range(HEADS):
        for r in range(MLA_ROPE):
            rep[r, MLA_ROPE * h + r] = 1.0
    return jnp.asarray(place, BF), jnp.asarray(ones, F32), jnp.asarray(rep, BF)


def _rot_cols(w):
    half = MLA_ROPE // 2
    return jnp.concatenate([-w[..., half:], w[..., :half]], axis=-1)


def _pad_cols(w, width):
    return jnp.pad(w, ((0, 0), (0, width - w.shape[1])))


def _layer_weights(w_in, b_f, w_q_up, w_uk, w_uv, w_out, w_up, conv_w, conv_b, w_down):
    wq, wk, wv, wf, wqc, wckv, wkr = jnp.split(w_in, IN_SPLITS, axis=1)
    wa = jnp.concatenate([wq, wk, wv, wqc, wckv, _pad_cols(wkr, LANES), _pad_cols(_rot_cols(wkr), LANES),
                          _pad_cols(wf, LANES)], axis=1).astype(BF)
    bfp = _pad_cols(b_f[None, :], LANES).astype(F32)
    wq3 = w_q_up.reshape(Q_LORA, HEADS, MLA_NOPE + MLA_ROPE)
    wq_nope = wq3[:, :, :MLA_NOPE].reshape(Q_LORA, HEADS * MLA_NOPE)
    wq_rope = wq3[:, :, MLA_NOPE:]
    wq2 = jnp.concatenate([wq_nope, wq_rope.reshape(Q_LORA, -1), _rot_cols(wq_rope).reshape(Q_LORA, -1)],
                          axis=1).astype(BF)
    eye = jnp.eye(4, dtype=F32)
    wk = jnp.transpose(w_uk, (1, 2, 0)).reshape(2, 4, 1, MLA_NOPE, KV_LORA)
    wukp = (wk * eye[None, :, :, None, None]).reshape(HEADS, 4 * MLA_NOPE, KV_LORA)
    wv = jnp.transpose(w_uv, (1, 0, 2)).reshape(2, 4, KV_LORA, 1, MLA_VDIM)
    wuvp = (wv * eye[None, :, None, :, None]).reshape(HEADS, KV_LORA, 4 * MLA_VDIM)
    wup = w_up.astype(BF)
    cw = jnp.concatenate([conv_w, conv_b[None, :]], axis=0)
    wdn = w_down.astype(BF)
    zpad = ((0, 0), (0, 0), (0, LANES - MLA_NOPE - MLA_ROPE))
    wq_a = jnp.pad(wq3, zpad).reshape(Q_LORA, HEADS * LANES)
    wq_b = jnp.pad(jnp.concatenate([jnp.zeros_like(wq3[:, :, :MLA_NOPE]), _rot_cols(wq_rope)], axis=2), zpad)
    wqm = jnp.concatenate([wq_a, wq_b.reshape(Q_LORA, HEADS * LANES)], axis=1).astype(BF)
    wkm = jnp.pad(w_uk, ((0, 0), (0, 0), (0, LANES - MLA_NOPE))).reshape(KV_LORA, HEADS * LANES).astype(BF)
    wvm = w_uv.reshape(KV_LORA, HEADS * MLA_VDIM).astype(BF)
    return wa, bfp, wq2, wukp.astype(BF), wuvp.astype(BF), w_out.astype(BF), wup, cw, wdn, wqm, wkm, wvm


def kernel(x_prompt, x_sample, cache_fox_k, cache_fox_v, cache_fox_logf, cache_mla_latent, cache_mla_krope,
           state_ffn_conv, attn_norm, w_in, b_forget, q_norm, w_q_up, kv_norm, w_uk, w_uv, w_out, ffn_norm,
           w_up, conv_w, conv_b, w_down, final_norm):
    assert attn_norm.shape[0] == 1, "single-layer stack"
    bp, tp, _ = x_prompt.shape
    bs, ts, _ = x_sample.shape
    past = cache_fox_k.shape[2]
    assert ts == CHUNK and past % CHUNK == 0

    wa, bfp, wq2, wukp, wuvp, wo, wup, cw, wdn, wqm, wkm, wvm = _layer_weights(
        w_in[0], b_forget[0], w_q_up[0], w_uk[0], w_uv[0], w_out[0], w_up[0], conv_w[0], conv_b[0], w_down[0])
    place, ones, rep = _placement()
    g_attn = attn_norm[0][None, :]
    g_q = q_norm[0][None, :]
    g_kv = kv_norm[0][None, :]
    g_ffn = ffn_norm[0][None, :]
    g_fin = final_norm[None, :]

    tm = 512
    outs = []
    for grp in ("prompt", "sample"):
        if grp == "prompt":
            x2d = x_prompt.reshape(bp * tp, D_MODEL)
            b, t = bp, tp
            tabs = _rope_tables(0, tp)
            pconv = jnp.zeros((bp, CONV_W - 1, 2 * D_FF), F32)
        else:
            x2d = x_sample.reshape(bs * ts, D_MODEL)
            b, t = bs, ts
            tabs = _rope_tables(past, ts, reps=tm // ts)
            pconv = state_ffn_conv[0]

        absorbed = grp == "sample"
        wm = (wq2, wukp, rep) if absorbed else (wqm, wkm, wvm)
        (k, v, logf, lat, kr, qx, kx, vb, m1, m2, m3, fqs) = _proj(
            x2d, *tabs, g_attn, wa, bfp, g_q, g_kv, *wm, place, ones, seq_len=t, tm=tm, absorbed=absorbed)

        if grp == "prompt":
            fox = _fox_prompt(qx, kx, vb, batch=b, seq=t, t=512, kind="fox")
            mla = _fox_prompt(m1, m2, m3, batch=b, seq=t, t=512, kind="mla")
        else:
            mkey, qlat, qrope = m1, m2, m3
            gk = _fox_prep(jnp.transpose(cache_fox_logf[0], (0, 2, 1)))
            kct = jnp.transpose(cache_fox_k[0], (0, 2, 3, 1))
            vct = jnp.transpose(cache_fox_v[0], (0, 2, 3, 1))
            krt = jnp.transpose(cache_mla_krope[0], (0, 2, 1))
            fox = _fox_sample(qx, kx, vb, fqs, kct, vct, gk, batch=b, seq=t, tkc=min(past, 2048))
            mla = _mla_sample(qlat, qrope, mkey, cache_mla_latent[0], krt, wuvp, batch=b, seq=t,
                              tkc=min(past, 2048), rc=256)

        def small_state(a, w, b=b, t=t):
            return (jnp.transpose(a, (0, 2, 1)) if a.ndim == 3 else a.reshape(b, t, w))[None]

        y, cst = _ffn(x2d, fox, mla, pconv, wo, g_ffn, wup, cw, wdn, g_fin, seq_len=t, tm=tm)
        outs.append((y.reshape(b, t, D_MODEL),
                     k.reshape(1, b, t, HEADS, FOX_DIM), v.reshape(1, b, t, HEADS, FOX_DIM),
                     small_state(logf, HEADS), lat.reshape(1, b, t, KV_LORA), small_state(kr, MLA_ROPE),
                     cst[None]))
    (yp, *st_p), (ys, *st_s) = outs
    return (yp, ys, *st_p, *st_s)
```

```python
import functools

import numpy as np
import jax
import jax.numpy as jnp
from jax import lax
from jax.experimental import pallas as pl
from jax.experimental.pallas import tpu as pltpu

D_MODEL = 1024
CHUNK = 64
HEADS = 8
FOX_DIM = 64
MLA_NOPE = 64
MLA_ROPE = 32
MLA_VDIM = 64
Q_LORA = 384
KV_LORA = 256
D_FF = 2816
CONV_W = 3
ROPE_THETA = 10000.0
EPS = 1e-6
NEG = -1e30
LOG2E = 1.4426950408889634

FOX_W = HEADS * FOX_DIM
IN_SIZES = [FOX_W, FOX_W, FOX_W, HEADS, Q_LORA, KV_LORA, MLA_ROPE]
IN_SPLITS = [int(s) for s in np.cumsum(IN_SIZES)[:-1]]

LANES = 128
AUG = 16
N_PAIR = HEADS // 2
FF_CHUNK = 256
N_FF = D_FF // FF_CHUNK
VMEM_LIMIT = 56 * 1024 * 1024

C_Q, C_K, C_V = 0, 512, 1024
C_QC = 1536
C_CKV = 1920
C_KR = 2176
C_KRR = 2304
C_F = 2432
N_A = 2560

BF = jnp.bfloat16
F32 = jnp.float32


def _dot(a, b):
    return jnp.dot(a, b, preferred_element_type=F32)


def _dot_nt(a, b):
    return lax.dot_general(a, b, (((1,), (1,)), ((), ())), preferred_element_type=F32)


def _split3(x):
    hi = x.astype(BF)
    r1 = x - hi.astype(F32)
    mid = r1.astype(BF)
    lo = (r1 - mid.astype(F32)).astype(BF)
    return hi, mid, lo


def _log_sigmoid(x):
    return jnp.minimum(x, 0.0) - jnp.log1p(jnp.exp(-jnp.abs(x)))


def _lanes(x, n):
    if n % LANES == 0:
        return x if n == LANES else jnp.concatenate([x] * (n // LANES), axis=1)
    assert n < LANES
    return x[:, :n]


def _params(n_axes):
    return pltpu.CompilerParams(dimension_semantics=("arbitrary",) * n_axes, vmem_limit_bytes=VMEM_LIMIT)


def _const_spec(shape):
    nd = len(shape)
    return pl.BlockSpec(shape, lambda *_: (0,) * nd, pipeline_mode=pl.Buffered(1))


def _proj_kernel(x_ref, cos_ref, sin_ref, cosm_ref, sinm_ref, g_ref, wa_ref, bf_ref, gq_ref, gkv_ref,
                 w1_ref, w2_ref, w3_ref, place_ref, ones_ref,
                 k_ref, v_ref, logf_ref, lat_ref, kr_ref, qx_ref, kx_ref, vb_ref, o1_ref, o2_ref, o3_ref,
                 fqs_ref, carry_ref, *, tm, seq_len, mla_scale, absorbed):
    x = x_ref[...]
    ms = jnp.mean(x * x, axis=-1, keepdims=True)
    hn = (x * lax.rsqrt(ms + EPS) * g_ref[...]).astype(BF)
    zs = _dot(hn, wa_ref[:, C_QC:])
    z = _dot(hn, wa_ref[:, :C_QC])

    lane = lax.broadcasted_iota(jnp.int32, (tm, LANES), 1)
    logf = _log_sigmoid(zs[:, C_F - C_QC:C_F - C_QC + LANES] + bf_ref[...])
    logf = jnp.where(lane < HEADS, logf, 0.0)
    logf_t = logf.T[:HEADS, :]
    if seq_len >= tm:
        logf_ref[0] = logf_t
    else:
        logf_ref[...] = logf[:, :HEADS]
    seg = min(seq_len, tm)
    assert seg & (seg - 1) == 0
    pos = lax.broadcasted_iota(jnp.int32, (HEADS, tm), 1) & (seg - 1)
    fc = logf_t
    shift = 1
    while shift < seg:
        fc = fc + jnp.where(pos >= shift, pltpu.roll(fc, shift, 1), 0.0)
        shift *= 2
    if seq_len > tm:
        @pl.when(pl.program_id(0) % (seq_len // tm) == 0)
        def _():
            carry_ref[...] = jnp.zeros_like(carry_ref)
        fc = fc + carry_ref[:, 0:1]
        carry_ref[...] = jnp.broadcast_to(fc[:, tm - 1:tm], carry_ref.shape)
    fcum = jnp.concatenate([fc, jnp.zeros((LANES - HEADS, tm), F32)], axis=0).T

    zc = zs[:, C_CKV - C_QC:C_CKV - C_QC + KV_LORA]
    ckv = zc * lax.rsqrt(jnp.mean(zc * zc, axis=-1, keepdims=True) + EPS) * gkv_ref[...]
    lat_ref[...] = ckv
    cos = cos_ref[...]
    sin = sin_ref[...]
    krb = (zs[:, C_KR - C_QC:C_KR - C_QC + LANES] * cos[:, :LANES]
           + zs[:, C_KRR - C_QC:C_KRR - C_QC + LANES] * sin[:, :LANES])
    if seq_len >= tm:
        kr_ref[0] = krb.T[:MLA_ROPE, :]
    else:
        kr_ref[...] = krb[:, :MLA_ROPE]
    cb = ckv.astype(BF)

    zqc = zs[:, :Q_LORA]
    qcn = (zqc * lax.rsqrt(jnp.mean(zqc * zqc, axis=-1, keepdims=True) + EPS) * gq_ref[...]).astype(BF)
    zq2 = _dot(qcn, w1_ref[...])
    if absorbed:
        mkey_ref, qlat_ref, qrope_ref = o1_ref, o2_ref, o3_ref
        mkey_ref[:, :KV_LORA] = cb
        mkey_ref[:, KV_LORA:] = _dot(krb.astype(BF), w3_ref[...]).astype(BF)
        nr = HEADS * MLA_ROPE
        qr = zq2[:, FOX_W:FOX_W + nr] * cos + zq2[:, FOX_W + nr:FOX_W + 2 * nr] * sin
        qrope_ref[...] = (qr * mla_scale).astype(BF)
        for h in range(HEADS):
            g = h // 4
            qn = zq2[:, 256 * g:256 * (g + 1)].astype(BF)
            qlat_ref[h] = (_dot(qn, w2_ref[h]) * mla_scale).astype(BF)
    else:
        qm_ref, km_ref, vm_ref = o1_ref, o2_ref, o3_ref
        hw = HEADS * LANES
        cosm = jnp.concatenate([cosm_ref[...]] * HEADS, axis=1)
        sinm = jnp.concatenate([sinm_ref[...]] * HEADS, axis=1)
        qm_ref[...] = ((zq2[:, :hw] * cosm + zq2[:, hw:] * sinm) * mla_scale).astype(BF)
        krs = pltpu.roll(krb, MLA_NOPE, 1)
        km_ref[...] = (_dot(cb, w2_ref[...]) + jnp.concatenate([krs] * HEADS, axis=1)).astype(BF)
        vm_ref[...] = _dot(cb, w3_ref[...]).astype(BF)

    zk = z[:, C_K:C_K + FOX_W]
    zv = z[:, C_V:C_V + FOX_W]
    for h in range(HEADS):
        k_ref[pl.ds(h, tm, stride=HEADS), :] = zk[:, FOX_DIM * h:FOX_DIM * (h + 1)]
        v_ref[pl.ds(h, tm, stride=HEADS), :] = zv[:, FOX_DIM * h:FOX_DIM * (h + 1)]
    vb_ref[...] = zv.astype(BF)

    fqs = fcum * LOG2E
    fqs_ref[...] = fqs
    hi, mid, lo = _split3(fqs)
    comb = hi.astype(F32) + pltpu.roll(mid.astype(F32), HEADS, 1) + pltpu.roll(lo.astype(F32), 2 * HEADS, 1)
    aug = _dot(comb.astype(BF), place_ref[...]) + ones_ref[...]
    zq = z[:, C_Q:C_Q + FOX_W] * (FOX_DIM ** -0.5 * LOG2E)
    for p in range(N_PAIR):
        lo_, hi_ = 2 * LANES * p, 2 * LANES * p + LANES
        qx_ref[:, lo_:hi_] = zq[:, LANES * p:LANES * (p + 1)].astype(BF)
        qx_ref[:, hi_:hi_ + LANES] = aug[:, LANES * p:LANES * (p + 1)].astype(BF)
        kx_ref[:, lo_:hi_] = zk[:, LANES * p:LANES * (p + 1)].astype(BF)
        kx_ref[:, hi_:hi_ + LANES] = aug[:, FOX_W + LANES * p:FOX_W + LANES * (p + 1)].astype(BF)


def _proj(x2d, cos, sin, cosm, sinm, g, wa, bfp, gq, gkv, w1, w2, w3, place, ones, *, seq_len, tm, absorbed):
    n = x2d.shape[0]
    nt = n // tm
    if seq_len >= tm:
        tps = seq_len // tm
        tab_map = lambda i: (i % tps, 0)
    else:
        tab_map = lambda i: (0, 0)
    row = lambda w: pl.BlockSpec((tm, w), lambda i: (i, 0))
    if seq_len >= tm:
        small = lambda w: jax.ShapeDtypeStruct((n // seq_len, w, seq_len), F32)
        small_spec = lambda w: pl.BlockSpec((1, w, tm), lambda i: (i // tps, 0, i % tps))
    else:
        small = lambda w: jax.ShapeDtypeStruct((n, w), F32)
        small_spec = row
    if absorbed:
        mla_shapes = (jax.ShapeDtypeStruct((n, 2 * KV_LORA), BF), jax.ShapeDtypeStruct((HEADS, n, KV_LORA), BF),
                      jax.ShapeDtypeStruct((n, HEADS * MLA_ROPE), BF))
        mla_specs = (row(2 * KV_LORA), pl.BlockSpec((HEADS, tm, KV_LORA), lambda i: (0, i, 0)),
                     row(HEADS * MLA_ROPE))
    else:
        mla_shapes = (jax.ShapeDtypeStruct((n, HEADS * LANES), BF), jax.ShapeDtypeStruct((n, HEADS * LANES), BF),
                      jax.ShapeDtypeStruct((n, HEADS * MLA_VDIM), BF))
        mla_specs = (row(HEADS * LANES), row(HEADS * LANES), row(HEADS * MLA_VDIM))
    out_shape = (
        jax.ShapeDtypeStruct((n * HEADS, FOX_DIM), F32),
        jax.ShapeDtypeStruct((n * HEADS, FOX_DIM), F32),
        small(HEADS),
        jax.ShapeDtypeStruct((n, KV_LORA), F32),
        small(MLA_ROPE),
        jax.ShapeDtypeStruct((n, 2 * FOX_W), BF),
        jax.ShapeDtypeStruct((n, 2 * FOX_W), BF),
        jax.ShapeDtypeStruct((n, FOX_W), BF),
        *mla_shapes,
        jax.ShapeDtypeStruct((n, LANES), F32),
    )
    kv_spec = pl.BlockSpec((tm * HEADS, FOX_DIM), lambda i: (i, 0))
    out_specs = (kv_spec, kv_spec, small_spec(HEADS), row(KV_LORA), small_spec(MLA_ROPE), row(2 * FOX_W),
                 row(2 * FOX_W), row(FOX_W), *mla_specs, row(LANES))
    in_specs = [row(D_MODEL),
                pl.BlockSpec((tm, 2 * LANES), tab_map), pl.BlockSpec((tm, 2 * LANES), tab_map),
                pl.BlockSpec((tm, LANES), tab_map), pl.BlockSpec((tm, LANES), tab_map),
                _const_spec(g.shape), _const_spec(wa.shape), _const_spec(bfp.shape), _const_spec(gq.shape),
                _const_spec(gkv.shape), _const_spec(w1.shape), _const_spec(w2.shape), _const_spec(w3.shape),
                _const_spec(place.shape), _const_spec(ones.shape)]
    kern = functools.partial(_proj_kernel, tm=tm, seq_len=seq_len, absorbed=absorbed,
                             mla_scale=float((MLA_NOPE + MLA_ROPE) ** -0.5 * LOG2E))
    return pl.pallas_call(
        kern, grid=(nt,), in_specs=in_specs, out_specs=out_specs, out_shape=out_shape,
        scratch_shapes=[pltpu.VMEM((8, LANES), F32)],
        compiler_params=_params(1), name="proj")(
            x2d, cos, sin, cosm, sinm, g, wa, bfp, gq, gkv, w1, w2, w3, place, ones)


def _fox_prep_kernel(plt_ref, g_ref):
    nb, h, n = plt_ref.shape
    x = plt_ref[...].reshape(nb * h, n)
    lane = lax.broadcasted_iota(jnp.int32, x.shape, 1)
    shift = 1
    while shift < n:
        x = x + jnp.where(lane >= shift, pltpu.roll(x, shift, 1), 0.0)
        shift *= 2
    g_ref[...] = ((x - x[:, n - 1:n]) * LOG2E).reshape(nb, h, n)


def _fox_prep(plt, nb=4):
    b, h, p = plt.shape
    nb = nb if b % nb == 0 else 1
    return pl.pallas_call(
        _fox_prep_kernel, grid=(b // nb,),
        in_specs=[pl.BlockSpec((nb, h, p), lambda i: (i, 0, 0))],
        out_specs=pl.BlockSpec((nb, h, p), lambda i: (i, 0, 0)),
        out_shape=jax.ShapeDtypeStruct((b, h, p), F32),
        compiler_params=_params(1), name="fox_prep")(plt)


def _softmax_step(s, m_prev):
    m_new = jnp.maximum(m_prev, jnp.max(s, axis=-1, keepdims=True))
    p = jnp.exp2(s - _lanes(m_new, s.shape[1]))
    alpha = jnp.exp2(m_prev - m_new)
    return p.astype(BF), alpha, m_new


def _fox_qpad(qx, h, kind="fox"):
    p, hh = divmod(h, 2)
    lane = lax.broadcasted_iota(jnp.int32, (qx.shape[0], 2 * LANES), 1)
    blk = qx[:, 2 * LANES * p:2 * LANES * (p + 1)]
    if kind == "fox":
        keep = ((lane >= FOX_DIM * hh) & (lane < FOX_DIM * (hh + 1))) | (
            (lane >= LANES + AUG * h) & (lane < LANES + AUG * (h + 1)))
    else:
        keep = (lane >= LANES * hh) & (lane < LANES * (hh + 1))
    return jnp.where(keep, blk, jnp.zeros_like(blk))


def _fox_rows_step(s, vpa, m_ref, l_ref, h, r0, r1):
    pb, alpha, m_new = _softmax_step(s, m_ref[h, r0:r1])
    pv = _dot(pb, vpa)
    l_ref[h, r0:r1] = alpha * l_ref[h, r0:r1] + pv[:, LANES:]
    m_ref[h, r0:r1] = m_new
    return alpha, pv[:, :LANES]


def _fox_pair_acc(acc_ref, p, r0, r1, a, b):
    first = lax.broadcasted_iota(jnp.int32, (r1 - r0, LANES), 1) < FOX_DIM
    acc_ref[p, r0:r1] = acc_ref[p, r0:r1] * jnp.where(first, a[0], b[0]) + jnp.where(first, a[1], b[1])


def _fox_prompt_tile(kx, vb, qpad_ref, m_ref, l_ref, acc_ref):
    tq = qpad_ref.shape[1]
    ones = jnp.ones((kx.shape[0], LANES), BF)

    def scores(h):
        return _dot_nt(qpad_ref[h], kx[:, 2 * LANES * (h // 2):2 * LANES * (h // 2 + 1)])

    s_next = scores(0)
    for p in range(N_PAIR):
        vpa = jnp.concatenate([vb[:, LANES * p:LANES * (p + 1)], ones], axis=1)
        res = []
        for hh in range(2):
            h = 2 * p + hh
            s = s_next
            if h + 1 < HEADS:
                s_next = scores(h + 1)
            res.append(_fox_rows_step(s, vpa, m_ref, l_ref, h, 0, tq))
        _fox_pair_acc(acc_ref, p, 0, tq, *res)


def _fox_diag_tile(kx, vb, qpad_ref, m_ref, l_ref, acc_ref, kind):
    tq = qpad_ref.shape[1]
    half = tq // 2
    ones = jnp.ones((tq, LANES), BF)
    col = lax.broadcasted_iota(jnp.int32, (half, half), 1)
    row = lax.broadcasted_iota(jnp.int32, (half, half), 0)
    if kind == "mla":
        shift = CHUNK.bit_length() - 1
        col, row = lax.shift_right_logical(col, shift), lax.shift_right_logical(row, shift)
    tri = col <= row

    def scores(h):
        kxp = kx[:, 2 * LANES * (h // 2):2 * LANES * (h // 2 + 1)]
        q = qpad_ref[h]
        return _dot_nt(q, kxp[:half]), _dot_nt(q[half:], kxp[half:])

    s_next = scores(0)
    for p in range(N_PAIR):
        vpa = jnp.concatenate([vb[:, LANES * p:LANES * (p + 1)], ones], axis=1)
        top, bot = [], []
        for hh in range(2):
            h = 2 * p + hh
            s_a, s_b = s_next
            if h + 1 < HEADS:
                s_next = scores(h + 1)
            s_top = jnp.where(tri, s_a[:half], NEG)
            s_bot = jnp.concatenate([s_a[half:], jnp.where(tri, s_b, NEG)], axis=1)
            top.append(_fox_rows_step(s_top, vpa[:half], m_ref, l_ref, h, 0, half))
            bot.append(_fox_rows_step(s_bot, vpa, m_ref, l_ref, h, half, tq))
        _fox_pair_acc(acc_ref, p, 0, half, *top)
        _fox_pair_acc(acc_ref, p, half, tq, *bot)


def _fox_prompt_kernel(qi_ref, kj_ref, qx_ref, kx_ref, vb_ref, o_ref, qpad_ref, m_ref, l_ref, acc_ref, *, kind):
    i = qi_ref[pl.program_id(1)]
    j = kj_ref[pl.program_id(1)]

    @pl.when(j == 0)
    def _():
        qx = qx_ref[...]
        for h in range(HEADS):
            qpad_ref[h] = _fox_qpad(qx, h, kind)
        m_ref[...] = jnp.full(m_ref.shape, NEG, F32)
        l_ref[...] = jnp.zeros(l_ref.shape, F32)
        acc_ref[...] = jnp.zeros(acc_ref.shape, F32)

    @pl.when(j < i)
    def _():
        _fox_prompt_tile(kx_ref[...], vb_ref[...], qpad_ref, m_ref, l_ref, acc_ref)

    @pl.when(j == i)
    def _():
        _fox_diag_tile(kx_ref[...], vb_ref[...], qpad_ref, m_ref, l_ref, acc_ref, kind)
        first = lax.broadcasted_iota(jnp.int32, (qpad_ref.shape[1], LANES), 1) < FOX_DIM
        for p in range(N_PAIR):
            l_sel = jnp.where(first, l_ref[2 * p], l_ref[2 * p + 1])
            o_ref[:, LANES * p:LANES * (p + 1)] = (acc_ref[p] / l_sel).astype(o_ref.dtype)


def _fox_prompt(qx, kx, vb, *, batch, seq, t, kind):
    nq = seq // t
    assert CHUNK & (CHUNK - 1) == 0 and t % CHUNK == 0
    steps = [(i, j) for i in range(nq) for j in range(i + 1)]
    qi = jnp.asarray([s[0] for s in steps], jnp.int32)
    kj = jnp.asarray([s[1] for s in steps], jnp.int32)
    grid_spec = pltpu.PrefetchScalarGridSpec(
        num_scalar_prefetch=2, grid=(batch, len(steps)),
        in_specs=[pl.BlockSpec((t, 2 * FOX_W), lambda b, s, qi, kj: (b * nq + qi[s], 0)),
                  pl.BlockSpec((t, 2 * FOX_W), lambda b, s, qi, kj: (b * nq + kj[s], 0)),
                  pl.BlockSpec((t, FOX_W), lambda b, s, qi, kj: (b * nq + kj[s], 0))],
        out_specs=pl.BlockSpec((t, FOX_W), lambda b, s, qi, kj: (b * nq + qi[s], 0)),
        scratch_shapes=[pltpu.VMEM((HEADS, t, 2 * LANES), BF), pltpu.VMEM((HEADS, t, LANES), F32),
                        pltpu.VMEM((HEADS, t, LANES), F32), pltpu.VMEM((N_PAIR, t, LANES), F32)])
    return pl.pallas_call(
        functools.partial(_fox_prompt_kernel, kind=kind), grid_spec=grid_spec,
        out_shape=jax.ShapeDtypeStruct((batch * seq, FOX_W), BF),
        compiler_params=_params(2), name=kind + "_prompt")(qi, kj, qx, kx, vb)


def _fox_sample_kernel(qx_ref, kx_ref, vb_ref, fqs_ref, kct_ref, vct_ref, gk_ref, o_ref,
                       qbd_ref, qpad_ref, fq_ref, m_ref, l_ref, acc_ref, *, n_cache):
    j = pl.program_id(1)
    tq = qx_ref.shape[0]
    rows = HEADS * tq

    def update(s, pv_fn, g0=0, g1=2):
        a, b = 4 * tq * g0, 4 * tq * g1
        m_prev = m_ref[a:b]
        m_new = jnp.maximum(m_prev, jnp.max(s, axis=-1, keepdims=True))
        p = jnp.exp2(s - _lanes(m_new, s.shape[1]))
        alpha = jnp.exp2(m_prev - m_new)
        l_ref[a:b] = alpha * l_ref[a:b] + jnp.sum(p, axis=-1, keepdims=True)
        m_ref[a:b] = m_new
        pb = p.astype(BF)
        for pr in range(2 * g0, 2 * g1):
            r0, r1 = 2 * tq * pr, 2 * tq * (pr + 1)
            acc_ref[r0:r1] = acc_ref[r0:r1] * alpha[r0 - a:r1 - a] + pv_fn(pr, pb[r0 - a:r1 - a])

    @pl.when(j == 0)
    def _():
        qx = qx_ref[...]
        fqs = fqs_ref[...]
        qc = jnp.concatenate([qx[:, 2 * LANES * p:2 * LANES * p + LANES] for p in range(N_PAIR)], axis=1)
        lane = lax.broadcasted_iota(jnp.int32, qc.shape, 1)
        for h in range(HEADS):
            qbd_ref[h * tq:(h + 1) * tq, :] = jnp.where((lane // FOX_DIM) == h, qc, jnp.zeros_like(qc))
            qpad_ref[h] = _fox_qpad(qx, h)
            fq_ref[h * tq:(h + 1) * tq, :] = jnp.broadcast_to(fqs[:, h:h + 1], (tq, LANES))
        m_ref[...] = jnp.full(m_ref.shape, NEG, F32)
        l_ref[...] = jnp.zeros(l_ref.shape, F32)
        acc_ref[...] = jnp.zeros(acc_ref.shape, F32)
        kx = kx_ref[...]
        vb = vb_ref[...]
        tk = kx.shape[0]
        s = jnp.concatenate([_dot_nt(qpad_ref[h], kx[:, 2 * LANES * (h // 2):2 * LANES * (h // 2 + 1)])
                             for h in range(HEADS)], axis=0)
        row = lax.broadcasted_iota(jnp.int32, (rows, tk), 0) & (tq - 1)
        s = jnp.where(lax.broadcasted_iota(jnp.int32, (rows, tk), 1) <= row, s, NEG)
        update(s, lambda pr, pb: _dot(pb, vb[:, LANES * pr:LANES * (pr + 1)]))

    @pl.when(j > 0)
    def _():
        tk = kct_ref.shape[3]
        gk = gk_ref[0]

        def scores(g):
            kt = kct_ref[0, 4 * g:4 * g + 4].reshape(4 * FOX_DIM, tk).astype(BF)
            gk_rows = jnp.concatenate([jnp.broadcast_to(gk[h:h + 1, :], (tq, tk)) for h in range(4 * g, 4 * g + 4)],
                                      axis=0)
            qg = qbd_ref[4 * tq * g:4 * tq * (g + 1), 4 * FOX_DIM * g:4 * FOX_DIM * (g + 1)]
            return _dot(qg, kt) + (_lanes(fq_ref[4 * tq * g:4 * tq * (g + 1)], tk) - gk_rows)

        def pv(pr, pb):
            return _dot_nt(pb, vct_ref[0, 2 * pr:2 * pr + 2].reshape(2 * FOX_DIM, tk).astype(BF))

        s0 = scores(0)
        s1 = scores(1)
        update(s0, pv, 0, 1)
        update(s1, pv, 1, 2)

    @pl.when(j == n_cache)
    def _():
        first = lax.broadcasted_iota(jnp.int32, (tq, LANES), 1) < FOX_DIM
        o = acc_ref[...] / l_ref[...]
        for pr in range(N_PAIR):
            o_ref[:, LANES * pr:LANES * (pr + 1)] = jnp.where(
                first, o[2 * tq * pr:2 * tq * pr + tq], o[2 * tq * pr + tq:2 * tq * (pr + 1)]).astype(o_ref.dtype)


def _fox_sample(qx, kx, vb, fqs, kct, vct, gk, *, batch, seq, tkc):
    past = kct.shape[3]
    n_cache = past // tkc
    kern = functools.partial(_fox_sample_kernel, n_cache=n_cache)
    row = lambda w: pl.BlockSpec((seq, w), lambda b, j: (b, 0))
    return pl.pallas_call(
        kern, grid=(batch, n_cache + 1),
        in_specs=[row(2 * FOX_W), row(2 * FOX_W), row(FOX_W), row(LANES),
                  pl.BlockSpec((1, HEADS, FOX_DIM, tkc), lambda b, j: (b, 0, 0, jnp.maximum(j - 1, 0))),
                  pl.BlockSpec((1, HEADS, FOX_DIM, tkc), lambda b, j: (b, 0, 0, jnp.maximum(j - 1, 0))),
                  pl.BlockSpec((1, HEADS, tkc), lambda b, j: (b, 0, jnp.maximum(j - 1, 0)))],
        out_specs=row(FOX_W),
        out_shape=jax.ShapeDtypeStruct((batch * seq, FOX_W), BF),
        scratch_shapes=[pltpu.VMEM((HEADS * seq, FOX_W), BF), pltpu.VMEM((HEADS, seq, 2 * LANES), BF)]
        + [pltpu.VMEM((HEADS * seq, LANES), F32)] * 4,
        compiler_params=_params(2), name="fox_sample")(qx, kx, vb, fqs, kct, vct, gk)


def _chunk_mask(shape, tq, row0, q0, k0):
    assert tq & (tq - 1) == 0 and CHUNK & (CHUNK - 1) == 0
    shift = CHUNK.bit_length() - 1
    qpos = q0 + ((row0 + lax.broadcasted_iota(jnp.int32, shape, 0)) & (tq - 1))
    kpos = k0 + lax.broadcasted_iota(jnp.int32, shape, 1)
    return lax.shift_right_logical(kpos, shift) <= lax.shift_right_logical(qpos, shift)


def _mla_rows_update(s, val, m_ref, l_ref, acc_ref, r0, r1):
    m_prev = m_ref[r0:r1]
    m_new = jnp.maximum(m_prev, jnp.max(s, axis=-1, keepdims=True))
    p = jnp.exp2(s - _lanes(m_new, s.shape[1]))
    alpha = jnp.exp2(m_prev - m_new)
    l_ref[r0:r1] = alpha * l_ref[r0:r1] + jnp.sum(p, axis=-1, keepdims=True)
    m_ref[r0:r1] = m_new
    acc_ref[r0:r1] = acc_ref[r0:r1] * _lanes(alpha, KV_LORA) + _dot(p.astype(BF), val)


def _mla_init_stats(m_ref, l_ref, acc_ref):
    m_ref[...] = jnp.full(m_ref.shape, NEG, F32)
    l_ref[...] = jnp.zeros(l_ref.shape, F32)
    acc_ref[...] = jnp.zeros(acc_ref.shape, F32)


def _mla_finalize(o_ref, wuv_ref, l_ref, acc_ref, tq):
    olat = (acc_ref[...] / _lanes(l_ref[...], KV_LORA)).astype(BF)
    for g in range(2):
        out = None
        for h in range(4 * g, 4 * g + 4):
            part = _dot(olat[h * tq:(h + 1) * tq, :], wuv_ref[h])
            out = part if out is None else out + part
        o_ref[:, 256 * g:256 * (g + 1)] = out.astype(o_ref.dtype)


def _mla_scratch(rows):
    return [pltpu.VMEM((rows, LANES), F32), pltpu.VMEM((rows, LANES), F32), pltpu.VMEM((rows, KV_LORA), F32)]


def _mla_sample_kernel(qlat_ref, qrope_ref, mkey_ref, latc_ref, krt_ref, wuv_ref, o_ref,
                       ql_ref, qr_ref, m_ref, l_ref, acc_ref, *, n_cache, tq, past, rc):
    j = pl.program_id(1)
    rows = HEADS * tq

    @pl.when(j == 0)
    def _():
        qrope = qrope_ref[...]
        for h in range(HEADS):
            ql_ref[h * tq:(h + 1) * tq, :] = qlat_ref[h]
            qr_ref[h * tq:(h + 1) * tq, :] = qrope[:, MLA_ROPE * h:MLA_ROPE * (h + 1)]
        _mla_init_stats(m_ref, l_ref, acc_ref)
        mk = mkey_ref[...]
        s = _dot_nt(ql_ref[...], mk[:, :KV_LORA]) + _dot_nt(qr_ref[...], mk[:, KV_LORA:KV_LORA + MLA_ROPE])
        if past % CHUNK != 0 or tq > CHUNK:
            s = jnp.where(_chunk_mask(s.shape, tq, 0, past, past), s, NEG)
        _mla_rows_update(s, mk[:, :KV_LORA], m_ref, l_ref, acc_ref, 0, rows)

    @pl.when(j > 0)
    def _():
        c = latc_ref[0].astype(BF)
        krt = krt_ref[0].astype(BF)
        def scores(r0):
            return _dot_nt(ql_ref[r0:r0 + rc], c) + _dot(qr_ref[r0:r0 + rc], krt)

        s_next = scores(0)
        for r0 in range(0, rows, rc):
            s = s_next
            if r0 + rc < rows:
                s_next = scores(r0 + rc)
            _mla_rows_update(s, c, m_ref, l_ref, acc_ref, r0, r0 + rc)

    @pl.when(j == n_cache)
    def _():
        _mla_finalize(o_ref, wuv_ref, l_ref, acc_ref, tq)


def _mla_sample(qlat, qrope, mkey, latc, krt, wuvp, *, batch, seq, tkc, rc):
    past = latc.shape[1]
    n_cache = past // tkc
    kern = functools.partial(_mla_sample_kernel, n_cache=n_cache, tq=seq, past=past, rc=rc)
    return pl.pallas_call(
        kern, grid=(batch, n_cache + 1),
        in_specs=[pl.BlockSpec((HEADS, seq, KV_LORA), lambda b, j: (0, b, 0)),
                  pl.BlockSpec((seq, HEADS * MLA_ROPE), lambda b, j: (b, 0)),
                  pl.BlockSpec((seq, 2 * KV_LORA), lambda b, j: (b, 0)),
                  pl.BlockSpec((1, tkc, KV_LORA), lambda b, j: (b, jnp.maximum(j - 1, 0), 0)),
                  pl.BlockSpec((1, MLA_ROPE, tkc), lambda b, j: (b, 0, jnp.maximum(j - 1, 0))),
                  _const_spec(wuvp.shape)],
        out_specs=pl.BlockSpec((seq, FOX_W), lambda b, j: (b, 0)),
        out_shape=jax.ShapeDtypeStruct((batch * seq, FOX_W), BF),
        scratch_shapes=[pltpu.VMEM((HEADS * seq, KV_LORA), BF), pltpu.VMEM((HEADS * seq, MLA_ROPE), BF)]
        + _mla_scratch(HEADS * seq),
        compiler_params=_params(2), name="mla_sample")(qlat, qrope, mkey, latc, krt, wuvp)


def _ffn_kernel(x_ref, fox_ref, mla_ref, pconv_ref, wo_ref, gf_ref, wup_ref, cw_ref, wdn_ref, gfin_ref,
                y_ref, cst_ref, hn_ref, act_ref, ush_ref, carry_ref, *, tm, seq_len):
    n_seg = max(1, tm // seq_len)
    seg = tm // n_seg
    mixed = jnp.concatenate([fox_ref[...], mla_ref[...]], axis=1)
    x1 = x_ref[...] + _dot(mixed, wo_ref[...])
    hn_ref[...] = (x1 * lax.rsqrt(jnp.mean(x1 * x1, axis=-1, keepdims=True) + EPS) * gf_ref[...]).astype(BF)
    y_ref[...] = x1

    if n_seg == 1:
        @pl.when(pl.program_id(0) % (seq_len // tm) == 0)
        def _():
            carry_ref[...] = pconv_ref[0]

    def conv(u, col, slot):
        cols = slice(col, col + FF_CHUNK)
        for s in range(n_seg):
            ush_ref[slot, s, 6:8, :] = carry_ref[:, cols] if n_seg == 1 else pconv_ref[s, :, cols]
            ush_ref[slot, s, 8:8 + seg, :] = u[seg * s:seg * (s + 1), :]
            cst_ref[s, :, cols] = u[seg * (s + 1) - 2:seg * (s + 1), :]
        if n_seg == 1:
            carry_ref[:, cols] = u[tm - 2:tm, :]
        u1 = jnp.concatenate([ush_ref[slot, s, 7:7 + seg, :] for s in range(n_seg)], axis=0)
        u2 = jnp.concatenate([ush_ref[slot, s, 6:6 + seg, :] for s in range(n_seg)], axis=0)
        cw = cw_ref[:, cols]
        return cw[3:4, :] + cw[0:1, :] * u2 + cw[1:2, :] * u1 + cw[2:3, :] * u

    def up(c):
        hn = hn_ref[...]
        g0, v0 = FF_CHUNK * c, D_FF + FF_CHUNK * c
        return _dot(hn, wup_ref[:, g0:g0 + FF_CHUNK]), _dot(hn, wup_ref[:, v0:v0 + FF_CHUNK])

    u_next = up(0)
    for c in range(N_FF):
        ug, uv = u_next
        if c + 1 < N_FF:
            u_next = up(c + 1)
        gate = conv(ug, FF_CHUNK * c, 2 * (c % 2))
        val = conv(uv, D_FF + FF_CHUNK * c, 2 * (c % 2) + 1)
        act_ref[:, FF_CHUNK * c:FF_CHUNK * (c + 1)] = (gate * jax.nn.sigmoid(gate) * val).astype(BF)

    x2 = y_ref[...] + _dot(act_ref[...], wdn_ref[...])
    y_ref[...] = x2 * lax.rsqrt(jnp.mean(x2 * x2, axis=-1, keepdims=True) + EPS) * gfin_ref[...]


def _ffn(x2d, fox, mla, pconv, wo, gf, wup, cw, wdn, gfin, *, seq_len, tm):
    n = x2d.shape[0]
    nt = n // tm
    n_seq = n // seq_len
    n_seg = max(1, tm // seq_len)
    if n_seg == 1:
        tps = seq_len // tm
        st_map = lambda i: (i // tps, 0, 0)
    else:
        st_map = lambda i: (i, 0, 0)
    row = lambda w: pl.BlockSpec((tm, w), lambda i: (i, 0))
    kern = functools.partial(_ffn_kernel, tm=tm, seq_len=seq_len)
    return pl.pallas_call(
        kern, grid=(nt,),
        in_specs=[row(D_MODEL), row(FOX_W), row(FOX_W),
                  pl.BlockSpec((n_seg, CONV_W - 1, 2 * D_FF), st_map),
                  _const_spec(wo.shape), _const_spec(gf.shape), _const_spec(wup.shape), _const_spec(cw.shape),
                  _const_spec(wdn.shape), _const_spec(gfin.shape)],
        out_specs=(row(D_MODEL), pl.BlockSpec((n_seg, CONV_W - 1, 2 * D_FF), st_map)),
        out_shape=(jax.ShapeDtypeStruct((n, D_MODEL), F32),
                   jax.ShapeDtypeStruct((n_seq, CONV_W - 1, 2 * D_FF), F32)),
        scratch_shapes=[pltpu.VMEM((tm, D_MODEL), BF), pltpu.VMEM((tm, D_FF), BF),
                        pltpu.VMEM((4, n_seg, 8 + tm // n_seg, FF_CHUNK), F32),
                        pltpu.VMEM((CONV_W - 1, 2 * D_FF), F32)],
        compiler_params=_params(1), name="ffn")(x2d, fox, mla, pconv, wo, gf, wup, cw, wdn, gfin)


def _rope_tables(start, n, reps=1):
    half = MLA_ROPE // 2
    inv = ROPE_THETA ** (-np.arange(half, dtype=np.float64) / half)
    ang = (start + np.arange(n, dtype=np.float64))[:, None] * inv[None, :]
    cos = np.tile(np.cos(ang), (1, 2 * HEADS))
    sin = np.tile(np.sin(ang), (1, 2 * HEADS))
    ones = np.ones((n, MLA_NOPE))
    zeros = np.zeros((n, LANES - MLA_NOPE - MLA_ROPE))
    cosm = np.concatenate([ones, cos[:, :MLA_ROPE], zeros], axis=1)
    sinm = np.concatenate([0.0 * ones, sin[:, :MLA_ROPE], zeros], axis=1)
    return [jnp.asarray(np.tile(a, (reps, 1)), F32) for a in (cos, sin, cosm, sinm)]


def _placement():
    place = np.zeros((LANES, 2 * N_PAIR * LANES), np.float32)
    ones = np.zeros((1, 2 * N_PAIR * LANES), np.float32)
    for h in range(HEADS):
        for i in range(3):
            for p in range(N_PAIR):
                place[HEADS * i + h, LANES * p + AUG * h + i] = 1.0
                ones[0, LANES * p + AUG * h + 3 + i] = 1.0
                ones[0, N_PAIR * LANES + LANES * p + AUG * h + i] = 1.0
                place[HEADS * i + h, N_PAIR * LANES + LANES * p + AUG * h + 3 + i] = -1.0
    rep = np.zeros((LANES, HEADS * MLA_ROPE), np.float32)
    for h in range(HEADS):
        for r in range(MLA_ROPE):
            rep[r, MLA_ROPE * h + r] = 1.0
    return jnp.asarray(place, BF), jnp.asarray(ones, F32), jnp.asarray(rep, BF)


def _rot_cols(w):
    half = MLA_ROPE // 2
    return jnp.concatenate([-w[..., half:], w[..., :half]], axis=-1)


def _pad_cols(w, width):
    return jnp.pad(w, ((0, 0), (0, width - w.shape[1])))


def _layer_weights(w_in, b_f, w_q_up, w_uk, w_uv, w_out, w_up, conv_w, conv_b, w_down):
    wq, wk, wv, wf, wqc, wckv, wkr = jnp.split(w_in, IN_SPLITS, axis=1)
    wa = jnp.concatenate([wq, wk, wv, wqc, wckv, _pad_cols(wkr, LANES), _pad_cols(_rot_cols(wkr), LANES),
                          _pad_cols(wf, LANES)], axis=1).astype(BF)
    bfp = _pad_cols(b_f[None, :], LANES).astype(F32)
    wq3 = w_q_up.reshape(Q_LORA, HEADS, MLA_NOPE + MLA_ROPE)
    wq_nope = wq3[:, :, :MLA_NOPE].reshape(Q_LORA, HEADS * MLA_NOPE)
    wq_rope = wq3[:, :, MLA_NOPE:]
    wq2 = jnp.concatenate([wq_nope, wq_rope.reshape(Q_LORA, -1), _rot_cols(wq_rope).reshape(Q_LORA, -1)],
                          axis=1).astype(BF)
    eye = jnp.eye(4, dtype=F32)
    wk = jnp.transpose(w_uk, (1, 2, 0)).reshape(2, 4, 1, MLA_NOPE, KV_LORA)
    wukp = (wk * eye[None, :, :, None, None]).reshape(HEADS, 4 * MLA_NOPE, KV_LORA)
    wv = jnp.transpose(w_uv, (1, 0, 2)).reshape(2, 4, KV_LORA, 1, MLA_VDIM)
    wuvp = (wv * eye[None, :, None, :, None]).reshape(HEADS, KV_LORA, 4 * MLA_VDIM)
    wup = w_up.astype(BF)
    cw = jnp.concatenate([conv_w, conv_b[None, :]], axis=0)
    wdn = w_down.astype(BF)
    zpad = ((0, 0), (0, 0), (0, LANES - MLA_NOPE - MLA_ROPE))
    wq_a = jnp.pad(wq3, zpad).reshape(Q_LORA, HEADS * LANES)
    wq_b = jnp.pad(jnp.concatenate([jnp.zeros_like(wq3[:, :, :MLA_NOPE]), _rot_cols(wq_rope)], axis=2), zpad)
    wqm = jnp.concatenate([wq_a, wq_b.reshape(Q_LORA, HEADS * LANES)], axis=1).astype(BF)
    wkm = jnp.pad(w_uk, ((0, 0), (0, 0), (0, LANES - MLA_NOPE))).reshape(KV_LORA, HEADS * LANES).astype(BF)
    wvm = w_uv.reshape(KV_LORA, HEADS * MLA_VDIM).astype(BF)
    return wa, bfp, wq2, wukp.astype(BF), wuvp.astype(BF), w_out.astype(BF), wup, cw, wdn, wqm, wkm, wvm


def kernel(x_prompt, x_sample, cache_fox_k, cache_fox_v, cache_fox_logf, cache_mla_latent, cache_mla_krope,
           state_ffn_conv, attn_norm, w_in, b_forget, q_norm, w_q_up, kv_norm, w_uk, w_uv, w_out, ffn_norm,
           w_up, conv_w, conv_b, w_down, final_norm):
    assert attn_norm.shape[0] == 1, "single-layer stack"
    bp, tp, _ = x_prompt.shape
    bs, ts, _ = x_sample.shape
    past = cache_fox_k.shape[2]
    assert ts == CHUNK and past % CHUNK == 0

    wa, bfp, wq2, wukp, wuvp, wo, wup, cw, wdn, wqm, wkm, wvm = _layer_weights(
        w_in[0], b_forget[0], w_q_up[0], w_uk[0], w_uv[0], w_out[0], w_up[0], conv_w[0], conv_b[0], w_down[0])
    place, ones, rep = _placement()
    g_attn = attn_norm[0][None, :]
    g_q = q_norm[0][None, :]
    g_kv = kv_norm[0][None, :]
    g_ffn = ffn_norm[0][None, :]
    g_fin = final_norm[None, :]

    tm = 512
    outs = []
    for grp in ("prompt", "sample"):
        if grp == "prompt":
            x2d = x_prompt.reshape(bp * tp, D_MODEL)
            b, t = bp, tp
            tabs = _rope_tables(0, tp)
            pconv = jnp.zeros((bp, CONV_W - 1, 2 * D_FF), F32)
        else:
            x2d = x_sample.reshape(bs * ts, D_MODEL)
            b, t = bs, ts
            tabs = _rope_tables(past, ts, reps=tm // ts)
            pconv = state_ffn_conv[0]

        absorbed = grp == "sample"
        wm = (wq2, wukp, rep) if absorbed else (wqm, wkm, wvm)
        (k, v, logf, lat, kr, qx, kx, vb, m1, m2, m3, fqs) = _proj(
            x2d, *tabs, g_attn, wa, bfp, g_q, g_kv, *wm, place, ones, seq_len=t, tm=tm, absorbed=absorbed)

        if grp == "prompt":
            fox = _fox_prompt(qx, kx, vb, batch=b, seq=t, t=512, kind="fox")
            mla = _fox_prompt(m1, m2, m3, batch=b, seq=t, t=512, kind="mla")
        else:
            mkey, qlat, qrope = m1, m2, m3
            gk = _fox_prep(jnp.transpose(cache_fox_logf[0], (0, 2, 1)))
            kct = jnp.transpose(cache_fox_k[0], (0, 2, 3, 1))
            vct = jnp.transpose(cache_fox_v[0], (0, 2, 3, 1))
            krt = jnp.transpose(cache_mla_krope[0], (0, 2, 1))
            fox = _fox_sample(qx, kx, vb, fqs, kct, vct, gk, batch=b, seq=t, tkc=min(past, 4096))
            mla = _mla_sample(qlat, qrope, mkey, cache_mla_latent[0], krt, wuvp, batch=b, seq=t,
                              tkc=min(past, 2048), rc=256)

        def small_state(a, w, b=b, t=t):
            return (jnp.transpose(a, (0, 2, 1)) if a.ndim == 3 else a.reshape(b, t, w))[None]

        y, cst = _ffn(x2d, fox, mla, pconv, wo, g_ffn, wup, cw, wdn, g_fin, seq_len=t, tm=tm)
        outs.append((y.reshape(b, t, D_MODEL),
                     k.reshape(1, b, t, HEADS, FOX_DIM), v.reshape(1, b, t, HEADS, FOX_DIM),
                     small_state(logf, HEADS), lat.reshape(1, b, t, KV_LORA), small_state(kr, MLA_ROPE),
                     cst[None]))
    (yp, *st_p), (ys, *st_s) = outs
    return (yp, ys, *st_p, *st_s)
```

```python
import functools

import numpy as np
import jax
import jax.numpy as jnp
from jax import lax
from jax.experimental import pallas as pl
from jax.experimental.pallas import tpu as pltpu

D_MODEL = 1024
CHUNK = 64
HEADS = 8
FOX_DIM = 64
MLA_NOPE = 64
MLA_ROPE = 32
MLA_VDIM = 64
Q_LORA = 384
KV_LORA = 256
D_FF = 2816
CONV_W = 3
ROPE_THETA = 10000.0
EPS = 1e-6
NEG = -1e30
LOG2E = 1.4426950408889634

FOX_W = HEADS * FOX_DIM
IN_SIZES = [FOX_W, FOX_W, FOX_W, HEADS, Q_LORA, KV_LORA, MLA_ROPE]
IN_SPLITS = [int(s) for s in np.cumsum(IN_SIZES)[:-1]]

LANES = 128
AUG = 16
N_PAIR = HEADS // 2
FF_CHUNK = 256
N_FF = D_FF // FF_CHUNK
VMEM_LIMIT = 56 * 1024 * 1024

C_Q, C_K, C_V = 0, 512, 1024
C_QC = 1536
C_CKV = 1920
C_KR = 2176
C_KRR = 2304
C_F = 2432
N_A = 2560

BF = jnp.bfloat16
F32 = jnp.float32


def _dot(a, b):
    return jnp.dot(a, b, preferred_element_type=F32)


def _dot_nt(a, b):
    return lax.dot_general(a, b, (((1,), (1,)), ((), ())), preferred_element_type=F32)


def _split3(x):
    hi = x.astype(BF)
    r1 = x - hi.astype(F32)
    mid = r1.astype(BF)
    lo = (r1 - mid.astype(F32)).astype(BF)
    return hi, mid, lo


def _log_sigmoid(x):
    return jnp.minimum(x, 0.0) - jnp.log1p(jnp.exp(-jnp.abs(x)))


def _lanes(x, n):
    if n % LANES == 0:
        return x if n == LANES else jnp.concatenate([x] * (n // LANES), axis=1)
    assert n < LANES
    return x[:, :n]


def _params(n_axes):
    return pltpu.CompilerParams(dimension_semantics=("arbitrary",) * n_axes, vmem_limit_bytes=VMEM_LIMIT)


def _const_spec(shape):
    nd = len(shape)
    return pl.BlockSpec(shape, lambda *_: (0,) * nd, pipeline_mode=pl.Buffered(1))


def _proj_kernel(x_ref, cos_ref, sin_ref, cosm_ref, sinm_ref, g_ref, wa_ref, bf_ref, gq_ref, gkv_ref,
                 w1_ref, w2_ref, w3_ref, place_ref, ones_ref,
                 k_ref, v_ref, logf_ref, lat_ref, kr_ref, qx_ref, kx_ref, vb_ref, o1_ref, o2_ref, o3_ref,
                 fqs_ref, carry_ref, *, tm, seq_len, mla_scale, absorbed):
    x = x_ref[...]
    ms = jnp.mean(x * x, axis=-1, keepdims=True)
    hn = (x * lax.rsqrt(ms + EPS) * g_ref[...]).astype(BF)
    zs = _dot(hn, wa_ref[:, C_QC:])
    z = _dot(hn, wa_ref[:, :C_QC])

    lane = lax.broadcasted_iota(jnp.int32, (tm, LANES), 1)
    logf = _log_sigmoid(zs[:, C_F - C_QC:C_F - C_QC + LANES] + bf_ref[...])
    logf = jnp.where(lane < HEADS, logf, 0.0)
    logf_t = logf.T[:HEADS, :]
    if seq_len >= tm:
        logf_ref[0] = logf_t
    else:
        logf_ref[...] = logf[:, :HEADS]
    seg = min(seq_len, tm)
    assert seg & (seg - 1) == 0
    pos = lax.broadcasted_iota(jnp.int32, (HEADS, tm), 1) & (seg - 1)
    fc = logf_t
    shift = 1
    while shift < seg:
        fc = fc + jnp.where(pos >= shift, pltpu.roll(fc, shift, 1), 0.0)
        shift *= 2
    if seq_len > tm:
        @pl.when(pl.program_id(0) % (seq_len // tm) == 0)
        def _():
            carry_ref[...] = jnp.zeros_like(carry_ref)
        fc = fc + carry_ref[:, 0:1]
        carry_ref[...] = jnp.broadcast_to(fc[:, tm - 1:tm], carry_ref.shape)
    fcum = jnp.concatenate([fc, jnp.zeros((LANES - HEADS, tm), F32)], axis=0).T

    zc = zs[:, C_CKV - C_QC:C_CKV - C_QC + KV_LORA]
    ckv = zc * lax.rsqrt(jnp.mean(zc * zc, axis=-1, keepdims=True) + EPS) * gkv_ref[...]
    lat_ref[...] = ckv
    cos = cos_ref[...]
    sin = sin_ref[...]
    krb = (zs[:, C_KR - C_QC:C_KR - C_QC + LANES] * cos[:, :LANES]
           + zs[:, C_KRR - C_QC:C_KRR - C_QC + LANES] * sin[:, :LANES])
    if seq_len >= tm:
        kr_ref[0] = krb.T[:MLA_ROPE, :]
    else:
        kr_ref[...] = krb[:, :MLA_ROPE]
    cb = ckv.astype(BF)

    zqc = zs[:, :Q_LORA]
    qcn = (zqc * lax.rsqrt(jnp.mean(zqc * zqc, axis=-1, keepdims=True) + EPS) * gq_ref[...]).astype(BF)
    zq2 = _dot(qcn, w1_ref[...])
    if absorbed:
        mkey_ref, qlat_ref, qrope_ref = o1_ref, o2_ref, o3_ref
        mkey_ref[:, :KV_LORA] = cb
        mkey_ref[:, KV_LORA:] = _dot(krb.astype(BF), w3_ref[...]).astype(BF)
        nr = HEADS * MLA_ROPE
        qr = zq2[:, FOX_W:FOX_W + nr] * cos + zq2[:, FOX_W + nr:FOX_W + 2 * nr] * sin
        qrope_ref[...] = (qr * mla_scale).astype(BF)
        for h in range(HEADS):
            g = h // 4
            qn = zq2[:, 256 * g:256 * (g + 1)].astype(BF)
            qlat_ref[h] = (_dot(qn, w2_ref[h]) * mla_scale).astype(BF)
    else:
        qm_ref, km_ref, vm_ref = o1_ref, o2_ref, o3_ref
        hw = HEADS * LANES
        cosm = jnp.concatenate([cosm_ref[...]] * HEADS, axis=1)
        sinm = jnp.concatenate([sinm_ref[...]] * HEADS, axis=1)
        qm_ref[...] = ((zq2[:, :hw] * cosm + zq2[:, hw:] * sinm) * mla_scale).astype(BF)
        krs = pltpu.roll(krb, MLA_NOPE, 1)
        km_ref[...] = (_dot(cb, w2_ref[...]) + jnp.concatenate([krs] * HEADS, axis=1)).astype(BF)
        vm_ref[...] = _dot(cb, w3_ref[...]).astype(BF)

    zk = z[:, C_K:C_K + FOX_W]
    zv = z[:, C_V:C_V + FOX_W]
    for h in range(HEADS):
        k_ref[pl.ds(h, tm, stride=HEADS), :] = zk[:, FOX_DIM * h:FOX_DIM * (h + 1)]
        v_ref[pl.ds(h, tm, stride=HEADS), :] = zv[:, FOX_DIM * h:FOX_DIM * (h + 1)]
    vb_ref[...] = zv.astype(BF)

    fqs = fcum * LOG2E
    fqs_ref[...] = fqs
    hi, mid, lo = _split3(fqs)
    comb = hi.astype(F32) + pltpu.roll(mid.astype(F32), HEADS, 1) + pltpu.roll(lo.astype(F32), 2 * HEADS, 1)
    aug = _dot(comb.astype(BF), place_ref[...]) + ones_ref[...]
    zq = z[:, C_Q:C_Q + FOX_W] * (FOX_DIM ** -0.5 * LOG2E)
    for p in range(N_PAIR):
        lo_, hi_ = 2 * LANES * p, 2 * LANES * p + LANES
        qx_ref[:, lo_:hi_] = zq[:, LANES * p:LANES * (p + 1)].astype(BF)
        qx_ref[:, hi_:hi_ + LANES] = aug[:, LANES * p:LANES * (p + 1)].astype(BF)
        kx_ref[:, lo_:hi_] = zk[:, LANES * p:LANES * (p + 1)].astype(BF)
        kx_ref[:, hi_:hi_ + LANES] = aug[:, FOX_W + LANES * p:FOX_W + LANES * (p + 1)].astype(BF)


def _proj(x2d, cos, sin, cosm, sinm, g, wa, bfp, gq, gkv, w1, w2, w3, place, ones, *, seq_len, tm, absorbed):
    n = x2d.shape[0]
    nt = n // tm
    if seq_len >= tm:
        tps = seq_len // tm
        tab_map = lambda i: (i % tps, 0)
    else:
        tab_map = lambda i: (0, 0)
    row = lambda w: pl.BlockSpec((tm, w), lambda i: (i, 0))
    if seq_len >= tm:
        small = lambda w: jax.ShapeDtypeStruct((n // seq_len, w, seq_len), F32)
        small_spec = lambda w: pl.BlockSpec((1, w, tm), lambda i: (i // tps, 0, i % tps))
    else:
        small = lambda w: jax.ShapeDtypeStruct((n, w), F32)
        small_spec = row
    if absorbed:
        mla_shapes = (jax.ShapeDtypeStruct((n, 2 * KV_LORA), BF), jax.ShapeDtypeStruct((HEADS, n, KV_LORA), BF),
                      jax.ShapeDtypeStruct((n, HEADS * MLA_ROPE), BF))
        mla_specs = (row(2 * KV_LORA), pl.BlockSpec((HEADS, tm, KV_LORA), lambda i: (0, i, 0)),
                     row(HEADS * MLA_ROPE))
    else:
        mla_shapes = (jax.ShapeDtypeStruct((n, HEADS * LANES), BF), jax.ShapeDtypeStruct((n, HEADS * LANES), BF),
                      jax.ShapeDtypeStruct((n, HEADS * MLA_VDIM), BF))
        mla_specs = (row(HEADS * LANES), row(HEADS * LANES), row(HEADS * MLA_VDIM))
    out_shape = (
        jax.ShapeDtypeStruct((n * HEADS, FOX_DIM), F32),
        jax.ShapeDtypeStruct((n * HEADS, FOX_DIM), F32),
        small(HEADS),
        jax.ShapeDtypeStruct((n, KV_LORA), F32),
        small(MLA_ROPE),
        jax.ShapeDtypeStruct((n, 2 * FOX_W), BF),
        jax.ShapeDtypeStruct((n, 2 * FOX_W), BF),
        jax.ShapeDtypeStruct((n, FOX_W), BF),
        *mla_shapes,
        jax.ShapeDtypeStruct((n, LANES), F32),
    )
    kv_spec = pl.BlockSpec((tm * HEADS, FOX_DIM), lambda i: (i, 0))
    out_specs = (kv_spec, kv_spec, small_spec(HEADS), row(KV_LORA), small_spec(MLA_ROPE), row(2 * FOX_W),
                 row(2 * FOX_W), row(FOX_W), *mla_specs, row(LANES))
    in_specs = [row(D_MODEL),
                pl.BlockSpec((tm, 2 * LANES), tab_map), pl.BlockSpec((tm, 2 * LANES), tab_map),
                pl.BlockSpec((tm, LANES), tab_map), pl.BlockSpec((tm, LANES), tab_map),
                _const_spec(g.shape), _const_spec(wa.shape), _const_spec(bfp.shape), _const_spec(gq.shape),
                _const_spec(gkv.shape), _const_spec(w1.shape), _const_spec(w2.shape), _const_spec(w3.shape),
                _const_spec(place.shape), _const_spec(ones.shape)]
    kern = functools.partial(_proj_kernel, tm=tm, seq_len=seq_len, absorbed=absorbed,
                             mla_scale=float((MLA_NOPE + MLA_ROPE) ** -0.5 * LOG2E))
    return pl.pallas_call(
        kern, grid=(nt,), in_specs=in_specs, out_specs=out_specs, out_shape=out_shape,
        scratch_shapes=[pltpu.VMEM((8, LANES), F32)],
        compiler_params=_params(1), name="proj")(
            x2d, cos, sin, cosm, sinm, g, wa, bfp, gq, gkv, w1, w2, w3, place, ones)


def _fox_prep_kernel(plt_ref, g_ref):
    nb, h, n = plt_ref.shape
    x = plt_ref[...].reshape(nb * h, n)
    lane = lax.broadcasted_iota(jnp.int32, x.shape, 1)
    shift = 1
    while shift < n:
        x = x + jnp.where(lane >= shift, pltpu.roll(x, shift, 1), 0.0)
        shift *= 2
    g_ref[...] = ((x - x[:, n - 1:n]) * LOG2E).reshape(nb, h, n)


def _fox_prep(plt, nb=4):
    b, h, p = plt.shape
    nb = nb if b % nb == 0 else 1
    return pl.pallas_call(
        _fox_prep_kernel, grid=(b // nb,),
        in_specs=[pl.BlockSpec((nb, h, p), lambda i: (i, 0, 0))],
        out_specs=pl.BlockSpec((nb, h, p), lambda i: (i, 0, 0)),
        out_shape=jax.ShapeDtypeStruct((b, h, p), F32),
        compiler_params=_params(1), name="fox_prep")(plt)


def _softmax_step(s, m_prev):
    m_new = jnp.maximum(m_prev, jnp.max(s, axis=-1, keepdims=True))
    p = jnp.exp2(s - _lanes(m_new, s.shape[1]))
    alpha = jnp.exp2(m_prev - m_new)
    return p.astype(BF), alpha, m_new


def _fox_qpad(qx, h, kind="fox"):
    p, hh = divmod(h, 2)
    lane = lax.broadcasted_iota(jnp.int32, (qx.shape[0], 2 * LANES), 1)
    blk = qx[:, 2 * LANES * p:2 * LANES * (p + 1)]
    if kind == "fox":
        keep = ((lane >= FOX_DIM * hh) & (lane < FOX_DIM * (hh + 1))) | (
            (lane >= LANES + AUG * h) & (lane < LANES + AUG * (h + 1)))
    else:
        keep = (lane >= LANES * hh) & (lane < LANES * (hh + 1))
    return jnp.where(keep, blk, jnp.zeros_like(blk))


def _fox_rows_step(s, vpa, m_ref, l_ref, h, r0, r1):
    pb, alpha, m_new = _softmax_step(s, m_ref[h, r0:r1])
    pv = _dot(pb, vpa)
    l_ref[h, r0:r1] = alpha * l_ref[h, r0:r1] + pv[:, LANES:]
    m_ref[h, r0:r1] = m_new
    return alpha, pv[:, :LANES]


def _fox_pair_acc(acc_ref, p, r0, r1, a, b):
    first = lax.broadcasted_iota(jnp.int32, (r1 - r0, LANES), 1) < FOX_DIM
    acc_ref[p, r0:r1] = acc_ref[p, r0:r1] * jnp.where(first, a[0], b[0]) + jnp.where(first, a[1], b[1])


def _fox_prompt_tile(kx, vb, qpad_ref, m_ref, l_ref, acc_ref):
    tq = qpad_ref.shape[1]
    ones = jnp.ones((kx.shape[0], LANES), BF)

    def scores(h):
        return _dot_nt(qpad_ref[h], kx[:, 2 * LANES * (h // 2):2 * LANES * (h // 2 + 1)])

    s_next = scores(0)
    for p in range(N_PAIR):
        vpa = jnp.concatenate([vb[:, LANES * p:LANES * (p + 1)], ones], axis=1)
        res = []
        for hh in range(2):
            h = 2 * p + hh
            s = s_next
            if h + 1 < HEADS:
                s_next = scores(h + 1)
            res.append(_fox_rows_step(s, vpa, m_ref, l_ref, h, 0, tq))
        _fox_pair_acc(acc_ref, p, 0, tq, *res)


def _fox_diag_tile(kx, vb, qpad_ref, m_ref, l_ref, acc_ref, kind):
    tq = qpad_ref.shape[1]
    half = tq // 2
    ones = jnp.ones((tq, LANES), BF)
    col = lax.broadcasted_iota(jnp.int32, (half, half), 1)
    row = lax.broadcasted_iota(jnp.int32, (half, half), 0)
    if kind == "mla":
        shift = CHUNK.bit_length() - 1
        col, row = lax.shift_right_logical(col, shift), lax.shift_right_logical(row, shift)
    tri = col <= row

    def scores(h):
        kxp = kx[:, 2 * LANES * (h // 2):2 * LANES * (h // 2 + 1)]
        q = qpad_ref[h]
        return _dot_nt(q, kxp[:half]), _dot_nt(q[half:], kxp[half:])

    s_next = scores(0)
    for p in range(N_PAIR):
        vpa = jnp.concatenate([vb[:, LANES * p:LANES * (p + 1)], ones], axis=1)
        top, bot = [], []
        for hh in range(2):
            h = 2 * p + hh
            s_a, s_b = s_next
            if h + 1 < HEADS:
                s_next = scores(h + 1)
            s_top = jnp.where(tri, s_a[:half], NEG)
            s_bot = jnp.concatenate([s_a[half:], jnp.where(tri, s_b, NEG)], axis=1)
            top.append(_fox_rows_step(s_top, vpa[:half], m_ref, l_ref, h, 0, half))
            bot.append(_fox_rows_step(s_bot, vpa, m_ref, l_ref, h, half, tq))
        _fox_pair_acc(acc_ref, p, 0, half, *top)
        _fox_pair_acc(acc_ref, p, half, tq, *bot)


def _fox_prompt_kernel(qi_ref, kj_ref, qx_ref, kx_ref, vb_ref, o_ref, qpad_ref, m_ref, l_ref, acc_ref, *, kind):
    i = qi_ref[pl.program_id(1)]
    j = kj_ref[pl.program_id(1)]

    @pl.when(j == 0)
    def _():
        qx = qx_ref[...]
        for h in range(HEADS):
            qpad_ref[h] = _fox_qpad(qx, h, kind)
        m_ref[...] = jnp.full(m_ref.shape, NEG, F32)
        l_ref[...] = jnp.zeros(l_ref.shape, F32)
        acc_ref[...] = jnp.zeros(acc_ref.shape, F32)

    @pl.when(j < i)
    def _():
        _fox_prompt_tile(kx_ref[...], vb_ref[...], qpad_ref, m_ref, l_ref, acc_ref)

    @pl.when(j == i)
    def _():
        _fox_diag_tile(kx_ref[...], vb_ref[...], qpad_ref, m_ref, l_ref, acc_ref, kind)
        first = lax.broadcasted_iota(jnp.int32, (qpad_ref.shape[1], LANES), 1) < FOX_DIM
        for p in range(N_PAIR):
            l_sel = jnp.where(first, l_ref[2 * p], l_ref[2 * p + 1])
            o_ref[:, LANES * p:LANES * (p + 1)] = (acc_ref[p] / l_sel).astype(o_ref.dtype)


def _fox_prompt(qx, kx, vb, *, batch, seq, t, kind):
    nq = seq // t
    assert CHUNK & (CHUNK - 1) == 0 and t % CHUNK == 0
    steps = [(i, j) for i in range(nq) for j in range(i + 1)]
    qi = jnp.asarray([s[0] for s in steps], jnp.int32)
    kj = jnp.asarray([s[1] for s in steps], jnp.int32)
    grid_spec = pltpu.PrefetchScalarGridSpec(
        num_scalar_prefetch=2, grid=(batch, len(steps)),
        in_specs=[pl.BlockSpec((t, 2 * FOX_W), lambda b, s, qi, kj: (b * nq + qi[s], 0)),
                  pl.BlockSpec((t, 2 * FOX_W), lambda b, s, qi, kj: (b * nq + kj[s], 0)),
                  pl.BlockSpec((t, FOX_W), lambda b, s, qi, kj: (b * nq + kj[s], 0))],
        out_specs=pl.BlockSpec((t, FOX_W), lambda b, s, qi, kj: (b * nq + qi[s], 0)),
        scratch_shapes=[pltpu.VMEM((HEADS, t, 2 * LANES), BF), pltpu.VMEM((HEADS, t, LANES), F32),
                        pltpu.VMEM((HEADS, t, LANES), F32), pltpu.VMEM((N_PAIR, t, LANES), F32)])
    return pl.pallas_call(
        functools.partial(_fox_prompt_kernel, kind=kind), grid_spec=grid_spec,
        out_shape=jax.ShapeDtypeStruct((batch * seq, FOX_W), BF),
        compiler_params=_params(2), name=kind + "_prompt")(qi, kj, qx, kx, vb)


def _fox_sample_kernel(qx_ref, kx_ref, vb_ref, fqs_ref, kct_ref, vct_ref, gk_ref, o_ref,
                       qbd_ref, qpad_ref, fq_ref, m_ref, l_ref, acc_ref, *, n_cache):
    j = pl.program_id(1)
    tq = qx_ref.shape[0]
    rows = HEADS * tq

    def update(s, pv_fn, g0=0, g1=2):
        a, b = 4 * tq * g0, 4 * tq * g1
        m_prev = m_ref[a:b]
        m_new = jnp.maximum(m_prev, jnp.max(s, axis=-1, keepdims=True))
        p = jnp.exp2(s - _lanes(m_new, s.shape[1]))
        alpha = jnp.exp2(m_prev - m_new)
        l_ref[a:b] = alpha * l_ref[a:b] + jnp.sum(p, axis=-1, keepdims=True)
        m_ref[a:b] = m_new
        pb = p.astype(BF)
        for pr in range(2 * g0, 2 * g1):
            r0, r1 = 2 * tq * pr, 2 * tq * (pr + 1)
            acc_ref[r0:r1] = acc_ref[r0:r1] * alpha[r0 - a:r1 - a] + pv_fn(pr, pb[r0 - a:r1 - a])

    @pl.when(j == 0)
    def _():
        qx = qx_ref[...]
        fqs = fqs_ref[...]
        qc = jnp.concatenate([qx[:, 2 * LANES * p:2 * LANES * p + LANES] for p in range(N_PAIR)], axis=1)
        lane = lax.broadcasted_iota(jnp.int32, qc.shape, 1)
        for h in range(HEADS):
            qbd_ref[h * tq:(h + 1) * tq, :] = jnp.where((lane // FOX_DIM) == h, qc, jnp.zeros_like(qc))
            qpad_ref[h] = _fox_qpad(qx, h)
            fq_ref[h * tq:(h + 1) * tq, :] = jnp.broadcast_to(fqs[:, h:h + 1], (tq, LANES))
        m_ref[...] = jnp.full(m_ref.shape, NEG, F32)
        l_ref[...] = jnp.zeros(l_ref.shape, F32)
        acc_ref[...] = jnp.zeros(acc_ref.shape, F32)
        kx = kx_ref[...]
        vb = vb_ref[...]
        tk = kx.shape[0]
        s = jnp.concatenate([_dot_nt(qpad_ref[h], kx[:, 2 * LANES * (h // 2):2 * LANES * (h // 2 + 1)])
                             for h in range(HEADS)], axis=0)
        row = lax.broadcasted_iota(jnp.int32, (rows, tk), 0) & (tq - 1)
        s = jnp.where(lax.broadcasted_iota(jnp.int32, (rows, tk), 1) <= row, s, NEG)
        update(s, lambda pr, pb: _dot(pb, vb[:, LANES * pr:LANES * (pr + 1)]))

    def cache_tile():
        tk = kct_ref.shape[3]
        gk = gk_ref[0]

        def scores(g):
            kt = kct_ref[0, 4 * g:4 * g + 4].reshape(4 * FOX_DIM, tk).astype(BF)
            gk_rows = jnp.concatenate([jnp.broadcast_to(gk[h:h + 1, :], (tq, tk)) for h in range(4 * g, 4 * g + 4)],
                                      axis=0)
            qg = qbd_ref[4 * tq * g:4 * tq * (g + 1), 4 * FOX_DIM * g:4 * FOX_DIM * (g + 1)]
            return _dot(qg, kt) + (_lanes(fq_ref[4 * tq * g:4 * tq * (g + 1)], tk) - gk_rows)

        def pv(pr, pb):
            return _dot_nt(pb, vct_ref[0, 2 * pr:2 * pr + 2].reshape(2 * FOX_DIM, tk).astype(BF))

        s0 = scores(0)
        s1 = scores(1)
        update(s0, pv, 0, 1)
        update(s1, pv, 1, 2)

    cache_tile()

    @pl.when(j == n_cache - 1)
    def _():
        first = lax.broadcasted_iota(jnp.int32, (tq, LANES), 1) < FOX_DIM
        o = acc_ref[...] / l_ref[...]
        for pr in range(N_PAIR):
            o_ref[:, LANES * pr:LANES * (pr + 1)] = jnp.where(
                first, o[2 * tq * pr:2 * tq * pr + tq], o[2 * tq * pr + tq:2 * tq * (pr + 1)]).astype(o_ref.dtype)


def _fox_sample(qx, kx, vb, fqs, kct, vct, gk, *, batch, seq, tkc):
    past = kct.shape[3]
    n_cache = past // tkc
    kern = functools.partial(_fox_sample_kernel, n_cache=n_cache)
    row = lambda w: pl.BlockSpec((seq, w), lambda b, j: (b, 0))
    return pl.pallas_call(
        kern, grid=(batch, n_cache),
        in_specs=[row(2 * FOX_W), row(2 * FOX_W), row(FOX_W), row(LANES),
                  pl.BlockSpec((1, HEADS, FOX_DIM, tkc), lambda b, j: (b, 0, 0, j)),
                  pl.BlockSpec((1, HEADS, FOX_DIM, tkc), lambda b, j: (b, 0, 0, j)),
                  pl.BlockSpec((1, HEADS, tkc), lambda b, j: (b, 0, j))],
        out_specs=row(FOX_W),
        out_shape=jax.ShapeDtypeStruct((batch * seq, FOX_W), BF),
        scratch_shapes=[pltpu.VMEM((HEADS * seq, FOX_W), BF), pltpu.VMEM((HEADS, seq, 2 * LANES), BF)]
        + [pltpu.VMEM((HEADS * seq, LANES), F32)] * 4,
        compiler_params=_params(2), name="fox_sample")(qx, kx, vb, fqs, kct, vct, gk)


def _chunk_mask(shape, tq, row0, q0, k0):
    assert tq & (tq - 1) == 0 and CHUNK & (CHUNK - 1) == 0
    shift = CHUNK.bit_length() - 1
    qpos = q0 + ((row0 + lax.broadcasted_iota(jnp.int32, shape, 0)) & (tq - 1))
    kpos = k0 + lax.broadcasted_iota(jnp.int32, shape, 1)
    return lax.shift_right_logical(kpos, shift) <= lax.shift_right_logical(qpos, shift)


def _mla_rows_update(s, val, m_ref, l_ref, acc_ref, r0, r1):
    m_prev = m_ref[r0:r1]
    m_new = jnp.maximum(m_prev, jnp.max(s, axis=-1, keepdims=True))
    p = jnp.exp2(s - _lanes(m_new, s.shape[1]))
    alpha = jnp.exp2(m_prev - m_new)
    l_ref[r0:r1] = alpha * l_ref[r0:r1] + jnp.sum(p, axis=-1, keepdims=True)
    m_ref[r0:r1] = m_new
    acc_ref[r0:r1] = acc_ref[r0:r1] * _lanes(alpha, KV_LORA) + _dot(p.astype(BF), val)


def _mla_init_stats(m_ref, l_ref, acc_ref):
    m_ref[...] = jnp.full(m_ref.shape, NEG, F32)
    l_ref[...] = jnp.zeros(l_ref.shape, F32)
    acc_ref[...] = jnp.zeros(acc_ref.shape, F32)


def _mla_finalize(o_ref, wuv_ref, l_ref, acc_ref, tq):
    olat = (acc_ref[...] / _lanes(l_ref[...], KV_LORA)).astype(BF)
    for g in range(2):
        out = None
        for h in range(4 * g, 4 * g + 4):
            part = _dot(olat[h * tq:(h + 1) * tq, :], wuv_ref[h])
            out = part if out is None else out + part
        o_ref[:, 256 * g:256 * (g + 1)] = out.astype(o_ref.dtype)


def _mla_scratch(rows):
    return [pltpu.VMEM((rows, LANES), F32), pltpu.VMEM((rows, LANES), F32), pltpu.VMEM((rows, KV_LORA), F32)]


def _mla_sample_kernel(qlat_ref, qrope_ref, mkey_ref, latc_ref, krt_ref, wuv_ref, o_ref,
                       ql_ref, qr_ref, m_ref, l_ref, acc_ref, *, n_cache, tq, past, rc):
    j = pl.program_id(1)
    rows = HEADS * tq

    @pl.when(j == 0)
    def _():
        qrope = qrope_ref[...]
        for h in range(HEADS):
            ql_ref[h * tq:(h + 1) * tq, :] = qlat_ref[h]
            qr_ref[h * tq:(h + 1) * tq, :] = qrope[:, MLA_ROPE * h:MLA_ROPE * (h + 1)]
        _mla_init_stats(m_ref, l_ref, acc_ref)
        mk = mkey_ref[...]
        s = _dot_nt(ql_ref[...], mk[:, :KV_LORA]) + _dot_nt(qr_ref[...], mk[:, KV_LORA:KV_LORA + MLA_ROPE])
        if past % CHUNK != 0 or tq > CHUNK:
            s = jnp.where(_chunk_mask(s.shape, tq, 0, past, past), s, NEG)
        _mla_rows_update(s, mk[:, :KV_LORA], m_ref, l_ref, acc_ref, 0, rows)

    def cache_tile():
        c = latc_ref[0].astype(BF)
        krt = krt_ref[0].astype(BF)
        def scores(r0):
            return _dot_nt(ql_ref[r0:r0 + rc], c) + _dot(qr_ref[r0:r0 + rc], krt)

        s_next = scores(0)
        for r0 in range(0, rows, rc):
            s = s_next
            if r0 + rc < rows:
                s_next = scores(r0 + rc)
            _mla_rows_update(s, c, m_ref, l_ref, acc_ref, r0, r0 + rc)

    cache_tile()

    @pl.when(j == n_cache - 1)
    def _():
        _mla_finalize(o_ref, wuv_ref, l_ref, acc_ref, tq)


def _mla_sample(qlat, qrope, mkey, latc, krt, wuvp, *, batch, seq, tkc, rc):
    past = latc.shape[1]
    n_cache = past // tkc
    kern = functools.partial(_mla_sample_kernel, n_cache=n_cache, tq=seq, past=past, rc=rc)
    return pl.pallas_call(
        kern, grid=(batch, n_cache),
        in_specs=[pl.BlockSpec((HEADS, seq, KV_LORA), lambda b, j: (0, b, 0)),
                  pl.BlockSpec((seq, HEADS * MLA_ROPE), lambda b, j: (b, 0)),
                  pl.BlockSpec((seq, 2 * KV_LORA), lambda b, j: (b, 0)),
                  pl.BlockSpec((1, tkc, KV_LORA), lambda b, j: (b, j, 0)),
                  pl.BlockSpec((1, MLA_ROPE, tkc), lambda b, j: (b, 0, j)),
                  _const_spec(wuvp.shape)],
        out_specs=pl.BlockSpec((seq, FOX_W), lambda b, j: (b, 0)),
        out_shape=jax.ShapeDtypeStruct((batch * seq, FOX_W), BF),
        scratch_shapes=[pltpu.VMEM((HEADS * seq, KV_LORA), BF), pltpu.VMEM((HEADS * seq, MLA_ROPE), BF)]
        + _mla_scratch(HEADS * seq),
        compiler_params=_params(2), name="mla_sample")(qlat, qrope, mkey, latc, krt, wuvp)


def _ffn_kernel(x_ref, fox_ref, mla_ref, pconv_ref, wo_ref, gf_ref, wup_ref, cw_ref, wdn_ref, gfin_ref,
                y_ref, cst_ref, hn_ref, act_ref, ush_ref, carry_ref, *, tm, seq_len):
    n_seg = max(1, tm // seq_len)
    seg = tm // n_seg
    mixed = jnp.concatenate([fox_ref[...], mla_ref[...]], axis=1)
    x1 = x_ref[...] + _dot(mixed, wo_ref[...])
    hn_ref[...] = (x1 * lax.rsqrt(jnp.mean(x1 * x1, axis=-1, keepdims=True) + EPS) * gf_ref[...]).astype(BF)
    y_ref[...] = x1

    if n_seg == 1:
        @pl.when(pl.program_id(0) % (seq_len // tm) == 0)
        def _():
            carry_ref[...] = pconv_ref[0]

    def conv(u, col, slot):
        cols = slice(col, col + FF_CHUNK)
        for s in range(n_seg):
            ush_ref[slot, s, 6:8, :] = carry_ref[:, cols] if n_seg == 1 else pconv_ref[s, :, cols]
            ush_ref[slot, s, 8:8 + seg, :] = u[seg * s:seg * (s + 1), :]
            cst_ref[s, :, cols] = u[seg * (s + 1) - 2:seg * (s + 1), :]
        if n_seg == 1:
            carry_ref[:, cols] = u[tm - 2:tm, :]
        u1 = jnp.concatenate([ush_ref[slot, s, 7:7 + seg, :] for s in range(n_seg)], axis=0)
        u2 = jnp.concatenate([ush_ref[slot, s, 6:6 + seg, :] for s in range(n_seg)], axis=0)
        cw = cw_ref[:, cols]
        return cw[3:4, :] + cw[0:1, :] * u2 + cw[1:2, :] * u1 + cw[2:3, :] * u

    def up(c):
        hn = hn_ref[...]
        g0, v0 = FF_CHUNK * c, D_FF + FF_CHUNK * c
        return _dot(hn, wup_ref[:, g0:g0 + FF_CHUNK]), _dot(hn, wup_ref[:, v0:v0 + FF_CHUNK])

    u_next = up(0)
    for c in range(N_FF):
        ug, uv = u_next
        if c + 1 < N_FF:
            u_next = up(c + 1)
        gate = conv(ug, FF_CHUNK * c, 2 * (c % 2))
        val = conv(uv, D_FF + FF_CHUNK * c, 2 * (c % 2) + 1)
        act_ref[:, FF_CHUNK * c:FF_CHUNK * (c + 1)] = (gate * jax.nn.sigmoid(gate) * val).astype(BF)

    x2 = y_ref[...] + _dot(act_ref[...], wdn_ref[...])
    y_ref[...] = x2 * lax.rsqrt(jnp.mean(x2 * x2, axis=-1, keepdims=True) + EPS) * gfin_ref[...]


def _ffn(x2d, fox, mla, pconv, wo, gf, wup, cw, wdn, gfin, *, seq_len, tm):
    n = x2d.shape[0]
    nt = n // tm
    n_seq = n // seq_len
    n_seg = max(1, tm // seq_len)
    if n_seg == 1:
        tps = seq_len // tm
        st_map = lambda i: (i // tps, 0, 0)
    else:
        st_map = lambda i: (i, 0, 0)
    row = lambda w: pl.BlockSpec((tm, w), lambda i: (i, 0))
    kern = functools.partial(_ffn_kernel, tm=tm, seq_len=seq_len)
    return pl.pallas_call(
        kern, grid=(nt,),
        in_specs=[row(D_MODEL), row(FOX_W), row(FOX_W),
                  pl.BlockSpec((n_seg, CONV_W - 1, 2 * D_FF), st_map),
                  _const_spec(wo.shape), _const_spec(gf.shape), _const_spec(wup.shape), _const_spec(cw.shape),
                  _const_spec(wdn.shape), _const_spec(gfin.shape)],
        out_specs=(row(D_MODEL), pl.BlockSpec((n_seg, CONV_W - 1, 2 * D_FF), st_map)),
        out_shape=(jax.ShapeDtypeStruct((n, D_MODEL), F32),
                   jax.ShapeDtypeStruct((n_seq, CONV_W - 1, 2 * D_FF), F32)),
        scratch_shapes=[pltpu.VMEM((tm, D_MODEL), BF), pltpu.VMEM((tm, D_FF), BF),
                        pltpu.VMEM((4, n_seg, 8 + tm // n_seg, FF_CHUNK), F32),
                        pltpu.VMEM((CONV_W - 1, 2 * D_FF), F32)],
        compiler_params=_params(1), name="ffn")(x2d, fox, mla, pconv, wo, gf, wup, cw, wdn, gfin)


def _rope_tables(start, n, reps=1):
    half = MLA_ROPE // 2
    inv = ROPE_THETA ** (-np.arange(half, dtype=np.float64) / half)
    ang = (start + np.arange(n, dtype=np.float64))[:, None] * inv[None, :]
    cos = np.tile(np.cos(ang), (1, 2 * HEADS))
    sin = np.tile(np.sin(ang), (1, 2 * HEADS))
    ones = np.ones((n, MLA_NOPE))
    zeros = np.zeros((n, LANES - MLA_NOPE - MLA_ROPE))
    cosm = np.concatenate([ones, cos[:, :MLA_ROPE], zeros], axis=1)
    sinm = np.concatenate([0.0 * ones, sin[:, :MLA_ROPE], zeros], axis=1)
    return [jnp.asarray(np.tile(a, (reps, 1)), F32) for a in (cos, sin, cosm, sinm)]


def _placement():
    place = np.zeros((LANES, 2 * N_PAIR * LANES), np.float32)
    ones = np.zeros((1, 2 * N_PAIR * LANES), np.float32)
    for h in range(HEADS):
        for i in range(3):
            for p in range(N_PAIR):
                place[HEADS * i + h, LANES * p + AUG * h + i] = 1.0
                ones[0, LANES * p + AUG * h + 3 + i] = 1.0
                ones[0, N_PAIR * LANES + LANES * p + AUG * h + i] = 1.0
                place[HEADS * i + h, N_PAIR * LANES + LANES * p + AUG * h + 3 + i] = -1.0
    rep = np.zeros((LANES, HEADS * MLA_ROPE), np.float32)
    for h in range(HEADS):
        for r in range(MLA_ROPE):
            rep[r, MLA_ROPE * h + r] = 1.0
    return jnp.asarray(place, BF), jnp.asarray(ones, F32), jnp.asarray(rep, BF)


def _rot_cols(w):
    half = MLA_ROPE // 2
    return jnp.concatenate([-w[..., half:], w[..., :half]], axis=-1)


def _pad_cols(w, width):
    return jnp.pad(w, ((0, 0), (0, width - w.shape[1])))


def _layer_weights(w_in, b_f, w_q_up, w_uk, w_uv, w_out, w_up, conv_w, conv_b, w_down):
    wq, wk, wv, wf, wqc, wckv, wkr = jnp.split(w_in, IN_SPLITS, axis=1)
    wa = jnp.concatenate([wq, wk, wv, wqc, wckv, _pad_cols(wkr, LANES), _pad_cols(_rot_cols(wkr), LANES),
                          _pad_cols(wf, LANES)], axis=1).astype(BF)
    bfp = _pad_cols(b_f[None, :], LANES).astype(F32)
    wq3 = w_q_up.reshape(Q_LORA, HEADS, MLA_NOPE + MLA_ROPE)
    wq_nope = wq3[:, :, :MLA_NOPE].reshape(Q_LORA, HEADS * MLA_NOPE)
    wq_rope = wq3[:, :, MLA_NOPE:]
    wq2 = jnp.concatenate([wq_nope, wq_rope.reshape(Q_LORA, -1), _rot_cols(wq_rope).reshape(Q_LORA, -1)],
                          axis=1).astype(BF)
    eye = jnp.eye(4, dtype=F32)
    wk = jnp.transpose(w_uk, (1, 2, 0)).reshape(2, 4, 1, MLA_NOPE, KV_LORA)
    wukp = (wk * eye[None, :, :, None, None]).reshape(HEADS, 4 * MLA_NOPE, KV_LORA)
    wv = jnp.transpose(w_uv, (1, 0, 2)).reshape(2, 4, KV_LORA, 1, MLA_VDIM)
    wuvp = (wv * eye[None, :, None, :, None]).reshape(HEADS, KV_LORA, 4 * MLA_VDIM)
    wup = w_up.astype(BF)
    cw = jnp.concatenate([conv_w, conv_b[None, :]], axis=0)
    wdn = w_down.astype(BF)
    zpad = ((0, 0), (0, 0), (0, LANES - MLA_NOPE - MLA_ROPE))
    wq_a = jnp.pad(wq3, zpad).reshape(Q_LORA, HEADS * LANES)
    wq_b = jnp.pad(jnp.concatenate([jnp.zeros_like(wq3[:, :, :MLA_NOPE]), _rot_cols(wq_rope)], axis=2), zpad)
    wqm = jnp.concatenate([wq_a, wq_b.reshape(Q_LORA, HEADS * LANES)], axis=1).astype(BF)
    wkm = jnp.pad(w_uk, ((0, 0), (0, 0), (0, LANES - MLA_NOPE))).reshape(KV_LORA, HEADS * LANES).astype(BF)
    wvm = w_uv.reshape(KV_LORA, HEADS * MLA_VDIM).astype(BF)
    return wa, bfp, wq2, wukp.astype(BF), wuvp.astype(BF), w_out.astype(BF), wup, cw, wdn, wqm, wkm, wvm


def kernel(x_prompt, x_sample, cache_fox_k, cache_fox_v, cache_fox_logf, cache_mla_latent, cache_mla_krope,
           state_ffn_conv, attn_norm, w_in, b_forget, q_norm, w_q_up, kv_norm, w_uk, w_uv, w_out, ffn_norm,
           w_up, conv_w, conv_b, w_down, final_norm):
    assert attn_norm.shape[0] == 1, "single-layer stack"
    bp, tp, _ = x_prompt.shape
    bs, ts, _ = x_sample.shape
    past = cache_fox_k.shape[2]
    assert ts == CHUNK and past % CHUNK == 0

    wa, bfp, wq2, wukp, wuvp, wo, wup, cw, wdn, wqm, wkm, wvm = _layer_weights(
        w_in[0], b_forget[0], w_q_up[0], w_uk[0], w_uv[0], w_out[0], w_up[0], conv_w[0], conv_b[0], w_down[0])
    place, ones, rep = _placement()
    g_attn = attn_norm[0][None, :]
    g_q = q_norm[0][None, :]
    g_kv = kv_norm[0][None, :]
    g_ffn = ffn_norm[0][None, :]
    g_fin = final_norm[None, :]

    tm = 512
    outs = []
    for grp in ("prompt", "sample"):
        if grp == "prompt":
            x2d = x_prompt.reshape(bp * tp, D_MODEL)
            b, t = bp, tp
            tabs = _rope_tables(0, tp)
            pconv = jnp.zeros((bp, CONV_W - 1, 2 * D_FF), F32)
        else:
            x2d = x_sample.reshape(bs * ts, D_MODEL)
            b, t = bs, ts
            tabs = _rope_tables(past, ts, reps=tm // ts)
            pconv = state_ffn_conv[0]

        absorbed = grp == "sample"
        wm = (wq2, wukp, rep) if absorbed else (wqm, wkm, wvm)
        (k, v, logf, lat, kr, qx, kx, vb, m1, m2, m3, fqs) = _proj(
            x2d, *tabs, g_attn, wa, bfp, g_q, g_kv, *wm, place, ones, seq_len=t, tm=tm, absorbed=absorbed)

        if grp == "prompt":
            fox = _fox_prompt(qx, kx, vb, batch=b, seq=t, t=512, kind="fox")
            mla = _fox_prompt(m1, m2, m3, batch=b, seq=t, t=512, kind="mla")
        else:
            mkey, qlat, qrope = m1, m2, m3
            gk = _fox_prep(jnp.transpose(cache_fox_logf[0], (0, 2, 1)))
            kct = jnp.transpose(cache_fox_k[0], (0, 2, 3, 1))
            vct = jnp.transpose(cache_fox_v[0], (0, 2, 3, 1))
            krt = jnp.transpose(cache_mla_krope[0], (0, 2, 1))
            fox = _fox_sample(qx, kx, vb, fqs, kct, vct, gk, batch=b, seq=t, tkc=min(past, 4096))
            mla = _mla_sample(qlat, qrope, mkey, cache_mla_latent[0], krt, wuvp, batch=b, seq=t,
                              tkc=min(past, 2048), rc=256)

        def small_state(a, w, b=b, t=t):
            return (jnp.transpose(a, (0, 2, 1)) if a.ndim == 3 else a.reshape(b, t, w))[None]

        y, cst = _ffn(x2d, fox, mla, pconv, wo, g_ffn, wup, cw, wdn, g_fin, seq_len=t, tm=tm)
        outs.append((y.reshape(b, t, D_MODEL),
                     k.reshape(1, b, t, HEADS, FOX_DIM), v.reshape(1, b, t, HEADS, FOX_DIM),
                     small_state(logf, HEADS), lat.reshape(1, b, t, KV_LORA), small_state(kr, MLA_ROPE),
                     cst[None]))
    (yp, *st_p), (ys, *st_s) = outs
    return (yp, ys, *st_p, *st_s)
```

```python
import functools

import numpy as np
import jax
import jax.numpy as jnp
from jax import lax
from jax.experimental import pallas as pl
from jax.experimental.pallas import tpu as pltpu

D_MODEL = 1024
CHUNK = 64
HEADS = 8
FOX_DIM = 64
MLA_NOPE = 64
MLA_ROPE = 32
MLA_VDIM = 64
Q_LORA = 384
KV_LORA = 256
D_FF = 2816
CONV_W = 3
ROPE_THETA = 10000.0
EPS = 1e-6
NEG = -1e30
LOG2E = 1.4426950408889634

FOX_W = HEADS * FOX_DIM
IN_SIZES = [FOX_W, FOX_W, FOX_W, HEADS, Q_LORA, KV_LORA, MLA_ROPE]
IN_SPLITS = [int(s) for s in np.cumsum(IN_SIZES)[:-1]]

LANES = 128
MXU_TILE = 256
AUG = 16
N_PAIR = HEADS // 2
FF_CHUNK = MXU_TILE
N_FF = D_FF // FF_CHUNK
VMEM_LIMIT = 56 * 1024 * 1024

ROW_TILE = 512
ATTN_TILE = 512
FOX_CACHE_TILE = 4096
MLA_CACHE_TILE = 2048
MLA_ROW_CHUNK = 256

C_Q, C_K, C_V = 0, FOX_W, 2 * FOX_W
C_QC = 3 * FOX_W
C_CKV = C_QC + Q_LORA
C_KR = C_CKV + KV_LORA
C_KRR = C_KR + LANES
C_F = C_KRR + LANES
N_A = C_F + LANES

BF = jnp.bfloat16
F32 = jnp.float32


def _dot(a, b):
    return jnp.dot(a, b, preferred_element_type=F32)


def _dot_nt(a, b):
    return lax.dot_general(a, b, (((1,), (1,)), ((), ())), preferred_element_type=F32)


def _split3(x):
    hi = x.astype(BF)
    r1 = x - hi.astype(F32)
    mid = r1.astype(BF)
    lo = (r1 - mid.astype(F32)).astype(BF)
    return hi, mid, lo


def _log_sigmoid(x):
    return jnp.minimum(x, 0.0) - jnp.log1p(jnp.exp(-jnp.abs(x)))


def _lanes(x, n):
    if n % LANES == 0:
        return x if n == LANES else jnp.concatenate([x] * (n // LANES), axis=1)
    assert n < LANES
    return x[:, :n]


def _params(n_axes):
    return pltpu.CompilerParams(dimension_semantics=("arbitrary",) * n_axes, vmem_limit_bytes=VMEM_LIMIT)


def _const_spec(shape):
    nd = len(shape)
    return pl.BlockSpec(shape, lambda *_: (0,) * nd, pipeline_mode=pl.Buffered(1))


def _proj_kernel(x_ref, cos_ref, sin_ref, cosm_ref, sinm_ref, g_ref, wa_ref, bf_ref, gq_ref, gkv_ref,
                 w1_ref, w2_ref, w3_ref, place_ref, ones_ref,
                 k_ref, v_ref, logf_ref, lat_ref, kr_ref, qx_ref, kx_ref, vb_ref, o1_ref, o2_ref, o3_ref,
                 fqs_ref, carry_ref, *, tm, seq_len, mla_scale, absorbed):
    x = x_ref[...]
    ms = jnp.mean(x * x, axis=-1, keepdims=True)
    hn = (x * lax.rsqrt(ms + EPS) * g_ref[...]).astype(BF)
    zs = _dot(hn, wa_ref[:, C_QC:])
    z = _dot(hn, wa_ref[:, :C_QC])

    lane = lax.broadcasted_iota(jnp.int32, (tm, LANES), 1)
    logf = _log_sigmoid(zs[:, C_F - C_QC:C_F - C_QC + LANES] + bf_ref[...])
    logf = jnp.where(lane < HEADS, logf, 0.0)
    logf_t = logf.T[:HEADS, :]
    if seq_len >= tm:
        logf_ref[0] = logf_t
    else:
        logf_ref[...] = logf[:, :HEADS]
    seg = min(seq_len, tm)
    assert seg & (seg - 1) == 0
    pos = lax.broadcasted_iota(jnp.int32, (HEADS, tm), 1) & (seg - 1)
    fc = logf_t
    shift = 1
    while shift < seg:
        fc = fc + jnp.where(pos >= shift, pltpu.roll(fc, shift, 1), 0.0)
        shift *= 2
    if seq_len > tm:
        @pl.when(pl.program_id(0) % (seq_len // tm) == 0)
        def _():
            carry_ref[...] = jnp.zeros_like(carry_ref)
        fc = fc + carry_ref[:, 0:1]
        carry_ref[...] = jnp.broadcast_to(fc[:, tm - 1:tm], carry_ref.shape)
    fcum = jnp.concatenate([fc, jnp.zeros((LANES - HEADS, tm), F32)], axis=0).T

    zc = zs[:, C_CKV - C_QC:C_CKV - C_QC + KV_LORA]
    ckv = zc * lax.rsqrt(jnp.mean(zc * zc, axis=-1, keepdims=True) + EPS) * gkv_ref[...]
    lat_ref[...] = ckv
    cos = cos_ref[...]
    sin = sin_ref[...]
    krb = (zs[:, C_KR - C_QC:C_KR - C_QC + LANES] * cos[:, :LANES]
           + zs[:, C_KRR - C_QC:C_KRR - C_QC + LANES] * sin[:, :LANES])
    if seq_len >= tm:
        kr_ref[0] = krb.T[:MLA_ROPE, :]
    else:
        kr_ref[...] = krb[:, :MLA_ROPE]
    cb = ckv.astype(BF)

    zqc = zs[:, :Q_LORA]
    qcn = (zqc * lax.rsqrt(jnp.mean(zqc * zqc, axis=-1, keepdims=True) + EPS) * gq_ref[...]).astype(BF)
    zq2 = _dot(qcn, w1_ref[...])
    if absorbed:
        mkey_ref, qlat_ref, qrope_ref = o1_ref, o2_ref, o3_ref
        mkey_ref[:, :KV_LORA] = cb
        mkey_ref[:, KV_LORA:] = _dot(krb.astype(BF), w3_ref[...]).astype(BF)
        nr = HEADS * MLA_ROPE
        qr = zq2[:, FOX_W:FOX_W + nr] * cos + zq2[:, FOX_W + nr:FOX_W + 2 * nr] * sin
        qrope_ref[...] = (qr * mla_scale).astype(BF)
        for h in range(HEADS):
            g = h // 4
            qn = zq2[:, 4 * MLA_NOPE * g:4 * MLA_NOPE * (g + 1)].astype(BF)
            qlat_ref[h] = (_dot(qn, w2_ref[h]) * mla_scale).astype(BF)
    else:
        qm_ref, km_ref, vm_ref = o1_ref, o2_ref, o3_ref
        hw = HEADS * LANES
        cosm = jnp.concatenate([cosm_ref[...]] * HEADS, axis=1)
        sinm = jnp.concatenate([sinm_ref[...]] * HEADS, axis=1)
        qm_ref[...] = ((zq2[:, :hw] * cosm + zq2[:, hw:] * sinm) * mla_scale).astype(BF)
        krs = pltpu.roll(krb, MLA_NOPE, 1)
        km_ref[...] = (_dot(cb, w2_ref[...]) + jnp.concatenate([krs] * HEADS, axis=1)).astype(BF)
        vm_ref[...] = _dot(cb, w3_ref[...]).astype(BF)

    zk = z[:, C_K:C_K + FOX_W]
    zv = z[:, C_V:C_V + FOX_W]
    for h in range(HEADS):
        k_ref[pl.ds(h, tm, stride=HEADS), :] = zk[:, FOX_DIM * h:FOX_DIM * (h + 1)]
        v_ref[pl.ds(h, tm, stride=HEADS), :] = zv[:, FOX_DIM * h:FOX_DIM * (h + 1)]
    vb_ref[...] = zv.astype(BF)

    fqs = fcum * LOG2E
    fqs_ref[...] = fqs
    hi, mid, lo = _split3(fqs)
    comb = hi.astype(F32) + pltpu.roll(mid.astype(F32), HEADS, 1) + pltpu.roll(lo.astype(F32), 2 * HEADS, 1)
    aug = _dot(comb.astype(BF), place_ref[...]) + ones_ref[...]
    zq = z[:, C_Q:C_Q + FOX_W] * (FOX_DIM ** -0.5 * LOG2E)
    for p in range(N_PAIR):
        lo_, hi_ = 2 * LANES * p, 2 * LANES * p + LANES
        qx_ref[:, lo_:hi_] = zq[:, LANES * p:LANES * (p + 1)].astype(BF)
        qx_ref[:, hi_:hi_ + LANES] = aug[:, LANES * p:LANES * (p + 1)].astype(BF)
        kx_ref[:, lo_:hi_] = zk[:, LANES * p:LANES * (p + 1)].astype(BF)
        kx_ref[:, hi_:hi_ + LANES] = aug[:, FOX_W + LANES * p:FOX_W + LANES * (p + 1)].astype(BF)


def _proj(x2d, cos, sin, cosm, sinm, g, wa, bfp, gq, gkv, w1, w2, w3, place, ones, *, seq_len, tm, absorbed):
    n = x2d.shape[0]
    nt = n // tm
    if seq_len >= tm:
        tps = seq_len // tm
        tab_map = lambda i: (i % tps, 0)
    else:
        tab_map = lambda i: (0, 0)
    row = lambda w: pl.BlockSpec((tm, w), lambda i: (i, 0))
    if seq_len >= tm:
        small = lambda w: jax.ShapeDtypeStruct((n // seq_len, w, seq_len), F32)
        small_spec = lambda w: pl.BlockSpec((1, w, tm), lambda i: (i // tps, 0, i % tps))
    else:
        small = lambda w: jax.ShapeDtypeStruct((n, w), F32)
        small_spec = row
    if absorbed:
        mla_shapes = (jax.ShapeDtypeStruct((n, 2 * KV_LORA), BF), jax.ShapeDtypeStruct((HEADS, n, KV_LORA), BF),
                      jax.ShapeDtypeStruct((n, HEADS * MLA_ROPE), BF))
        mla_specs = (row(2 * KV_LORA), pl.BlockSpec((HEADS, tm, KV_LORA), lambda i: (0, i, 0)),
                     row(HEADS * MLA_ROPE))
    else:
        mla_shapes = (jax.ShapeDtypeStruct((n, HEADS * LANES), BF), jax.ShapeDtypeStruct((n, HEADS * LANES), BF),
                      jax.ShapeDtypeStruct((n, HEADS * MLA_VDIM), BF))
        mla_specs = (row(HEADS * LANES), row(HEADS * LANES), row(HEADS * MLA_VDIM))
    out_shape = (
        jax.ShapeDtypeStruct((n * HEADS, FOX_DIM), F32),
        jax.ShapeDtypeStruct((n * HEADS, FOX_DIM), F32),
        small(HEADS),
        jax.ShapeDtypeStruct((n, KV_LORA), F32),
        small(MLA_ROPE),
        jax.ShapeDtypeStruct((n, 2 * FOX_W), BF),
        jax.ShapeDtypeStruct((n, 2 * FOX_W), BF),
        jax.ShapeDtypeStruct((n, FOX_W), BF),
        *mla_shapes,
        jax.ShapeDtypeStruct((n, LANES), F32),
    )
    kv_spec = pl.BlockSpec((tm * HEADS, FOX_DIM), lambda i: (i, 0))
    out_specs = (kv_spec, kv_spec, small_spec(HEADS), row(KV_LORA), small_spec(MLA_ROPE), row(2 * FOX_W),
                 row(2 * FOX_W), row(FOX_W), *mla_specs, row(LANES))
    in_specs = [row(D_MODEL),
                pl.BlockSpec((tm, 2 * LANES), tab_map), pl.BlockSpec((tm, 2 * LANES), tab_map),
                pl.BlockSpec((tm, LANES), tab_map), pl.BlockSpec((tm, LANES), tab_map),
                _const_spec(g.shape), _const_spec(wa.shape), _const_spec(bfp.shape), _const_spec(gq.shape),
                _const_spec(gkv.shape), _const_spec(w1.shape), _const_spec(w2.shape), _const_spec(w3.shape),
                _const_spec(place.shape), _const_spec(ones.shape)]
    kern = functools.partial(_proj_kernel, tm=tm, seq_len=seq_len, absorbed=absorbed,
                             mla_scale=float((MLA_NOPE + MLA_ROPE) ** -0.5 * LOG2E))
    return pl.pallas_call(
        kern, grid=(nt,), in_specs=in_specs, out_specs=out_specs, out_shape=out_shape,
        scratch_shapes=[pltpu.VMEM((8, LANES), F32)],
        compiler_params=_params(1), name="proj")(
            x2d, cos, sin, cosm, sinm, g, wa, bfp, gq, gkv, w1, w2, w3, place, ones)


def _fox_prep_kernel(plt_ref, g_ref):
    nb, h, n = plt_ref.shape
    x = plt_ref[...].reshape(nb * h, n)
    lane = lax.broadcasted_iota(jnp.int32, x.shape, 1)
    shift = 1
    while shift < n:
        x = x + jnp.where(lane >= shift, pltpu.roll(x, shift, 1), 0.0)
        shift *= 2
    g_ref[...] = ((x - x[:, n - 1:n]) * LOG2E).reshape(nb, h, n)


def _fox_prep(plt, nb=4):
    b, h, p = plt.shape
    nb = nb if b % nb == 0 else 1
    return pl.pallas_call(
        _fox_prep_kernel, grid=(b // nb,),
        in_specs=[pl.BlockSpec((nb, h, p), lambda i: (i, 0, 0))],
        out_specs=pl.BlockSpec((nb, h, p), lambda i: (i, 0, 0)),
        out_shape=jax.ShapeDtypeStruct((b, h, p), F32),
        compiler_params=_params(1), name="fox_prep")(plt)


def _softmax_step(s, m_prev):
    m_new = jnp.maximum(m_prev, jnp.max(s, axis=-1, keepdims=True))
    p = jnp.exp2(s - _lanes(m_new, s.shape[1]))
    alpha = jnp.exp2(m_prev - m_new)
    return p.astype(BF), alpha, m_new


def _fox_qpad(qx, h, kind="fox"):
    p, hh = divmod(h, 2)
    lane = lax.broadcasted_iota(jnp.int32, (qx.shape[0], 2 * LANES), 1)
    blk = qx[:, 2 * LANES * p:2 * LANES * (p + 1)]
    if kind == "fox":
        keep = ((lane >= FOX_DIM * hh) & (lane < FOX_DIM * (hh + 1))) | (
            (lane >= LANES + AUG * h) & (lane < LANES + AUG * (h + 1)))
    else:
        keep = (lane >= LANES * hh) & (lane < LANES * (hh + 1))
    return jnp.where(keep, blk, jnp.zeros_like(blk))


def _fox_rows_step(s, vpa, m_ref, l_ref, h, r0, r1):
    pb, alpha, m_new = _softmax_step(s, m_ref[h, r0:r1])
    pv = _dot(pb, vpa)
    l_ref[h, r0:r1] = alpha * l_ref[h, r0:r1] + pv[:, LANES:]
    m_ref[h, r0:r1] = m_new
    return alpha, pv[:, :LANES]


def _fox_pair_acc(acc_ref, p, r0, r1, a, b):
    first = lax.broadcasted_iota(jnp.int32, (r1 - r0, LANES), 1) < FOX_DIM
    acc_ref[p, r0:r1] = acc_ref[p, r0:r1] * jnp.where(first, a[0], b[0]) + jnp.where(first, a[1], b[1])


def _fox_prompt_tile(kx, vb, qpad_ref, m_ref, l_ref, acc_ref):
    tq = qpad_ref.shape[1]
    ones = jnp.ones((kx.shape[0], LANES), BF)

    def scores(h):
        return _dot_nt(qpad_ref[h], kx[:, 2 * LANES * (h // 2):2 * LANES * (h // 2 + 1)])

    s_next = scores(0)
    for p in range(N_PAIR):
        vpa = jnp.concatenate([vb[:, LANES * p:LANES * (p + 1)], ones], axis=1)
        res = []
        for hh in range(2):
            h = 2 * p + hh
            s = s_next
            if h + 1 < HEADS:
                s_next = scores(h + 1)
            res.append(_fox_rows_step(s, vpa, m_ref, l_ref, h, 0, tq))
        _fox_pair_acc(acc_ref, p, 0, tq, *res)


def _fox_diag_tile(kx, vb, qpad_ref, m_ref, l_ref, acc_ref, kind):
    tq = qpad_ref.shape[1]
    half = tq // 2
    ones = jnp.ones((tq, LANES), BF)
    col = lax.broadcasted_iota(jnp.int32, (half, half), 1)
    row = lax.broadcasted_iota(jnp.int32, (half, half), 0)
    if kind == "mla":
        shift = CHUNK.bit_length() - 1
        col, row = lax.shift_right_logical(col, shift), lax.shift_right_logical(row, shift)
    tri = col <= row

    def scores(h):
        kxp = kx[:, 2 * LANES * (h // 2):2 * LANES * (h // 2 + 1)]
        q = qpad_ref[h]
        return _dot_nt(q, kxp[:half]), _dot_nt(q[half:], kxp[half:])

    s_next = scores(0)
    for p in range(N_PAIR):
        vpa = jnp.concatenate([vb[:, LANES * p:LANES * (p + 1)], ones], axis=1)
        top, bot = [], []
        for hh in range(2):
            h = 2 * p + hh
            s_a, s_b = s_next
            if h + 1 < HEADS:
                s_next = scores(h + 1)
            s_top = jnp.where(tri, s_a[:half], NEG)
            s_bot = jnp.concatenate([s_a[half:], jnp.where(tri, s_b, NEG)], axis=1)
            top.append(_fox_rows_step(s_top, vpa[:half], m_ref, l_ref, h, 0, half))
            bot.append(_fox_rows_step(s_bot, vpa, m_ref, l_ref, h, half, tq))
        _fox_pair_acc(acc_ref, p, 0, half, *top)
        _fox_pair_acc(acc_ref, p, half, tq, *bot)


def _fox_prompt_kernel(qi_ref, kj_ref, qx_ref, kx_ref, vb_ref, o_ref, qpad_ref, m_ref, l_ref, acc_ref, *, kind):
    i = qi_ref[pl.program_id(1)]
    j = kj_ref[pl.program_id(1)]

    @pl.when(j == 0)
    def _():
        qx = qx_ref[...]
        for h in range(HEADS):
            qpad_ref[h] = _fox_qpad(qx, h, kind)
        m_ref[...] = jnp.full(m_ref.shape, NEG, F32)
        l_ref[...] = jnp.zeros(l_ref.shape, F32)
        acc_ref[...] = jnp.zeros(acc_ref.shape, F32)

    @pl.when(j < i)
    def _():
        _fox_prompt_tile(kx_ref[...], vb_ref[...], qpad_ref, m_ref, l_ref, acc_ref)

    @pl.when(j == i)
    def _():
        _fox_diag_tile(kx_ref[...], vb_ref[...], qpad_ref, m_ref, l_ref, acc_ref, kind)
        first = lax.broadcasted_iota(jnp.int32, (qpad_ref.shape[1], LANES), 1) < FOX_DIM
        for p in range(N_PAIR):
            l_sel = jnp.where(first, l_ref[2 * p], l_ref[2 * p + 1])
            o_ref[:, LANES * p:LANES * (p + 1)] = (acc_ref[p] / l_sel).astype(o_ref.dtype)


def _fox_prompt(qx, kx, vb, *, batch, seq, t, kind):
    nq = seq // t
    assert CHUNK & (CHUNK - 1) == 0 and t % CHUNK == 0
    steps = [(i, j) for i in range(nq) for j in range(i + 1)]
    qi = jnp.asarray([s[0] for s in steps], jnp.int32)
    kj = jnp.asarray([s[1] for s in steps], jnp.int32)
    grid_spec = pltpu.PrefetchScalarGridSpec(
        num_scalar_prefetch=2, grid=(batch, len(steps)),
        in_specs=[pl.BlockSpec((t, 2 * FOX_W), lambda b, s, qi, kj: (b * nq + qi[s], 0)),
                  pl.BlockSpec((t, 2 * FOX_W), lambda b, s, qi, kj: (b * nq + kj[s], 0)),
                  pl.BlockSpec((t, FOX_W), lambda b, s, qi, kj: (b * nq + kj[s], 0))],
        out_specs=pl.BlockSpec((t, FOX_W), lambda b, s, qi, kj: (b * nq + qi[s], 0)),
        scratch_shapes=[pltpu.VMEM((HEADS, t, 2 * LANES), BF), pltpu.VMEM((HEADS, t, LANES), F32),
                        pltpu.VMEM((HEADS, t, LANES), F32), pltpu.VMEM((N_PAIR, t, LANES), F32)])
    return pl.pallas_call(
        functools.partial(_fox_prompt_kernel, kind=kind), grid_spec=grid_spec,
        out_shape=jax.ShapeDtypeStruct((batch * seq, FOX_W), BF),
        compiler_params=_params(2), name=kind + "_prompt")(qi, kj, qx, kx, vb)


def _fox_sample_kernel(qx_ref, kx_ref, vb_ref, fqs_ref, kct_ref, vct_ref, gk_ref, o_ref,
                       qbd_ref, qpad_ref, fq_ref, m_ref, l_ref, acc_ref, *, n_cache):
    j = pl.program_id(1)
    tq = qx_ref.shape[0]
    rows = HEADS * tq

    def update(s, pv_fn, g0=0, g1=2):
        a, b = 4 * tq * g0, 4 * tq * g1
        m_prev = m_ref[a:b]
        m_new = jnp.maximum(m_prev, jnp.max(s, axis=-1, keepdims=True))
        p = jnp.exp2(s - _lanes(m_new, s.shape[1]))
        alpha = jnp.exp2(m_prev - m_new)
        l_ref[a:b] = alpha * l_ref[a:b] + jnp.sum(p, axis=-1, keepdims=True)
        m_ref[a:b] = m_new
        pb = p.astype(BF)
        for pr in range(2 * g0, 2 * g1):
            r0, r1 = 2 * tq * pr, 2 * tq * (pr + 1)
            acc_ref[r0:r1] = acc_ref[r0:r1] * alpha[r0 - a:r1 - a] + pv_fn(pr, pb[r0 - a:r1 - a])

    @pl.when(j == 0)
    def _():
        qx = qx_ref[...]
        fqs = fqs_ref[...]
        qc = jnp.concatenate([qx[:, 2 * LANES * p:2 * LANES * p + LANES] for p in range(N_PAIR)], axis=1)
        lane = lax.broadcasted_iota(jnp.int32, qc.shape, 1)
        for h in range(HEADS):
            qbd_ref[h * tq:(h + 1) * tq, :] = jnp.where((lane // FOX_DIM) == h, qc, jnp.zeros_like(qc))
            qpad_ref[h] = _fox_qpad(qx, h)
            fq_ref[h * tq:(h + 1) * tq, :] = jnp.broadcast_to(fqs[:, h:h + 1], (tq, LANES))
        m_ref[...] = jnp.full(m_ref.shape, NEG, F32)
        l_ref[...] = jnp.zeros(l_ref.shape, F32)
        acc_ref[...] = jnp.zeros(acc_ref.shape, F32)
        kx = kx_ref[...]
        vb = vb_ref[...]
        tk = kx.shape[0]
        s = jnp.concatenate([_dot_nt(qpad_ref[h], kx[:, 2 * LANES * (h // 2):2 * LANES * (h // 2 + 1)])
                             for h in range(HEADS)], axis=0)
        row = lax.broadcasted_iota(jnp.int32, (rows, tk), 0) & (tq - 1)
        s = jnp.where(lax.broadcasted_iota(jnp.int32, (rows, tk), 1) <= row, s, NEG)
        update(s, lambda pr, pb: _dot(pb, vb[:, LANES * pr:LANES * (pr + 1)]))

    def cache_tile():
        tk = kct_ref.shape[3]
        gk = gk_ref[0]

        def scores(g):
            kt = kct_ref[0, 4 * g:4 * g + 4].reshape(4 * FOX_DIM, tk).astype(BF)
            gk_rows = jnp.concatenate([jnp.broadcast_to(gk[h:h + 1, :], (tq, tk)) for h in range(4 * g, 4 * g + 4)],
                                      axis=0)
            qg = qbd_ref[4 * tq * g:4 * tq * (g + 1), 4 * FOX_DIM * g:4 * FOX_DIM * (g + 1)]
            return _dot(qg, kt) + (_lanes(fq_ref[4 * tq * g:4 * tq * (g + 1)], tk) - gk_rows)

        def pv(pr, pb):
            return _dot_nt(pb, vct_ref[0, 2 * pr:2 * pr + 2].reshape(2 * FOX_DIM, tk).astype(BF))

        s0 = scores(0)
        s1 = scores(1)
        update(s0, pv, 0, 1)
        update(s1, pv, 1, 2)

    cache_tile()

    @pl.when(j == n_cache - 1)
    def _():
        first = lax.broadcasted_iota(jnp.int32, (tq, LANES), 1) < FOX_DIM
        o = acc_ref[...] / l_ref[...]
        for pr in range(N_PAIR):
            o_ref[:, LANES * pr:LANES * (pr + 1)] = jnp.where(
                first, o[2 * tq * pr:2 * tq * pr + tq], o[2 * tq * pr + tq:2 * tq * (pr + 1)]).astype(o_ref.dtype)


def _fox_sample(qx, kx, vb, fqs, kct, vct, gk, *, batch, seq, tkc):
    past = kct.shape[3]
    n_cache = past // tkc
    kern = functools.partial(_fox_sample_kernel, n_cache=n_cache)
    row = lambda w: pl.BlockSpec((seq, w), lambda b, j: (b, 0))
    return pl.pallas_call(
        kern, grid=(batch, n_cache),
        in_specs=[row(2 * FOX_W), row(2 * FOX_W), row(FOX_W), row(LANES),
                  pl.BlockSpec((1, HEADS, FOX_DIM, tkc), lambda b, j: (b, 0, 0, j)),
                  pl.BlockSpec((1, HEADS, FOX_DIM, tkc), lambda b, j: (b, 0, 0, j)),
                  pl.BlockSpec((1, HEADS, tkc), lambda b, j: (b, 0, j))],
        out_specs=row(FOX_W),
        out_shape=jax.ShapeDtypeStruct((batch * seq, FOX_W), BF),
        scratch_shapes=[pltpu.VMEM((HEADS * seq, FOX_W), BF), pltpu.VMEM((HEADS, seq, 2 * LANES), BF)]
        + [pltpu.VMEM((HEADS * seq, LANES), F32)] * 4,
        compiler_params=_params(2), name="fox_sample")(qx, kx, vb, fqs, kct, vct, gk)


def _chunk_mask(shape, tq, row0, q0, k0):
    assert tq & (tq - 1) == 0 and CHUNK & (CHUNK - 1) == 0
    shift = CHUNK.bit_length() - 1
    qpos = q0 + ((row0 + lax.broadcasted_iota(jnp.int32, shape, 0)) & (tq - 1))
    kpos = k0 + lax.broadcasted_iota(jnp.int32, shape, 1)
    return lax.shift_right_logical(kpos, shift) <= lax.shift_right_logical(qpos, shift)


def _mla_rows_update(s, val, m_ref, l_ref, acc_ref, r0, r1):
    m_prev = m_ref[r0:r1]
    m_new = jnp.maximum(m_prev, jnp.max(s, axis=-1, keepdims=True))
    p = jnp.exp2(s - _lanes(m_new, s.shape[1]))
    alpha = jnp.exp2(m_prev - m_new)
    l_ref[r0:r1] = alpha * l_ref[r0:r1] + jnp.sum(p, axis=-1, keepdims=True)
    m_ref[r0:r1] = m_new
    acc_ref[r0:r1] = acc_ref[r0:r1] * _lanes(alpha, KV_LORA) + _dot(p.astype(BF), val)


def _mla_init_stats(m_ref, l_ref, acc_ref):
    m_ref[...] = jnp.full(m_ref.shape, NEG, F32)
    l_ref[...] = jnp.zeros(l_ref.shape, F32)
    acc_ref[...] = jnp.zeros(acc_ref.shape, F32)


def _mla_finalize(o_ref, wuv_ref, l_ref, acc_ref, tq):
    olat = (acc_ref[...] / _lanes(l_ref[...], KV_LORA)).astype(BF)
    for g in range(2):
        out = None
        for h in range(4 * g, 4 * g + 4):
            part = _dot(olat[h * tq:(h + 1) * tq, :], wuv_ref[h])
            out = part if out is None else out + part
        o_ref[:, 4 * MLA_VDIM * g:4 * MLA_VDIM * (g + 1)] = out.astype(o_ref.dtype)


def _mla_scratch(rows):
    return [pltpu.VMEM((rows, LANES), F32), pltpu.VMEM((rows, LANES), F32), pltpu.VMEM((rows, KV_LORA), F32)]


def _mla_sample_kernel(qlat_ref, qrope_ref, mkey_ref, latc_ref, krt_ref, wuv_ref, o_ref,
                       ql_ref, qr_ref, m_ref, l_ref, acc_ref, *, n_cache, tq, past, rc):
    j = pl.program_id(1)
    rows = HEADS * tq

    @pl.when(j == 0)
    def _():
        qrope = qrope_ref[...]
        for h in range(HEADS):
            ql_ref[h * tq:(h + 1) * tq, :] = qlat_ref[h]
            qr_ref[h * tq:(h + 1) * tq, :] = qrope[:, MLA_ROPE * h:MLA_ROPE * (h + 1)]
        _mla_init_stats(m_ref, l_ref, acc_ref)
        mk = mkey_ref[...]
        s = _dot_nt(ql_ref[...], mk[:, :KV_LORA]) + _dot_nt(qr_ref[...], mk[:, KV_LORA:KV_LORA + MLA_ROPE])
        if past % CHUNK != 0 or tq > CHUNK:
            s = jnp.where(_chunk_mask(s.shape, tq, 0, past, past), s, NEG)
        _mla_rows_update(s, mk[:, :KV_LORA], m_ref, l_ref, acc_ref, 0, rows)

    def cache_tile():
        c = latc_ref[0].astype(BF)
        krt = krt_ref[0].astype(BF)
        def scores(r0):
            return _dot_nt(ql_ref[r0:r0 + rc], c) + _dot(qr_ref[r0:r0 + rc], krt)

        s_next = scores(0)
        for r0 in range(0, rows, rc):
            s = s_next
            if r0 + rc < rows:
                s_next = scores(r0 + rc)
            _mla_rows_update(s, c, m_ref, l_ref, acc_ref, r0, r0 + rc)

    cache_tile()

    @pl.when(j == n_cache - 1)
    def _():
        _mla_finalize(o_ref, wuv_ref, l_ref, acc_ref, tq)


def _mla_sample(qlat, qrope, mkey, latc, krt, wuvp, *, batch, seq, tkc, rc):
    past = latc.shape[1]
    n_cache = past // tkc
    kern = functools.partial(_mla_sample_kernel, n_cache=n_cache, tq=seq, past=past, rc=rc)
    return pl.pallas_call(
        kern, grid=(batch, n_cache),
        in_specs=[pl.BlockSpec((HEADS, seq, KV_LORA), lambda b, j: (0, b, 0)),
                  pl.BlockSpec((seq, HEADS * MLA_ROPE), lambda b, j: (b, 0)),
                  pl.BlockSpec((seq, 2 * KV_LORA), lambda b, j: (b, 0)),
                  pl.BlockSpec((1, tkc, KV_LORA), lambda b, j: (b, j, 0)),
                  pl.BlockSpec((1, MLA_ROPE, tkc), lambda b, j: (b, 0, j)),
                  _const_spec(wuvp.shape)],
        out_specs=pl.BlockSpec((seq, FOX_W), lambda b, j: (b, 0)),
        out_shape=jax.ShapeDtypeStruct((batch * seq, FOX_W), BF),
        scratch_shapes=[pltpu.VMEM((HEADS * seq, KV_LORA), BF), pltpu.VMEM((HEADS * seq, MLA_ROPE), BF)]
        + _mla_scratch(HEADS * seq),
        compiler_params=_params(2), name="mla_sample")(qlat, qrope, mkey, latc, krt, wuvp)


def _ffn_kernel(x_ref, fox_ref, mla_ref, pconv_ref, wo_ref, gf_ref, wup_ref, cw_ref, wdn_ref, gfin_ref,
                y_ref, cst_ref, hn_ref, act_ref, ush_ref, carry_ref, *, tm, seq_len):
    n_seg = max(1, tm // seq_len)
    seg = tm // n_seg
    mixed = jnp.concatenate([fox_ref[...], mla_ref[...]], axis=1)
    x1 = x_ref[...] + _dot(mixed, wo_ref[...])
    hn_ref[...] = (x1 * lax.rsqrt(jnp.mean(x1 * x1, axis=-1, keepdims=True) + EPS) * gf_ref[...]).astype(BF)
    y_ref[...] = x1

    if n_seg == 1:
        @pl.when(pl.program_id(0) % (seq_len // tm) == 0)
        def _():
            carry_ref[...] = pconv_ref[0]

    def conv(u, col, slot):
        cols = slice(col, col + FF_CHUNK)
        for s in range(n_seg):
            ush_ref[slot, s, 6:8, :] = carry_ref[:, cols] if n_seg == 1 else pconv_ref[s, :, cols]
            ush_ref[slot, s, 8:8 + seg, :] = u[seg * s:seg * (s + 1), :]
            cst_ref[s, :, cols] = u[seg * (s + 1) - 2:seg * (s + 1), :]
        if n_seg == 1:
            carry_ref[:, cols] = u[tm - 2:tm, :]
        u1 = jnp.concatenate([ush_ref[slot, s, 7:7 + seg, :] for s in range(n_seg)], axis=0)
        u2 = jnp.concatenate([ush_ref[slot, s, 6:6 + seg, :] for s in range(n_seg)], axis=0)
        cw = cw_ref[:, cols]
        return cw[3:4, :] + cw[0:1, :] * u2 + cw[1:2, :] * u1 + cw[2:3, :] * u

    def up(c):
        hn = hn_ref[...]
        g0, v0 = FF_CHUNK * c, D_FF + FF_CHUNK * c
        return _dot(hn, wup_ref[:, g0:g0 + FF_CHUNK]), _dot(hn, wup_ref[:, v0:v0 + FF_CHUNK])

    u_next = up(0)
    for c in range(N_FF):
        ug, uv = u_next
        if c + 1 < N_FF:
            u_next = up(c + 1)
        gate = conv(ug, FF_CHUNK * c, 2 * (c % 2))
        val = conv(uv, D_FF + FF_CHUNK * c, 2 * (c % 2) + 1)
        act_ref[:, FF_CHUNK * c:FF_CHUNK * (c + 1)] = (gate * jax.nn.sigmoid(gate) * val).astype(BF)

    x2 = y_ref[...] + _dot(act_ref[...], wdn_ref[...])
    y_ref[...] = x2 * lax.rsqrt(jnp.mean(x2 * x2, axis=-1, keepdims=True) + EPS) * gfin_ref[...]


def _ffn(x2d, fox, mla, pconv, wo, gf, wup, cw, wdn, gfin, *, seq_len, tm):
    n = x2d.shape[0]
    nt = n // tm
    n_seq = n // seq_len
    n_seg = max(1, tm // seq_len)
    if n_seg == 1:
        tps = seq_len // tm
        st_map = lambda i: (i // tps, 0, 0)
    else:
        st_map = lambda i: (i, 0, 0)
    row = lambda w: pl.BlockSpec((tm, w), lambda i: (i, 0))
    kern = functools.partial(_ffn_kernel, tm=tm, seq_len=seq_len)
    return pl.pallas_call(
        kern, grid=(nt,),
        in_specs=[row(D_MODEL), row(FOX_W), row(FOX_W),
                  pl.BlockSpec((n_seg, CONV_W - 1, 2 * D_FF), st_map),
                  _const_spec(wo.shape), _const_spec(gf.shape), _const_spec(wup.shape), _const_spec(cw.shape),
                  _const_spec(wdn.shape), _const_spec(gfin.shape)],
        out_specs=(row(D_MODEL), pl.BlockSpec((n_seg, CONV_W - 1, 2 * D_FF), st_map)),
        out_shape=(jax.ShapeDtypeStruct((n, D_MODEL), F32),
                   jax.ShapeDtypeStruct((n_seq, CONV_W - 1, 2 * D_FF), F32)),
        scratch_shapes=[pltpu.VMEM((tm, D_MODEL), BF), pltpu.VMEM((tm, D_FF), BF),
                        pltpu.VMEM((4, n_seg, 8 + tm // n_seg, FF_CHUNK), F32),
                        pltpu.VMEM((CONV_W - 1, 2 * D_FF), F32)],
        compiler_params=_params(1), name="ffn")(x2d, fox, mla, pconv, wo, gf, wup, cw, wdn, gfin)


def _rope_tables(start, n, reps=1):
    half = MLA_ROPE // 2
    inv = ROPE_THETA ** (-np.arange(half, dtype=np.float64) / half)
    ang = (start + np.arange(n, dtype=np.float64))[:, None] * inv[None, :]
    cos = np.tile(np.cos(ang), (1, 2 * HEADS))
    sin = np.tile(np.sin(ang), (1, 2 * HEADS))
    ones = np.ones((n, MLA_NOPE))
    zeros = np.zeros((n, LANES - MLA_NOPE - MLA_ROPE))
    cosm = np.concatenate([ones, cos[:, :MLA_ROPE], zeros], axis=1)
    sinm = np.concatenate([0.0 * ones, sin[:, :MLA_ROPE], zeros], axis=1)
    return [jnp.asarray(np.tile(a, (reps, 1)), F32) for a in (cos, sin, cosm, sinm)]


def _placement():
    place = np.zeros((LANES, 2 * N_PAIR * LANES), np.float32)
    ones = np.zeros((1, 2 * N_PAIR * LANES), np.float32)
    for h in range(HEADS):
        for i in range(3):
            for p in range(N_PAIR):
                place[HEADS * i + h, LANES * p + AUG * h + i] = 1.0
                ones[0, LANES * p + AUG * h + 3 + i] = 1.0
                ones[0, N_PAIR * LANES + LANES * p + AUG * h + i] = 1.0
                place[HEADS * i + h, N_PAIR * LANES + LANES * p + AUG * h + 3 + i] = -1.0
    rep = np.zeros((LANES, HEADS * MLA_ROPE), np.float32)
    for h in range(HEADS):
        for r in range(MLA_ROPE):
            rep[r, MLA_ROPE * h + r] = 1.0
    return jnp.asarray(place, BF), jnp.asarray(ones, F32), jnp.asarray(rep, BF)


def _rot_cols(w):
    half = MLA_ROPE // 2
    return jnp.concatenate([-w[..., half:], w[..., :half]], axis=-1)


def _pad_cols(w, width):
    return jnp.pad(w, ((0, 0), (0, width - w.shape[1])))


def _layer_weights(w_in, b_f, w_q_up, w_uk, w_uv, w_out, w_up, conv_w, conv_b, w_down):
    wq, wk, wv, wf, wqc, wckv, wkr = jnp.split(w_in, IN_SPLITS, axis=1)
    wa = jnp.concatenate([wq, wk, wv, wqc, wckv, _pad_cols(wkr, LANES), _pad_cols(_rot_cols(wkr), LANES),
                          _pad_cols(wf, LANES)], axis=1).astype(BF)
    bfp = _pad_cols(b_f[None, :], LANES).astype(F32)
    wq3 = w_q_up.reshape(Q_LORA, HEADS, MLA_NOPE + MLA_ROPE)
    wq_nope = wq3[:, :, :MLA_NOPE].reshape(Q_LORA, HEADS * MLA_NOPE)
    wq_rope = wq3[:, :, MLA_NOPE:]
    wq2 = jnp.concatenate([wq_nope, wq_rope.reshape(Q_LORA, -1), _rot_cols(wq_rope).reshape(Q_LORA, -1)],
                          axis=1).astype(BF)
    eye = jnp.eye(4, dtype=F32)
    wk = jnp.transpose(w_uk, (1, 2, 0)).reshape(2, 4, 1, MLA_NOPE, KV_LORA)
    wukp = (wk * eye[None, :, :, None, None]).reshape(HEADS, 4 * MLA_NOPE, KV_LORA)
    wv = jnp.transpose(w_uv, (1, 0, 2)).reshape(2, 4, KV_LORA, 1, MLA_VDIM)
    wuvp = (wv * eye[None, :, None, :, None]).reshape(HEADS, KV_LORA, 4 * MLA_VDIM)
    wup = w_up.astype(BF)
    cw = jnp.concatenate([conv_w, conv_b[None, :]], axis=0)
    wdn = w_down.astype(BF)
    zpad = ((0, 0), (0, 0), (0, LANES - MLA_NOPE - MLA_ROPE))
    wq_a = jnp.pad(wq3, zpad).reshape(Q_LORA, HEADS * LANES)
    wq_b = jnp.pad(jnp.concatenate([jnp.zeros_like(wq3[:, :, :MLA_NOPE]), _rot_cols(wq_rope)], axis=2), zpad)
    wqm = jnp.concatenate([wq_a, wq_b.reshape(Q_LORA, HEADS * LANES)], axis=1).astype(BF)
    wkm = jnp.pad(w_uk, ((0, 0), (0, 0), (0, LANES - MLA_NOPE))).reshape(KV_LORA, HEADS * LANES).astype(BF)
    wvm = w_uv.reshape(KV_LORA, HEADS * MLA_VDIM).astype(BF)
    return wa, bfp, wq2, wukp.astype(BF), wuvp.astype(BF), w_out.astype(BF), wup, cw, wdn, wqm, wkm, wvm


def kernel(x_prompt, x_sample, cache_fox_k, cache_fox_v, cache_fox_logf, cache_mla_latent, cache_mla_krope,
           state_ffn_conv, attn_norm, w_in, b_forget, q_norm, w_q_up, kv_norm, w_uk, w_uv, w_out, ffn_norm,
           w_up, conv_w, conv_b, w_down, final_norm):
    assert attn_norm.shape[0] == 1, "single-layer stack"
    bp, tp, _ = x_prompt.shape
    bs, ts, _ = x_sample.shape
    past = cache_fox_k.shape[2]
    assert ts == CHUNK and past % CHUNK == 0

    wa, bfp, wq2, wukp, wuvp, wo, wup, cw, wdn, wqm, wkm, wvm = _layer_weights(
        w_in[0], b_forget[0], w_q_up[0], w_uk[0], w_uv[0], w_out[0], w_up[0], conv_w[0], conv_b[0], w_down[0])
    place, ones, rep = _placement()
    g_attn = attn_norm[0][None, :]
    g_q = q_norm[0][None, :]
    g_kv = kv_norm[0][None, :]
    g_ffn = ffn_norm[0][None, :]
    g_fin = final_norm[None, :]

    tm = ROW_TILE
    assert (bp * tp) % tm == 0 and (bs * ts) % tm == 0 and tp % ATTN_TILE == 0
    outs = []
    for grp in ("prompt", "sample"):
        if grp == "prompt":
            x2d = x_prompt.reshape(bp * tp, D_MODEL)
            b, t = bp, tp
            tabs = _rope_tables(0, tp)
            pconv = jnp.zeros((bp, CONV_W - 1, 2 * D_FF), F32)
        else:
            x2d = x_sample.reshape(bs * ts, D_MODEL)
            b, t = bs, ts
            tabs = _rope_tables(past, ts, reps=tm // ts)
            pconv = state_ffn_conv[0]

        absorbed = grp == "sample"
        wm = (wq2, wukp, rep) if absorbed else (wqm, wkm, wvm)
        (k, v, logf, lat, kr, qx, kx, vb, m1, m2, m3, fqs) = _proj(
            x2d, *tabs, g_attn, wa, bfp, g_q, g_kv, *wm, place, ones, seq_len=t, tm=tm, absorbed=absorbed)

        if grp == "prompt":
            fox = _fox_prompt(qx, kx, vb, batch=b, seq=t, t=ATTN_TILE, kind="fox")
            mla = _fox_prompt(m1, m2, m3, batch=b, seq=t, t=ATTN_TILE, kind="mla")
        else:
            mkey, qlat, qrope = m1, m2, m3
            gk = _fox_prep(jnp.transpose(cache_fox_logf[0], (0, 2, 1)))
            kct = jnp.transpose(cache_fox_k[0], (0, 2, 3, 1))
            vct = jnp.transpose(cache_fox_v[0], (0, 2, 3, 1))
            krt = jnp.transpose(cache_mla_krope[0], (0, 2, 1))
            assert past % min(past, FOX_CACHE_TILE) == 0 and past % min(past, MLA_CACHE_TILE) == 0
            fox = _fox_sample(qx, kx, vb, fqs, kct, vct, gk, batch=b, seq=t, tkc=min(past, FOX_CACHE_TILE))
            mla = _mla_sample(qlat, qrope, mkey, cache_mla_latent[0], krt, wuvp, batch=b, seq=t,
                              tkc=min(past, MLA_CACHE_TILE), rc=MLA_ROW_CHUNK)

        def small_state(a, w, b=b, t=t):
            return (jnp.transpose(a, (0, 2, 1)) if a.ndim == 3 else a.reshape(b, t, w))[None]

        y, cst = _ffn(x2d, fox, mla, pconv, wo, g_ffn, wup, cw, wdn, g_fin, seq_len=t, tm=tm)
        outs.append((y.reshape(b, t, D_MODEL),
                     k.reshape(1, b, t, HEADS, FOX_DIM), v.reshape(1, b, t, HEADS, FOX_DIM),
                     small_state(logf, HEADS), lat.reshape(1, b, t, KV_LORA), small_state(kr, MLA_ROPE),
                     cst[None]))
    (yp, *st_p), (ys, *st_s) = outs
    return (yp, ys, *st_p, *st_s)
```

```python
import functools

import numpy as np
import jax
import jax.numpy as jnp
from jax import lax
from jax.experimental import pallas as pl
from jax.experimental.pallas import tpu as pltpu

D_MODEL = 1024
CHUNK = 64
HEADS = 8
FOX_DIM = 64
MLA_NOPE = 64
MLA_ROPE = 32
MLA_VDIM = 64
Q_LORA = 384
KV_LORA = 256
D_FF = 2816
CONV_W = 3
ROPE_THETA = 10000.0
EPS = 1e-6
NEG = -1e30
LOG2E = 1.4426950408889634

FOX_W = HEADS * FOX_DIM
IN_SIZES = [FOX_W, FOX_W, FOX_W, HEADS, Q_LORA, KV_LORA, MLA_ROPE]
IN_SPLITS = [int(s) for s in np.cumsum(IN_SIZES)[:-1]]

LANES = 128
MXU_TILE = 256
AUG = 16
N_PAIR = HEADS // 2
FF_CHUNK = MXU_TILE
N_FF = D_FF // FF_CHUNK
VMEM_LIMIT = 56 * 1024 * 1024

ROW_TILE = 512
ATTN_TILE = 512
FOX_CACHE_TILE = 4096
MLA_CACHE_TILE = 2048
MLA_ROW_CHUNK = 256

C_Q, C_K, C_V = 0, FOX_W, 2 * FOX_W
C_QC = 3 * FOX_W
C_CKV = C_QC + Q_LORA
C_KR = C_CKV + KV_LORA
C_KRR = C_KR + LANES
C_F = C_KRR + LANES
N_A = C_F + LANES

BF = jnp.bfloat16
F32 = jnp.float32


def _dot(a, b):
    return jnp.dot(a, b, preferred_element_type=F32)


def _dot_nt(a, b):
    return lax.dot_general(a, b, (((1,), (1,)), ((), ())), preferred_element_type=F32)


def _split3(x):
    hi = x.astype(BF)
    r1 = x - hi.astype(F32)
    mid = r1.astype(BF)
    lo = (r1 - mid.astype(F32)).astype(BF)
    return hi, mid, lo


def _log_sigmoid(x):
    return jnp.minimum(x, 0.0) - jnp.log1p(jnp.exp(-jnp.abs(x)))


def _lanes(x, n):
    if n % LANES == 0:
        return x if n == LANES else jnp.concatenate([x] * (n // LANES), axis=1)
    assert n < LANES
    return x[:, :n]


def _params(n_axes):
    return pltpu.CompilerParams(dimension_semantics=("arbitrary",) * n_axes, vmem_limit_bytes=VMEM_LIMIT)


def _const_spec(shape):
    nd = len(shape)
    return pl.BlockSpec(shape, lambda *_: (0,) * nd, pipeline_mode=pl.Buffered(1))


def _proj_kernel(x_ref, cos_ref, sin_ref, cosm_ref, sinm_ref, g_ref, wa_ref, bf_ref, gq_ref, gkv_ref,
                 w1_ref, w2_ref, w3_ref, place_ref, ones_ref,
                 k_ref, v_ref, logf_ref, lat_ref, kr_ref, qx_ref, kx_ref, vb_ref, o1_ref, o2_ref, o3_ref,
                 fqs_ref, carry_ref, *, tm, seq_len, mla_scale, absorbed):
    x = x_ref[...]
    ms = jnp.mean(x * x, axis=-1, keepdims=True)
    hn = (x * lax.rsqrt(ms + EPS) * g_ref[...]).astype(BF)
    zs = _dot(hn, wa_ref[:, C_QC:])
    z = _dot(hn, wa_ref[:, :C_QC])

    lane = lax.broadcasted_iota(jnp.int32, (tm, LANES), 1)
    logf = _log_sigmoid(zs[:, C_F - C_QC:C_F - C_QC + LANES] + bf_ref[...])
    logf = jnp.where(lane < HEADS, logf, 0.0)
    logf_t = logf.T[:HEADS, :]
    if seq_len >= tm:
        logf_ref[0] = logf_t
    else:
        logf_ref[...] = logf[:, :HEADS]
    seg = min(seq_len, tm)
    assert seg & (seg - 1) == 0
    pos = lax.broadcasted_iota(jnp.int32, (HEADS, tm), 1) & (seg - 1)
    fc = logf_t
    shift = 1
    while shift < seg:
        fc = fc + jnp.where(pos >= shift, pltpu.roll(fc, shift, 1), 0.0)
        shift *= 2
    if seq_len > tm:
        @pl.when(pl.program_id(0) % (seq_len // tm) == 0)
        def _():
            carry_ref[...] = jnp.zeros_like(carry_ref)
        fc = fc + carry_ref[:, 0:1]
        carry_ref[...] = jnp.broadcast_to(fc[:, tm - 1:tm], carry_ref.shape)
    fcum = jnp.concatenate([fc, jnp.zeros((LANES - HEADS, tm), F32)], axis=0).T

    zc = zs[:, C_CKV - C_QC:C_CKV - C_QC + KV_LORA]
    ckv = zc * lax.rsqrt(jnp.mean(zc * zc, axis=-1, keepdims=True) + EPS) * gkv_ref[...]
    lat_ref[...] = ckv
    cos = cos_ref[...]
    sin = sin_ref[...]
    krb = (zs[:, C_KR - C_QC:C_KR - C_QC + LANES] * cos[:, :LANES]
           + zs[:, C_KRR - C_QC:C_KRR - C_QC + LANES] * sin[:, :LANES])
    if seq_len >= tm:
        kr_ref[0] = krb.T[:MLA_ROPE, :]
    else:
        kr_ref[...] = krb[:, :MLA_ROPE]
    cb = ckv.astype(BF)

    zqc = zs[:, :Q_LORA]
    qcn = (zqc * lax.rsqrt(jnp.mean(zqc * zqc, axis=-1, keepdims=True) + EPS) * gq_ref[...]).astype(BF)
    zq2 = _dot(qcn, w1_ref[...])
    if absorbed:
        mkey_ref, qlat_ref, qrope_ref = o1_ref, o2_ref, o3_ref
        mkey_ref[:, :KV_LORA] = cb
        mkey_ref[:, KV_LORA:] = _dot(krb.astype(BF), w3_ref[...]).astype(BF)
        nr = HEADS * MLA_ROPE
        qr = zq2[:, FOX_W:FOX_W + nr] * cos + zq2[:, FOX_W + nr:FOX_W + 2 * nr] * sin
        qrope_ref[...] = (qr * mla_scale).astype(BF)
        for h in range(HEADS):
            g = h // 4
            qn = zq2[:, 4 * MLA_NOPE * g:4 * MLA_NOPE * (g + 1)].astype(BF)
            qlat_ref[h] = (_dot(qn, w2_ref[h]) * mla_scale).astype(BF)
    else:
        qm_ref, km_ref, vm_ref = o1_ref, o2_ref, o3_ref
        hw = HEADS * LANES
        cosm = jnp.concatenate([cosm_ref[...]] * HEADS, axis=1)
        sinm = jnp.concatenate([sinm_ref[...]] * HEADS, axis=1)
        qm_ref[...] = ((zq2[:, :hw] * cosm + zq2[:, hw:] * sinm) * mla_scale).astype(BF)
        krs = pltpu.roll(krb, MLA_NOPE, 1)
        km_ref[...] = (_dot(cb, w2_ref[...]) + jnp.concatenate([krs] * HEADS, axis=1)).astype(BF)
        vm_ref[...] = _dot(cb, w3_ref[...]).astype(BF)

    zk = z[:, C_K:C_K + FOX_W]
    zv = z[:, C_V:C_V + FOX_W]
    for h in range(HEADS):
        k_ref[pl.ds(h, tm, stride=HEADS), :] = zk[:, FOX_DIM * h:FOX_DIM * (h + 1)]
        v_ref[pl.ds(h, tm, stride=HEADS), :] = zv[:, FOX_DIM * h:FOX_DIM * (h + 1)]
    vb_ref[...] = zv.astype(BF)

    fqs = fcum * LOG2E
    fqs_ref[...] = fqs
    hi, mid, lo = _split3(fqs)
    comb = hi.astype(F32) + pltpu.roll(mid.astype(F32), HEADS, 1) + pltpu.roll(lo.astype(F32), 2 * HEADS, 1)
    aug = _dot(comb.astype(BF), place_ref[...]) + ones_ref[...]
    zq = z[:, C_Q:C_Q + FOX_W] * (FOX_DIM ** -0.5 * LOG2E)
    for p in range(N_PAIR):
        lo_, hi_ = 2 * LANES * p, 2 * LANES * p + LANES
        qx_ref[:, lo_:hi_] = zq[:, LANES * p:LANES * (p + 1)].astype(BF)
        qx_ref[:, hi_:hi_ + LANES] = aug[:, LANES * p:LANES * (p + 1)].astype(BF)
        kx_ref[:, lo_:hi_] = zk[:, LANES * p:LANES * (p + 1)].astype(BF)
        kx_ref[:, hi_:hi_ + LANES] = aug[:, FOX_W + LANES * p:FOX_W + LANES * (p + 1)].astype(BF)


def _proj(x2d, cos, sin, cosm, sinm, g, wa, bfp, gq, gkv, w1, w2, w3, place, ones, *, seq_len, tm, absorbed):
    n = x2d.shape[0]
    nt = n // tm
    if seq_len >= tm:
        tps = seq_len // tm
        tab_map = lambda i: (i % tps, 0)
    else:
        tab_map = lambda i: (0, 0)
    row = lambda w: pl.BlockSpec((tm, w), lambda i: (i, 0))
    if seq_len >= tm:
        small = lambda w: jax.ShapeDtypeStruct((n // seq_len, w, seq_len), F32)
        small_spec = lambda w: pl.BlockSpec((1, w, tm), lambda i: (i // tps, 0, i % tps))
    else:
        small = lambda w: jax.ShapeDtypeStruct((n, w), F32)
        small_spec = row
    if absorbed:
        mla_shapes = (jax.ShapeDtypeStruct((n, 2 * KV_LORA), BF), jax.ShapeDtypeStruct((HEADS, n, KV_LORA), BF),
                      jax.ShapeDtypeStruct((n, HEADS * MLA_ROPE), BF))
        mla_specs = (row(2 * KV_LORA), pl.BlockSpec((HEADS, tm, KV_LORA), lambda i: (0, i, 0)),
                     row(HEADS * MLA_ROPE))
    else:
        mla_shapes = (jax.ShapeDtypeStruct((n, HEADS * LANES), BF), jax.ShapeDtypeStruct((n, HEADS * LANES), BF),
                      jax.ShapeDtypeStruct((n, HEADS * MLA_VDIM), BF))
        mla_specs = (row(HEADS * LANES), row(HEADS * LANES), row(HEADS * MLA_VDIM))
    out_shape = (
        jax.ShapeDtypeStruct((n * HEADS, FOX_DIM), F32),
        jax.ShapeDtypeStruct((n * HEADS, FOX_DIM), F32),
        small(HEADS),
        jax.ShapeDtypeStruct((n, KV_LORA), F32),
        small(MLA_ROPE),
        jax.ShapeDtypeStruct((n, 2 * FOX_W), BF),
        jax.ShapeDtypeStruct((n, 2 * FOX_W), BF),
        jax.ShapeDtypeStruct((n, FOX_W), BF),
        *mla_shapes,
        jax.ShapeDtypeStruct((n, LANES), F32),
    )
    kv_spec = pl.BlockSpec((tm * HEADS, FOX_DIM), lambda i: (i, 0))
    out_specs = (kv_spec, kv_spec, small_spec(HEADS), row(KV_LORA), small_spec(MLA_ROPE), row(2 * FOX_W),
                 row(2 * FOX_W), row(FOX_W), *mla_specs, row(LANES))
    in_specs = [row(D_MODEL),
                pl.BlockSpec((tm, 2 * LANES), tab_map), pl.BlockSpec((tm, 2 * LANES), tab_map),
                pl.BlockSpec((tm, LANES), tab_map), pl.BlockSpec((tm, LANES), tab_map),
                _const_spec(g.shape), _const_spec(wa.shape), _const_spec(bfp.shape), _const_spec(gq.shape),
                _const_spec(gkv.shape), _const_spec(w1.shape), _const_spec(w2.shape), _const_spec(w3.shape),
                _const_spec(place.shape), _const_spec(ones.shape)]
    kern = functools.partial(_proj_kernel, tm=tm, seq_len=seq_len, absorbed=absorbed,
                             mla_scale=float((MLA_NOPE + MLA_ROPE) ** -0.5 * LOG2E))
    return pl.pallas_call(
        kern, grid=(nt,), in_specs=in_specs, out_specs=out_specs, out_shape=out_shape,
        scratch_shapes=[pltpu.VMEM((8, LANES), F32)],
        compiler_params=_params(1), name="proj")(
            x2d, cos, sin, cosm, sinm, g, wa, bfp, gq, gkv, w1, w2, w3, place, ones)


def _fox_prep_kernel(plt_ref, g_ref):
    nb, h, n = plt_ref.shape
    x = plt_ref[...].reshape(nb * h, n)
    lane = lax.broadcasted_iota(jnp.int32, x.shape, 1)
    shift = 1
    while shift < n:
        x = x + jnp.where(lane >= shift, pltpu.roll(x, shift, 1), 0.0)
        shift *= 2
    g_ref[...] = ((x - x[:, n - 1:n]) * LOG2E).reshape(nb, h, n)


def _fox_prep(plt, nb=4):
    b, h, p = plt.shape
    nb = nb if b % nb == 0 else 1
    return pl.pallas_call(
        _fox_prep_kernel, grid=(b // nb,),
        in_specs=[pl.BlockSpec((nb, h, p), lambda i: (i, 0, 0))],
        out_specs=pl.BlockSpec((nb, h, p), lambda i: (i, 0, 0)),
        out_shape=jax.ShapeDtypeStruct((b, h, p), F32),
        compiler_params=_params(1), name="fox_prep")(plt)


def _softmax_step(s, m_prev):
    m_new = jnp.maximum(m_prev, jnp.max(s, axis=-1, keepdims=True))
    p = jnp.exp2(s - _lanes(m_new, s.shape[1]))
    alpha = jnp.exp2(m_prev - m_new)
    return p.astype(BF), alpha, m_new


def _fox_qpad(qx, h, kind="fox"):
    p, hh = divmod(h, 2)
    lane = lax.broadcasted_iota(jnp.int32, (qx.shape[0], 2 * LANES), 1)
    blk = qx[:, 2 * LANES * p:2 * LANES * (p + 1)]
    if kind == "fox":
        keep = ((lane >= FOX_DIM * hh) & (lane < FOX_DIM * (hh + 1))) | (
            (lane >= LANES + AUG * h) & (lane < LANES + AUG * (h + 1)))
    else:
        keep = (lane >= LANES * hh) & (lane < LANES * (hh + 1))
    return jnp.where(keep, blk, jnp.zeros_like(blk))


def _fox_rows_step(s, vpa, m_ref, l_ref, h, r0, r1, start):
    if start:
        m_new = jnp.broadcast_to(jnp.max(s, axis=-1, keepdims=True), (r1 - r0, LANES))
        pv = _dot(jnp.exp2(s - _lanes(m_new, s.shape[1])).astype(BF), vpa)
        l_ref[h, r0:r1] = pv[:, LANES:]
        alpha = None
    else:
        pb, alpha, m_new = _softmax_step(s, m_ref[h, r0:r1])
        pv = _dot(pb, vpa)
        l_ref[h, r0:r1] = alpha * l_ref[h, r0:r1] + pv[:, LANES:]
    m_ref[h, r0:r1] = m_new
    return alpha, pv[:, :LANES]


def _fox_pair_acc(acc_ref, p, r0, r1, a, b, start):
    first = lax.broadcasted_iota(jnp.int32, (r1 - r0, LANES), 1) < FOX_DIM
    new = jnp.where(first, a[1], b[1])
    acc_ref[p, r0:r1] = new if start else acc_ref[p, r0:r1] * jnp.where(first, a[0], b[0]) + new


def _fox_prompt_tile(kx, vb, qpad_ref, m_ref, l_ref, acc_ref, start):
    tq = qpad_ref.shape[1]
    ones = jnp.ones((kx.shape[0], LANES), BF)

    def scores(h):
        return _dot_nt(qpad_ref[h], kx[:, 2 * LANES * (h // 2):2 * LANES * (h // 2 + 1)])

    s_next = scores(0)
    for p in range(N_PAIR):
        vpa = jnp.concatenate([vb[:, LANES * p:LANES * (p + 1)], ones], axis=1)
        res = []
        for hh in range(2):
            h = 2 * p + hh
            s = s_next
            if h + 1 < HEADS:
                s_next = scores(h + 1)
            res.append(_fox_rows_step(s, vpa, m_ref, l_ref, h, 0, tq, start))
        _fox_pair_acc(acc_ref, p, 0, tq, *res, start)


def _fox_diag_tile(kx, vb, qpad_ref, m_ref, l_ref, acc_ref, kind, start):
    tq = qpad_ref.shape[1]
    half = tq // 2
    ones = jnp.ones((tq, LANES), BF)
    col = lax.broadcasted_iota(jnp.int32, (half, half), 1)
    row = lax.broadcasted_iota(jnp.int32, (half, half), 0)
    if kind == "mla":
        shift = CHUNK.bit_length() - 1
        col, row = lax.shift_right_logical(col, shift), lax.shift_right_logical(row, shift)
    tri = col <= row

    def scores(h):
        kxp = kx[:, 2 * LANES * (h // 2):2 * LANES * (h // 2 + 1)]
        q = qpad_ref[h]
        return _dot_nt(q, kxp[:half]), _dot_nt(q[half:], kxp[half:])

    s_next = scores(0)
    for p in range(N_PAIR):
        vpa = jnp.concatenate([vb[:, LANES * p:LANES * (p + 1)], ones], axis=1)
        top, bot = [], []
        for hh in range(2):
            h = 2 * p + hh
            s_a, s_b = s_next
            if h + 1 < HEADS:
                s_next = scores(h + 1)
            s_top = jnp.where(tri, s_a[:half], NEG)
            s_bot = jnp.concatenate([s_a[half:], jnp.where(tri, s_b, NEG)], axis=1)
            top.append(_fox_rows_step(s_top, vpa[:half], m_ref, l_ref, h, 0, half, start))
            bot.append(_fox_rows_step(s_bot, vpa, m_ref, l_ref, h, half, tq, start))
        _fox_pair_acc(acc_ref, p, 0, half, *top, start)
        _fox_pair_acc(acc_ref, p, half, tq, *bot, start)


def _fox_prompt_kernel(qi_ref, kj_ref, qx_ref, kx_ref, vb_ref, o_ref, qpad_ref, m_ref, l_ref, acc_ref, *, kind):
    i = qi_ref[pl.program_id(1)]
    j = kj_ref[pl.program_id(1)]

    @pl.when(j == 0)
    def _():
        qx = qx_ref[...]
        for h in range(HEADS):
            qpad_ref[h] = _fox_qpad(qx, h, kind)

    for start in (True, False):
        is_start = (j == 0) if start else (j > 0)

        @pl.when((j < i) & is_start)
        def _(start=start):
            _fox_prompt_tile(kx_ref[...], vb_ref[...], qpad_ref, m_ref, l_ref, acc_ref, start)

        @pl.when((j == i) & is_start)
        def _(start=start):
            _fox_diag_tile(kx_ref[...], vb_ref[...], qpad_ref, m_ref, l_ref, acc_ref, kind, start)

    @pl.when(j == i)
    def _():
        first = lax.broadcasted_iota(jnp.int32, (qpad_ref.shape[1], LANES), 1) < FOX_DIM
        for p in range(N_PAIR):
            l_sel = jnp.where(first, l_ref[2 * p], l_ref[2 * p + 1])
            o_ref[:, LANES * p:LANES * (p + 1)] = (acc_ref[p] / l_sel).astype(o_ref.dtype)


def _fox_prompt(qx, kx, vb, *, batch, seq, t, kind):
    nq = seq // t
    assert CHUNK & (CHUNK - 1) == 0 and t % CHUNK == 0
    steps = [(i, j) for i in range(nq) for j in range(i + 1)]
    qi = jnp.asarray([s[0] for s in steps], jnp.int32)
    kj = jnp.asarray([s[1] for s in steps], jnp.int32)
    grid_spec = pltpu.PrefetchScalarGridSpec(
        num_scalar_prefetch=2, grid=(batch, len(steps)),
        in_specs=[pl.BlockSpec((t, 2 * FOX_W), lambda b, s, qi, kj: (b * nq + qi[s], 0)),
                  pl.BlockSpec((t, 2 * FOX_W), lambda b, s, qi, kj: (b * nq + kj[s], 0)),
                  pl.BlockSpec((t, FOX_W), lambda b, s, qi, kj: (b * nq + kj[s], 0))],
        out_specs=pl.BlockSpec((t, FOX_W), lambda b, s, qi, kj: (b * nq + qi[s], 0)),
        scratch_shapes=[pltpu.VMEM((HEADS, t, 2 * LANES), BF), pltpu.VMEM((HEADS, t, LANES), F32),
                        pltpu.VMEM((HEADS, t, LANES), F32), pltpu.VMEM((N_PAIR, t, LANES), F32)])
    return pl.pallas_call(
        functools.partial(_fox_prompt_kernel, kind=kind), grid_spec=grid_spec,
        out_shape=jax.ShapeDtypeStruct((batch * seq, FOX_W), BF),
        compiler_params=_params(2), name=kind + "_prompt")(qi, kj, qx, kx, vb)


def _fox_sample_kernel(qx_ref, kx_ref, vb_ref, fqs_ref, kct_ref, vct_ref, gk_ref, o_ref,
                       qbd_ref, qpad_ref, fq_ref, m_ref, l_ref, acc_ref, *, n_cache):
    j = pl.program_id(1)
    tq = qx_ref.shape[0]
    rows = HEADS * tq

    def update(s, pv_fn, g0=0, g1=2):
        a, b = 4 * tq * g0, 4 * tq * g1
        m_prev = m_ref[a:b]
        m_new = jnp.maximum(m_prev, jnp.max(s, axis=-1, keepdims=True))
        p = jnp.exp2(s - _lanes(m_new, s.shape[1]))
        alpha = jnp.exp2(m_prev - m_new)
        l_ref[a:b] = alpha * l_ref[a:b] + jnp.sum(p, axis=-1, keepdims=True)
        m_ref[a:b] = m_new
        pb = p.astype(BF)
        for pr in range(2 * g0, 2 * g1):
            r0, r1 = 2 * tq * pr, 2 * tq * (pr + 1)
            acc_ref[r0:r1] = acc_ref[r0:r1] * alpha[r0 - a:r1 - a] + pv_fn(pr, pb[r0 - a:r1 - a])

    @pl.when(j == 0)
    def _():
        qx = qx_ref[...]
        fqs = fqs_ref[...]
        qc = jnp.concatenate([qx[:, 2 * LANES * p:2 * LANES * p + LANES] for p in range(N_PAIR)], axis=1)
        lane = lax.broadcasted_iota(jnp.int32, qc.shape, 1)
        for h in range(HEADS):
            qbd_ref[h * tq:(h + 1) * tq, :] = jnp.where((lane // FOX_DIM) == h, qc, jnp.zeros_like(qc))
            qpad_ref[h] = _fox_qpad(qx, h)
            fq_ref[h * tq:(h + 1) * tq, :] = jnp.broadcast_to(fqs[:, h:h + 1], (tq, LANES))
        m_ref[...] = jnp.full(m_ref.shape, NEG, F32)
        l_ref[...] = jnp.zeros(l_ref.shape, F32)
        acc_ref[...] = jnp.zeros(acc_ref.shape, F32)
        kx = kx_ref[...]
        vb = vb_ref[...]
        tk = kx.shape[0]
        s = jnp.concatenate([_dot_nt(qpad_ref[h], kx[:, 2 * LANES * (h // 2):2 * LANES * (h // 2 + 1)])
                             for h in range(HEADS)], axis=0)
        row = lax.broadcasted_iota(jnp.int32, (rows, tk), 0) & (tq - 1)
        s = jnp.where(lax.broadcasted_iota(jnp.int32, (rows, tk), 1) <= row, s, NEG)
        update(s, lambda pr, pb: _dot(pb, vb[:, LANES * pr:LANES * (pr + 1)]))

    def cache_tile():
        tk = kct_ref.shape[3]
        gk = gk_ref[0]

        def scores(g):
            kt = kct_ref[0, 4 * g:4 * g + 4].reshape(4 * FOX_DIM, tk).astype(BF)
            gk_rows = jnp.concatenate([jnp.broadcast_to(gk[h:h + 1, :], (tq, tk)) for h in range(4 * g, 4 * g + 4)],
                                      axis=0)
            qg = qbd_ref[4 * tq * g:4 * tq * (g + 1), 4 * FOX_DIM * g:4 * FOX_DIM * (g + 1)]
            return _dot(qg, kt) + (_lanes(fq_ref[4 * tq * g:4 * tq * (g + 1)], tk) - gk_rows)

        def pv(pr, pb):
            return _dot_nt(pb, vct_ref[0, 2 * pr:2 * pr + 2].reshape(2 * FOX_DIM, tk).astype(BF))

        s0 = scores(0)
        s1 = scores(1)
        update(s0, pv, 0, 1)
        update(s1, pv, 1, 2)

    cache_tile()

    @pl.when(j == n_cache - 1)
    def _():
        first = lax.broadcasted_iota(jnp.int32, (tq, LANES), 1) < FOX_DIM
        o = acc_ref[...] / l_ref[...]
        for pr in range(N_PAIR):
            o_ref[:, LANES * pr:LANES * (pr + 1)] = jnp.where(
                first, o[2 * tq * pr:2 * tq * pr + tq], o[2 * tq * pr + tq:2 * tq * (pr + 1)]).astype(o_ref.dtype)


def _fox_sample(qx, kx, vb, fqs, kct, vct, gk, *, batch, seq, tkc):
    past = kct.shape[3]
    n_cache = past // tkc
    kern = functools.partial(_fox_sample_kernel, n_cache=n_cache)
    row = lambda w: pl.BlockSpec((seq, w), lambda b, j: (b, 0))
    return pl.pallas_call(
        kern, grid=(batch, n_cache),
        in_specs=[row(2 * FOX_W), row(2 * FOX_W), row(FOX_W), row(LANES),
                  pl.BlockSpec((1, HEADS, FOX_DIM, tkc), lambda b, j: (b, 0, 0, j)),
                  pl.BlockSpec((1, HEADS, FOX_DIM, tkc), lambda b, j: (b, 0, 0, j)),
                  pl.BlockSpec((1, HEADS, tkc), lambda b, j: (b, 0, j))],
        out_specs=row(FOX_W),
        out_shape=jax.ShapeDtypeStruct((batch * seq, FOX_W), BF),
        scratch_shapes=[pltpu.VMEM((HEADS * seq, FOX_W), BF), pltpu.VMEM((HEADS, seq, 2 * LANES), BF)]
        + [pltpu.VMEM((HEADS * seq, LANES), F32)] * 4,
        compiler_params=_params(2), name="fox_sample")(qx, kx, vb, fqs, kct, vct, gk)


def _chunk_mask(shape, tq, row0, q0, k0):
    assert tq & (tq - 1) == 0 and CHUNK & (CHUNK - 1) == 0
    shift = CHUNK.bit_length() - 1
    qpos = q0 + ((row0 + lax.broadcasted_iota(jnp.int32, shape, 0)) & (tq - 1))
    kpos = k0 + lax.broadcasted_iota(jnp.int32, shape, 1)
    return lax.shift_right_logical(kpos, shift) <= lax.shift_right_logical(qpos, shift)


def _mla_rows_update(s, val, m_ref, l_ref, acc_ref, r0, r1):
    m_prev = m_ref[r0:r1]
    m_new = jnp.maximum(m_prev, jnp.max(s, axis=-1, keepdims=True))
    p = jnp.exp2(s - _lanes(m_new, s.shape[1]))
    alpha = jnp.exp2(m_prev - m_new)
    l_ref[r0:r1] = alpha * l_ref[r0:r1] + jnp.sum(p, axis=-1, keepdims=True)
    m_ref[r0:r1] = m_new
    acc_ref[r0:r1] = acc_ref[r0:r1] * _lanes(alpha, KV_LORA) + _dot(p.astype(BF), val)


def _mla_init_stats(m_ref, l_ref, acc_ref):
    m_ref[...] = jnp.full(m_ref.shape, NEG, F32)
    l_ref[...] = jnp.zeros(l_ref.shape, F32)
    acc_ref[...] = jnp.zeros(acc_ref.shape, F32)


def _mla_finalize(o_ref, wuv_ref, l_ref, acc_ref, tq):
    olat = (acc_ref[...] / _lanes(l_ref[...], KV_LORA)).astype(BF)
    for g in range(2):
        out = None
        for h in range(4 * g, 4 * g + 4):
            part = _dot(olat[h * tq:(h + 1) * tq, :], wuv_ref[h])
            out = part if out is None else out + part
        o_ref[:, 4 * MLA_VDIM * g:4 * MLA_VDIM * (g + 1)] = out.astype(o_ref.dtype)


def _mla_scratch(rows):
    return [pltpu.VMEM((rows, LANES), F32), pltpu.VMEM((rows, LANES), F32), pltpu.VMEM((rows, KV_LORA), F32)]


def _mla_sample_kernel(qlat_ref, qrope_ref, mkey_ref, latc_ref, krt_ref, wuv_ref, o_ref,
                       ql_ref, qr_ref, m_ref, l_ref, acc_ref, *, n_cache, tq, past, rc):
    j = pl.program_id(1)
    rows = HEADS * tq

    @pl.when(j == 0)
    def _():
        qrope = qrope_ref[...]
        for h in range(HEADS):
            ql_ref[h * tq:(h + 1) * tq, :] = qlat_ref[h]
            qr_ref[h * tq:(h + 1) * tq, :] = qrope[:, MLA_ROPE * h:MLA_ROPE * (h + 1)]
        _mla_init_stats(m_ref, l_ref, acc_ref)
        mk = mkey_ref[...]
        s = _dot_nt(ql_ref[...], mk[:, :KV_LORA]) + _dot_nt(qr_ref[...], mk[:, KV_LORA:KV_LORA + MLA_ROPE])
        if past % CHUNK != 0 or tq > CHUNK:
            s = jnp.where(_chunk_mask(s.shape, tq, 0, past, past), s, NEG)
        _mla_rows_update(s, mk[:, :KV_LORA], m_ref, l_ref, acc_ref, 0, rows)

    def cache_tile():
        c = latc_ref[0].astype(BF)
        krt = krt_ref[0].astype(BF)
        def scores(r0):
            return _dot_nt(ql_ref[r0:r0 + rc], c) + _dot(qr_ref[r0:r0 + rc], krt)

        s_next = scores(0)
        for r0 in range(0, rows, rc):
            s = s_next
            if r0 + rc < rows:
                s_next = scores(r0 + rc)
            _mla_rows_update(s, c, m_ref, l_ref, acc_ref, r0, r0 + rc)

    cache_tile()

    @pl.when(j == n_cache - 1)
    def _():
        _mla_finalize(o_ref, wuv_ref, l_ref, acc_ref, tq)


def _mla_sample(qlat, qrope, mkey, latc, krt, wuvp, *, batch, seq, tkc, rc):
    past = latc.shape[1]
    n_cache = past // tkc
    kern = functools.partial(_mla_sample_kernel, n_cache=n_cache, tq=seq, past=past, rc=rc)
    return pl.pallas_call(
        kern, grid=(batch, n_cache),
        in_specs=[pl.BlockSpec((HEADS, seq, KV_LORA), lambda b, j: (0, b, 0)),
                  pl.BlockSpec((seq, HEADS * MLA_ROPE), lambda b, j: (b, 0)),
                  pl.BlockSpec((seq, 2 * KV_LORA), lambda b, j: (b, 0)),
                  pl.BlockSpec((1, tkc, KV_LORA), lambda b, j: (b, j, 0)),
                  pl.BlockSpec((1, MLA_ROPE, tkc), lambda b, j: (b, 0, j)),
                  _const_spec(wuvp.shape)],
        out_specs=pl.BlockSpec((seq, FOX_W), lambda b, j: (b, 0)),
        out_shape=jax.ShapeDtypeStruct((batch * seq, FOX_W), BF),
        scratch_shapes=[pltpu.VMEM((HEADS * seq, KV_LORA), BF), pltpu.VMEM((HEADS * seq, MLA_ROPE), BF)]
        + _mla_scratch(HEADS * seq),
        compiler_params=_params(2), name="mla_sample")(qlat, qrope, mkey, latc, krt, wuvp)


def _ffn_kernel(x_ref, fox_ref, mla_ref, pconv_ref, wo_ref, gf_ref, wup_ref, cw_ref, wdn_ref, gfin_ref,
                y_ref, cst_ref, hn_ref, act_ref, ush_ref, carry_ref, *, tm, seq_len):
    n_seg = max(1, tm // seq_len)
    seg = tm // n_seg
    mixed = jnp.concatenate([fox_ref[...], mla_ref[...]], axis=1)
    x1 = x_ref[...] + _dot(mixed, wo_ref[...])
    hn_ref[...] = (x1 * lax.rsqrt(jnp.mean(x1 * x1, axis=-1, keepdims=True) + EPS) * gf_ref[...]).astype(BF)
    y_ref[...] = x1

    if n_seg == 1:
        @pl.when(pl.program_id(0) % (seq_len // tm) == 0)
        def _():
            carry_ref[...] = pconv_ref[0]

    def conv(u, col, slot):
        cols = slice(col, col + FF_CHUNK)
        for s in range(n_seg):
            ush_ref[slot, s, 6:8, :] = carry_ref[:, cols] if n_seg == 1 else pconv_ref[s, :, cols]
            ush_ref[slot, s, 8:8 + seg, :] = u[seg * s:seg * (s + 1), :]
            cst_ref[s, :, cols] = u[seg * (s + 1) - 2:seg * (s + 1), :]
        if n_seg == 1:
            carry_ref[:, cols] = u[tm - 2:tm, :]
        u1 = jnp.concatenate([ush_ref[slot, s, 7:7 + seg, :] for s in range(n_seg)], axis=0)
        u2 = jnp.concatenate([ush_ref[slot, s, 6:6 + seg, :] for s in range(n_seg)], axis=0)
        cw = cw_ref[:, cols]
        return cw[3:4, :] + cw[0:1, :] * u2 + cw[1:2, :] * u1 + cw[2:3, :] * u

    def up(c):
        hn = hn_ref[...]
        g0, v0 = FF_CHUNK * c, D_FF + FF_CHUNK * c
        return _dot(hn, wup_ref[:, g0:g0 + FF_CHUNK]), _dot(hn, wup_ref[:, v0:v0 + FF_CHUNK])

    u_next = up(0)
    for c in range(N_FF):
        ug, uv = u_next
        if c + 1 < N_FF:
            u_next = up(c + 1)
        gate = conv(ug, FF_CHUNK * c, 2 * (c % 2))
        val = conv(uv, D_FF + FF_CHUNK * c, 2 * (c % 2) + 1)
        act_ref[:, FF_CHUNK * c:FF_CHUNK * (c + 1)] = (gate * jax.nn.sigmoid(gate) * val).astype(BF)

    x2 = y_ref[...] + _dot(act_ref[...], wdn_ref[...])
    y_ref[...] = x2 * lax.rsqrt(jnp.mean(x2 * x2, axis=-1, keepdims=True) + EPS) * gfin_ref[...]


def _ffn(x2d, fox, mla, pconv, wo, gf, wup, cw, wdn, gfin, *, seq_len, tm):
    n = x2d.shape[0]
    nt = n // tm
    n_seq = n // seq_len
    n_seg = max(1, tm // seq_len)
    if n_seg == 1:
        tps = seq_len // tm
        st_map = lambda i: (i // tps, 0, 0)
    else:
        st_map = lambda i: (i, 0, 0)
    row = lambda w: pl.BlockSpec((tm, w), lambda i: (i, 0))
    kern = functools.partial(_ffn_kernel, tm=tm, seq_len=seq_len)
    return pl.pallas_call(
        kern, grid=(nt,),
        in_specs=[row(D_MODEL), row(FOX_W), row(FOX_W),
                  pl.BlockSpec((n_seg, CONV_W - 1, 2 * D_FF), st_map),
                  _const_spec(wo.shape), _const_spec(gf.shape), _const_spec(wup.shape), _const_spec(cw.shape),
                  _const_spec(wdn.shape), _const_spec(gfin.shape)],
        out_specs=(row(D_MODEL), pl.BlockSpec((n_seg, CONV_W - 1, 2 * D_FF), st_map)),
        out_shape=(jax.ShapeDtypeStruct((n, D_MODEL), F32),
                   jax.ShapeDtypeStruct((n_seq, CONV_W - 1, 2 * D_FF), F32)),
        scratch_shapes=[pltpu.VMEM((tm, D_MODEL), BF), pltpu.VMEM((tm, D_FF), BF),
                        pltpu.VMEM((4, n_seg, 8 + tm // n_seg, FF_CHUNK), F32),
                        pltpu.VMEM((CONV_W - 1, 2 * D_FF), F32)],
        compiler_params=_params(1), name="ffn")(x2d, fox, mla, pconv, wo, gf, wup, cw, wdn, gfin)


def _rope_tables(start, n, reps=1):
    half = MLA_ROPE // 2
    inv = ROPE_THETA ** (-np.arange(half, dtype=np.float64) / half)
    ang = (start + np.arange(n, dtype=np.float64))[:, None] * inv[None, :]
    cos = np.tile(np.cos(ang), (1, 2 * HEADS))
    sin = np.tile(np.sin(ang), (1, 2 * HEADS))
    ones = np.ones((n, MLA_NOPE))
    zeros = np.zeros((n, LANES - MLA_NOPE - MLA_ROPE))
    cosm = np.concatenate([ones, cos[:, :MLA_ROPE], zeros], axis=1)
    sinm = np.concatenate([0.0 * ones, sin[:, :MLA_ROPE], zeros], axis=1)
    return [jnp.asarray(np.tile(a, (reps, 1)), F32) for a in (cos, sin, cosm, sinm)]


def _placement():
    place = np.zeros((LANES, 2 * N_PAIR * LANES), np.float32)
    ones = np.zeros((1, 2 * N_PAIR * LANES), np.float32)
    for h in range(HEADS):
        for i in range(3):
            for p in range(N_PAIR):
                place[HEADS * i + h, LANES * p + AUG * h + i] = 1.0
                ones[0, LANES * p + AUG * h + 3 + i] = 1.0
                ones[0, N_PAIR * LANES + LANES * p + AUG * h + i] = 1.0
                place[HEADS * i + h, N_PAIR * LANES + LANES * p + AUG * h + 3 + i] = -1.0
    rep = np.zeros((LANES, HEADS * MLA_ROPE), np.float32)
    for h in range(HEADS):
        for r in range(MLA_ROPE):
            rep[r, MLA_ROPE * h + r] = 1.0
    return jnp.asarray(place, BF), jnp.asarray(ones, F32), jnp.asarray(rep, BF)


def _rot_cols(w):
    half = MLA_ROPE // 2
    return jnp.concatenate([-w[..., half:], w[..., :half]], axis=-1)


def _pad_cols(w, width):
    return jnp.pad(w, ((0, 0), (0, width - w.shape[1])))


def _layer_weights(w_in, b_f, w_q_up, w_uk, w_uv, w_out, w_up, conv_w, conv_b, w_down):
    wq, wk, wv, wf, wqc, wckv, wkr = jnp.split(w_in, IN_SPLITS, axis=1)
    wa = jnp.concatenate([wq, wk, wv, wqc, wckv, _pad_cols(wkr, LANES), _pad_cols(_rot_cols(wkr), LANES),
                          _pad_cols(wf, LANES)], axis=1).astype(BF)
    bfp = _pad_cols(b_f[None, :], LANES).astype(F32)
    wq3 = w_q_up.reshape(Q_LORA, HEADS, MLA_NOPE + MLA_ROPE)
    wq_nope = wq3[:, :, :MLA_NOPE].reshape(Q_LORA, HEADS * MLA_NOPE)
    wq_rope = wq3[:, :, MLA_NOPE:]
    wq2 = jnp.concatenate([wq_nope, wq_rope.reshape(Q_LORA, -1), _rot_cols(wq_rope).reshape(Q_LORA, -1)],
                          axis=1).astype(BF)
    eye = jnp.eye(4, dtype=F32)
    wk = jnp.transpose(w_uk, (1, 2, 0)).reshape(2, 4, 1, MLA_NOPE, KV_LORA)
    wukp = (wk * eye[None, :, :, None, None]).reshape(HEADS, 4 * MLA_NOPE, KV_LORA)
    wv = jnp.transpose(w_uv, (1, 0, 2)).reshape(2, 4, KV_LORA, 1, MLA_VDIM)
    wuvp = (wv * eye[None, :, None, :, None]).reshape(HEADS, KV_LORA, 4 * MLA_VDIM)
    wup = w_up.astype(BF)
    cw = jnp.concatenate([conv_w, conv_b[None, :]], axis=0)
    wdn = w_down.astype(BF)
    zpad = ((0, 0), (0, 0), (0, LANES - MLA_NOPE - MLA_ROPE))
    wq_a = jnp.pad(wq3, zpad).reshape(Q_LORA, HEADS * LANES)
    wq_b = jnp.pad(jnp.concatenate([jnp.zeros_like(wq3[:, :, :MLA_NOPE]), _rot_cols(wq_rope)], axis=2), zpad)
    wqm = jnp.concatenate([wq_a, wq_b.reshape(Q_LORA, HEADS * LANES)], axis=1).astype(BF)
    wkm = jnp.pad(w_uk, ((0, 0), (0, 0), (0, LANES - MLA_NOPE))).reshape(KV_LORA, HEADS * LANES).astype(BF)
    wvm = w_uv.reshape(KV_LORA, HEADS * MLA_VDIM).astype(BF)
    return wa, bfp, wq2, wukp.astype(BF), wuvp.astype(BF), w_out.astype(BF), wup, cw, wdn, wqm, wkm, wvm


def kernel(x_prompt, x_sample, cache_fox_k, cache_fox_v, cache_fox_logf, cache_mla_latent, cache_mla_krope,
           state_ffn_conv, attn_norm, w_in, b_forget, q_norm, w_q_up, kv_norm, w_uk, w_uv, w_out, ffn_norm,
           w_up, conv_w, conv_b, w_down, final_norm):
    assert attn_norm.shape[0] == 1, "single-layer stack"
    bp, tp, _ = x_prompt.shape
    bs, ts, _ = x_sample.shape
    past = cache_fox_k.shape[2]
    assert ts == CHUNK and past % CHUNK == 0

    wa, bfp, wq2, wukp, wuvp, wo, wup, cw, wdn, wqm, wkm, wvm = _layer_weights(
        w_in[0], b_forget[0], w_q_up[0], w_uk[0], w_uv[0], w_out[0], w_up[0], conv_w[0], conv_b[0], w_down[0])
    place, ones, rep = _placement()
    g_attn = attn_norm[0][None, :]
    g_q = q_norm[0][None, :]
    g_kv = kv_norm[0][None, :]
    g_ffn = ffn_norm[0][None, :]
    g_fin = final_norm[None, :]

    tm = ROW_TILE
    assert (bp * tp) % tm == 0 and (bs * ts) % tm == 0 and tp % ATTN_TILE == 0
    outs = []
    for grp in ("prompt", "sample"):
        if grp == "prompt":
            x2d = x_prompt.reshape(bp * tp, D_MODEL)
            b, t = bp, tp
            tabs = _rope_tables(0, tp)
            pconv = jnp.zeros((bp, CONV_W - 1, 2 * D_FF), F32)
        else:
            x2d = x_sample.reshape(bs * ts, D_MODEL)
            b, t = bs, ts
            tabs = _rope_tables(past, ts, reps=tm // ts)
            pconv = state_ffn_conv[0]

        absorbed = grp == "sample"
        wm = (wq2, wukp, rep) if absorbed else (wqm, wkm, wvm)
        (k, v, logf, lat, kr, qx, kx, vb, m1, m2, m3, fqs) = _proj(
            x2d, *tabs, g_attn, wa, bfp, g_q, g_kv, *wm, place, ones, seq_len=t, tm=tm, absorbed=absorbed)

        if grp == "prompt":
            fox = _fox_prompt(qx, kx, vb, batch=b, seq=t, t=ATTN_TILE, kind="fox")
            mla = _fox_prompt(m1, m2, m3, batch=b, seq=t, t=ATTN_TILE, kind="mla")
        else:
            mkey, qlat, qrope = m1, m2, m3
            gk = _fox_prep(jnp.transpose(cache_fox_logf[0], (0, 2, 1)))
            kct = jnp.transpose(cache_fox_k[0], (0, 2, 3, 1))
            vct = jnp.transpose(cache_fox_v[0], (0, 2, 3, 1))
            krt = jnp.transpose(cache_mla_krope[0], (0, 2, 1))
            assert past % min(past, FOX_CACHE_TILE) == 0 and past % min(past, MLA_CACHE_TILE) == 0
            fox = _fox_sample(qx, kx, vb, fqs, kct, vct, gk, batch=b, seq=t, tkc=min(past, FOX_CACHE_TILE))
            mla = _mla_sample(qlat, qrope, mkey, cache_mla_latent[0], krt, wuvp, batch=b, seq=t,
                              tkc=min(past, MLA_CACHE_TILE), rc=MLA_ROW_CHUNK)

        def small_state(a, w, b=b, t=t):
            return (jnp.transpose(a, (0, 2, 1)) if a.ndim == 3 else a.reshape(b, t, w))[None]

        y, cst = _ffn(x2d, fox, mla, pconv, wo, g_ffn, wup, cw, wdn, g_fin, seq_len=t, tm=tm)
        outs.append((y.reshape(b, t, D_MODEL),
                     k.reshape(1, b, t, HEADS, FOX_DIM), v.reshape(1, b, t, HEADS, FOX_DIM),
                     small_state(logf, HEADS), lat.reshape(1, b, t, KV_LORA), small_state(kr, MLA_ROPE),
                     cst[None]))
    (yp, *st_p), (ys, *st_s) = outs
    return (yp, ys, *st_p, *st_s)
```

```python
import functools

import numpy as np
import jax
import jax.numpy as jnp
from jax import lax
from jax.experimental import pallas as pl
from jax.experimental.pallas import tpu as pltpu

D_MODEL = 1024
CHUNK = 64
HEADS = 8
FOX_DIM = 64
MLA_NOPE = 64
MLA_ROPE = 32
MLA_VDIM = 64
Q_LORA = 384
KV_LORA = 256
D_FF = 2816
CONV_W = 3
ROPE_THETA = 10000.0
EPS = 1e-6
NEG = -1e30
LOG2E = 1.4426950408889634

FOX_W = HEADS * FOX_DIM
IN_SIZES = [FOX_W, FOX_W, FOX_W, HEADS, Q_LORA, KV_LORA, MLA_ROPE]
IN_SPLITS = [int(s) for s in np.cumsum(IN_SIZES)[:-1]]

LANES = 128
MXU_TILE = 256
AUG = 16
N_PAIR = HEADS // 2
FF_CHUNK = MXU_TILE
N_FF = D_FF // FF_CHUNK
VMEM_LIMIT = 56 * 1024 * 1024

ROW_TILE = 512
ATTN_TILE = 512
FOX_CACHE_TILE = 4096
MLA_CACHE_TILE = 2048
MLA_ROW_CHUNK = 256

C_Q, C_K, C_V = 0, FOX_W, 2 * FOX_W
C_QC = 3 * FOX_W
C_CKV = C_QC + Q_LORA
C_KR = C_CKV + KV_LORA
C_KRR = C_KR + LANES
C_F = C_KRR + LANES
N_A = C_F + LANES

BF = jnp.bfloat16
F32 = jnp.float32


def _dot(a, b):
    return jnp.dot(a, b, preferred_element_type=F32)


def _dot_nt(a, b):
    return lax.dot_general(a, b, (((1,), (1,)), ((), ())), preferred_element_type=F32)


def _split3(x):
    hi = x.astype(BF)
    r1 = x - hi.astype(F32)
    mid = r1.astype(BF)
    lo = (r1 - mid.astype(F32)).astype(BF)
    return hi, mid, lo


def _log_sigmoid(x):
    return jnp.minimum(x, 0.0) - jnp.log1p(jnp.exp(-jnp.abs(x)))


def _lanes(x, n):
    if n % LANES == 0:
        return x if n == LANES else jnp.concatenate([x] * (n // LANES), axis=1)
    assert n < LANES
    return x[:, :n]


def _params(n_axes):
    return pltpu.CompilerParams(dimension_semantics=("arbitrary",) * n_axes, vmem_limit_bytes=VMEM_LIMIT)


def _const_spec(shape):
    nd = len(shape)
    return pl.BlockSpec(shape, lambda *_: (0,) * nd, pipeline_mode=pl.Buffered(1))


def _proj_kernel(x_ref, cos_ref, sin_ref, cosm_ref, sinm_ref, g_ref, wa_ref, bf_ref, gq_ref, gkv_ref,
                 w1_ref, w2_ref, w3_ref, place_ref, ones_ref,
                 k_ref, v_ref, logf_ref, lat_ref, kr_ref, qx_ref, kx_ref, vb_ref, o1_ref, o2_ref, o3_ref,
                 fqs_ref, carry_ref, *, tm, seq_len, mla_scale, absorbed):
    x = x_ref[...]
    ms = jnp.mean(x * x, axis=-1, keepdims=True)
    hn = (x * lax.rsqrt(ms + EPS) * g_ref[...]).astype(BF)
    zs = _dot(hn, wa_ref[:, C_QC:])
    z = _dot(hn, wa_ref[:, :C_QC])

    lane = lax.broadcasted_iota(jnp.int32, (tm, LANES), 1)
    logf = _log_sigmoid(zs[:, C_F - C_QC:C_F - C_QC + LANES] + bf_ref[...])
    logf = jnp.where(lane < HEADS, logf, 0.0)
    logf_t = logf.T[:HEADS, :]
    logf_ref[0] = logf_t
    seg = min(seq_len, tm)
    assert seg & (seg - 1) == 0
    pos = lax.broadcasted_iota(jnp.int32, (HEADS, tm), 1) & (seg - 1)
    fc = logf_t
    shift = 1
    while shift < seg:
        fc = fc + jnp.where(pos >= shift, pltpu.roll(fc, shift, 1), 0.0)
        shift *= 2
    if seq_len > tm:
        @pl.when(pl.program_id(0) % (seq_len // tm) == 0)
        def _():
            carry_ref[...] = jnp.zeros_like(carry_ref)
        fc = fc + carry_ref[:, 0:1]
        carry_ref[...] = jnp.broadcast_to(fc[:, tm - 1:tm], carry_ref.shape)
    fcum = jnp.concatenate([fc, jnp.zeros((LANES - HEADS, tm), F32)], axis=0).T

    zc = zs[:, C_CKV - C_QC:C_CKV - C_QC + KV_LORA]
    ckv = zc * lax.rsqrt(jnp.mean(zc * zc, axis=-1, keepdims=True) + EPS) * gkv_ref[...]
    lat_ref[...] = ckv
    cos = cos_ref[...]
    sin = sin_ref[...]
    krb = (zs[:, C_KR - C_QC:C_KR - C_QC + LANES] * cos[:, :LANES]
           + zs[:, C_KRR - C_QC:C_KRR - C_QC + LANES] * sin[:, :LANES])
    kr_ref[0] = krb.T[:MLA_ROPE, :]
    cb = ckv.astype(BF)

    zqc = zs[:, :Q_LORA]
    qcn = (zqc * lax.rsqrt(jnp.mean(zqc * zqc, axis=-1, keepdims=True) + EPS) * gq_ref[...]).astype(BF)
    zq2 = _dot(qcn, w1_ref[...])
    if absorbed:
        mkey_ref, qlat_ref, qrope_ref = o1_ref, o2_ref, o3_ref
        mkey_ref[:, :KV_LORA] = cb
        mkey_ref[:, KV_LORA:] = _dot(krb.astype(BF), w3_ref[...]).astype(BF)
        nr = HEADS * MLA_ROPE
        qr = zq2[:, FOX_W:FOX_W + nr] * cos + zq2[:, FOX_W + nr:FOX_W + 2 * nr] * sin
        qrope_ref[...] = (qr * mla_scale).astype(BF)
        for h in range(HEADS):
            g = h // 4
            qn = zq2[:, 4 * MLA_NOPE * g:4 * MLA_NOPE * (g + 1)].astype(BF)
            qlat_ref[h] = (_dot(qn, w2_ref[h]) * mla_scale).astype(BF)
    else:
        qm_ref, km_ref, vm_ref = o1_ref, o2_ref, o3_ref
        hw = HEADS * LANES
        cosm = jnp.concatenate([cosm_ref[...]] * HEADS, axis=1)
        sinm = jnp.concatenate([sinm_ref[...]] * HEADS, axis=1)
        qm_ref[...] = ((zq2[:, :hw] * cosm + zq2[:, hw:] * sinm) * mla_scale).astype(BF)
        krs = pltpu.roll(krb, MLA_NOPE, 1)
        km_ref[...] = (_dot(cb, w2_ref[...]) + jnp.concatenate([krs] * HEADS, axis=1)).astype(BF)
        vm_ref[...] = _dot(cb, w3_ref[...]).astype(BF)

    zk = z[:, C_K:C_K + FOX_W]
    zv = z[:, C_V:C_V + FOX_W]
    for h in range(HEADS):
        k_ref[pl.ds(h, tm, stride=HEADS), :] = zk[:, FOX_DIM * h:FOX_DIM * (h + 1)]
        v_ref[pl.ds(h, tm, stride=HEADS), :] = zv[:, FOX_DIM * h:FOX_DIM * (h + 1)]
    vb_ref[...] = zv.astype(BF)

    fqs = fcum * LOG2E
    fqs_ref[...] = fqs
    hi, mid, lo = _split3(fqs)
    comb = hi.astype(F32) + pltpu.roll(mid.astype(F32), HEADS, 1) + pltpu.roll(lo.astype(F32), 2 * HEADS, 1)
    aug = _dot(comb.astype(BF), place_ref[...]) + ones_ref[...]
    zq = z[:, C_Q:C_Q + FOX_W] * (FOX_DIM ** -0.5 * LOG2E)
    for p in range(N_PAIR):
        lo_, hi_ = 2 * LANES * p, 2 * LANES * p + LANES
        qx_ref[:, lo_:hi_] = zq[:, LANES * p:LANES * (p + 1)].astype(BF)
        qx_ref[:, hi_:hi_ + LANES] = aug[:, LANES * p:LANES * (p + 1)].astype(BF)
        kx_ref[:, lo_:hi_] = zk[:, LANES * p:LANES * (p + 1)].astype(BF)
        kx_ref[:, hi_:hi_ + LANES] = aug[:, FOX_W + LANES * p:FOX_W + LANES * (p + 1)].astype(BF)


def _proj(x2d, cos, sin, cosm, sinm, g, wa, bfp, gq, gkv, w1, w2, w3, place, ones, *, seq_len, tm, absorbed):
    n = x2d.shape[0]
    nt = n // tm
    if seq_len >= tm:
        tps = seq_len // tm
        tab_map = lambda i: (i % tps, 0)
    else:
        tab_map = lambda i: (0, 0)
    row = lambda w: pl.BlockSpec((tm, w), lambda i: (i, 0))
    if seq_len >= tm:
        small = lambda w: jax.ShapeDtypeStruct((n // seq_len, w, seq_len), F32)
        small_spec = lambda w: pl.BlockSpec((1, w, tm), lambda i: (i // tps, 0, i % tps))
    else:
        small = lambda w: jax.ShapeDtypeStruct((nt, w, tm), F32)
        small_spec = lambda w: pl.BlockSpec((1, w, tm), lambda i: (i, 0, 0))
    if absorbed:
        mla_shapes = (jax.ShapeDtypeStruct((n, 2 * KV_LORA), BF), jax.ShapeDtypeStruct((HEADS, n, KV_LORA), BF),
                      jax.ShapeDtypeStruct((n, HEADS * MLA_ROPE), BF))
        mla_specs = (row(2 * KV_LORA), pl.BlockSpec((HEADS, tm, KV_LORA), lambda i: (0, i, 0)),
                     row(HEADS * MLA_ROPE))
    else:
        mla_shapes = (jax.ShapeDtypeStruct((n, HEADS * LANES), BF), jax.ShapeDtypeStruct((n, HEADS * LANES), BF),
                      jax.ShapeDtypeStruct((n, HEADS * MLA_VDIM), BF))
        mla_specs = (row(HEADS * LANES), row(HEADS * LANES), row(HEADS * MLA_VDIM))
    out_shape = (
        jax.ShapeDtypeStruct((n * HEADS, FOX_DIM), F32),
        jax.ShapeDtypeStruct((n * HEADS, FOX_DIM), F32),
        small(HEADS),
        jax.ShapeDtypeStruct((n, KV_LORA), F32),
        small(MLA_ROPE),
        jax.ShapeDtypeStruct((n, 2 * FOX_W), BF),
        jax.ShapeDtypeStruct((n, 2 * FOX_W), BF),
        jax.ShapeDtypeStruct((n, FOX_W), BF),
        *mla_shapes,
        jax.ShapeDtypeStruct((n, LANES), F32),
    )
    kv_spec = pl.BlockSpec((tm * HEADS, FOX_DIM), lambda i: (i, 0))
    out_specs = (kv_spec, kv_spec, small_spec(HEADS), row(KV_LORA), small_spec(MLA_ROPE), row(2 * FOX_W),
                 row(2 * FOX_W), row(FOX_W), *mla_specs, row(LANES))
    in_specs = [row(D_MODEL),
                pl.BlockSpec((tm, 2 * LANES), tab_map), pl.BlockSpec((tm, 2 * LANES), tab_map),
                pl.BlockSpec((tm, LANES), tab_map), pl.BlockSpec((tm, LANES), tab_map),
                _const_spec(g.shape), _const_spec(wa.shape), _const_spec(bfp.shape), _const_spec(gq.shape),
                _const_spec(gkv.shape), _const_spec(w1.shape), _const_spec(w2.shape), _const_spec(w3.shape),
                _const_spec(place.shape), _const_spec(ones.shape)]
    kern = functools.partial(_proj_kernel, tm=tm, seq_len=seq_len, absorbed=absorbed,
                             mla_scale=float((MLA_NOPE + MLA_ROPE) ** -0.5 * LOG2E))
    return pl.pallas_call(
        kern, grid=(nt,), in_specs=in_specs, out_specs=out_specs, out_shape=out_shape,
        scratch_shapes=[pltpu.VMEM((8, LANES), F32)],
        compiler_params=_params(1), name="proj")(
            x2d, cos, sin, cosm, sinm, g, wa, bfp, gq, gkv, w1, w2, w3, place, ones)


def _fox_prep_kernel(plt_ref, g_ref):
    nb, h, n = plt_ref.shape
    x = plt_ref[...].reshape(nb * h, n)
    lane = lax.broadcasted_iota(jnp.int32, x.shape, 1)
    shift = 1
    while shift < n:
        x = x + jnp.where(lane >= shift, pltpu.roll(x, shift, 1), 0.0)
        shift *= 2
    g_ref[...] = ((x - x[:, n - 1:n]) * LOG2E).reshape(nb, h, n)


def _fox_prep(plt, nb=4):
    b, h, p = plt.shape
    nb = nb if b % nb == 0 else 1
    return pl.pallas_call(
        _fox_prep_kernel, grid=(b // nb,),
        in_specs=[pl.BlockSpec((nb, h, p), lambda i: (i, 0, 0))],
        out_specs=pl.BlockSpec((nb, h, p), lambda i: (i, 0, 0)),
        out_shape=jax.ShapeDtypeStruct((b, h, p), F32),
        compiler_params=_params(1), name="fox_prep")(plt)


def _softmax_step(s, m_prev):
    m_new = jnp.maximum(m_prev, jnp.max(s, axis=-1, keepdims=True))
    p = jnp.exp2(s - _lanes(m_new, s.shape[1]))
    alpha = jnp.exp2(m_prev - m_new)
    return p.astype(BF), alpha, m_new


def _fox_qpad(qx, h, kind="fox"):
    p, hh = divmod(h, 2)
    lane = lax.broadcasted_iota(jnp.int32, (qx.shape[0], 2 * LANES), 1)
    blk = qx[:, 2 * LANES * p:2 * LANES * (p + 1)]
    if kind == "fox":
        keep = ((lane >= FOX_DIM * hh) & (lane < FOX_DIM * (hh + 1))) | (
            (lane >= LANES + AUG * h) & (lane < LANES + AUG * (h + 1)))
    else:
        keep = (lane >= LANES * hh) & (lane < LANES * (hh + 1))
    return jnp.where(keep, blk, jnp.zeros_like(blk))


def _fox_rows_step(s, vpa, m_ref, l_ref, h, r0, r1, start):
    if start:
        m_new = jnp.broadcast_to(jnp.max(s, axis=-1, keepdims=True), (r1 - r0, LANES))
        pv = _dot(jnp.exp2(s - _lanes(m_new, s.shape[1])).astype(BF), vpa)
        l_ref[h, r0:r1] = pv[:, LANES:]
        alpha = None
    else:
        pb, alpha, m_new = _softmax_step(s, m_ref[h, r0:r1])
        pv = _dot(pb, vpa)
        l_ref[h, r0:r1] = alpha * l_ref[h, r0:r1] + pv[:, LANES:]
    m_ref[h, r0:r1] = m_new
    return alpha, pv[:, :LANES]


def _fox_pair_acc(acc_ref, p, r0, r1, a, b, start):
    first = lax.broadcasted_iota(jnp.int32, (r1 - r0, LANES), 1) < FOX_DIM
    new = jnp.where(first, a[1], b[1])
    acc_ref[p, r0:r1] = new if start else acc_ref[p, r0:r1] * jnp.where(first, a[0], b[0]) + new


def _fox_prompt_tile(kx, vb, qpad_ref, m_ref, l_ref, acc_ref, start):
    tq = qpad_ref.shape[1]
    ones = jnp.ones((kx.shape[0], LANES), BF)

    def scores(h):
        return _dot_nt(qpad_ref[h], kx[:, 2 * LANES * (h // 2):2 * LANES * (h // 2 + 1)])

    s_next = scores(0)
    for p in range(N_PAIR):
        vpa = jnp.concatenate([vb[:, LANES * p:LANES * (p + 1)], ones], axis=1)
        res = []
        for hh in range(2):
            h = 2 * p + hh
            s = s_next
            if h + 1 < HEADS:
                s_next = scores(h + 1)
            res.append(_fox_rows_step(s, vpa, m_ref, l_ref, h, 0, tq, start))
        _fox_pair_acc(acc_ref, p, 0, tq, *res, start)


def _fox_diag_tile(kx, vb, qpad_ref, m_ref, l_ref, acc_ref, kind, start):
    tq = qpad_ref.shape[1]
    half = tq // 2
    ones = jnp.ones((tq, LANES), BF)
    col = lax.broadcasted_iota(jnp.int32, (half, half), 1)
    row = lax.broadcasted_iota(jnp.int32, (half, half), 0)
    if kind == "mla":
        shift = CHUNK.bit_length() - 1
        col, row = lax.shift_right_logical(col, shift), lax.shift_right_logical(row, shift)
    tri = col <= row

    def scores(h):
        kxp = kx[:, 2 * LANES * (h // 2):2 * LANES * (h // 2 + 1)]
        q = qpad_ref[h]
        return _dot_nt(q, kxp[:half]), _dot_nt(q[half:], kxp[half:])

    s_next = scores(0)
    for p in range(N_PAIR):
        vpa = jnp.concatenate([vb[:, LANES * p:LANES * (p + 1)], ones], axis=1)
        top, bot = [], []
        for hh in range(2):
            h = 2 * p + hh
            s_a, s_b = s_next
            if h + 1 < HEADS:
                s_next = scores(h + 1)
            s_top = jnp.where(tri, s_a[:half], NEG)
            s_bot = jnp.concatenate([s_a[half:], jnp.where(tri, s_b, NEG)], axis=1)
            top.append(_fox_rows_step(s_top, vpa[:half], m_ref, l_ref, h, 0, half, start))
            bot.append(_fox_rows_step(s_bot, vpa, m_ref, l_ref, h, half, tq, start))
        _fox_pair_acc(acc_ref, p, 0, half, *top, start)
        _fox_pair_acc(acc_ref, p, half, tq, *bot, start)


def _fox_prompt_kernel(qi_ref, kj_ref, qx_ref, kx_ref, vb_ref, o_ref, qpad_ref, m_ref, l_ref, acc_ref, *, kind):
    i = qi_ref[pl.program_id(1)]
    j = kj_ref[pl.program_id(1)]

    @pl.when(j == 0)
    def _():
        qx = qx_ref[...]
        for h in range(HEADS):
            qpad_ref[h] = _fox_qpad(qx, h, kind)

    for start in (True, False):
        is_start = (j == 0) if start else (j > 0)

        @pl.when((j < i) & is_start)
        def _(start=start):
            _fox_prompt_tile(kx_ref[...], vb_ref[...], qpad_ref, m_ref, l_ref, acc_ref, start)

        @pl.when((j == i) & is_start)
        def _(start=start):
            _fox_diag_tile(kx_ref[...], vb_ref[...], qpad_ref, m_ref, l_ref, acc_ref, kind, start)

    @pl.when(j == i)
    def _():
        first = lax.broadcasted_iota(jnp.int32, (qpad_ref.shape[1], LANES), 1) < FOX_DIM
        for p in range(N_PAIR):
            l_sel = jnp.where(first, l_ref[2 * p], l_ref[2 * p + 1])
            o_ref[:, LANES * p:LANES * (p + 1)] = (acc_ref[p] / l_sel).astype(o_ref.dtype)


def _fox_prompt(qx, kx, vb, *, batch, seq, t, kind):
    nq = seq // t
    assert CHUNK & (CHUNK - 1) == 0 and t % CHUNK == 0
    steps = [(i, j) for i in range(nq) for j in range(i + 1)]
    qi = jnp.asarray([s[0] for s in steps], jnp.int32)
    kj = jnp.asarray([s[1] for s in steps], jnp.int32)
    grid_spec = pltpu.PrefetchScalarGridSpec(
        num_scalar_prefetch=2, grid=(batch, len(steps)),
        in_specs=[pl.BlockSpec((t, 2 * FOX_W), lambda b, s, qi, kj: (b * nq + qi[s], 0)),
                  pl.BlockSpec((t, 2 * FOX_W), lambda b, s, qi, kj: (b * nq + kj[s], 0)),
                  pl.BlockSpec((t, FOX_W), lambda b, s, qi, kj: (b * nq + kj[s], 0))],
        out_specs=pl.BlockSpec((t, FOX_W), lambda b, s, qi, kj: (b * nq + qi[s], 0)),
        scratch_shapes=[pltpu.VMEM((HEADS, t, 2 * LANES), BF), pltpu.VMEM((HEADS, t, LANES), F32),
                        pltpu.VMEM((HEADS, t, LANES), F32), pltpu.VMEM((N_PAIR, t, LANES), F32)])
    return pl.pallas_call(
        functools.partial(_fox_prompt_kernel, kind=kind), grid_spec=grid_spec,
        out_shape=jax.ShapeDtypeStruct((batch * seq, FOX_W), BF),
        compiler_params=_params(2), name=kind + "_prompt")(qi, kj, qx, kx, vb)


def _fox_sample_kernel(qx_ref, kx_ref, vb_ref, fqs_ref, kct_ref, vct_ref, gk_ref, o_ref,
                       qbd_ref, qpad_ref, fq_ref, m_ref, l_ref, acc_ref, *, n_cache):
    j = pl.program_id(1)
    tq = qx_ref.shape[0]
    rows = HEADS * tq

    def update(s, pv_fn, g0=0, g1=2):
        a, b = 4 * tq * g0, 4 * tq * g1
        m_prev = m_ref[a:b]
        m_new = jnp.maximum(m_prev, jnp.max(s, axis=-1, keepdims=True))
        p = jnp.exp2(s - _lanes(m_new, s.shape[1]))
        alpha = jnp.exp2(m_prev - m_new)
        l_ref[a:b] = alpha * l_ref[a:b] + jnp.sum(p, axis=-1, keepdims=True)
        m_ref[a:b] = m_new
        pb = p.astype(BF)
        for pr in range(2 * g0, 2 * g1):
            r0, r1 = 2 * tq * pr, 2 * tq * (pr + 1)
            acc_ref[r0:r1] = acc_ref[r0:r1] * alpha[r0 - a:r1 - a] + pv_fn(pr, pb[r0 - a:r1 - a])

    @pl.when(j == 0)
    def _():
        qx = qx_ref[...]
        fqs = fqs_ref[...]
        qc = jnp.concatenate([qx[:, 2 * LANES * p:2 * LANES * p + LANES] for p in range(N_PAIR)], axis=1)
        lane = lax.broadcasted_iota(jnp.int32, qc.shape, 1)
        for h in range(HEADS):
            qbd_ref[h * tq:(h + 1) * tq, :] = jnp.where((lane // FOX_DIM) == h, qc, jnp.zeros_like(qc))
            qpad_ref[h] = _fox_qpad(qx, h)
            fq_ref[h * tq:(h + 1) * tq, :] = jnp.broadcast_to(fqs[:, h:h + 1], (tq, LANES))
        m_ref[...] = jnp.full(m_ref.shape, NEG, F32)
        l_ref[...] = jnp.zeros(l_ref.shape, F32)
        acc_ref[...] = jnp.zeros(acc_ref.shape, F32)
        kx = kx_ref[...]
        vb = vb_ref[...]
        tk = kx.shape[0]
        s = jnp.concatenate([_dot_nt(qpad_ref[h], kx[:, 2 * LANES * (h // 2):2 * LANES * (h // 2 + 1)])
                             for h in range(HEADS)], axis=0)
        row = lax.broadcasted_iota(jnp.int32, (rows, tk), 0) & (tq - 1)
        s = jnp.where(lax.broadcasted_iota(jnp.int32, (rows, tk), 1) <= row, s, NEG)
        update(s, lambda pr, pb: _dot(pb, vb[:, LANES * pr:LANES * (pr + 1)]))

    def cache_tile():
        tk = kct_ref.shape[3]
        gk = gk_ref[0]

        def scores(g):
            kt = kct_ref[0, 4 * g:4 * g + 4].reshape(4 * FOX_DIM, tk).astype(BF)
            gk_rows = jnp.concatenate([jnp.broadcast_to(gk[h:h + 1, :], (tq, tk)) for h in range(4 * g, 4 * g + 4)],
                                      axis=0)
            qg = qbd_ref[4 * tq * g:4 * tq * (g + 1), 4 * FOX_DIM * g:4 * FOX_DIM * (g + 1)]
            return _dot(qg, kt) + (_lanes(fq_ref[4 * tq * g:4 * tq * (g + 1)], tk) - gk_rows)

        def pv(pr, pb):
            return _dot_nt(pb, vct_ref[0, 2 * pr:2 * pr + 2].reshape(2 * FOX_DIM, tk).astype(BF))

        s0 = scores(0)
        s1 = scores(1)
        update(s0, pv, 0, 1)
        update(s1, pv, 1, 2)

    cache_tile()

    @pl.when(j == n_cache - 1)
    def _():
        first = lax.broadcasted_iota(jnp.int32, (tq, LANES), 1) < FOX_DIM
        o = acc_ref[...] / l_ref[...]
        for pr in range(N_PAIR):
            o_ref[:, LANES * pr:LANES * (pr + 1)] = jnp.where(
                first, o[2 * tq * pr:2 * tq * pr + tq], o[2 * tq * pr + tq:2 * tq * (pr + 1)]).astype(o_ref.dtype)


def _fox_sample(qx, kx, vb, fqs, kct, vct, gk, *, batch, seq, tkc):
    past = kct.shape[3]
    n_cache = past // tkc
    kern = functools.partial(_fox_sample_kernel, n_cache=n_cache)
    row = lambda w: pl.BlockSpec((seq, w), lambda b, j: (b, 0))
    return pl.pallas_call(
        kern, grid=(batch, n_cache),
        in_specs=[row(2 * FOX_W), row(2 * FOX_W), row(FOX_W), row(LANES),
                  pl.BlockSpec((1, HEADS, FOX_DIM, tkc), lambda b, j: (b, 0, 0, j)),
                  pl.BlockSpec((1, HEADS, FOX_DIM, tkc), lambda b, j: (b, 0, 0, j)),
                  pl.BlockSpec((1, HEADS, tkc), lambda b, j: (b, 0, j))],
        out_specs=row(FOX_W),
        out_shape=jax.ShapeDtypeStruct((batch * seq, FOX_W), BF),
        scratch_shapes=[pltpu.VMEM((HEADS * seq, FOX_W), BF), pltpu.VMEM((HEADS, seq, 2 * LANES), BF)]
        + [pltpu.VMEM((HEADS * seq, LANES), F32)] * 4,
        compiler_params=_params(2), name="fox_sample")(qx, kx, vb, fqs, kct, vct, gk)


def _chunk_mask(shape, tq, row0, q0, k0):
    assert tq & (tq - 1) == 0 and CHUNK & (CHUNK - 1) == 0
    shift = CHUNK.bit_length() - 1
    qpos = q0 + ((row0 + lax.broadcasted_iota(jnp.int32, shape, 0)) & (tq - 1))
    kpos = k0 + lax.broadcasted_iota(jnp.int32, shape, 1)
    return lax.shift_right_logical(kpos, shift) <= lax.shift_right_logical(qpos, shift)


def _mla_rows_update(s, val, m_ref, l_ref, acc_ref, r0, r1):
    m_prev = m_ref[r0:r1]
    m_new = jnp.maximum(m_prev, jnp.max(s, axis=-1, keepdims=True))
    p = jnp.exp2(s - _lanes(m_new, s.shape[1]))
    alpha = jnp.exp2(m_prev - m_new)
    l_ref[r0:r1] = alpha * l_ref[r0:r1] + jnp.sum(p, axis=-1, keepdims=True)
    m_ref[r0:r1] = m_new
    acc_ref[r0:r1] = acc_ref[r0:r1] * _lanes(alpha, KV_LORA) + _dot(p.astype(BF), val)


def _mla_init_stats(m_ref, l_ref, acc_ref):
    m_ref[...] = jnp.full(m_ref.shape, NEG, F32)
    l_ref[...] = jnp.zeros(l_ref.shape, F32)
    acc_ref[...] = jnp.zeros(acc_ref.shape, F32)


def _mla_finalize(o_ref, wuv_ref, l_ref, acc_ref, tq):
    olat = (acc_ref[...] / _lanes(l_ref[...], KV_LORA)).astype(BF)
    for g in range(2):
        out = None
        for h in range(4 * g, 4 * g + 4):
            part = _dot(olat[h * tq:(h + 1) * tq, :], wuv_ref[h])
            out = part if out is None else out + part
        o_ref[:, 4 * MLA_VDIM * g:4 * MLA_VDIM * (g + 1)] = out.astype(o_ref.dtype)


def _mla_scratch(rows):
    return [pltpu.VMEM((rows, LANES), F32), pltpu.VMEM((rows, LANES), F32), pltpu.VMEM((rows, KV_LORA), F32)]


def _mla_sample_kernel(qlat_ref, qrope_ref, mkey_ref, latc_ref, krt_ref, wuv_ref, o_ref,
                       ql_ref, qr_ref, m_ref, l_ref, acc_ref, *, n_cache, tq, past, rc):
    j = pl.program_id(1)
    rows = HEADS * tq

    @pl.when(j == 0)
    def _():
        qrope = qrope_ref[...]
        for h in range(HEADS):
            ql_ref[h * tq:(h + 1) * tq, :] = qlat_ref[h]
            qr_ref[h * tq:(h + 1) * tq, :] = qrope[:, MLA_ROPE * h:MLA_ROPE * (h + 1)]
        _mla_init_stats(m_ref, l_ref, acc_ref)
        mk = mkey_ref[...]
        s = _dot_nt(ql_ref[...], mk[:, :KV_LORA]) + _dot_nt(qr_ref[...], mk[:, KV_LORA:KV_LORA + MLA_ROPE])
        if past % CHUNK != 0 or tq > CHUNK:
            s = jnp.where(_chunk_mask(s.shape, tq, 0, past, past), s, NEG)
        _mla_rows_update(s, mk[:, :KV_LORA], m_ref, l_ref, acc_ref, 0, rows)

    def cache_tile():
        c = latc_ref[0].astype(BF)
        krt = krt_ref[0].astype(BF)
        def scores(r0):
            return _dot_nt(ql_ref[r0:r0 + rc], c) + _dot(qr_ref[r0:r0 + rc], krt)

        s_next = scores(0)
        for r0 in range(0, rows, rc):
            s = s_next
            if r0 + rc < rows:
                s_next = scores(r0 + rc)
            _mla_rows_update(s, c, m_ref, l_ref, acc_ref, r0, r0 + rc)

    cache_tile()

    @pl.when(j == n_cache - 1)
    def _():
        _mla_finalize(o_ref, wuv_ref, l_ref, acc_ref, tq)


def _mla_sample(qlat, qrope, mkey, latc, krt, wuvp, *, batch, seq, tkc, rc):
    past = latc.shape[1]
    n_cache = past // tkc
    kern = functools.partial(_mla_sample_kernel, n_cache=n_cache, tq=seq, past=past, rc=rc)
    return pl.pallas_call(
        kern, grid=(batch, n_cache),
        in_specs=[pl.BlockSpec((HEADS, seq, KV_LORA), lambda b, j: (0, b, 0)),
                  pl.BlockSpec((seq, HEADS * MLA_ROPE), lambda b, j: (b, 0)),
                  pl.BlockSpec((seq, 2 * KV_LORA), lambda b, j: (b, 0)),
                  pl.BlockSpec((1, tkc, KV_LORA), lambda b, j: (b, j, 0)),
                  pl.BlockSpec((1, MLA_ROPE, tkc), lambda b, j: (b, 0, j)),
                  _const_spec(wuvp.shape)],
        out_specs=pl.BlockSpec((seq, FOX_W), lambda b, j: (b, 0)),
        out_shape=jax.ShapeDtypeStruct((batch * seq, FOX_W), BF),
        scratch_shapes=[pltpu.VMEM((HEADS * seq, KV_LORA), BF), pltpu.VMEM((HEADS * seq, MLA_ROPE), BF)]
        + _mla_scratch(HEADS * seq),
        compiler_params=_params(2), name="mla_sample")(qlat, qrope, mkey, latc, krt, wuvp)


def _ffn_kernel(x_ref, fox_ref, mla_ref, pconv_ref, wo_ref, gf_ref, wup_ref, cw_ref, wdn_ref, gfin_ref,
                y_ref, cst_ref, hn_ref, act_ref, ush_ref, carry_ref, *, tm, seq_len):
    n_seg = max(1, tm // seq_len)
    seg = tm // n_seg
    mixed = jnp.concatenate([fox_ref[...], mla_ref[...]], axis=1)
    x1 = x_ref[...] + _dot(mixed, wo_ref[...])
    hn_ref[...] = (x1 * lax.rsqrt(jnp.mean(x1 * x1, axis=-1, keepdims=True) + EPS) * gf_ref[...]).astype(BF)
    y_ref[...] = x1

    if n_seg == 1:
        @pl.when(pl.program_id(0) % (seq_len // tm) == 0)
        def _():
            carry_ref[...] = pconv_ref[0]

    def conv(u, col, slot):
        cols = slice(col, col + FF_CHUNK)
        for s in range(n_seg):
            ush_ref[slot, s, 6:8, :] = carry_ref[:, cols] if n_seg == 1 else pconv_ref[s, :, cols]
            ush_ref[slot, s, 8:8 + seg, :] = u[seg * s:seg * (s + 1), :]
            cst_ref[s, :, cols] = u[seg * (s + 1) - 2:seg * (s + 1), :]
        if n_seg == 1:
            carry_ref[:, cols] = u[tm - 2:tm, :]
        u1 = jnp.concatenate([ush_ref[slot, s, 7:7 + seg, :] for s in range(n_seg)], axis=0)
        u2 = jnp.concatenate([ush_ref[slot, s, 6:6 + seg, :] for s in range(n_seg)], axis=0)
        cw = cw_ref[:, cols]
        return cw[3:4, :] + cw[0:1, :] * u2 + cw[1:2, :] * u1 + cw[2:3, :] * u

    def up(c):
        hn = hn_ref[...]
        g0, v0 = FF_CHUNK * c, D_FF + FF_CHUNK * c
        return _dot(hn, wup_ref[:, g0:g0 + FF_CHUNK]), _dot(hn, wup_ref[:, v0:v0 + FF_CHUNK])

    u_next = up(0)
    for c in range(N_FF):
        ug, uv = u_next
        if c + 1 < N_FF:
            u_next = up(c + 1)
        gate = conv(ug, FF_CHUNK * c, 2 * (c % 2))
        val = conv(uv, D_FF + FF_CHUNK * c, 2 * (c % 2) + 1)
        act_ref[:, FF_CHUNK * c:FF_CHUNK * (c + 1)] = (gate * jax.nn.sigmoid(gate) * val).astype(BF)

    x2 = y_ref[...] + _dot(act_ref[...], wdn_ref[...])
    y_ref[...] = x2 * lax.rsqrt(jnp.mean(x2 * x2, axis=-1, keepdims=True) + EPS) * gfin_ref[...]


def _ffn(x2d, fox, mla, pconv, wo, gf, wup, cw, wdn, gfin, *, seq_len, tm):
    n = x2d.shape[0]
    nt = n // tm
    n_seq = n // seq_len
    n_seg = max(1, tm // seq_len)
    if n_seg == 1:
        tps = seq_len // tm
        st_map = lambda i: (i // tps, 0, 0)
    else:
        st_map = lambda i: (i, 0, 0)
    row = lambda w: pl.BlockSpec((tm, w), lambda i: (i, 0))
    kern = functools.partial(_ffn_kernel, tm=tm, seq_len=seq_len)
    return pl.pallas_call(
        kern, grid=(nt,),
        in_specs=[row(D_MODEL), row(FOX_W), row(FOX_W),
                  pl.BlockSpec((n_seg, CONV_W - 1, 2 * D_FF), st_map),
                  _const_spec(wo.shape), _const_spec(gf.shape), _const_spec(wup.shape), _const_spec(cw.shape),
                  _const_spec(wdn.shape), _const_spec(gfin.shape)],
        out_specs=(row(D_MODEL), pl.BlockSpec((n_seg, CONV_W - 1, 2 * D_FF), st_map)),
        out_shape=(jax.ShapeDtypeStruct((n, D_MODEL), F32),
                   jax.ShapeDtypeStruct((n_seq, CONV_W - 1, 2 * D_FF), F32)),
        scratch_shapes=[pltpu.VMEM((tm, D_MODEL), BF), pltpu.VMEM((tm, D_FF), BF),
                        pltpu.VMEM((4, n_seg, 8 + tm // n_seg, FF_CHUNK), F32),
                        pltpu.VMEM((CONV_W - 1, 2 * D_FF), F32)],
        compiler_params=_params(1), name="ffn")(x2d, fox, mla, pconv, wo, gf, wup, cw, wdn, gfin)


def _rope_tables(start, n, reps=1):
    half = MLA_ROPE // 2
    inv = ROPE_THETA ** (-np.arange(half, dtype=np.float64) / half)
    ang = (start + np.arange(n, dtype=np.float64))[:, None] * inv[None, :]
    cos = np.tile(np.cos(ang), (1, 2 * HEADS))
    sin = np.tile(np.sin(ang), (1, 2 * HEADS))
    ones = np.ones((n, MLA_NOPE))
    zeros = np.zeros((n, LANES - MLA_NOPE - MLA_ROPE))
    cosm = np.concatenate([ones, cos[:, :MLA_ROPE], zeros], axis=1)
    sinm = np.concatenate([0.0 * ones, sin[:, :MLA_ROPE], zeros], axis=1)
    return [jnp.asarray(np.tile(a, (reps, 1)), F32) for a in (cos, sin, cosm, sinm)]


def _placement():
    place = np.zeros((LANES, 2 * N_PAIR * LANES), np.float32)
    ones = np.zeros((1, 2 * N_PAIR * LANES), np.float32)
    for h in range(HEADS):
        for i in range(3):
            for p in range(N_PAIR):
                place[HEADS * i + h, LANES * p + AUG * h + i] = 1.0
                ones[0, LANES * p + AUG * h + 3 + i] = 1.0
                ones[0, N_PAIR * LANES + LANES * p + AUG * h + i] = 1.0
                place[HEADS * i + h, N_PAIR * LANES + LANES * p + AUG * h + 3 + i] = -1.0
    rep = np.zeros((LANES, HEADS * MLA_ROPE), np.float32)
    for h in range(HEADS):
        for r in range(MLA_ROPE):
            rep[r, MLA_ROPE * h + r] = 1.0
    return jnp.asarray(place, BF), jnp.asarray(ones, F32), jnp.asarray(rep, BF)


def _rot_cols(w):
    half = MLA_ROPE // 2
    return jnp.concatenate([-w[..., half:], w[..., :half]], axis=-1)


def _pad_cols(w, width):
    return jnp.pad(w, ((0, 0), (0, width - w.shape[1])))


def _layer_weights(w_in, b_f, w_q_up, w_uk, w_uv, w_out, w_up, conv_w, conv_b, w_down):
    wq, wk, wv, wf, wqc, wckv, wkr = jnp.split(w_in, IN_SPLITS, axis=1)
    wa = jnp.concatenate([wq, wk, wv, wqc, wckv, _pad_cols(wkr, LANES), _pad_cols(_rot_cols(wkr), LANES),
                          _pad_cols(wf, LANES)], axis=1).astype(BF)
    bfp = _pad_cols(b_f[None, :], LANES).astype(F32)
    wq3 = w_q_up.reshape(Q_LORA, HEADS, MLA_NOPE + MLA_ROPE)
    wq_nope = wq3[:, :, :MLA_NOPE].reshape(Q_LORA, HEADS * MLA_NOPE)
    wq_rope = wq3[:, :, MLA_NOPE:]
    wq2 = jnp.concatenate([wq_nope, wq_rope.reshape(Q_LORA, -1), _rot_cols(wq_rope).reshape(Q_LORA, -1)],
                          axis=1).astype(BF)
    eye = jnp.eye(4, dtype=F32)
    wk = jnp.transpose(w_uk, (1, 2, 0)).reshape(2, 4, 1, MLA_NOPE, KV_LORA)
    wukp = (wk * eye[None, :, :, None, None]).reshape(HEADS, 4 * MLA_NOPE, KV_LORA)
    wv = jnp.transpose(w_uv, (1, 0, 2)).reshape(2, 4, KV_LORA, 1, MLA_VDIM)
    wuvp = (wv * eye[None, :, None, :, None]).reshape(HEADS, KV_LORA, 4 * MLA_VDIM)
    wup = w_up.astype(BF)
    cw = jnp.concatenate([conv_w, conv_b[None, :]], axis=0)
    wdn = w_down.astype(BF)
    zpad = ((0, 0), (0, 0), (0, LANES - MLA_NOPE - MLA_ROPE))
    wq_a = jnp.pad(wq3, zpad).reshape(Q_LORA, HEADS * LANES)
    wq_b = jnp.pad(jnp.concatenate([jnp.zeros_like(wq3[:, :, :MLA_NOPE]), _rot_cols(wq_rope)], axis=2), zpad)
    wqm = jnp.concatenate([wq_a, wq_b.reshape(Q_LORA, HEADS * LANES)], axis=1).astype(BF)
    wkm = jnp.pad(w_uk, ((0, 0), (0, 0), (0, LANES - MLA_NOPE))).reshape(KV_LORA, HEADS * LANES).astype(BF)
    wvm = w_uv.reshape(KV_LORA, HEADS * MLA_VDIM).astype(BF)
    return wa, bfp, wq2, wukp.astype(BF), wuvp.astype(BF), w_out.astype(BF), wup, cw, wdn, wqm, wkm, wvm


def kernel(x_prompt, x_sample, cache_fox_k, cache_fox_v, cache_fox_logf, cache_mla_latent, cache_mla_krope,
           state_ffn_conv, attn_norm, w_in, b_forget, q_norm, w_q_up, kv_norm, w_uk, w_uv, w_out, ffn_norm,
           w_up, conv_w, conv_b, w_down, final_norm):
    assert attn_norm.shape[0] == 1, "single-layer stack"
    bp, tp, _ = x_prompt.shape
    bs, ts, _ = x_sample.shape
    past = cache_fox_k.shape[2]
    assert ts == CHUNK and past % CHUNK == 0

    wa, bfp, wq2, wukp, wuvp, wo, wup, cw, wdn, wqm, wkm, wvm = _layer_weights(
        w_in[0], b_forget[0], w_q_up[0], w_uk[0], w_uv[0], w_out[0], w_up[0], conv_w[0], conv_b[0], w_down[0])
    place, ones, rep = _placement()
    g_attn = attn_norm[0][None, :]
    g_q = q_norm[0][None, :]
    g_kv = kv_norm[0][None, :]
    g_ffn = ffn_norm[0][None, :]
    g_fin = final_norm[None, :]

    tm = ROW_TILE
    assert (bp * tp) % tm == 0 and (bs * ts) % tm == 0 and tp % ATTN_TILE == 0
    outs = []
    for grp in ("prompt", "sample"):
        if grp == "prompt":
            x2d = x_prompt.reshape(bp * tp, D_MODEL)
            b, t = bp, tp
            tabs = _rope_tables(0, tp)
            pconv = jnp.zeros((bp, CONV_W - 1, 2 * D_FF), F32)
        else:
            x2d = x_sample.reshape(bs * ts, D_MODEL)
            b, t = bs, ts
            tabs = _rope_tables(past, ts, reps=tm // ts)
            pconv = state_ffn_conv[0]

        absorbed = grp == "sample"
        wm = (wq2, wukp, rep) if absorbed else (wqm, wkm, wvm)
        (k, v, logf, lat, kr, qx, kx, vb, m1, m2, m3, fqs) = _proj(
            x2d, *tabs, g_attn, wa, bfp, g_q, g_kv, *wm, place, ones, seq_len=t, tm=tm, absorbed=absorbed)

        if grp == "prompt":
            fox = _fox_prompt(qx, kx, vb, batch=b, seq=t, t=ATTN_TILE, kind="fox")
            mla = _fox_prompt(m1, m2, m3, batch=b, seq=t, t=ATTN_TILE, kind="mla")
        else:
            mkey, qlat, qrope = m1, m2, m3
            gk = _fox_prep(jnp.transpose(cache_fox_logf[0], (0, 2, 1)))
            kct = jnp.transpose(cache_fox_k[0], (0, 2, 3, 1))
            vct = jnp.transpose(cache_fox_v[0], (0, 2, 3, 1))
            krt = jnp.transpose(cache_mla_krope[0], (0, 2, 1))
            assert past % min(past, FOX_CACHE_TILE) == 0 and past % min(past, MLA_CACHE_TILE) == 0
            fox = _fox_sample(qx, kx, vb, fqs, kct, vct, gk, batch=b, seq=t, tkc=min(past, FOX_CACHE_TILE))
            mla = _mla_sample(qlat, qrope, mkey, cache_mla_latent[0], krt, wuvp, batch=b, seq=t,
                              tkc=min(past, MLA_CACHE_TILE), rc=MLA_ROW_CHUNK)

        def small_state(a, w, b=b, t=t):
            if t >= tm:
                return jnp.transpose(a, (0, 2, 1))[None]
            a = a.reshape(a.shape[0], w, tm // t, t)
            return jnp.transpose(a, (0, 2, 3, 1)).reshape(b, t, w)[None]

        y, cst = _ffn(x2d, fox, mla, pconv, wo, g_ffn, wup, cw, wdn, g_fin, seq_len=t, tm=tm)
        outs.append((y.reshape(b, t, D_MODEL),
                     k.reshape(1, b, t, HEADS, FOX_DIM), v.reshape(1, b, t, HEADS, FOX_DIM),
                     small_state(logf, HEADS), lat.reshape(1, b, t, KV_LORA), small_state(kr, MLA_ROPE),
                     cst[None]))
    (yp, *st_p), (ys, *st_s) = outs
    return (yp, ys, *st_p, *st_s)
```

```python
import functools

import numpy as np
import jax
import jax.numpy as jnp
from jax import lax
from jax.experimental import pallas as pl
from jax.experimental.pallas import tpu as pltpu

D_MODEL = 1024
CHUNK = 64
HEADS = 8
FOX_DIM = 64
MLA_NOPE = 64
MLA_ROPE = 32
MLA_VDIM = 64
Q_LORA = 384
KV_LORA = 256
D_FF = 2816
CONV_W = 3
ROPE_THETA = 10000.0
EPS = 1e-6
NEG = -1e30
LOG2E = 1.4426950408889634

FOX_W = HEADS * FOX_DIM
IN_SIZES = [FOX_W, FOX_W, FOX_W, HEADS, Q_LORA, KV_LORA, MLA_ROPE]
IN_SPLITS = [int(s) for s in np.cumsum(IN_SIZES)[:-1]]

LANES = 128
MXU_TILE = 256
AUG = 16
N_PAIR = HEADS // 2
FF_CHUNK = MXU_TILE
N_FF = D_FF // FF_CHUNK
VMEM_LIMIT = 56 * 1024 * 1024

ROW_TILE = 512
ATTN_TILE = 512
FOX_CACHE_TILE = 4096
MLA_CACHE_TILE = 4096
MLA_ROW_CHUNK = 256

C_Q, C_K, C_V = 0, FOX_W, 2 * FOX_W
C_QC = 3 * FOX_W
C_CKV = C_QC + Q_LORA
C_KR = C_CKV + KV_LORA
C_KRR = C_KR + LANES
C_F = C_KRR + LANES
N_A = C_F + LANES

BF = jnp.bfloat16
F32 = jnp.float32


def _dot(a, b):
    return jnp.dot(a, b, preferred_element_type=F32)


def _dot_nt(a, b):
    return lax.dot_general(a, b, (((1,), (1,)), ((), ())), preferred_element_type=F32)


def _split3(x):
    hi = x.astype(BF)
    r1 = x - hi.astype(F32)
    mid = r1.astype(BF)
    lo = (r1 - mid.astype(F32)).astype(BF)
    return hi, mid, lo


def _log_sigmoid(x):
    return jnp.minimum(x, 0.0) - jnp.log1p(jnp.exp(-jnp.abs(x)))


def _lanes(x, n):
    if n % LANES == 0:
        return x if n == LANES else jnp.concatenate([x] * (n // LANES), axis=1)
    assert n < LANES
    return x[:, :n]


def _params(n_axes):
    return pltpu.CompilerParams(dimension_semantics=("arbitrary",) * n_axes, vmem_limit_bytes=VMEM_LIMIT)


def _const_spec(shape):
    nd = len(shape)
    return pl.BlockSpec(shape, lambda *_: (0,) * nd, pipeline_mode=pl.Buffered(1))


def _proj_kernel(x_ref, cos_ref, sin_ref, cosm_ref, sinm_ref, g_ref, wa_ref, bf_ref, gq_ref, gkv_ref,
                 w1_ref, w2_ref, w3_ref, place_ref, ones_ref,
                 k_ref, v_ref, logf_ref, lat_ref, kr_ref, qx_ref, kx_ref, vb_ref, o1_ref, o2_ref, o3_ref,
                 fqs_ref, carry_ref, *, tm, seq_len, mla_scale, absorbed):
    x = x_ref[...]
    ms = jnp.mean(x * x, axis=-1, keepdims=True)
    hn = (x * lax.rsqrt(ms + EPS) * g_ref[...]).astype(BF)
    zs = _dot(hn, wa_ref[:, C_QC:])
    z = _dot(hn, wa_ref[:, :C_QC])

    lane = lax.broadcasted_iota(jnp.int32, (tm, LANES), 1)
    logf = _log_sigmoid(zs[:, C_F - C_QC:C_F - C_QC + LANES] + bf_ref[...])
    logf = jnp.where(lane < HEADS, logf, 0.0)
    logf_t = logf.T[:HEADS, :]
    logf_ref[0] = logf_t
    seg = min(seq_len, tm)
    assert seg & (seg - 1) == 0
    pos = lax.broadcasted_iota(jnp.int32, (HEADS, tm), 1) & (seg - 1)
    fc = logf_t
    shift = 1
    while shift < seg:
        fc = fc + jnp.where(pos >= shift, pltpu.roll(fc, shift, 1), 0.0)
        shift *= 2
    if seq_len > tm:
        @pl.when(pl.program_id(0) % (seq_len // tm) == 0)
        def _():
            carry_ref[...] = jnp.zeros_like(carry_ref)
        fc = fc + carry_ref[:, 0:1]
        carry_ref[...] = jnp.broadcast_to(fc[:, tm - 1:tm], carry_ref.shape)
    fcum = jnp.concatenate([fc, jnp.zeros((LANES - HEADS, tm), F32)], axis=0).T

    zc = zs[:, C_CKV - C_QC:C_CKV - C_QC + KV_LORA]
    ckv = zc * lax.rsqrt(jnp.mean(zc * zc, axis=-1, keepdims=True) + EPS) * gkv_ref[...]
    lat_ref[...] = ckv
    cos = cos_ref[...]
    sin = sin_ref[...]
    krb = (zs[:, C_KR - C_QC:C_KR - C_QC + LANES] * cos[:, :LANES]
           + zs[:, C_KRR - C_QC:C_KRR - C_QC + LANES] * sin[:, :LANES])
    kr_ref[0] = krb.T[:MLA_ROPE, :]
    cb = ckv.astype(BF)

    zqc = zs[:, :Q_LORA]
    qcn = (zqc * lax.rsqrt(jnp.mean(zqc * zqc, axis=-1, keepdims=True) + EPS) * gq_ref[...]).astype(BF)
    zq2 = _dot(qcn, w1_ref[...])
    if absorbed:
        mkey_ref, qlat_ref, qrope_ref = o1_ref, o2_ref, o3_ref
        mkey_ref[:, :KV_LORA] = cb
        mkey_ref[:, KV_LORA:] = _dot(krb.astype(BF), w3_ref[...]).astype(BF)
        nr = HEADS * MLA_ROPE
        qr = zq2[:, FOX_W:FOX_W + nr] * cos + zq2[:, FOX_W + nr:FOX_W + 2 * nr] * sin
        qrope_ref[...] = (qr * mla_scale).astype(BF)
        for h in range(HEADS):
            g = h // 4
            qn = zq2[:, 4 * MLA_NOPE * g:4 * MLA_NOPE * (g + 1)].astype(BF)
            qlat_ref[h] = (_dot(qn, w2_ref[h]) * mla_scale).astype(BF)
    else:
        qm_ref, km_ref, vm_ref = o1_ref, o2_ref, o3_ref
        hw = HEADS * LANES
        cosm = jnp.concatenate([cosm_ref[...]] * HEADS, axis=1)
        sinm = jnp.concatenate([sinm_ref[...]] * HEADS, axis=1)
        qm_ref[...] = ((zq2[:, :hw] * cosm + zq2[:, hw:] * sinm) * mla_scale).astype(BF)
        krs = pltpu.roll(krb, MLA_NOPE, 1)
        km_ref[...] = (_dot(cb, w2_ref[...]) + jnp.concatenate([krs] * HEADS, axis=1)).astype(BF)
        vm_ref[...] = _dot(cb, w3_ref[...]).astype(BF)

    zk = z[:, C_K:C_K + FOX_W]
    zv = z[:, C_V:C_V + FOX_W]
    for h in range(HEADS):
        k_ref[pl.ds(h, tm, stride=HEADS), :] = zk[:, FOX_DIM * h:FOX_DIM * (h + 1)]
        v_ref[pl.ds(h, tm, stride=HEADS), :] = zv[:, FOX_DIM * h:FOX_DIM * (h + 1)]
    vb_ref[...] = zv.astype(BF)

    fqs = fcum * LOG2E
    fqs_ref[...] = fqs
    hi, mid, lo = _split3(fqs)
    comb = hi.astype(F32) + pltpu.roll(mid.astype(F32), HEADS, 1) + pltpu.roll(lo.astype(F32), 2 * HEADS, 1)
    aug = _dot(comb.astype(BF), place_ref[...]) + ones_ref[...]
    zq = z[:, C_Q:C_Q + FOX_W] * (FOX_DIM ** -0.5 * LOG2E)
    for p in range(N_PAIR):
        lo_, hi_ = 2 * LANES * p, 2 * LANES * p + LANES
        qx_ref[:, lo_:hi_] = zq[:, LANES * p:LANES * (p + 1)].astype(BF)
        qx_ref[:, hi_:hi_ + LANES] = aug[:, LANES * p:LANES * (p + 1)].astype(BF)
        kx_ref[:, lo_:hi_] = zk[:, LANES * p:LANES * (p + 1)].astype(BF)
        kx_ref[:, hi_:hi_ + LANES] = aug[:, FOX_W + LANES * p:FOX_W + LANES * (p + 1)].astype(BF)


def _proj(x2d, cos, sin, cosm, sinm, g, wa, bfp, gq, gkv, w1, w2, w3, place, ones, *, seq_len, tm, absorbed):
    n = x2d.shape[0]
    nt = n // tm
    if seq_len >= tm:
        tps = seq_len // tm
        tab_map = lambda i: (i % tps, 0)
    else:
        tab_map = lambda i: (0, 0)
    row = lambda w: pl.BlockSpec((tm, w), lambda i: (i, 0))
    if seq_len >= tm:
        small = lambda w: jax.ShapeDtypeStruct((n // seq_len, w, seq_len), F32)
        small_spec = lambda w: pl.BlockSpec((1, w, tm), lambda i: (i // tps, 0, i % tps))
    else:
        small = lambda w: jax.ShapeDtypeStruct((nt, w, tm), F32)
        small_spec = lambda w: pl.BlockSpec((1, w, tm), lambda i: (i, 0, 0))
    if absorbed:
        mla_shapes = (jax.ShapeDtypeStruct((n, 2 * KV_LORA), BF), jax.ShapeDtypeStruct((HEADS, n, KV_LORA), BF),
                      jax.ShapeDtypeStruct((n, HEADS * MLA_ROPE), BF))
        mla_specs = (row(2 * KV_LORA), pl.BlockSpec((HEADS, tm, KV_LORA), lambda i: (0, i, 0)),
                     row(HEADS * MLA_ROPE))
    else:
        mla_shapes = (jax.ShapeDtypeStruct((n, HEADS * LANES), BF), jax.ShapeDtypeStruct((n, HEADS * LANES), BF),
                      jax.ShapeDtypeStruct((n, HEADS * MLA_VDIM), BF))
        mla_specs = (row(HEADS * LANES), row(HEADS * LANES), row(HEADS * MLA_VDIM))
    out_shape = (
        jax.ShapeDtypeStruct((n * HEADS, FOX_DIM), F32),
        jax.ShapeDtypeStruct((n * HEADS, FOX_DIM), F32),
        small(HEADS),
        jax.ShapeDtypeStruct((n, KV_LORA), F32),
        small(MLA_ROPE),
        jax.ShapeDtypeStruct((n, 2 * FOX_W), BF),
        jax.ShapeDtypeStruct((n, 2 * FOX_W), BF),
        jax.ShapeDtypeStruct((n, FOX_W), BF),
        *mla_shapes,
        jax.ShapeDtypeStruct((n, LANES), F32),
    )
    kv_spec = pl.BlockSpec((tm * HEADS, FOX_DIM), lambda i: (i, 0))
    out_specs = (kv_spec, kv_spec, small_spec(HEADS), row(KV_LORA), small_spec(MLA_ROPE), row(2 * FOX_W),
                 row(2 * FOX_W), row(FOX_W), *mla_specs, row(LANES))
    in_specs = [row(D_MODEL),
                pl.BlockSpec((tm, 2 * LANES), tab_map), pl.BlockSpec((tm, 2 * LANES), tab_map),
                pl.BlockSpec((tm, LANES), tab_map), pl.BlockSpec((tm, LANES), tab_map),
                _const_spec(g.shape), _const_spec(wa.shape), _const_spec(bfp.shape), _const_spec(gq.shape),
                _const_spec(gkv.shape), _const_spec(w1.shape), _const_spec(w2.shape), _const_spec(w3.shape),
                _const_spec(place.shape), _const_spec(ones.shape)]
    kern = functools.partial(_proj_kernel, tm=tm, seq_len=seq_len, absorbed=absorbed,
                             mla_scale=float((MLA_NOPE + MLA_ROPE) ** -0.5 * LOG2E))
    return pl.pallas_call(
        kern, grid=(nt,), in_specs=in_specs, out_specs=out_specs, out_shape=out_shape,
        scratch_shapes=[pltpu.VMEM((8, LANES), F32)],
        compiler_params=_params(1), name="proj")(
            x2d, cos, sin, cosm, sinm, g, wa, bfp, gq, gkv, w1, w2, w3, place, ones)


def _fox_prep_kernel(plt_ref, g_ref):
    nb, h, n = plt_ref.shape
    x = plt_ref[...].reshape(nb * h, n)
    lane = lax.broadcasted_iota(jnp.int32, x.shape, 1)
    shift = 1
    while shift < n:
        x = x + jnp.where(lane >= shift, pltpu.roll(x, shift, 1), 0.0)
        shift *= 2
    g_ref[...] = ((x - x[:, n - 1:n]) * LOG2E).reshape(nb, h, n)


def _fox_prep(plt, nb=4):
    b, h, p = plt.shape
    nb = nb if b % nb == 0 else 1
    return pl.pallas_call(
        _fox_prep_kernel, grid=(b // nb,),
        in_specs=[pl.BlockSpec((nb, h, p), lambda i: (i, 0, 0))],
        out_specs=pl.BlockSpec((nb, h, p), lambda i: (i, 0, 0)),
        out_shape=jax.ShapeDtypeStruct((b, h, p), F32),
        compiler_params=_params(1), name="fox_prep")(plt)


def _softmax_step(s, m_prev):
    m_new = jnp.maximum(m_prev, jnp.max(s, axis=-1, keepdims=True))
    p = jnp.exp2(s - _lanes(m_new, s.shape[1]))
    alpha = jnp.exp2(m_prev - m_new)
    return p.astype(BF), alpha, m_new


def _fox_qpad(qx, h, kind="fox"):
    p, hh = divmod(h, 2)
    lane = lax.broadcasted_iota(jnp.int32, (qx.shape[0], 2 * LANES), 1)
    blk = qx[:, 2 * LANES * p:2 * LANES * (p + 1)]
    if kind == "fox":
        keep = ((lane >= FOX_DIM * hh) & (lane < FOX_DIM * (hh + 1))) | (
            (lane >= LANES + AUG * h) & (lane < LANES + AUG * (h + 1)))
    else:
        keep = (lane >= LANES * hh) & (lane < LANES * (hh + 1))
    return jnp.where(keep, blk, jnp.zeros_like(blk))


def _fox_rows_step(s, vpa, m_ref, l_ref, h, r0, r1, start):
    if start:
        m_new = jnp.broadcast_to(jnp.max(s, axis=-1, keepdims=True), (r1 - r0, LANES))
        pv = _dot(jnp.exp2(s - _lanes(m_new, s.shape[1])).astype(BF), vpa)
        l_ref[h, r0:r1] = pv[:, LANES:]
        alpha = None
    else:
        pb, alpha, m_new = _softmax_step(s, m_ref[h, r0:r1])
        pv = _dot(pb, vpa)
        l_ref[h, r0:r1] = alpha * l_ref[h, r0:r1] + pv[:, LANES:]
    m_ref[h, r0:r1] = m_new
    return alpha, pv[:, :LANES]


def _fox_pair_acc(acc_ref, p, r0, r1, a, b, start):
    first = lax.broadcasted_iota(jnp.int32, (r1 - r0, LANES), 1) < FOX_DIM
    new = jnp.where(first, a[1], b[1])
    acc_ref[p, r0:r1] = new if start else acc_ref[p, r0:r1] * jnp.where(first, a[0], b[0]) + new


def _fox_prompt_tile(kx, vb, qpad_ref, m_ref, l_ref, acc_ref, start):
    tq = qpad_ref.shape[1]
    ones = jnp.ones((kx.shape[0], LANES), BF)

    def scores(h):
        return _dot_nt(qpad_ref[h], kx[:, 2 * LANES * (h // 2):2 * LANES * (h // 2 + 1)])

    s_next = scores(0)
    for p in range(N_PAIR):
        vpa = jnp.concatenate([vb[:, LANES * p:LANES * (p + 1)], ones], axis=1)
        res = []
        for hh in range(2):
            h = 2 * p + hh
            s = s_next
            if h + 1 < HEADS:
                s_next = scores(h + 1)
            res.append(_fox_rows_step(s, vpa, m_ref, l_ref, h, 0, tq, start))
        _fox_pair_acc(acc_ref, p, 0, tq, *res, start)


def _fox_diag_tile(kx, vb, qpad_ref, m_ref, l_ref, acc_ref, kind, start):
    tq = qpad_ref.shape[1]
    half = tq // 2
    ones = jnp.ones((tq, LANES), BF)
    col = lax.broadcasted_iota(jnp.int32, (half, half), 1)
    row = lax.broadcasted_iota(jnp.int32, (half, half), 0)
    if kind == "mla":
        shift = CHUNK.bit_length() - 1
        col, row = lax.shift_right_logical(col, shift), lax.shift_right_logical(row, shift)
    tri = col <= row

    def scores(h):
        kxp = kx[:, 2 * LANES * (h // 2):2 * LANES * (h // 2 + 1)]
        q = qpad_ref[h]
        return _dot_nt(q, kxp[:half]), _dot_nt(q[half:], kxp[half:])

    s_next = scores(0)
    for p in range(N_PAIR):
        vpa = jnp.concatenate([vb[:, LANES * p:LANES * (p + 1)], ones], axis=1)
        top, bot = [], []
        for hh in range(2):
            h = 2 * p + hh
            s_a, s_b = s_next
            if h + 1 < HEADS:
                s_next = scores(h + 1)
            s_top = jnp.where(tri, s_a[:half], NEG)
            s_bot = jnp.concatenate([s_a[half:], jnp.where(tri, s_b, NEG)], axis=1)
            top.append(_fox_rows_step(s_top, vpa[:half], m_ref, l_ref, h, 0, half, start))
            bot.append(_fox_rows_step(s_bot, vpa, m_ref, l_ref, h, half, tq, start))
        _fox_pair_acc(acc_ref, p, 0, half, *top, start)
        _fox_pair_acc(acc_ref, p, half, tq, *bot, start)


def _fox_prompt_kernel(qi_ref, kj_ref, qx_ref, kx_ref, vb_ref, o_ref, qpad_ref, m_ref, l_ref, acc_ref, *, kind):
    i = qi_ref[pl.program_id(1)]
    j = kj_ref[pl.program_id(1)]

    @pl.when(j == 0)
    def _():
        qx = qx_ref[...]
        for h in range(HEADS):
            qpad_ref[h] = _fox_qpad(qx, h, kind)

    for start in (True, False):
        is_start = (j == 0) if start else (j > 0)

        @pl.when((j < i) & is_start)
        def _(start=start):
            _fox_prompt_tile(kx_ref[...], vb_ref[...], qpad_ref, m_ref, l_ref, acc_ref, start)

        @pl.when((j == i) & is_start)
        def _(start=start):
            _fox_diag_tile(kx_ref[...], vb_ref[...], qpad_ref, m_ref, l_ref, acc_ref, kind, start)

    @pl.when(j == i)
    def _():
        first = lax.broadcasted_iota(jnp.int32, (qpad_ref.shape[1], LANES), 1) < FOX_DIM
        for p in range(N_PAIR):
            l_sel = jnp.where(first, l_ref[2 * p], l_ref[2 * p + 1])
            o_ref[:, LANES * p:LANES * (p + 1)] = (acc_ref[p] / l_sel).astype(o_ref.dtype)


def _fox_prompt(qx, kx, vb, *, batch, seq, t, kind):
    nq = seq // t
    assert CHUNK & (CHUNK - 1) == 0 and t % CHUNK == 0
    steps = [(i, j) for i in range(nq) for j in range(i + 1)]
    qi = jnp.asarray([s[0] for s in steps], jnp.int32)
    kj = jnp.asarray([s[1] for s in steps], jnp.int32)
    grid_spec = pltpu.PrefetchScalarGridSpec(
        num_scalar_prefetch=2, grid=(batch, len(steps)),
        in_specs=[pl.BlockSpec((t, 2 * FOX_W), lambda b, s, qi, kj: (b * nq + qi[s], 0)),
                  pl.BlockSpec((t, 2 * FOX_W), lambda b, s, qi, kj: (b * nq + kj[s], 0)),
                  pl.BlockSpec((t, FOX_W), lambda b, s, qi, kj: (b * nq + kj[s], 0))],
        out_specs=pl.BlockSpec((t, FOX_W), lambda b, s, qi, kj: (b * nq + qi[s], 0)),
        scratch_shapes=[pltpu.VMEM((HEADS, t, 2 * LANES), BF), pltpu.VMEM((HEADS, t, LANES), F32),
                        pltpu.VMEM((HEADS, t, LANES), F32), pltpu.VMEM((N_PAIR, t, LANES), F32)])
    return pl.pallas_call(
        functools.partial(_fox_prompt_kernel, kind=kind), grid_spec=grid_spec,
        out_shape=jax.ShapeDtypeStruct((batch * seq, FOX_W), BF),
        compiler_params=_params(2), name=kind + "_prompt")(qi, kj, qx, kx, vb)


def _fox_sample_kernel(qx_ref, kx_ref, vb_ref, fqs_ref, kct_ref, vct_ref, gk_ref, o_ref,
                       qbd_ref, qpad_ref, fq_ref, m_ref, l_ref, acc_ref, *, n_cache):
    j = pl.program_id(1)
    tq = qx_ref.shape[0]
    rows = HEADS * tq

    def update(s, pv_fn, g0=0, g1=2):
        a, b = 4 * tq * g0, 4 * tq * g1
        m_prev = m_ref[a:b]
        m_new = jnp.maximum(m_prev, jnp.max(s, axis=-1, keepdims=True))
        p = jnp.exp2(s - _lanes(m_new, s.shape[1]))
        alpha = jnp.exp2(m_prev - m_new)
        l_ref[a:b] = alpha * l_ref[a:b] + jnp.sum(p, axis=-1, keepdims=True)
        m_ref[a:b] = m_new
        pb = p.astype(BF)
        for pr in range(2 * g0, 2 * g1):
            r0, r1 = 2 * tq * pr, 2 * tq * (pr + 1)
            acc_ref[r0:r1] = acc_ref[r0:r1] * alpha[r0 - a:r1 - a] + pv_fn(pr, pb[r0 - a:r1 - a])

    @pl.when(j == 0)
    def _():
        qx = qx_ref[...]
        fqs = fqs_ref[...]
        qc = jnp.concatenate([qx[:, 2 * LANES * p:2 * LANES * p + LANES] for p in range(N_PAIR)], axis=1)
        lane = lax.broadcasted_iota(jnp.int32, qc.shape, 1)
        for h in range(HEADS):
            qbd_ref[h * tq:(h + 1) * tq, :] = jnp.where((lane // FOX_DIM) == h, qc, jnp.zeros_like(qc))
            qpad_ref[h] = _fox_qpad(qx, h)
            fq_ref[h * tq:(h + 1) * tq, :] = jnp.broadcast_to(fqs[:, h:h + 1], (tq, LANES))
        m_ref[...] = jnp.full(m_ref.shape, NEG, F32)
        l_ref[...] = jnp.zeros(l_ref.shape, F32)
        acc_ref[...] = jnp.zeros(acc_ref.shape, F32)
        kx = kx_ref[...]
        vb = vb_ref[...]
        tk = kx.shape[0]
        s = jnp.concatenate([_dot_nt(qpad_ref[h], kx[:, 2 * LANES * (h // 2):2 * LANES * (h // 2 + 1)])
                             for h in range(HEADS)], axis=0)
        row = lax.broadcasted_iota(jnp.int32, (rows, tk), 0) & (tq - 1)
        s = jnp.where(lax.broadcasted_iota(jnp.int32, (rows, tk), 1) <= row, s, NEG)
        update(s, lambda pr, pb: _dot(pb, vb[:, LANES * pr:LANES * (pr + 1)]))

    def cache_tile():
        tk = kct_ref.shape[3]
        gk = gk_ref[0]

        def scores(g):
            kt = kct_ref[0, 4 * g:4 * g + 4].reshape(4 * FOX_DIM, tk).astype(BF)
            gk_rows = jnp.concatenate([jnp.broadcast_to(gk[h:h + 1, :], (tq, tk)) for h in range(4 * g, 4 * g + 4)],
                                      axis=0)
            qg = qbd_ref[4 * tq * g:4 * tq * (g + 1), 4 * FOX_DIM * g:4 * FOX_DIM * (g + 1)]
            return _dot(qg, kt) + (_lanes(fq_ref[4 * tq * g:4 * tq * (g + 1)], tk) - gk_rows)

        def pv(pr, pb):
            return _dot_nt(pb, vct_ref[0, 2 * pr:2 * pr + 2].reshape(2 * FOX_DIM, tk).astype(BF))

        s0 = scores(0)
        s1 = scores(1)
        update(s0, pv, 0, 1)
        update(s1, pv, 1, 2)

    cache_tile()

    @pl.when(j == n_cache - 1)
    def _():
        first = lax.broadcasted_iota(jnp.int32, (tq, LANES), 1) < FOX_DIM
        o = acc_ref[...] / l_ref[...]
        for pr in range(N_PAIR):
            o_ref[:, LANES * pr:LANES * (pr + 1)] = jnp.where(
                first, o[2 * tq * pr:2 * tq * pr + tq], o[2 * tq * pr + tq:2 * tq * (pr + 1)]).astype(o_ref.dtype)


def _fox_sample(qx, kx, vb, fqs, kct, vct, gk, *, batch, seq, tkc):
    past = kct.shape[3]
    n_cache = past // tkc
    kern = functools.partial(_fox_sample_kernel, n_cache=n_cache)
    row = lambda w: pl.BlockSpec((seq, w), lambda b, j: (b, 0))
    return pl.pallas_call(
        kern, grid=(batch, n_cache),
        in_specs=[row(2 * FOX_W), row(2 * FOX_W), row(FOX_W), row(LANES),
                  pl.BlockSpec((1, HEADS, FOX_DIM, tkc), lambda b, j: (b, 0, 0, j)),
                  pl.BlockSpec((1, HEADS, FOX_DIM, tkc), lambda b, j: (b, 0, 0, j)),
                  pl.BlockSpec((1, HEADS, tkc), lambda b, j: (b, 0, j))],
        out_specs=row(FOX_W),
        out_shape=jax.ShapeDtypeStruct((batch * seq, FOX_W), BF),
        scratch_shapes=[pltpu.VMEM((HEADS * seq, FOX_W), BF), pltpu.VMEM((HEADS, seq, 2 * LANES), BF)]
        + [pltpu.VMEM((HEADS * seq, LANES), F32)] * 4,
        compiler_params=_params(2), name="fox_sample")(qx, kx, vb, fqs, kct, vct, gk)


def _chunk_mask(shape, tq, row0, q0, k0):
    assert tq & (tq - 1) == 0 and CHUNK & (CHUNK - 1) == 0
    shift = CHUNK.bit_length() - 1
    qpos = q0 + ((row0 + lax.broadcasted_iota(jnp.int32, shape, 0)) & (tq - 1))
    kpos = k0 + lax.broadcasted_iota(jnp.int32, shape, 1)
    return lax.shift_right_logical(kpos, shift) <= lax.shift_right_logical(qpos, shift)


def _mla_rows_update(s, val, m_ref, l_ref, acc_ref, r0, r1):
    m_prev = m_ref[r0:r1]
    m_new = jnp.maximum(m_prev, jnp.max(s, axis=-1, keepdims=True))
    p = jnp.exp2(s - _lanes(m_new, s.shape[1]))
    alpha = jnp.exp2(m_prev - m_new)
    l_ref[r0:r1] = alpha * l_ref[r0:r1] + jnp.sum(p, axis=-1, keepdims=True)
    m_ref[r0:r1] = m_new
    acc_ref[r0:r1] = acc_ref[r0:r1] * _lanes(alpha, KV_LORA) + _dot(p.astype(BF), val)


def _mla_init_stats(m_ref, l_ref, acc_ref):
    m_ref[...] = jnp.full(m_ref.shape, NEG, F32)
    l_ref[...] = jnp.zeros(l_ref.shape, F32)
    acc_ref[...] = jnp.zeros(acc_ref.shape, F32)


def _mla_finalize(o_ref, wuv_ref, l_ref, acc_ref, tq):
    olat = (acc_ref[...] / _lanes(l_ref[...], KV_LORA)).astype(BF)
    for g in range(2):
        out = None
        for h in range(4 * g, 4 * g + 4):
            part = _dot(olat[h * tq:(h + 1) * tq, :], wuv_ref[h])
            out = part if out is None else out + part
        o_ref[:, 4 * MLA_VDIM * g:4 * MLA_VDIM * (g + 1)] = out.astype(o_ref.dtype)


def _mla_scratch(rows):
    return [pltpu.VMEM((rows, LANES), F32), pltpu.VMEM((rows, LANES), F32), pltpu.VMEM((rows, KV_LORA), F32)]


def _mla_sample_kernel(qlat_ref, qrope_ref, mkey_ref, latc_ref, krt_ref, wuv_ref, o_ref,
                       ql_ref, qr_ref, m_ref, l_ref, acc_ref, *, n_cache, tq, past, rc):
    j = pl.program_id(1)
    rows = HEADS * tq

    @pl.when(j == 0)
    def _():
        qrope = qrope_ref[...]
        for h in range(HEADS):
            ql_ref[h * tq:(h + 1) * tq, :] = qlat_ref[h]
            qr_ref[h * tq:(h + 1) * tq, :] = qrope[:, MLA_ROPE * h:MLA_ROPE * (h + 1)]
        _mla_init_stats(m_ref, l_ref, acc_ref)
        mk = mkey_ref[...]
        s = _dot_nt(ql_ref[...], mk[:, :KV_LORA]) + _dot_nt(qr_ref[...], mk[:, KV_LORA:KV_LORA + MLA_ROPE])
        if past % CHUNK != 0 or tq > CHUNK:
            s = jnp.where(_chunk_mask(s.shape, tq, 0, past, past), s, NEG)
        _mla_rows_update(s, mk[:, :KV_LORA], m_ref, l_ref, acc_ref, 0, rows)

    def cache_tile():
        c = latc_ref[0].astype(BF)
        krt = krt_ref[0].astype(BF)
        def scores(r0):
            return _dot_nt(ql_ref[r0:r0 + rc], c) + _dot(qr_ref[r0:r0 + rc], krt)

        s_next = scores(0)
        for r0 in range(0, rows, rc):
            s = s_next
            if r0 + rc < rows:
                s_next = scores(r0 + rc)
            _mla_rows_update(s, c, m_ref, l_ref, acc_ref, r0, r0 + rc)

    cache_tile()

    @pl.when(j == n_cache - 1)
    def _():
        _mla_finalize(o_ref, wuv_ref, l_ref, acc_ref, tq)


def _mla_sample(qlat, qrope, mkey, latc, krt, wuvp, *, batch, seq, tkc, rc):
    past = latc.shape[1]
    n_cache = past // tkc
    kern = functools.partial(_mla_sample_kernel, n_cache=n_cache, tq=seq, past=past, rc=rc)
    return pl.pallas_call(
        kern, grid=(batch, n_cache),
        in_specs=[pl.BlockSpec((HEADS, seq, KV_LORA), lambda b, j: (0, b, 0)),
                  pl.BlockSpec((seq, HEADS * MLA_ROPE), lambda b, j: (b, 0)),
                  pl.BlockSpec((seq, 2 * KV_LORA), lambda b, j: (b, 0)),
                  pl.BlockSpec((1, tkc, KV_LORA), lambda b, j: (b, j, 0)),
                  pl.BlockSpec((1, MLA_ROPE, tkc), lambda b, j: (b, 0, j)),
                  _const_spec(wuvp.shape)],
        out_specs=pl.BlockSpec((seq, FOX_W), lambda b, j: (b, 0)),
        out_shape=jax.ShapeDtypeStruct((batch * seq, FOX_W), BF),
        scratch_shapes=[pltpu.VMEM((HEADS * seq, KV_LORA), BF), pltpu.VMEM((HEADS * seq, MLA_ROPE), BF)]
        + _mla_scratch(HEADS * seq),
        compiler_params=_params(2), name="mla_sample")(qlat, qrope, mkey, latc, krt, wuvp)


def _ffn_kernel(x_ref, fox_ref, mla_ref, pconv_ref, wo_ref, gf_ref, wup_ref, cw_ref, wdn_ref, gfin_ref,
                y_ref, cst_ref, hn_ref, act_ref, ush_ref, carry_ref, *, tm, seq_len):
    n_seg = max(1, tm // seq_len)
    seg = tm // n_seg
    mixed = jnp.concatenate([fox_ref[...], mla_ref[...]], axis=1)
    x1 = x_ref[...] + _dot(mixed, wo_ref[...])
    hn_ref[...] = (x1 * lax.rsqrt(jnp.mean(x1 * x1, axis=-1, keepdims=True) + EPS) * gf_ref[...]).astype(BF)
    y_ref[...] = x1

    if n_seg == 1:
        @pl.when(pl.program_id(0) % (seq_len // tm) == 0)
        def _():
            carry_ref[...] = pconv_ref[0]

    def conv(u, col, slot):
        cols = slice(col, col + FF_CHUNK)
        for s in range(n_seg):
            ush_ref[slot, s, 6:8, :] = carry_ref[:, cols] if n_seg == 1 else pconv_ref[s, :, cols]
            ush_ref[slot, s, 8:8 + seg, :] = u[seg * s:seg * (s + 1), :]
            cst_ref[s, :, cols] = u[seg * (s + 1) - 2:seg * (s + 1), :]
        if n_seg == 1:
            carry_ref[:, cols] = u[tm - 2:tm, :]
        u1 = jnp.concatenate([ush_ref[slot, s, 7:7 + seg, :] for s in range(n_seg)], axis=0)
        u2 = jnp.concatenate([ush_ref[slot, s, 6:6 + seg, :] for s in range(n_seg)], axis=0)
        cw = cw_ref[:, cols]
        return cw[3:4, :] + cw[0:1, :] * u2 + cw[1:2, :] * u1 + cw[2:3, :] * u

    def up(c):
        hn = hn_ref[...]
        g0, v0 = FF_CHUNK * c, D_FF + FF_CHUNK * c
        return _dot(hn, wup_ref[:, g0:g0 + FF_CHUNK]), _dot(hn, wup_ref[:, v0:v0 + FF_CHUNK])

    u_next = up(0)
    for c in range(N_FF):
        ug, uv = u_next
        if c + 1 < N_FF:
            u_next = up(c + 1)
        gate = conv(ug, FF_CHUNK * c, 2 * (c % 2))
        val = conv(uv, D_FF + FF_CHUNK * c, 2 * (c % 2) + 1)
        act_ref[:, FF_CHUNK * c:FF_CHUNK * (c + 1)] = (gate * jax.nn.sigmoid(gate) * val).astype(BF)

    x2 = y_ref[...] + _dot(act_ref[...], wdn_ref[...])
    y_ref[...] = x2 * lax.rsqrt(jnp.mean(x2 * x2, axis=-1, keepdims=True) + EPS) * gfin_ref[...]


def _ffn(x2d, fox, mla, pconv, wo, gf, wup, cw, wdn, gfin, *, seq_len, tm):
    n = x2d.shape[0]
    nt = n // tm
    n_seq = n // seq_len
    n_seg = max(1, tm // seq_len)
    if n_seg == 1:
        tps = seq_len // tm
        st_map = lambda i: (i // tps, 0, 0)
    else:
        st_map = lambda i: (i, 0, 0)
    row = lambda w: pl.BlockSpec((tm, w), lambda i: (i, 0))
    kern = functools.partial(_ffn_kernel, tm=tm, seq_len=seq_len)
    return pl.pallas_call(
        kern, grid=(nt,),
        in_specs=[row(D_MODEL), row(FOX_W), row(FOX_W),
                  pl.BlockSpec((n_seg, CONV_W - 1, 2 * D_FF), st_map),
                  _const_spec(wo.shape), _const_spec(gf.shape), _const_spec(wup.shape), _const_spec(cw.shape),
                  _const_spec(wdn.shape), _const_spec(gfin.shape)],
        out_specs=(row(D_MODEL), pl.BlockSpec((n_seg, CONV_W - 1, 2 * D_FF), st_map)),
        out_shape=(jax.ShapeDtypeStruct((n, D_MODEL), F32),
                   jax.ShapeDtypeStruct((n_seq, CONV_W - 1, 2 * D_FF), F32)),
        scratch_shapes=[pltpu.VMEM((tm, D_MODEL), BF), pltpu.VMEM((tm, D_FF), BF),
                        pltpu.VMEM((4, n_seg, 8 + tm // n_seg, FF_CHUNK), F32),
                        pltpu.VMEM((CONV_W - 1, 2 * D_FF), F32)],
        compiler_params=_params(1), name="ffn")(x2d, fox, mla, pconv, wo, gf, wup, cw, wdn, gfin)


def _rope_tables(start, n, reps=1):
    half = MLA_ROPE // 2
    inv = ROPE_THETA ** (-np.arange(half, dtype=np.float64) / half)
    ang = (start + np.arange(n, dtype=np.float64))[:, None] * inv[None, :]
    cos = np.tile(np.cos(ang), (1, 2 * HEADS))
    sin = np.tile(np.sin(ang), (1, 2 * HEADS))
    ones = np.ones((n, MLA_NOPE))
    zeros = np.zeros((n, LANES - MLA_NOPE - MLA_ROPE))
    cosm = np.concatenate([ones, cos[:, :MLA_ROPE], zeros], axis=1)
    sinm = np.concatenate([0.0 * ones, sin[:, :MLA_ROPE], zeros], axis=1)
    return [jnp.asarray(np.tile(a, (reps, 1)), F32) for a in (cos, sin, cosm, sinm)]


def _placement():
    place = np.zeros((LANES, 2 * N_PAIR * LANES), np.float32)
    ones = np.zeros((1, 2 * N_PAIR * LANES), np.float32)
    for h in range(HEADS):
        for i in range(3):
            for p in range(N_PAIR):
                place[HEADS * i + h, LANES * p + AUG * h + i] = 1.0
                ones[0, LANES * p + AUG * h + 3 + i] = 1.0
                ones[0, N_PAIR * LANES + LANES * p + AUG * h + i] = 1.0
                place[HEADS * i + h, N_PAIR * LANES + LANES * p + AUG * h + 3 + i] = -1.0
    rep = np.zeros((LANES, HEADS * MLA_ROPE), np.float32)
    for h in range(HEADS):
        for r in range(MLA_ROPE):
            rep[r, MLA_ROPE * h + r] = 1.0
    return jnp.asarray(place, BF), jnp.asarray(ones, F32), jnp.asarray(rep, BF)


def _rot_cols(w):
    half = MLA_ROPE // 2
    return jnp.concatenate([-w[..., half:], w[..., :half]], axis=-1)


def _pad_cols(w, width):
    return jnp.pad(w, ((0, 0), (0, width - w.shape[1])))


def _layer_weights(w_in, b_f, w_q_up, w_uk, w_uv, w_out, w_up, conv_w, conv_b, w_down):
    wq, wk, wv, wf, wqc, wckv, wkr = jnp.split(w_in, IN_SPLITS, axis=1)
    wa = jnp.concatenate([wq, wk, wv, wqc, wckv, _pad_cols(wkr, LANES), _pad_cols(_rot_cols(wkr), LANES),
                          _pad_cols(wf, LANES)], axis=1).astype(BF)
    bfp = _pad_cols(b_f[None, :], LANES).astype(F32)
    wq3 = w_q_up.reshape(Q_LORA, HEADS, MLA_NOPE + MLA_ROPE)
    wq_nope = wq3[:, :, :MLA_NOPE].reshape(Q_LORA, HEADS * MLA_NOPE)
    wq_rope = wq3[:, :, MLA_NOPE:]
    wq2 = jnp.concatenate([wq_nope, wq_rope.reshape(Q_LORA, -1), _rot_cols(wq_rope).reshape(Q_LORA, -1)],
                          axis=1).astype(BF)
    eye = jnp.eye(4, dtype=F32)
    wk = jnp.transpose(w_uk, (1, 2, 0)).reshape(2, 4, 1, MLA_NOPE, KV_LORA)
    wukp = (wk * eye[None, :, :, None, None]).reshape(HEADS, 4 * MLA_NOPE, KV_LORA)
    wv = jnp.transpose(w_uv, (1, 0, 2)).reshape(2, 4, KV_LORA, 1, MLA_VDIM)
    wuvp = (wv * eye[None, :, None, :, None]).reshape(HEADS, KV_LORA, 4 * MLA_VDIM)
    wup = w_up.astype(BF)
    cw = jnp.concatenate([conv_w, conv_b[None, :]], axis=0)
    wdn = w_down.astype(BF)
    zpad = ((0, 0), (0, 0), (0, LANES - MLA_NOPE - MLA_ROPE))
    wq_a = jnp.pad(wq3, zpad).reshape(Q_LORA, HEADS * LANES)
    wq_b = jnp.pad(jnp.concatenate([jnp.zeros_like(wq3[:, :, :MLA_NOPE]), _rot_cols(wq_rope)], axis=2), zpad)
    wqm = jnp.concatenate([wq_a, wq_b.reshape(Q_LORA, HEADS * LANES)], axis=1).astype(BF)
    wkm = jnp.pad(w_uk, ((0, 0), (0, 0), (0, LANES - MLA_NOPE))).reshape(KV_LORA, HEADS * LANES).astype(BF)
    wvm = w_uv.reshape(KV_LORA, HEADS * MLA_VDIM).astype(BF)
    return wa, bfp, wq2, wukp.astype(BF), wuvp.astype(BF), w_out.astype(BF), wup, cw, wdn, wqm, wkm, wvm


def kernel(x_prompt, x_sample, cache_fox_k, cache_fox_v, cache_fox_logf, cache_mla_latent, cache_mla_krope,
           state_ffn_conv, attn_norm, w_in, b_forget, q_norm, w_q_up, kv_norm, w_uk, w_uv, w_out, ffn_norm,
           w_up, conv_w, conv_b, w_down, final_norm):
    assert attn_norm.shape[0] == 1, "single-layer stack"
    bp, tp, _ = x_prompt.shape
    bs, ts, _ = x_sample.shape
    past = cache_fox_k.shape[2]
    assert ts == CHUNK and past % CHUNK == 0

    wa, bfp, wq2, wukp, wuvp, wo, wup, cw, wdn, wqm, wkm, wvm = _layer_weights(
        w_in[0], b_forget[0], w_q_up[0], w_uk[0], w_uv[0], w_out[0], w_up[0], conv_w[0], conv_b[0], w_down[0])
    place, ones, rep = _placement()
    g_attn = attn_norm[0][None, :]
    g_q = q_norm[0][None, :]
    g_kv = kv_norm[0][None, :]
    g_ffn = ffn_norm[0][None, :]
    g_fin = final_norm[None, :]

    tm = ROW_TILE
    assert (bp * tp) % tm == 0 and (bs * ts) % tm == 0 and tp % ATTN_TILE == 0
    outs = []
    for grp in ("prompt", "sample"):
        if grp == "prompt":
            x2d = x_prompt.reshape(bp * tp, D_MODEL)
            b, t = bp, tp
            tabs = _rope_tables(0, tp)
            pconv = jnp.zeros((bp, CONV_W - 1, 2 * D_FF), F32)
        else:
            x2d = x_sample.reshape(bs * ts, D_MODEL)
            b, t = bs, ts
            tabs = _rope_tables(past, ts, reps=tm // ts)
            pconv = state_ffn_conv[0]

        absorbed = grp == "sample"
        wm = (wq2, wukp, rep) if absorbed else (wqm, wkm, wvm)
        (k, v, logf, lat, kr, qx, kx, vb, m1, m2, m3, fqs) = _proj(
            x2d, *tabs, g_attn, wa, bfp, g_q, g_kv, *wm, place, ones, seq_len=t, tm=tm, absorbed=absorbed)

        if grp == "prompt":
            fox = _fox_prompt(qx, kx, vb, batch=b, seq=t, t=ATTN_TILE, kind="fox")
            mla = _fox_prompt(m1, m2, m3, batch=b, seq=t, t=ATTN_TILE, kind="mla")
        else:
            mkey, qlat, qrope = m1, m2, m3
            gk = _fox_prep(jnp.transpose(cache_fox_logf[0], (0, 2, 1)))
            kct = jnp.transpose(cache_fox_k[0], (0, 2, 3, 1))
            vct = jnp.transpose(cache_fox_v[0], (0, 2, 3, 1))
            krt = jnp.transpose(cache_mla_krope[0], (0, 2, 1))
            assert past % min(past, FOX_CACHE_TILE) == 0 and past % min(past, MLA_CACHE_TILE) == 0
            fox = _fox_sample(qx, kx, vb, fqs, kct, vct, gk, batch=b, seq=t, tkc=min(past, FOX_CACHE_TILE))
            mla = _mla_sample(qlat, qrope, mkey, cache_mla_latent[0], krt, wuvp, batch=b, seq=t,
                              tkc=min(past, MLA_CACHE_TILE), rc=MLA_ROW_CHUNK)

        def small_state(a, w, b=b, t=t):
            if t >= tm:
                return jnp.transpose(a, (0, 2, 1))[None]
            a = a.reshape(a.shape[0], w, tm // t, t)
            return jnp.transpose(a, (0, 2, 3, 1)).reshape(b, t, w)[None]

        y, cst = _ffn(x2d, fox, mla, pconv, wo, g_ffn, wup, cw, wdn, g_fin, seq_len=t, tm=tm)
        outs.append((y.reshape(b, t, D_MODEL),
                     k.reshape(1, b, t, HEADS, FOX_DIM), v.reshape(1, b, t, HEADS, FOX_DIM),
                     small_state(logf, HEADS), lat.reshape(1, b, t, KV_LORA), small_state(kr, MLA_ROPE),
                     cst[None]))
    (yp, *st_p), (ys, *st_s) = outs
    return (yp, ys, *st_p, *st_s)
```
